```python
import jax, jax.numpy as jnp
from jax import lax
import numpy as np

D_MODEL = 1024
BATCH = 8
SEQ = 2048
DEPTH = 1
DEC_BATCH = 8
DEC_SEQ = 64
PAST_LEN = 2048

CHUNK = 64
A_HEADS = 8
A_HEAD_DIM = 64
A_WIDTH = A_HEADS * A_HEAD_DIM
A_PAST_CHUNKS = 8
A_BAND = (A_PAST_CHUNKS + 1) * CHUNK
A_MAX_REL = 256
MLP_CHUNK = 128
MLP_GROUPS = 8
MLP_WIDTH = 512
MLP_GROUP_DIM = MLP_WIDTH // MLP_GROUPS
N_IN = 3 * A_WIDTH + 2 * MLP_WIDTH + 2 * D_MODEL
SPLIT_POINTS = (A_WIDTH, 2 * A_WIDTH, 3 * A_WIDTH, 3 * A_WIDTH + MLP_WIDTH,
                3 * A_WIDTH + 2 * MLP_WIDTH, 3 * A_WIDTH + 2 * MLP_WIDTH + D_MODEL)
N_EXPERTS = 64
TOP_K = 8
N_GROUPS = 8
TOPK_GROUPS = 4
EXPERT_DIM = 256
SHARED_DIM = 256
ROUTED_SCALE = 2.5
MOE_BLOCK = 256
PLE_DIM = 256
ALPHA = (2 * DEPTH) ** 0.25
BETA = (8 * DEPTH) ** -0.25
LN_EPS = 1e-5

kernel_name = 'streaming_hybrid_chunkattn_gmlp_moe_step'


def layer_norm(x, g, b):
    xf = x.astype(jnp.float32)
    xc = xf - xf.mean(-1, keepdims=True)
    var = jnp.mean(xc * xc, -1, keepdims=True)
    y = xc * lax.rsqrt(var + LN_EPS) * g.astype(jnp.float32) + b.astype(jnp.float32)
    return y.astype(x.dtype)


def relative_bias(table, rel):
    idx = jnp.clip(rel, -A_MAX_REL, A_MAX_REL) + A_MAX_REL
    return table[:, idx].astype(jnp.float32)


def chunk_attention_prompt(q, k, v, rel_table):
    b, s, h, dh = q.shape
    nc = s // CHUNK
    qc = q.reshape(b, nc, CHUNK, h, dh)
    pad = ((0, 0), (A_PAST_CHUNKS, 0), (0, 0), (0, 0), (0, 0))
    kp = jnp.pad(k.reshape(b, nc, CHUNK, h, dh), pad)
    vp = jnp.pad(v.reshape(b, nc, CHUNK, h, dh), pad)
    kb = jnp.concatenate([kp[:, w:w + nc] for w in range(A_PAST_CHUNKS + 1)], axis=2)
    vb = jnp.concatenate([vp[:, w:w + nc] for w in range(A_PAST_CHUNKS + 1)], axis=2)
    rel = (jnp.arange(A_BAND)[None, :] - A_PAST_CHUNKS * CHUNK) - jnp.arange(CHUNK)[:, None]
    bias = relative_bias(rel_table, rel)
    key_chunk = jnp.arange(nc)[:, None] - A_PAST_CHUNKS + jnp.arange(A_BAND)[None, :] // CHUNK
    valid = key_chunk >= 0
    sc = jnp.einsum('bnqhd,bnkhd->bnhqk', qc, kb).astype(jnp.float32) * (A_HEAD_DIM ** -0.5)
    sc = sc + bias[None, None]
    sc = jnp.where(valid[None, :, None, None, :], sc, -jnp.inf)
    pr = jax.nn.softmax(sc, axis=-1).astype(v.dtype)
    o = jnp.einsum('bnhqk,bnkhd->bnqhd', pr, vb)
    return o.reshape(b, s, h * dh)


def chunk_attention_sample(q, k, v, cache_k, cache_v, rel_table):
    b, t, h, dh = q.shape
    L = cache_k.shape[1]
    kk = jnp.concatenate([cache_k.astype(k.dtype), k], axis=1)
    vv = jnp.concatenate([cache_v.astype(v.dtype), v], axis=1)
    rel = jnp.arange(L + t)[None, :] - L - jnp.arange(t)[:, None]
    bias = relative_bias(rel_table, rel)
    sc = jnp.einsum('bqhd,bkhd->bhqk', q, kk).astype(jnp.float32) * (A_HEAD_DIM ** -0.5) + bias[None]
    pr = jax.nn.softmax(sc, axis=-1).astype(v.dtype)
    o = jnp.einsum('bhqk,bkhd->bqhd', pr, vv)
    return o.reshape(b, t, h * dh)


def spatial_gating(u, vn, w_s, b_s):
    b, t, _ = u.shape
    L = min(t, MLP_CHUNK)
    n = t // L
    w = w_s[:, :L, :L] * jnp.tril(jnp.ones((L, L), w_s.dtype))
    vc = vn.reshape(b, n, L, MLP_GROUPS, MLP_GROUP_DIM)
    mixed = jnp.einsum('gts,bnsgc->bntgc', w, vc) + b_s[:, :L].T[:, :, None]
    out = u.reshape(b, n, L, MLP_GROUPS, MLP_GROUP_DIM) * mixed
    return out.reshape(b, t, MLP_WIDTH).astype(u.dtype)


def moe_ffn(h, router_w, router_bias, exp_w1, exp_w3, exp_w2, sh_w1, sh_w3, sh_w2):
    t = h.shape[0]
    scores = jax.nn.sigmoid(jnp.dot(h.astype(jnp.float32), router_w.astype(jnp.float32)))
    sel = scores + router_bias.astype(jnp.float32)
    grp = sel.reshape(t, N_GROUPS, N_EXPERTS // N_GROUPS)
    grp_score = lax.top_k(grp, 2)[0].sum(-1)
    _, grp_idx = lax.top_k(grp_score, TOPK_GROUPS)
    grp_keep = (grp_idx[..., None] == jnp.arange(N_GROUPS)).any(axis=1)
    exp_keep = jnp.repeat(grp_keep, N_EXPERTS // N_GROUPS, axis=1)
    _, top_idx = lax.top_k(jnp.where(exp_keep, sel, -jnp.inf), TOP_K)
    top_w = jnp.take_along_axis(scores, top_idx, axis=-1)
    top_w = top_w / top_w.sum(-1, keepdims=True) * ROUTED_SCALE
    n_assign = t * TOP_K
    flat_e = top_idx.reshape(-1)
    flat_tok = jnp.repeat(jnp.arange(t, dtype=jnp.int32), TOP_K)
    flat_w = top_w.reshape(-1).astype(h.dtype)
    order = jnp.argsort(flat_e, stable=True)
    e_sorted = flat_e[order]
    counts = jnp.bincount(flat_e, length=N_EXPERTS)
    start = jnp.cumsum(counts) - counts
    padded = (counts + MOE_BLOCK - 1) // MOE_BLOCK * MOE_BLOCK
    pad_end = jnp.cumsum(padded)
    pad_start = pad_end - padded
    dest = pad_start[e_sorted] + jnp.arange(n_assign) - start[e_sorted]
    n_blocks = -(-n_assign // MOE_BLOCK) + N_EXPERTS
    n_slots = n_blocks * MOE_BLOCK
    slot_tok = jnp.zeros((n_slots,), jnp.int32).at[dest].set(flat_tok[order])
    slot_w = jnp.zeros((n_slots,), h.dtype).at[dest].set(flat_w[order])
    blk_e = jnp.minimum(jnp.searchsorted(pad_end, jnp.arange(n_blocks) * MOE_BLOCK, side='right'),
                        N_EXPERTS - 1)

    def expert_block(args):
        tok_b, w_b, e = args
        xb = h[tok_b]
        a = xb @ exp_w1[e]
        g = xb @ exp_w3[e]
        return ((jax.nn.silu(a) * g) @ exp_w2[e]) * w_b[:, None]

    y_slots = lax.map(expert_block, (slot_tok.reshape(n_blocks, MOE_BLOCK),
                                     slot_w.reshape(n_blocks, MOE_BLOCK), blk_e))
    routed = jax.ops.segment_sum(y_slots.reshape(n_slots, -1), slot_tok, num_segments=t)
    shared = (jax.nn.silu(h @ sh_w1) * (h @ sh_w3)) @ sh_w2
    return (routed + shared).astype(h.dtype)


def trunk_layer(x, p, cache_k, cache_v, w_in, rel_table, mlp_ln_g, mlp_ln_b, mlp_w_s, mlp_b_s,
                w_branch_a, w_branch_b, w_out, ln1_g, ln1_b, router_w, router_bias,
                exp_w1, exp_w3, exp_w2, sh_w1, sh_w3, sh_w2, ln2_g, ln2_b,
                ple_w_gate, ple_w_proj):
    b, t, d = x.shape
    z = x @ w_in
    q, k, v, u, vs, ga, gb = jnp.split(z, SPLIT_POINTS, axis=-1)
    q = q.reshape(b, t, A_HEADS, A_HEAD_DIM)
    k = k.reshape(b, t, A_HEADS, A_HEAD_DIM)
    v = v.reshape(b, t, A_HEADS, A_HEAD_DIM)
    if cache_k is None:
        oa = chunk_attention_prompt(q, k, v, rel_table)
        keep = min(A_PAST_CHUNKS * CHUNK, t)
        new_k, new_v = k[:, t - keep:], v[:, t - keep:]
    else:
        oa = chunk_attention_sample(q, k, v, cache_k, cache_v, rel_table)
        new_k, new_v = k, v
    un = jax.nn.gelu(u)
    vn = layer_norm(jax.nn.gelu(vs), mlp_ln_g, mlp_ln_b)
    ob = spatial_gating(un, vn, mlp_w_s, mlp_b_s)
    new_mlp_v = vn[:, ((t - 1) // MLP_CHUNK) * MLP_CHUNK:]
    mix = jax.nn.sigmoid(ga) * (oa @ w_branch_a) + jax.nn.sigmoid(gb) * (ob @ w_branch_b)
    x = layer_norm(ALPHA * x + mix @ w_out, ln1_g, ln1_b)
    ff = moe_ffn(x.reshape(b * t, d), router_w, router_bias, exp_w1, exp_w3, exp_w2,
                 sh_w1, sh_w3, sh_w2).reshape(b, t, d)
    x = layer_norm(ALPHA * x + ff, ln2_g, ln2_b)
    x = x + jax.nn.sigmoid(x @ ple_w_gate) * (p @ ple_w_proj)
    return x, new_k, new_v, new_mlp_v


def setup_inputs(seed: int = 0) -> dict:
    key = jax.random.key(seed)
    ks = jax.random.split(key, 40)
    f32 = jnp.float32

    def nrm(k, shape, scale=1.0):
        return jax.random.normal(k, shape, f32) * scale

    cache_rows = min(A_PAST_CHUNKS * CHUNK, PAST_LEN)
    return {
        'x_prompt': nrm(ks[0], (BATCH, SEQ, D_MODEL)),
        'x_sample': nrm(ks[1], (DEC_BATCH, DEC_SEQ, D_MODEL)),
        'cache_attn_k': nrm(ks[2], (DEPTH, DEC_BATCH, cache_rows, A_HEADS, A_HEAD_DIM)),
        'cache_attn_v': nrm(ks[3], (DEPTH, DEC_BATCH, cache_rows, A_HEADS, A_HEAD_DIM)),
        'p_prompt': nrm(ks[4], (DEPTH, BATCH, SEQ, PLE_DIM)),
        'p_sample': nrm(ks[5], (DEPTH, DEC_BATCH, DEC_SEQ, PLE_DIM)),
        'ln_in_g': 1.0 + nrm(ks[6], (D_MODEL,), 0.05),
        'ln_in_b': nrm(ks[7], (D_MODEL,), 0.05),
        'w_in': nrm(ks[8], (DEPTH, D_MODEL, N_IN), D_MODEL ** -0.5),
        'attn_rel_bias': nrm(ks[9], (DEPTH, A_HEADS, 2 * A_MAX_REL + 1), 0.1),
        'mlp_ln_g': 1.0 + nrm(ks[10], (DEPTH, MLP_WIDTH), 0.05),
        'mlp_ln_b': nrm(ks[11], (DEPTH, MLP_WIDTH), 0.05),
        'mlp_w_s': nrm(ks[12], (DEPTH, MLP_GROUPS, MLP_CHUNK, MLP_CHUNK), MLP_CHUNK ** -0.5),
        'mlp_b_s': 1.0 + nrm(ks[13], (DEPTH, MLP_GROUPS, MLP_CHUNK), 0.1),
        'w_branch_a': nrm(ks[14], (DEPTH, A_WIDTH, D_MODEL), A_WIDTH ** -0.5),
        'w_branch_b': nrm(ks[15], (DEPTH, MLP_WIDTH, D_MODEL), MLP_WIDTH ** -0.5),
        'w_out': nrm(ks[16], (DEPTH, D_MODEL, D_MODEL), BETA * D_MODEL ** -0.5),
        'ln1_g': 1.0 + nrm(ks[17], (DEPTH, D_MODEL), 0.05),
        'ln1_b': nrm(ks[18], (DEPTH, D_MODEL), 0.05),
        'router_w': nrm(ks[19], (DEPTH, D_MODEL, N_EXPERTS), D_MODEL ** -0.5),
        'router_bias': nrm(ks[20], (DEPTH, N_EXPERTS), 0.01),
        'exp_w1': nrm(ks[21], (DEPTH, N_EXPERTS, D_MODEL, EXPERT_DIM), D_MODEL ** -0.5),
        'exp_w3': nrm(ks[22], (DEPTH, N_EXPERTS, D_MODEL, EXPERT_DIM), D_MODEL ** -0.5),
        'exp_w2': nrm(ks[23], (DEPTH, N_EXPERTS, EXPERT_DIM, D_MODEL), BETA * EXPERT_DIM ** -0.5),
        'shared_w1': nrm(ks[24], (DEPTH, D_MODEL, SHARED_DIM), D_MODEL ** -0.5),
        'shared_w3': nrm(ks[25], (DEPTH, D_MODEL, SHARED_DIM), D_MODEL ** -0.5),
        'shared_w2': nrm(ks[26], (DEPTH, SHARED_DIM, D_MODEL), BETA * SHARED_DIM ** -0.5),
        'ln2_g': 1.0 + nrm(ks[27], (DEPTH, D_MODEL), 0.05),
        'ln2_b': nrm(ks[28], (DEPTH, D_MODEL), 0.05),
        'ple_w_gate': nrm(ks[29], (DEPTH, D_MODEL, D_MODEL), D_MODEL ** -0.5),
        'ple_w_proj': nrm(ks[30], (DEPTH, PLE_DIM, D_MODEL), PLE_DIM ** -0.5),
    }


def reference(x_prompt, x_sample, cache_attn_k, cache_attn_v, p_prompt, p_sample,
              ln_in_g, ln_in_b, w_in, attn_rel_bias, mlp_ln_g, mlp_ln_b, mlp_w_s, mlp_b_s,
              w_branch_a, w_branch_b, w_out, ln1_g, ln1_b, router_w, router_bias,
              exp_w1, exp_w3, exp_w2, shared_w1, shared_w3, shared_w2, ln2_g, ln2_b,
              ple_w_gate, ple_w_proj):
    xp = layer_norm(x_prompt, ln_in_g, ln_in_b)
    xs = layer_norm(x_sample, ln_in_g, ln_in_b)
    kp_l, vp_l, ks_l, vs_l, mp_l, ms_l = [], [], [], [], [], []
    for i in range(DEPTH):
        lw = (w_in[i], attn_rel_bias[i], mlp_ln_g[i], mlp_ln_b[i], mlp_w_s[i], mlp_b_s[i],
              w_branch_a[i], w_branch_b[i], w_out[i], ln1_g[i], ln1_b[i], router_w[i],
              router_bias[i], exp_w1[i], exp_w3[i], exp_w2[i], shared_w1[i], shared_w3[i],
              shared_w2[i], ln2_g[i], ln2_b[i], ple_w_gate[i], ple_w_proj[i])
        xp, kp, vp, mp = trunk_layer(xp, p_prompt[i], None, None, *lw)
        xs, ksn, vsn, msn = trunk_layer(xs, p_sample[i], cache_attn_k[i], cache_attn_v[i], *lw)
        kp_l.append(kp)
        vp_l.append(vp)
        ks_l.append(ksn)
        vs_l.append(vsn)
        mp_l.append(mp)
        ms_l.append(msn)
    return (xp, xs, jnp.stack(kp_l), jnp.stack(vp_l), jnp.stack(ks_l), jnp.stack(vs_l),
            jnp.stack(mp_l), jnp.stack(ms_l))
```

```python
import functools
import math

import jax
import jax.numpy as jnp
from jax import lax
from jax.experimental import pallas as pl
from jax.experimental.pallas import tpu as pltpu

F32 = jnp.float32
BF16 = jnp.bfloat16

D_MODEL = 1024
CHUNK = 64
HEADS = 8
HEAD_DIM = 64
A_WIDTH = HEADS * HEAD_DIM
PAST_CHUNKS = 8
MAX_REL = 256
MLP_CHUNK = 128
MLP_GROUPS = 8
MLP_WIDTH = 512
N_EXPERTS = 64
TOP_K = 8
N_GROUPS = 8
TOPK_GROUPS = 4
GROUP_SIZE = N_EXPERTS // N_GROUPS
EXPERT_DIM = 256
ROUTED_SCALE = 2.5
LN_EPS = 1e-5
MASKED = -1e30

LANES = 128
SUBLANES = 8
VMEM_LIMIT_BYTES = 56 * 1024 * 1024

ROW_TILE = 512
Q_CHUNKS = 4
MOE_ROWS = 128
MOE_GROUP_TOKENS = 2048
LHS_STRIDE = MOE_ROWS + SUBLANES
D_TILES = D_MODEL // LANES


def _layer_norm(x, g, b):
    xc = x - jnp.mean(x, axis=-1, keepdims=True)
    var = jnp.mean(xc * xc, axis=-1, keepdims=True)
    return xc * lax.rsqrt(var + LN_EPS) * g + b


def _dot(a, b):
    return jnp.dot(a, b, preferred_element_type=F32)


def _dot_nt(a, b, precision=None):
    return lax.dot_general(a, b, (((1,), (1,)), ((), ())),
                           preferred_element_type=F32, precision=precision)


def _const_spec(shape):
    zeros = (0,) * len(shape)
    return pl.BlockSpec(shape, lambda *_: zeros)


def _params(n_axes):
    return pltpu.CompilerParams(dimension_semantics=("arbitrary",) * n_axes,
                                vmem_limit_bytes=VMEM_LIMIT_BYTES)


def _proj_kernel(x_ref, lng_ref, lnb_ref, w_ref, mg_ref, mb_ref, ws_ref, bs_ref,
                 q_ref, k_ref, v_ref, kf_ref, vf_ref, vn_ref, ga_ref, gb_ref, ob_ref,
                 *, mlp_len, vn_rows):
    rows = x_ref.shape[0]
    xn = _layer_norm(x_ref[...], lng_ref[...], lnb_ref[...]).astype(BF16)

    def section(lo, width):
        return _dot(xn, w_ref[:, lo:lo + width])

    q = section(0, A_WIDTH)
    q_ref[...] = (q * (HEAD_DIM ** -0.5)).astype(BF16)
    k = section(A_WIDTH, A_WIDTH)
    k_ref[...] = k.astype(BF16)
    kf_ref[...] = k
    v = section(2 * A_WIDTH, A_WIDTH)
    v_ref[...] = v.astype(BF16)
    vf_ref[...] = v
    base = 3 * A_WIDTH
    un = jax.nn.gelu(section(base, MLP_WIDTH))
    vn = _layer_norm(jax.nn.gelu(section(base + MLP_WIDTH, MLP_WIDTH)), mg_ref[...], mb_ref[...])
    vn_ref[...] = vn[rows - vn_rows:, :]
    base += 2 * MLP_WIDTH
    ga_ref[...] = jax.nn.sigmoid(section(base, D_MODEL)).astype(BF16)
    gb_ref[...] = jax.nn.sigmoid(section(base + D_MODEL, D_MODEL)).astype(BF16)

    vnb = vn.astype(BF16)
    low_half = lax.broadcasted_iota(jnp.int32, (mlp_len, LANES), 1) < (MLP_WIDTH // MLP_GROUPS)
    for c in range(rows // mlp_len):
        r0 = c * mlp_len
        for p in range(MLP_GROUPS // 2):
            c0 = p * LANES
            slab = vnb[r0:r0 + mlp_len, c0:c0 + LANES]
            mixed = jnp.where(low_half, _dot(ws_ref[2 * p], slab), _dot(ws_ref[2 * p + 1], slab))
            gated = un[r0:r0 + mlp_len, c0:c0 + LANES] * (mixed + bs_ref[:, c0:c0 + LANES])
            ob_ref[r0:r0 + mlp_len, c0:c0 + LANES] = gated.astype(BF16)


def _proj(x2d, ln_g, ln_b, w_in, mlp_g, mlp_b, w_s, b_s, *, seq, mlp_len, keep_rows):
    n_rows = x2d.shape[0]
    tile = ROW_TILE
    assert n_rows % tile == 0 and tile % mlp_len == 0
    n_seq = n_rows // seq
    if seq >= tile:
        assert seq % tile == 0 and keep_rows == tile
        per_seq = seq // tile
        kv_rows, vn_rows = n_seq * tile, mlp_len
        kv_map = lambda i: (i // per_seq, 0)
    else:
        assert keep_rows == seq and mlp_len == seq
        kv_rows, vn_rows = n_rows, tile
        kv_map = lambda i: (i, 0)
    row = lambda width: pl.BlockSpec((tile, width), lambda i: (i, 0))
    n_in = w_in.shape[1]
    out_shape = (
        jax.ShapeDtypeStruct((n_rows, A_WIDTH), BF16),
        jax.ShapeDtypeStruct((n_rows, A_WIDTH), BF16),
        jax.ShapeDtypeStruct((n_rows, A_WIDTH), BF16),
        jax.ShapeDtypeStruct((kv_rows, A_WIDTH), F32),
        jax.ShapeDtypeStruct((kv_rows, A_WIDTH), F32),
        jax.ShapeDtypeStruct((kv_rows // tile * vn_rows, MLP_WIDTH), F32),
        jax.ShapeDtypeStruct((n_rows, D_MODEL), BF16),
        jax.ShapeDtypeStruct((n_rows, D_MODEL), BF16),
        jax.ShapeDtypeStruct((n_rows, MLP_WIDTH), BF16),
    )
    out_specs = (
        row(A_WIDTH), row(A_WIDTH), row(A_WIDTH),
        pl.BlockSpec((tile, A_WIDTH), kv_map), pl.BlockSpec((tile, A_WIDTH), kv_map),
        pl.BlockSpec((vn_rows, MLP_WIDTH), kv_map),
        row(D_MODEL), row(D_MODEL), row(MLP_WIDTH),
    )
    in_specs = [
        row(D_MODEL), _const_spec((1, D_MODEL)), _const_spec((1, D_MODEL)),
        _const_spec((D_MODEL, n_in)), _const_spec((1, MLP_WIDTH)), _const_spec((1, MLP_WIDTH)),
        _const_spec((MLP_GROUPS, mlp_len, mlp_len)), _const_spec((mlp_len, MLP_WIDTH)),
    ]
    return pl.pallas_call(
        functools.partial(_proj_kernel, mlp_len=mlp_len, vn_rows=vn_rows),
        grid=(n_rows // tile,), in_specs=in_specs, out_specs=out_specs, out_shape=out_shape,
        compiler_params=_params(1), name="proj",
    )(x2d, ln_g, ln_b, w_in, mlp_g, mlp_b, w_s, b_s)


def _attend_heads(q, keys, values, bias_ref, extra_mask, o_ref):
    n_q = q.shape[0]
    low_half = lax.broadcasted_iota(jnp.int32, (n_q, LANES), 1) < HEAD_DIM
    zero = jnp.zeros((), BF16)
    for p in range(HEADS // 2):
        c0 = p * LANES
        q2, k2, v2 = q[:, c0:c0 + LANES], keys[:, c0:c0 + LANES], values[:, c0:c0 + LANES]
        outs = []
        for h in range(2):
            qh = jnp.where(low_half if h == 0 else ~low_half, q2, zero)
            s = _dot_nt(qh, k2) + bias_ref[2 * p + h]
            if extra_mask is not None:
                s = s + extra_mask
            e = jnp.exp(s - jnp.max(s, axis=-1, keepdims=True))
            denom = jnp.sum(e, axis=-1, keepdims=True)
            outs.append(_dot(e.astype(BF16), v2) / denom)
        o_ref[:, c0:c0 + LANES] = jnp.where(low_half, outs[0], outs[1]).astype(o_ref.dtype)


def _attn_prompt_kernel(q_ref, k_ref, v_ref, bias_ref, o_ref, *, window):
    n_q = q_ref.shape[1]
    start = pl.multiple_of(pl.program_id(1) * n_q, n_q)
    keys = k_ref[0, pl.ds(start, window), :]
    values = v_ref[0, pl.ds(start, window), :]
    col = lax.broadcasted_iota(jnp.int32, (1, window), 1)
    pad_mask = jnp.where(col + start >= PAST_CHUNKS * CHUNK, 0.0, MASKED).astype(F32)
    _attend_heads(q_ref[0], keys, values, bias_ref, pad_mask, o_ref.at[0])


def _prompt_bias(rel_table):
    n_q, n_k = Q_CHUNKS * CHUNK, (Q_CHUNKS + PAST_CHUNKS) * CHUNK
    qi = jnp.arange(n_q)[:, None]
    kj = jnp.arange(n_k)[None, :]
    rel = (kj - PAST_CHUNKS * CHUNK) - qi
    lag = qi // CHUNK - (kj // CHUNK - PAST_CHUNKS)
    bias = rel_table[:, jnp.clip(rel, -MAX_REL, MAX_REL) + MAX_REL].astype(F32)
    return jnp.where(((lag >= 0) & (lag <= PAST_CHUNKS))[None], bias, MASKED)


def _attn_prompt(q, k, v, rel_table, *, batch, seq):
    n_q = Q_CHUNKS * CHUNK
    window = (Q_CHUNKS + PAST_CHUNKS) * CHUNK
    pad = PAST_CHUNKS * CHUNK
    q3 = q.reshape(batch, seq, A_WIDTH)
    kp = jnp.pad(k.reshape(batch, seq, A_WIDTH), ((0, 0), (pad, 0), (0, 0)))
    vp = jnp.pad(v.reshape(batch, seq, A_WIDTH), ((0, 0), (pad, 0), (0, 0)))
    bias = _prompt_bias(rel_table)
    qspec = pl.BlockSpec((1, n_q, A_WIDTH), lambda b, i: (b, i, 0))
    kvspec = pl.BlockSpec((1, seq + pad, A_WIDTH), lambda b, i: (b, 0, 0))
    out = pl.pallas_call(
        functools.partial(_attn_prompt_kernel, window=window),
        grid=(batch, seq // n_q),
        in_specs=[qspec, kvspec, kvspec, _const_spec(bias.shape)],
        out_specs=qspec,
        out_shape=jax.ShapeDtypeStruct((batch, seq, A_WIDTH), BF16),
        compiler_params=_params(2), name="attn_prompt",
    )(q3, kp, vp, bias)
    return out.reshape(batch * seq, A_WIDTH)


def _attn_sample_kernel(q_ref, k_ref, v_ref, ck_ref, cv_ref, bias_ref, o_ref, kk_ref, vv_ref):
    n_cache = ck_ref.shape[1]
    n_new = k_ref.shape[1]
    kk_ref[0:n_cache, :] = ck_ref[0].astype(BF16)
    kk_ref[n_cache:n_cache + n_new, :] = k_ref[0]
    vv_ref[0:n_cache, :] = cv_ref[0].astype(BF16)
    vv_ref[n_cache:n_cache + n_new, :] = v_ref[0]
    _attend_heads(q_ref[0], kk_ref[...], vv_ref[...], bias_ref, None, o_ref.at[0])


def _attn_sample(q, k, v, cache_k, cache_v, rel_table, *, batch, seq):
    n_cache = cache_k.shape[1]
    rel = jnp.arange(n_cache + seq)[None, :] - n_cache - jnp.arange(seq)[:, None]
    bias = rel_table[:, jnp.clip(rel, -MAX_REL, MAX_REL) + MAX_REL].astype(F32)
    new = pl.BlockSpec((1, seq, A_WIDTH), lambda b: (b, 0, 0))
    old = pl.BlockSpec((1, n_cache, A_WIDTH), lambda b: (b, 0, 0))
    out = pl.pallas_call(
        _attn_sample_kernel,
        grid=(batch,),
        in_specs=[new, new, new, old, old, _const_spec(bias.shape)],
        out_specs=new,
        out_shape=jax.ShapeDtypeStruct((batch, seq, A_WIDTH), BF16),
        scratch_shapes=[pltpu.VMEM((n_cache + seq, A_WIDTH), BF16),
                        pltpu.VMEM((n_cache + seq, A_WIDTH), BF16)],
        compiler_params=_params(1), name="attn_sample",
    )(q.reshape(batch, seq, A_WIDTH), k.reshape(batch, seq, A_WIDTH), v.reshape(batch, seq, A_WIDTH),
      cache_k.reshape(batch, n_cache, A_WIDTH), cache_v.reshape(batch, n_cache, A_WIDTH), bias)
    return out.reshape(batch * seq, A_WIDTH)


def _first_index(hit, iota, axis, limit):
    return jnp.min(jnp.where(hit, iota, limit), axis=axis, keepdims=True)


def _route(scores, sel):
    n_tok = scores.shape[1]
    neg = -jnp.inf
    grouped = sel.reshape(N_GROUPS, GROUP_SIZE, n_tok)
    member = lax.broadcasted_iota(jnp.int32, grouped.shape, 1)
    best = jnp.max(grouped, axis=1, keepdims=True)
    first = _first_index(grouped == best, member, 1, GROUP_SIZE)
    second = jnp.max(jnp.where(member == first, neg, grouped), axis=1, keepdims=True)
    group_score = best + second

    group_id = lax.broadcasted_iota(jnp.int32, group_score.shape, 0)
    keep = jnp.zeros(group_score.shape, F32)
    for _ in range(TOPK_GROUPS):
        top = jnp.max(group_score, axis=0, keepdims=True)
        hit = group_id == _first_index(group_score == top, group_id, 0, N_GROUPS)
        keep = jnp.where(hit, 1.0, keep)
        group_score = jnp.where(hit, neg, group_score)
    keep = jnp.broadcast_to(keep, grouped.shape).reshape(N_EXPERTS, n_tok)

    cand = jnp.where(keep > 0.0, sel, neg)
    expert_id = lax.broadcasted_iota(jnp.int32, cand.shape, 0)
    ids, weights = [], []
    for _ in range(TOP_K):
        top = jnp.max(cand, axis=0, keepdims=True)
        first = _first_index(cand == top, expert_id, 0, N_EXPERTS)
        hit = expert_id == first
        ids.append(first)
        weights.append(jnp.sum(jnp.where(hit, scores, 0.0), axis=0, keepdims=True))
        cand = jnp.where(hit, neg, cand)
    ids = jnp.concatenate(ids, axis=0)
    weights = jnp.concatenate(weights, axis=0)
    weights = weights / jnp.sum(weights, axis=0, keepdims=True) * ROUTED_SCALE
    return ids, weights


def _store_row_major_tiles(flat_ref, x):
    rows = x.shape[0]
    for j in range(D_TILES):
        flat_ref[pl.ds(j, rows, stride=D_TILES), :] = x[:, j * LANES:(j + 1) * LANES]


def _load_row_major_tiles(flat_ref, rows):
    return jnp.concatenate(
        [flat_ref[pl.ds(j, rows, stride=D_TILES), :] for j in range(D_TILES)], axis=1)


def _mix_kernel(x_ref, oa_ref, ob_ref, ga_ref, gb_ref, lng_ref, lnb_ref, wa_ref, wb_ref, wo_ref,
                g1_ref, b1_ref, rw_ref, rb_ref, x1_ref, x1t_ref, ids_ref, wts_ref, *, alpha):
    xn = _layer_norm(x_ref[...], lng_ref[...], lnb_ref[...])
    mix = (ga_ref[...].astype(F32) * _dot(oa_ref[...], wa_ref[...])
           + gb_ref[...].astype(F32) * _dot(ob_ref[...], wb_ref[...]))
    x1 = _layer_norm(alpha * xn + _dot(mix.astype(BF16), wo_ref[...]), g1_ref[...], b1_ref[...])
    x1_ref[...] = x1
    _store_row_major_tiles(x1t_ref, x1)
    scores = jax.nn.sigmoid(_dot_nt(rw_ref[...], x1, precision=lax.Precision.HIGHEST))
    ids, weights = _route(scores, scores + rb_ref[...])
    ids_ref[...] = ids
    wts_ref[...] = weights


def _mix(x2d, oa, ob, ga, gb, ln_g, ln_b, wa, wb, wo, g1, b1, rw_t, rb, *, alpha):
    n_rows = x2d.shape[0]
    tile = ROW_TILE
    row = lambda width: pl.BlockSpec((tile, width), lambda i: (i, 0))
    col = pl.BlockSpec((TOP_K, tile), lambda i: (0, i))
    return pl.pallas_call(
        functools.partial(_mix_kernel, alpha=alpha),
        grid=(n_rows // tile,),
        in_specs=[row(D_MODEL), row(A_WIDTH), row(MLP_WIDTH), row(D_MODEL), row(D_MODEL),
                  _const_spec((1, D_MODEL)), _const_spec((1, D_MODEL)),
                  _const_spec(wa.shape), _const_spec(wb.shape), _const_spec(wo.shape),
                  _const_spec((1, D_MODEL)), _const_spec((1, D_MODEL)),
                  _const_spec(rw_t.shape), _const_spec(rb.shape)],
        out_specs=(row(D_MODEL), pl.BlockSpec((tile * D_TILES, LANES), lambda i: (i, 0)), col, col),
        out_shape=(jax.ShapeDtypeStruct((n_rows, D_MODEL), F32),
                   jax.ShapeDtypeStruct((n_rows * D_TILES, LANES), F32),
                   jax.ShapeDtypeStruct((TOP_K, n_rows), jnp.int32),
                   jax.ShapeDtypeStruct((TOP_K, n_rows), F32)),
        compiler_params=_params(1), name="mix",
    )(x2d, oa, ob, ga, gb, ln_g, ln_b, wa, wb, wo, g1, b1, rw_t, rb)


def _moe_kernel(rows_ref, wts_ref, start_ref, x_ref, w1_ref, w3_ref, w2_ref, prev_ref, out_ref,
                w1b_ref, w3b_ref, w2b_ref, lhs_ref, y_ref):
    del prev_ref
    e = pl.program_id(0)

    @pl.when(e == 0)
    def _():
        out_ref[...] = jnp.zeros(out_ref.shape, out_ref.dtype)

    w1b_ref[...] = w1_ref[0].astype(BF16)
    w3b_ref[...] = w3_ref[0].astype(BF16)
    w2b_ref[...] = w2_ref[0].astype(BF16)
    first = start_ref[e]
    count = start_ref[e + 1] - first

    def tile_at(ref, row8):
        return ref.at[pl.ds(pl.multiple_of(row8, SUBLANES), SUBLANES), :]

    def block(b, carry):
        base = first + b * MOE_ROWS
        for m in range(MOE_ROWS):
            lhs_ref[pl.ds(m, D_TILES, stride=LHS_STRIDE), :] = tile_at(x_ref, rows_ref[base + m])[...]
        xb = jnp.concatenate(
            [lhs_ref[pl.ds(j * LHS_STRIDE, MOE_ROWS), :].astype(BF16) for j in range(D_TILES)], axis=1)
        act = jax.nn.silu(_dot(xb, w1b_ref[...])) * _dot(xb, w3b_ref[...])
        _store_row_major_tiles(y_ref, _dot(act.astype(BF16), w2b_ref[...]))

        valid = jnp.minimum(count - b * MOE_ROWS, MOE_ROWS)

        def contribution(m):
            dst = tile_at(out_ref, rows_ref[base + m])
            return dst, dst[...] + wts_ref[base + m] * tile_at(y_ref, m * SUBLANES)[...]

        def four(i, c):
            updates = [contribution(4 * i + u) for u in range(4)]
            for dst, val in updates:
                dst[...] = val
            return c

        lax.fori_loop(0, valid // 4, four, 0)

        def single(m, c):
            dst, val = contribution(m)
            dst[...] = val
            return c

        lax.fori_loop(valid // 4 * 4, valid, single, 0)
        return carry

    lax.fori_loop(0, pl.cdiv(count, MOE_ROWS), block, 0)


def _moe_kernel_first(rows_ref, wts_ref, start_ref, x_ref, w1_ref, w3_ref, w2_ref, out_ref, *scratch):
    _moe_kernel(rows_ref, wts_ref, start_ref, x_ref, w1_ref, w3_ref, w2_ref, None, out_ref, *scratch)


def _moe(ids, weights, x_tiles, w1, w3, w2):
    n_tok = ids.shape[1]
    g_tok = min(MOE_GROUP_TOKENS, n_tok)
    assert g_tok == MOE_GROUP_TOKENS or n_tok < MOE_GROUP_TOKENS
    n_groups = n_tok // g_tok
    per_group = g_tok * TOP_K
    tok = lax.broadcasted_iota(jnp.int32, ids.shape, 1)
    key = ((tok // g_tok) * N_EXPERTS + ids) * g_tok + tok % g_tok
    key, wts = lax.sort((key.reshape(-1), weights.reshape(-1)), num_keys=1)
    rows8 = (key % g_tok) * D_TILES
    bounds = jnp.arange(n_groups * N_EXPERTS + 1, dtype=jnp.int32) * g_tok
    start = jnp.searchsorted(key, bounds).astype(jnp.int32)
    pad_i = jnp.zeros((MOE_ROWS,), jnp.int32)
    pad_f = jnp.zeros((MOE_ROWS,), F32)
    out = None
    for g in range(n_groups):
        lo = g * per_group
        out = _moe_group_call(
            g, n_groups, g_tok,
            jnp.concatenate([rows8[lo:lo + per_group], pad_i]),
            jnp.concatenate([wts[lo:lo + per_group], pad_f]),
            start[g * N_EXPERTS:(g + 1) * N_EXPERTS + 1] - lo,
            x_tiles, w1, w3, w2, out)
    return out


def _moe_group_call(group, n_groups, g_tok, rows8, wts, start, x_tiles, w1, w3, w2, prev):
    g_rows = g_tok * D_TILES
    total_rows = x_tiles.shape[0]
    assert total_rows == n_groups * g_rows
    gspec = pl.BlockSpec((g_rows, LANES), lambda e, *_: (group, 0))
    in_specs = [
        gspec,
        pl.BlockSpec((1, D_MODEL, EXPERT_DIM), lambda e, *_: (e, 0, 0)),
        pl.BlockSpec((1, D_MODEL, EXPERT_DIM), lambda e, *_: (e, 0, 0)),
        pl.BlockSpec((1, EXPERT_DIM, D_MODEL), lambda e, *_: (e, 0, 0)),
    ]
    args = [x_tiles, w1, w3, w2]
    aliases = {}
    kernel = _moe_kernel_first
    if prev is not None:
        in_specs.append(pl.BlockSpec(memory_space=pl.ANY))
        args.append(prev)
        aliases = {3 + len(args) - 1: 0}
        kernel = _moe_kernel
    return pl.pallas_call(
        kernel,
        grid_spec=pltpu.PrefetchScalarGridSpec(
            num_scalar_prefetch=3, grid=(N_EXPERTS,), in_specs=in_specs, out_specs=gspec,
            scratch_shapes=[pltpu.VMEM((D_MODEL, EXPERT_DIM), BF16),
                            pltpu.VMEM((D_MODEL, EXPERT_DIM), BF16),
                            pltpu.VMEM((EXPERT_DIM, D_MODEL), BF16),
                            pltpu.VMEM((D_TILES * LHS_STRIDE, LANES), F32),
                            pltpu.VMEM((MOE_ROWS * D_TILES, LANES), F32)]),
        out_shape=jax.ShapeDtypeStruct((total_rows, LANES), F32),
        input_output_aliases=aliases,
        compiler_params=_params(1), name="moe",
    )(rows8, wts, start, *args)


def _final_kernel(x1_ref, routed_ref, p_ref, s1_ref, s3_ref, s2_ref, g2_ref, b2_ref, wg_ref, wp_ref,
                  o_ref, *, alpha):
    x1 = x1_ref[...]
    xb = x1.astype(BF16)
    act = jax.nn.silu(_dot(xb, s1_ref[...])) * _dot(xb, s3_ref[...])
    ff = _load_row_major_tiles(routed_ref, x1.shape[0]) + _dot(act.astype(BF16), s2_ref[...])
    x2 = _layer_norm(alpha * x1 + ff, g2_ref[...], b2_ref[...])
    gate = jax.nn.sigmoid(_dot(x2.astype(BF16), wg_ref[...]))
    o_ref[...] = x2 + gate * _dot(p_ref[...].astype(BF16), wp_ref[...])


def _final(x1, routed, p2d, s1, s3, s2, g2, b2, wg, wp, *, alpha):
    n_rows = x1.shape[0]
    tile = ROW_TILE
    row = lambda width: pl.BlockSpec((tile, width), lambda i: (i, 0))
    return pl.pallas_call(
        functools.partial(_final_kernel, alpha=alpha),
        grid=(n_rows // tile,),
        in_specs=[row(D_MODEL), pl.BlockSpec((tile * D_TILES, LANES), lambda i: (i, 0)),
                  row(p2d.shape[1]),
                  _const_spec(s1.shape), _const_spec(s3.shape), _const_spec(s2.shape),
                  _const_spec((1, D_MODEL)), _const_spec((1, D_MODEL)),
                  _const_spec(wg.shape), _const_spec(wp.shape)],
        out_specs=row(D_MODEL),
        out_shape=jax.ShapeDtypeStruct((n_rows, D_MODEL), F32),
        compiler_params=_params(1), name="final",
    )(x1, routed, p2d, s1, s3, s2, g2, b2, wg, wp)


def _trunk_layer(x, p, cache_k, cache_v, lw, *, alpha, first_layer_ln):
    batch, seq, d = x.shape
    n_rows = batch * seq
    x2d = x.reshape(n_rows, d)
    prompt = cache_k is None
    mlp_len = min(seq, MLP_CHUNK)
    keep = min(PAST_CHUNKS * CHUNK, seq) if prompt else seq
    ln_g, ln_b = first_layer_ln

    w_s = (lw["mlp_w_s"][:, :mlp_len, :mlp_len]
           * jnp.tril(jnp.ones((mlp_len, mlp_len), F32))).astype(BF16)
    b_s = jnp.repeat(lw["mlp_b_s"][:, :mlp_len].T, MLP_WIDTH // MLP_GROUPS, axis=1)
    q, k, v, kf, vf, vn, ga, gb, ob = _proj(
        x2d, ln_g, ln_b, lw["w_in"].astype(BF16), lw["mlp_ln_g"][None], lw["mlp_ln_b"][None],
        w_s, b_s, seq=seq, mlp_len=mlp_len, keep_rows=keep)
    if prompt:
        oa = _attn_prompt(q, k, v, lw["attn_rel_bias"], batch=batch, seq=seq)
    else:
        oa = _attn_sample(q, k, v, cache_k, cache_v, lw["attn_rel_bias"], batch=batch, seq=seq)
    x1, x1_tiles, ids, weights = _mix(
        x2d, oa, ob, ga, gb, ln_g, ln_b, lw["w_branch_a"].astype(BF16), lw["w_branch_b"].astype(BF16),
        lw["w_out"].astype(BF16), lw["ln1_g"][None], lw["ln1_b"][None],
        lw["router_w"].T, lw["router_bias"][:, None], alpha=alpha)
    routed = _moe(ids, weights, x1_tiles, lw["exp_w1"], lw["exp_w3"], lw["exp_w2"])
    y = _final(x1, routed, p.reshape(n_rows, -1), lw["shared_w1"].astype(BF16),
               lw["shared_w3"].astype(BF16), lw["shared_w2"].astype(BF16),
               lw["ln2_g"][None], lw["ln2_b"][None], lw["ple_w_gate"].astype(BF16),
               lw["ple_w_proj"].astype(BF16), alpha=alpha)
    new_k = kf.reshape(batch, keep, HEADS, HEAD_DIM)
    new_v = vf.reshape(batch, keep, HEADS, HEAD_DIM)
    new_mlp_v = vn.reshape(batch, mlp_len, MLP_WIDTH)
    return y.reshape(batch, seq, d), new_k, new_v, new_mlp_v


def kernel(x_prompt, x_sample, cache_attn_k, cache_attn_v, p_prompt, p_sample, ln_in_g, ln_in_b, w_in, attn_rel_bias, mlp_ln_g, mlp_ln_b, mlp_w_s, mlp_b_s, w_branch_a, w_branch_b, w_out, ln1_g, ln1_b, router_w, router_bias, exp_w1, exp_w3, exp_w2, shared_w1, shared_w3, shared_w2, ln2_g, ln2_b, ple_w_gate, ple_w_proj):
    depth = w_in.shape[0]
    assert depth == 1, "the fused proj/mix kernels apply LayerNorm_in themselves: single layer only"
    alpha = (2 * depth) ** 0.25
    stacked = dict(w_in=w_in, attn_rel_bias=attn_rel_bias, mlp_ln_g=mlp_ln_g, mlp_ln_b=mlp_ln_b,
                   mlp_w_s=mlp_w_s, mlp_b_s=mlp_b_s, w_branch_a=w_branch_a, w_branch_b=w_branch_b,
                   w_out=w_out, ln1_g=ln1_g, ln1_b=ln1_b, router_w=router_w, router_bias=router_bias,
                   exp_w1=exp_w1, exp_w3=exp_w3, exp_w2=exp_w2, shared_w1=shared_w1,
                   shared_w3=shared_w3, shared_w2=shared_w2, ln2_g=ln2_g, ln2_b=ln2_b,
                   ple_w_gate=ple_w_gate, ple_w_proj=ple_w_proj)
    lw = {name: value[0] for name, value in stacked.items()}
    ln_in = (ln_in_g[None], ln_in_b[None])
    yp, kp, vp, mp = _trunk_layer(x_prompt, p_prompt[0], None, None, lw, alpha=alpha, first_layer_ln=ln_in)
    ys, ks, vs, ms = _trunk_layer(x_sample, p_sample[0], cache_attn_k[0], cache_attn_v[0], lw,
                                  alpha=alpha, first_layer_ln=ln_in)
    return (yp, ys, kp[None], vp[None], ks[None], vs[None], mp[None], ms[None])
```

```python
import functools

import jax
import jax.numpy as jnp
import numpy as np
from jax import lax
from jax.experimental import pallas as pl
from jax.experimental.pallas import tpu as pltpu

F32 = jnp.float32
BF16 = jnp.bfloat16

D_MODEL = 1024
CHUNK = 64
HEADS = 8
HEAD_DIM = 64
A_WIDTH = HEADS * HEAD_DIM
PAST_CHUNKS = 8
MAX_REL = 256
MLP_CHUNK = 128
MLP_GROUPS = 8
MLP_WIDTH = 512
N_EXPERTS = 64
TOP_K = 8
N_GROUPS = 8
TOPK_GROUPS = 4
GROUP_SIZE = N_EXPERTS // N_GROUPS
EXPERT_DIM = 256
ROUTED_SCALE = 2.5
LN_EPS = 1e-5
MASKED = -1e30

LANES = 128
SUBLANES = 8
VMEM_LIMIT_BYTES = 56 * 1024 * 1024

ROW_TILE = 512
Q_CHUNKS = 4
MOE_ROWS = 128
MOE_GROUP_TOKENS = 2048
SCATTER_BATCH = 16
D_TILES = D_MODEL // LANES


def _layer_norm(x, g, b):
    xc = x - jnp.mean(x, axis=-1, keepdims=True)
    var = jnp.mean(xc * xc, axis=-1, keepdims=True)
    return xc * lax.rsqrt(var + LN_EPS) * g + b


def _dot(a, b):
    return jnp.dot(a, b, preferred_element_type=F32)


def _dot_nt(a, b, precision=None):
    return lax.dot_general(a, b, (((1,), (1,)), ((), ())),
                           preferred_element_type=F32, precision=precision)


def _const_spec(shape):
    zeros = (0,) * len(shape)
    return pl.BlockSpec(shape, lambda *_: zeros)


def _params(n_axes):
    return pltpu.CompilerParams(dimension_semantics=("arbitrary",) * n_axes,
                                vmem_limit_bytes=VMEM_LIMIT_BYTES)


def _proj_kernel(x_ref, lng_ref, lnb_ref, w_ref, mg_ref, mb_ref, ws_ref, bs_ref,
                 q_ref, k_ref, v_ref, kf_ref, vf_ref, vn_ref, ga_ref, gb_ref, ob_ref,
                 *, mlp_len, vn_rows):
    rows = x_ref.shape[0]
    xn = _layer_norm(x_ref[...], lng_ref[...], lnb_ref[...]).astype(BF16)

    def section(lo, width):
        return _dot(xn, w_ref[:, lo:lo + width])

    q = section(0, A_WIDTH)
    q_ref[...] = (q * (HEAD_DIM ** -0.5)).astype(BF16)
    k = section(A_WIDTH, A_WIDTH)
    k_ref[...] = k.astype(BF16)
    kf_ref[...] = k
    v = section(2 * A_WIDTH, A_WIDTH)
    v_ref[...] = v.astype(BF16)
    vf_ref[...] = v
    base = 3 * A_WIDTH
    un = jax.nn.gelu(section(base, MLP_WIDTH))
    vn = _layer_norm(jax.nn.gelu(section(base + MLP_WIDTH, MLP_WIDTH)), mg_ref[...], mb_ref[...])
    vn_ref[...] = vn[rows - vn_rows:, :]
    base += 2 * MLP_WIDTH
    ga_ref[...] = jax.nn.sigmoid(section(base, D_MODEL)).astype(BF16)
    gb_ref[...] = jax.nn.sigmoid(section(base + D_MODEL, D_MODEL)).astype(BF16)

    vnb = vn.astype(BF16)
    low_half = lax.broadcasted_iota(jnp.int32, (mlp_len, LANES), 1) < (MLP_WIDTH // MLP_GROUPS)
    for c in range(rows // mlp_len):
        r0 = c * mlp_len
        for p in range(MLP_GROUPS // 2):
            c0 = p * LANES
            slab = vnb[r0:r0 + mlp_len, c0:c0 + LANES]
            mixed = jnp.where(low_half, _dot(ws_ref[2 * p], slab), _dot(ws_ref[2 * p + 1], slab))
            gated = un[r0:r0 + mlp_len, c0:c0 + LANES] * (mixed + bs_ref[:, c0:c0 + LANES])
            ob_ref[r0:r0 + mlp_len, c0:c0 + LANES] = gated.astype(BF16)


def _proj(x2d, ln_g, ln_b, w_in, mlp_g, mlp_b, w_s, b_s, *, seq, mlp_len, keep_rows):
    n_rows = x2d.shape[0]
    tile = ROW_TILE
    assert n_rows % tile == 0 and tile % mlp_len == 0
    n_seq = n_rows // seq
    if seq >= tile:
        assert seq % tile == 0 and keep_rows == tile
        per_seq = seq // tile
        kv_rows, vn_rows = n_seq * tile, mlp_len
        kv_map = lambda i: (i // per_seq, 0)
    else:
        assert keep_rows == seq and mlp_len == seq
        kv_rows, vn_rows = n_rows, tile
        kv_map = lambda i: (i, 0)
    row = lambda width: pl.BlockSpec((tile, width), lambda i: (i, 0))
    n_in = w_in.shape[1]
    out_shape = (
        jax.ShapeDtypeStruct((n_rows, A_WIDTH), BF16),
        jax.ShapeDtypeStruct((n_rows, A_WIDTH), BF16),
        jax.ShapeDtypeStruct((n_rows, A_WIDTH), BF16),
        jax.ShapeDtypeStruct((kv_rows, A_WIDTH), F32),
        jax.ShapeDtypeStruct((kv_rows, A_WIDTH), F32),
        jax.ShapeDtypeStruct((kv_rows // tile * vn_rows, MLP_WIDTH), F32),
        jax.ShapeDtypeStruct((n_rows, D_MODEL), BF16),
        jax.ShapeDtypeStruct((n_rows, D_MODEL), BF16),
        jax.ShapeDtypeStruct((n_rows, MLP_WIDTH), BF16),
    )
    out_specs = (
        row(A_WIDTH), row(A_WIDTH), row(A_WIDTH),
        pl.BlockSpec((tile, A_WIDTH), kv_map), pl.BlockSpec((tile, A_WIDTH), kv_map),
        pl.BlockSpec((vn_rows, MLP_WIDTH), kv_map),
        row(D_MODEL), row(D_MODEL), row(MLP_WIDTH),
    )
    in_specs = [
        row(D_MODEL), _const_spec((1, D_MODEL)), _const_spec((1, D_MODEL)),
        _const_spec((D_MODEL, n_in)), _const_spec((1, MLP_WIDTH)), _const_spec((1, MLP_WIDTH)),
        _const_spec((MLP_GROUPS, mlp_len, mlp_len)), _const_spec((mlp_len, MLP_WIDTH)),
    ]
    return pl.pallas_call(
        functools.partial(_proj_kernel, mlp_len=mlp_len, vn_rows=vn_rows),
        grid=(n_rows // tile,), in_specs=in_specs, out_specs=out_specs, out_shape=out_shape,
        compiler_params=_params(1), name="proj",
    )(x2d, ln_g, ln_b, w_in, mlp_g, mlp_b, w_s, b_s)


def _attend_heads(q, keys, values, bias_ref, extra_mask, o_ref):
    n_q = q.shape[0]
    low_half = lax.broadcasted_iota(jnp.int32, (n_q, LANES), 1) < HEAD_DIM
    zero = jnp.zeros((), BF16)
    for p in range(HEADS // 2):
        c0 = p * LANES
        q2, k2, v2 = q[:, c0:c0 + LANES], keys[:, c0:c0 + LANES], values[:, c0:c0 + LANES]
        outs = []
        for h in range(2):
            qh = jnp.where(low_half if h == 0 else ~low_half, q2, zero)
            s = _dot_nt(qh, k2) + bias_ref[2 * p + h]
            if extra_mask is not None:
                s = s + extra_mask
            e = jnp.exp(s - jnp.max(s, axis=-1, keepdims=True))
            denom = jnp.sum(e, axis=-1, keepdims=True)
            outs.append(_dot(e.astype(BF16), v2) / denom)
        o_ref[:, c0:c0 + LANES] = jnp.where(low_half, outs[0], outs[1]).astype(o_ref.dtype)


def _attn_prompt_kernel(q_ref, k_ref, v_ref, bias_ref, o_ref, *, window):
    n_q = q_ref.shape[1]
    start = pl.multiple_of(pl.program_id(1) * n_q, n_q)
    keys = k_ref[0, pl.ds(start, window), :]
    values = v_ref[0, pl.ds(start, window), :]
    col = lax.broadcasted_iota(jnp.int32, (1, window), 1)
    pad_mask = jnp.where(col + start >= PAST_CHUNKS * CHUNK, 0.0, MASKED).astype(F32)
    _attend_heads(q_ref[0], keys, values, bias_ref, pad_mask, o_ref.at[0])


def _relative_bias(rel_table, n_q, n_k, key_offset):
    period = pl.next_power_of_2(n_q + n_k)
    m = np.arange(period)
    diag = np.where(m < n_k, m, m - period)
    idx = np.clip(diag - key_offset, -MAX_REL, MAX_REL) + MAX_REL
    values = rel_table[:, idx].astype(F32)
    rows = jnp.tile(values, (1, n_q))[:, :n_q * (period - 1)]
    return rows.reshape(rel_table.shape[0], n_q, period - 1)[:, :, :n_k]


def _prompt_bias(rel_table):
    n_q, n_k = Q_CHUNKS * CHUNK, (Q_CHUNKS + PAST_CHUNKS) * CHUNK
    lag = np.arange(n_q)[:, None] // CHUNK - (np.arange(n_k)[None, :] // CHUNK - PAST_CHUNKS)
    in_band = (lag >= 0) & (lag <= PAST_CHUNKS)
    bias = _relative_bias(rel_table, n_q, n_k, PAST_CHUNKS * CHUNK)
    return jnp.where(in_band[None], bias, MASKED)


def _attn_prompt(q, k, v, rel_table, *, batch, seq):
    n_q = Q_CHUNKS * CHUNK
    window = (Q_CHUNKS + PAST_CHUNKS) * CHUNK
    pad = PAST_CHUNKS * CHUNK
    q3 = q.reshape(batch, seq, A_WIDTH)
    kp = jnp.pad(k.reshape(batch, seq, A_WIDTH), ((0, 0), (pad, 0), (0, 0)))
    vp = jnp.pad(v.reshape(batch, seq, A_WIDTH), ((0, 0), (pad, 0), (0, 0)))
    bias = _prompt_bias(rel_table)
    qspec = pl.BlockSpec((1, n_q, A_WIDTH), lambda b, i: (b, i, 0))
    kvspec = pl.BlockSpec((1, seq + pad, A_WIDTH), lambda b, i: (b, 0, 0))
    out = pl.pallas_call(
        functools.partial(_attn_prompt_kernel, window=window),
        grid=(batch, seq // n_q),
        in_specs=[qspec, kvspec, kvspec, _const_spec(bias.shape)],
        out_specs=qspec,
        out_shape=jax.ShapeDtypeStruct((batch, seq, A_WIDTH), BF16),
        compiler_params=_params(2), name="attn_prompt",
    )(q3, kp, vp, bias)
    return out.reshape(batch * seq, A_WIDTH)


def _attn_sample_kernel(q_ref, k_ref, v_ref, ck_ref, cv_ref, bias_ref, o_ref, kk_ref, vv_ref):
    n_cache = ck_ref.shape[1]
    n_new = k_ref.shape[1]
    kk_ref[0:n_cache, :] = ck_ref[0].astype(BF16)
    kk_ref[n_cache:n_cache + n_new, :] = k_ref[0]
    vv_ref[0:n_cache, :] = cv_ref[0].astype(BF16)
    vv_ref[n_cache:n_cache + n_new, :] = v_ref[0]
    _attend_heads(q_ref[0], kk_ref[...], vv_ref[...], bias_ref, None, o_ref.at[0])


def _attn_sample(q, k, v, cache_k, cache_v, rel_table, *, batch, seq):
    n_cache = cache_k.shape[1]
    bias = _relative_bias(rel_table, seq, n_cache + seq, n_cache)
    new = pl.BlockSpec((1, seq, A_WIDTH), lambda b: (b, 0, 0))
    old = pl.BlockSpec((1, n_cache, A_WIDTH), lambda b: (b, 0, 0))
    out = pl.pallas_call(
        _attn_sample_kernel,
        grid=(batch,),
        in_specs=[new, new, new, old, old, _const_spec(bias.shape)],
        out_specs=new,
        out_shape=jax.ShapeDtypeStruct((batch, seq, A_WIDTH), BF16),
        scratch_shapes=[pltpu.VMEM((n_cache + seq, A_WIDTH), BF16),
                        pltpu.VMEM((n_cache + seq, A_WIDTH), BF16)],
        compiler_params=_params(1), name="attn_sample",
    )(q.reshape(batch, seq, A_WIDTH), k.reshape(batch, seq, A_WIDTH), v.reshape(batch, seq, A_WIDTH),
      cache_k.reshape(batch, n_cache, A_WIDTH), cache_v.reshape(batch, n_cache, A_WIDTH), bias)
    return out.reshape(batch * seq, A_WIDTH)


def _first_index(hit, iota, axis, limit):
    return jnp.min(jnp.where(hit, iota, limit), axis=axis, keepdims=True)


def _route(scores, sel):
    n_tok = scores.shape[1]
    neg = -jnp.inf
    grouped = sel.reshape(N_GROUPS, GROUP_SIZE, n_tok)
    member = lax.broadcasted_iota(jnp.int32, grouped.shape, 1)
    best = jnp.max(grouped, axis=1, keepdims=True)
    first = _first_index(grouped == best, member, 1, GROUP_SIZE)
    second = jnp.max(jnp.where(member == first, neg, grouped), axis=1, keepdims=True)
    group_score = best + second

    group_id = lax.broadcasted_iota(jnp.int32, group_score.shape, 0)
    keep = jnp.zeros(group_score.shape, F32)
    for _ in range(TOPK_GROUPS):
        top = jnp.max(group_score, axis=0, keepdims=True)
        hit = group_id == _first_index(group_score == top, group_id, 0, N_GROUPS)
        keep = jnp.where(hit, 1.0, keep)
        group_score = jnp.where(hit, neg, group_score)
    keep = jnp.broadcast_to(keep, grouped.shape).reshape(N_EXPERTS, n_tok)

    cand = jnp.where(keep > 0.0, sel, neg)
    expert_id = lax.broadcasted_iota(jnp.int32, cand.shape, 0)
    ids, weights = [], []
    for _ in range(TOP_K):
        top = jnp.max(cand, axis=0, keepdims=True)
        first = _first_index(cand == top, expert_id, 0, N_EXPERTS)
        hit = expert_id == first
        ids.append(first)
        weights.append(jnp.sum(jnp.where(hit, scores, 0.0), axis=0, keepdims=True))
        cand = jnp.where(hit, neg, cand)
    ids = jnp.concatenate(ids, axis=0)
    weights = jnp.concatenate(weights, axis=0)
    weights = weights / jnp.sum(weights, axis=0, keepdims=True) * ROUTED_SCALE
    return ids, weights


def _store_row_major_tiles(flat_ref, x):
    rows = x.shape[0]
    for j in range(D_TILES):
        flat_ref[pl.ds(j, rows, stride=D_TILES), :] = x[:, j * LANES:(j + 1) * LANES]


def _load_row_major_tiles(flat_ref, rows):
    return jnp.concatenate(
        [flat_ref[pl.ds(j, rows, stride=D_TILES), :] for j in range(D_TILES)], axis=1)


def _mix_kernel(x_ref, oa_ref, ob_ref, ga_ref, gb_ref, lng_ref, lnb_ref, wa_ref, wb_ref, wo_ref,
                g1_ref, b1_ref, rw_ref, rb_ref, x1_ref, x1t_ref, ids_ref, wts_ref, *, alpha):
    xn = _layer_norm(x_ref[...], lng_ref[...], lnb_ref[...])
    mix = (ga_ref[...].astype(F32) * _dot(oa_ref[...], wa_ref[...])
           + gb_ref[...].astype(F32) * _dot(ob_ref[...], wb_ref[...]))
    x1 = _layer_norm(alpha * xn + _dot(mix.astype(BF16), wo_ref[...]), g1_ref[...], b1_ref[...])
    x1_ref[...] = x1
    _store_row_major_tiles(x1t_ref, x1)
    scores = jax.nn.sigmoid(_dot_nt(rw_ref[...], x1, precision=lax.Precision.HIGHEST))
    ids, weights = _route(scores, scores + rb_ref[...])
    ids_ref[...] = ids
    wts_ref[...] = weights


def _mix(x2d, oa, ob, ga, gb, ln_g, ln_b, wa, wb, wo, g1, b1, rw_t, rb, *, alpha):
    n_rows = x2d.shape[0]
    tile = ROW_TILE
    row = lambda width: pl.BlockSpec((tile, width), lambda i: (i, 0))
    col = pl.BlockSpec((TOP_K, tile), lambda i: (0, i))
    return pl.pallas_call(
        functools.partial(_mix_kernel, alpha=alpha),
        grid=(n_rows // tile,),
        in_specs=[row(D_MODEL), row(A_WIDTH), row(MLP_WIDTH), row(D_MODEL), row(D_MODEL),
                  _const_spec((1, D_MODEL)), _const_spec((1, D_MODEL)),
                  _const_spec(wa.shape), _const_spec(wb.shape), _const_spec(wo.shape),
                  _const_spec((1, D_MODEL)), _const_spec((1, D_MODEL)),
                  _const_spec(rw_t.shape), _const_spec(rb.shape)],
        out_specs=(row(D_MODEL), pl.BlockSpec((tile * D_TILES, LANES), lambda i: (i, 0)), col, col),
        out_shape=(jax.ShapeDtypeStruct((n_rows, D_MODEL), F32),
                   jax.ShapeDtypeStruct((n_rows * D_TILES, LANES), F32),
                   jax.ShapeDtypeStruct((TOP_K, n_rows), jnp.int32),
                   jax.ShapeDtypeStruct((TOP_K, n_rows), F32)),
        compiler_params=_params(1), name="mix",
    )(x2d, oa, ob, ga, gb, ln_g, ln_b, wa, wb, wo, g1, b1, rw_t, rb)


def _moe_kernel(rows_ref, start_ref, x_ref, wts_ref, w1_ref, w3_ref, w2_ref, *rest):
    out_ref, w1b_ref, w3b_ref, w2b_ref, gat_ref, y_ref = rest[-6:]
    acc = out_ref.at[0]
    e = pl.program_id(0)

    @pl.when(e == 0)
    def _():
        out_ref[...] = jnp.zeros(out_ref.shape, out_ref.dtype)

    w1b_ref[...] = w1_ref[0].astype(BF16)
    w3b_ref[...] = w3_ref[0].astype(BF16)
    w2b_ref[...] = w2_ref[0].astype(BF16)
    first = start_ref[e]
    count = start_ref[e + 1] - first

    def tile_at(ref, row8):
        return ref.at[pl.ds(pl.multiple_of(row8, SUBLANES), SUBLANES), :]

    row_id = lax.broadcasted_iota(jnp.int32, (MOE_ROWS, 2 * LANES), 0)
    lane_id = lax.broadcasted_iota(jnp.int32, (MOE_ROWS, 2 * LANES), 1)

    def block(b, carry):
        nominal = b * MOE_ROWS
        begin = jnp.minimum(nominal, count - MOE_ROWS)
        redo = nominal - begin
        base = first + begin + MOE_ROWS
        for m in range(MOE_ROWS):
            gat_ref[m * SUBLANES:(m + 1) * SUBLANES, :] = tile_at(x_ref, rows_ref[base + m])[...]
        xb = _load_row_major_tiles(gat_ref, MOE_ROWS).astype(BF16)
        act = jax.nn.silu(_dot(xb, w1b_ref[...])) * _dot(xb, w3b_ref[...])
        y = _dot(act.astype(BF16), w2b_ref[...])

        w_row = base // LANES
        pair = jnp.concatenate([wts_ref[w_row], wts_ref[w_row + 1]], axis=1)
        picked = jnp.where((lane_id == row_id + base % LANES) & (row_id >= redo),
                           jnp.broadcast_to(pair, lane_id.shape), 0.0)
        _store_row_major_tiles(y_ref, y * jnp.sum(picked, axis=1, keepdims=True))

        for m0 in range(0, MOE_ROWS, SCATTER_BATCH):
            updates = []
            for m in range(m0, m0 + SCATTER_BATCH):
                dst = tile_at(acc, rows_ref[base + m])
                updates.append((dst, dst[...] + y_ref[m * SUBLANES:(m + 1) * SUBLANES, :]))
            for dst, val in updates:
                dst[...] = val
        return carry

    lax.fori_loop(0, pl.cdiv(count, MOE_ROWS), block, 0)


def _moe(ids, weights, x_tiles, w1, w3, w2):
    n_tok = ids.shape[1]
    g_tok = min(MOE_GROUP_TOKENS, n_tok)
    n_groups = n_tok // g_tok
    assert n_groups * g_tok == n_tok
    per_group = g_tok * TOP_K
    tok = lax.broadcasted_iota(jnp.int32, ids.shape, 1)
    key = ((tok // g_tok) * N_EXPERTS + ids) * g_tok + tok % g_tok
    key, wts = lax.sort((key.reshape(-1), weights.reshape(-1)), num_keys=1)
    rows8 = ((key % g_tok) * D_TILES).reshape(n_groups, per_group)
    wts = wts.reshape(n_groups, per_group)
    experts = jnp.arange(N_EXPERTS, dtype=jnp.int32)[None, None, :, None]
    counts = jnp.sum(ids.reshape(TOP_K, n_groups, 1, g_tok) == experts, axis=(0, 3), dtype=jnp.int32)
    start = jnp.concatenate([jnp.zeros((n_groups, 1), jnp.int32), jnp.cumsum(counts, axis=1)], axis=1)
    pad = ((0, 0), (MOE_ROWS, 2 * LANES))
    rows8 = jnp.pad(rows8, pad)
    wts = jnp.pad(wts, pad).reshape(n_groups, -1, 1, LANES)
    out = None
    for g in range(n_groups):
        out = _moe_group_call(g, n_groups, g_tok, rows8[g], start[g], x_tiles, wts[g], w1, w3, w2, out)
    return out


def _moe_group_call(group, n_groups, g_tok, rows8, start, x_tiles, wts, w1, w3, w2, prev):
    g_rows = g_tok * D_TILES
    assert x_tiles.shape[0] == n_groups * g_rows
    out_block = (1, g_rows, LANES)
    in_specs = [
        pl.BlockSpec((g_rows, LANES), lambda e, *_: (group, 0)),
        pl.BlockSpec(wts.shape, lambda e, *_: (0, 0, 0)),
        pl.BlockSpec((1, D_MODEL, EXPERT_DIM), lambda e, *_: (e, 0, 0)),
        pl.BlockSpec((1, D_MODEL, EXPERT_DIM), lambda e, *_: (e, 0, 0)),
        pl.BlockSpec((1, EXPERT_DIM, D_MODEL), lambda e, *_: (e, 0, 0)),
    ]
    args = [rows8, start, x_tiles, wts, w1, w3, w2]
    aliases = {}
    if prev is not None:
        in_specs.append(pl.BlockSpec(memory_space=pl.ANY))
        aliases = {len(args): 0}
        args.append(prev)
    return pl.pallas_call(
        _moe_kernel,
        grid_spec=pltpu.PrefetchScalarGridSpec(
            num_scalar_prefetch=2, grid=(N_EXPERTS,), in_specs=in_specs,
            out_specs=pl.BlockSpec(out_block, lambda e, *_: (group, 0, 0)),
            scratch_shapes=[pltpu.VMEM((D_MODEL, EXPERT_DIM), BF16),
                            pltpu.VMEM((D_MODEL, EXPERT_DIM), BF16),
                            pltpu.VMEM((EXPERT_DIM, D_MODEL), BF16),
                            pltpu.VMEM((MOE_ROWS * D_TILES, LANES), F32),
                            pltpu.VMEM((MOE_ROWS * D_TILES, LANES), F32)]),
        out_shape=jax.ShapeDtypeStruct((n_groups,) + out_block[1:], F32),
        input_output_aliases=aliases,
        compiler_params=_params(1), name="moe",
    )(*args)


def _final_kernel(x1_ref, routed_ref, p_ref, s1_ref, s3_ref, s2_ref, g2_ref, b2_ref, wg_ref, wp_ref,
                  o_ref, *, alpha):
    x1 = x1_ref[...]
    xb = x1.astype(BF16)
    act = jax.nn.silu(_dot(xb, s1_ref[...])) * _dot(xb, s3_ref[...])
    ff = _load_row_major_tiles(routed_ref.at[0], x1.shape[0]) + _dot(act.astype(BF16), s2_ref[...])
    x2 = _layer_norm(alpha * x1 + ff, g2_ref[...], b2_ref[...])
    gate = jax.nn.sigmoid(_dot(x2.astype(BF16), wg_ref[...]))
    o_ref[...] = x2 + gate * _dot(p_ref[...].astype(BF16), wp_ref[...])


def _final(x1, routed, p2d, s1, s3, s2, g2, b2, wg, wp, *, alpha):
    n_rows = x1.shape[0]
    tile = ROW_TILE
    row = lambda width: pl.BlockSpec((tile, width), lambda i: (i, 0))
    tiles_per_group = routed.shape[1] // D_TILES // tile
    routed_spec = pl.BlockSpec((1, tile * D_TILES, LANES),
                               lambda i: (i // tiles_per_group, i % tiles_per_group, 0))
    return pl.pallas_call(
        functools.partial(_final_kernel, alpha=alpha),
        grid=(n_rows // tile,),
        in_specs=[row(D_MODEL), routed_spec,
                  row(p2d.shape[1]),
                  _const_spec(s1.shape), _const_spec(s3.shape), _const_spec(s2.shape),
                  _const_spec((1, D_MODEL)), _const_spec((1, D_MODEL)),
                  _const_spec(wg.shape), _const_spec(wp.shape)],
        out_specs=row(D_MODEL),
        out_shape=jax.ShapeDtypeStruct((n_rows, D_MODEL), F32),
        compiler_params=_params(1), name="final",
    )(x1, routed, p2d, s1, s3, s2, g2, b2, wg, wp)


def _trunk_layer(x, p, cache_k, cache_v, lw, *, alpha, first_layer_ln):
    batch, seq, d = x.shape
    n_rows = batch * seq
    x2d = x.reshape(n_rows, d)
    prompt = cache_k is None
    mlp_len = min(seq, MLP_CHUNK)
    keep = min(PAST_CHUNKS * CHUNK, seq) if prompt else seq
    ln_g, ln_b = first_layer_ln

    w_s = (lw["mlp_w_s"][:, :mlp_len, :mlp_len]
           * jnp.tril(jnp.ones((mlp_len, mlp_len), F32))).astype(BF16)
    b_s = jnp.repeat(lw["mlp_b_s"][:, :mlp_len].T, MLP_WIDTH // MLP_GROUPS, axis=1)
    q, k, v, kf, vf, vn, ga, gb, ob = _proj(
        x2d, ln_g, ln_b, lw["w_in"].astype(BF16), lw["mlp_ln_g"][None], lw["mlp_ln_b"][None],
        w_s, b_s, seq=seq, mlp_len=mlp_len, keep_rows=keep)
    if prompt:
        oa = _attn_prompt(q, k, v, lw["attn_rel_bias"], batch=batch, seq=seq)
    else:
        oa = _attn_sample(q, k, v, cache_k, cache_v, lw["attn_rel_bias"], batch=batch, seq=seq)
    x1, x1_tiles, ids, weights = _mix(
        x2d, oa, ob, ga, gb, ln_g, ln_b, lw["w_branch_a"].astype(BF16), lw["w_branch_b"].astype(BF16),
        lw["w_out"].astype(BF16), lw["ln1_g"][None], lw["ln1_b"][None],
        lw["router_w"].T, lw["router_bias"][:, None], alpha=alpha)
    routed = _moe(ids, weights, x1_tiles, lw["exp_w1"], lw["exp_w3"], lw["exp_w2"])
    y = _final(x1, routed, p.reshape(n_rows, -1), lw["shared_w1"].astype(BF16),
               lw["shared_w3"].astype(BF16), lw["shared_w2"].astype(BF16),
               lw["ln2_g"][None], lw["ln2_b"][None], lw["ple_w_gate"].astype(BF16),
               lw["ple_w_proj"].astype(BF16), alpha=alpha)
    new_k = kf.reshape(batch, keep, HEADS, HEAD_DIM)
    new_v = vf.reshape(batch, keep, HEADS, HEAD_DIM)
    new_mlp_v = vn.reshape(batch, mlp_len, MLP_WIDTH)
    return y.reshape(batch, seq, d), new_k, new_v, new_mlp_v


def kernel(x_prompt, x_sample, cache_attn_k, cache_attn_v, p_prompt, p_sample, ln_in_g, ln_in_b, w_in, attn_rel_bias, mlp_ln_g, mlp_ln_b, mlp_w_s, mlp_b_s, w_branch_a, w_branch_b, w_out, ln1_g, ln1_b, router_w, router_bias, exp_w1, exp_w3, exp_w2, shared_w1, shared_w3, shared_w2, ln2_g, ln2_b, ple_w_gate, ple_w_proj):
    depth = w_in.shape[0]
    assert depth == 1, "the fused proj/mix kernels apply LayerNorm_in themselves: single layer only"
    alpha = (2 * depth) ** 0.25
    stacked = dict(w_in=w_in, attn_rel_bias=attn_rel_bias, mlp_ln_g=mlp_ln_g, mlp_ln_b=mlp_ln_b,
                   mlp_w_s=mlp_w_s, mlp_b_s=mlp_b_s, w_branch_a=w_branch_a, w_branch_b=w_branch_b,
                   w_out=w_out, ln1_g=ln1_g, ln1_b=ln1_b, router_w=router_w, router_bias=router_bias,
                   exp_w1=exp_w1, exp_w3=exp_w3, exp_w2=exp_w2, shared_w1=shared_w1,
                   shared_w3=shared_w3, shared_w2=shared_w2, ln2_g=ln2_g, ln2_b=ln2_b,
                   ple_w_gate=ple_w_gate, ple_w_proj=ple_w_proj)
    lw = {name: value[0] for name, value in stacked.items()}
    ln_in = (ln_in_g[None], ln_in_b[None])
    yp, kp, vp, mp = _trunk_layer(x_prompt, p_prompt[0], None, None, lw, alpha=alpha, first_layer_ln=ln_in)
    ys, ks, vs, ms = _trunk_layer(x_sample, p_sample[0], cache_attn_k[0], cache_attn_v[0], lw,
                                  alpha=alpha, first_layer_ln=ln_in)
    return (yp, ys, kp[None], vp[None], ks[None], vs[None], mp[None], ms[None])
```

```python
import functools

import jax
import jax.numpy as jnp
import numpy as np
from jax import lax
from jax.experimental import pallas as pl
from jax.experimental.pallas import tpu as pltpu

F32 = jnp.float32
BF16 = jnp.bfloat16

D_MODEL = 1024
CHUNK = 64
HEADS = 8
HEAD_DIM = 64
A_WIDTH = HEADS * HEAD_DIM
PAST_CHUNKS = 8
MAX_REL = 256
MLP_CHUNK = 128
MLP_GROUPS = 8
MLP_WIDTH = 512
N_EXPERTS = 64
TOP_K = 8
N_GROUPS = 8
TOPK_GROUPS = 4
GROUP_SIZE = N_EXPERTS // N_GROUPS
EXPERT_DIM = 256
ROUTED_SCALE = 2.5
LN_EPS = 1e-5
MASKED = -1e30

LANES = 128
SUBLANES = 8
VMEM_LIMIT_BYTES = 56 * 1024 * 1024

ROW_TILE = 512
Q_CHUNKS = 4
MOE_ROWS = 128
MOE_GROUP_TOKENS = 2048
SCATTER_BATCH = 16
D_TILES = D_MODEL // LANES


def _layer_norm(x, g, b):
    xc = x - jnp.mean(x, axis=-1, keepdims=True)
    var = jnp.mean(xc * xc, axis=-1, keepdims=True)
    return xc * lax.rsqrt(var + LN_EPS) * g + b


def _dot(a, b):
    return jnp.dot(a, b, preferred_element_type=F32)


def _dot_nt(a, b, precision=None):
    return lax.dot_general(a, b, (((1,), (1,)), ((), ())),
                           preferred_element_type=F32, precision=precision)


def _const_spec(shape):
    zeros = (0,) * len(shape)
    return pl.BlockSpec(shape, lambda *_: zeros)


def _params(n_axes):
    return pltpu.CompilerParams(dimension_semantics=("arbitrary",) * n_axes,
                                vmem_limit_bytes=VMEM_LIMIT_BYTES)


def _proj_kernel(x_ref, lng_ref, lnb_ref, w_ref, mg_ref, mb_ref, ws_ref, bs_ref,
                 q_ref, k_ref, v_ref, kf_ref, vf_ref, vn_ref, ga_ref, gb_ref, ob_ref,
                 *, mlp_len, vn_rows):
    rows = x_ref.shape[0]
    xn = _layer_norm(x_ref[...], lng_ref[...], lnb_ref[...]).astype(BF16)

    def section(lo, width):
        return _dot(xn, w_ref[:, lo:lo + width])

    q = section(0, A_WIDTH)
    q_ref[...] = (q * (HEAD_DIM ** -0.5)).astype(BF16)
    k = section(A_WIDTH, A_WIDTH)
    k_ref[...] = k.astype(BF16)
    kf_ref[...] = k
    v = section(2 * A_WIDTH, A_WIDTH)
    v_ref[...] = v.astype(BF16)
    vf_ref[...] = v
    base = 3 * A_WIDTH
    un = jax.nn.gelu(section(base, MLP_WIDTH))
    vn = _layer_norm(jax.nn.gelu(section(base + MLP_WIDTH, MLP_WIDTH)), mg_ref[...], mb_ref[...])
    vn_ref[...] = vn[rows - vn_rows:, :]
    base += 2 * MLP_WIDTH
    ga_ref[...] = jax.nn.sigmoid(section(base, D_MODEL)).astype(BF16)
    gb_ref[...] = jax.nn.sigmoid(section(base + D_MODEL, D_MODEL)).astype(BF16)

    vnb = vn.astype(BF16)
    low_half = lax.broadcasted_iota(jnp.int32, (mlp_len, LANES), 1) < (MLP_WIDTH // MLP_GROUPS)
    for c in range(rows // mlp_len):
        r0 = c * mlp_len
        for p in range(MLP_GROUPS // 2):
            c0 = p * LANES
            slab = vnb[r0:r0 + mlp_len, c0:c0 + LANES]
            mixed = jnp.where(low_half, _dot(ws_ref[2 * p], slab), _dot(ws_ref[2 * p + 1], slab))
            gated = un[r0:r0 + mlp_len, c0:c0 + LANES] * (mixed + bs_ref[:, c0:c0 + LANES])
            ob_ref[r0:r0 + mlp_len, c0:c0 + LANES] = gated.astype(BF16)


def _proj(x2d, ln_g, ln_b, w_in, mlp_g, mlp_b, w_s, b_s, *, seq, mlp_len, keep_rows):
    n_rows = x2d.shape[0]
    tile = ROW_TILE
    assert n_rows % tile == 0 and tile % mlp_len == 0
    n_seq = n_rows // seq
    if seq >= tile:
        assert seq % tile == 0 and keep_rows == tile
        per_seq = seq // tile
        kv_rows, vn_rows = n_seq * tile, mlp_len
        kv_map = lambda i: (i // per_seq, 0)
    else:
        assert keep_rows == seq and mlp_len == seq
        kv_rows, vn_rows = n_rows, tile
        kv_map = lambda i: (i, 0)
    row = lambda width: pl.BlockSpec((tile, width), lambda i: (i, 0))
    n_in = w_in.shape[1]
    out_shape = (
        jax.ShapeDtypeStruct((n_rows, A_WIDTH), BF16),
        jax.ShapeDtypeStruct((n_rows, A_WIDTH), BF16),
        jax.ShapeDtypeStruct((n_rows, A_WIDTH), BF16),
        jax.ShapeDtypeStruct((kv_rows, A_WIDTH), F32),
        jax.ShapeDtypeStruct((kv_rows, A_WIDTH), F32),
        jax.ShapeDtypeStruct((kv_rows // tile * vn_rows, MLP_WIDTH), F32),
        jax.ShapeDtypeStruct((n_rows, D_MODEL), BF16),
        jax.ShapeDtypeStruct((n_rows, D_MODEL), BF16),
        jax.ShapeDtypeStruct((n_rows, MLP_WIDTH), BF16),
    )
    out_specs = (
        row(A_WIDTH), row(A_WIDTH), row(A_WIDTH),
        pl.BlockSpec((tile, A_WIDTH), kv_map), pl.BlockSpec((tile, A_WIDTH), kv_map),
        pl.BlockSpec((vn_rows, MLP_WIDTH), kv_map),
        row(D_MODEL), row(D_MODEL), row(MLP_WIDTH),
    )
    in_specs = [
        row(D_MODEL), _const_spec((1, D_MODEL)), _const_spec((1, D_MODEL)),
        _const_spec((D_MODEL, n_in)), _const_spec((1, MLP_WIDTH)), _const_spec((1, MLP_WIDTH)),
        _const_spec((MLP_GROUPS, mlp_len, mlp_len)), _const_spec((mlp_len, MLP_WIDTH)),
    ]
    return pl.pallas_call(
        functools.partial(_proj_kernel, mlp_len=mlp_len, vn_rows=vn_rows),
        grid=(n_rows // tile,), in_specs=in_specs, out_specs=out_specs, out_shape=out_shape,
        compiler_params=_params(1), name="proj",
    )(x2d, ln_g, ln_b, w_in, mlp_g, mlp_b, w_s, b_s)


def _attend_heads(q, keys, values, bias_ref, extra_mask, o_ref):
    n_q = q.shape[0]
    low_half = lax.broadcasted_iota(jnp.int32, (n_q, LANES), 1) < HEAD_DIM
    zero = jnp.zeros((), BF16)
    for p in range(HEADS // 2):
        c0 = p * LANES
        q2, k2, v2 = q[:, c0:c0 + LANES], keys[:, c0:c0 + LANES], values[:, c0:c0 + LANES]
        outs = []
        for h in range(2):
            qh = jnp.where(low_half if h == 0 else ~low_half, q2, zero)
            s = _dot_nt(qh, k2) + bias_ref[2 * p + h]
            if extra_mask is not None:
                s = s + extra_mask
            e = jnp.exp(s - jnp.max(s, axis=-1, keepdims=True))
            denom = jnp.sum(e, axis=-1, keepdims=True)
            outs.append(_dot(e.astype(BF16), v2) / denom)
        o_ref[:, c0:c0 + LANES] = jnp.where(low_half, outs[0], outs[1]).astype(o_ref.dtype)


def _attn_prompt_kernel(q_ref, k_ref, v_ref, bias_ref, o_ref, *, window):
    n_q = q_ref.shape[1]
    start = pl.multiple_of(pl.program_id(1) * n_q, n_q)
    keys = k_ref[0, pl.ds(start, window), :]
    values = v_ref[0, pl.ds(start, window), :]
    col = lax.broadcasted_iota(jnp.int32, (1, window), 1)
    pad_mask = jnp.where(col + start >= PAST_CHUNKS * CHUNK, 0.0, MASKED).astype(F32)
    _attend_heads(q_ref[0], keys, values, bias_ref, pad_mask, o_ref.at[0])


def _relative_bias(rel_table, n_q, n_k, key_offset):
    period = pl.next_power_of_2(n_q + n_k)
    m = np.arange(period)
    diag = np.where(m < n_k, m, m - period)
    idx = np.clip(diag - key_offset, -MAX_REL, MAX_REL) + MAX_REL
    values = rel_table[:, idx].astype(F32)
    rows = jnp.tile(values, (1, n_q))[:, :n_q * (period - 1)]
    return rows.reshape(rel_table.shape[0], n_q, period - 1)[:, :, :n_k]


def _prompt_bias(rel_table):
    n_q, n_k = Q_CHUNKS * CHUNK, (Q_CHUNKS + PAST_CHUNKS) * CHUNK
    lag = np.arange(n_q)[:, None] // CHUNK - (np.arange(n_k)[None, :] // CHUNK - PAST_CHUNKS)
    in_band = (lag >= 0) & (lag <= PAST_CHUNKS)
    bias = _relative_bias(rel_table, n_q, n_k, PAST_CHUNKS * CHUNK)
    return jnp.where(in_band[None], bias, MASKED)


def _attn_prompt(q, k, v, rel_table, *, batch, seq):
    n_q = Q_CHUNKS * CHUNK
    window = (Q_CHUNKS + PAST_CHUNKS) * CHUNK
    pad = PAST_CHUNKS * CHUNK
    q3 = q.reshape(batch, seq, A_WIDTH)
    kp = jnp.pad(k.reshape(batch, seq, A_WIDTH), ((0, 0), (pad, 0), (0, 0)))
    vp = jnp.pad(v.reshape(batch, seq, A_WIDTH), ((0, 0), (pad, 0), (0, 0)))
    bias = _prompt_bias(rel_table)
    qspec = pl.BlockSpec((1, n_q, A_WIDTH), lambda b, i: (b, i, 0))
    kvspec = pl.BlockSpec((1, seq + pad, A_WIDTH), lambda b, i: (b, 0, 0))
    out = pl.pallas_call(
        functools.partial(_attn_prompt_kernel, window=window),
        grid=(batch, seq // n_q),
        in_specs=[qspec, kvspec, kvspec, _const_spec(bias.shape)],
        out_specs=qspec,
        out_shape=jax.ShapeDtypeStruct((batch, seq, A_WIDTH), BF16),
        compiler_params=_params(2), name="attn_prompt",
    )(q3, kp, vp, bias)
    return out.reshape(batch * seq, A_WIDTH)


def _attn_sample_kernel(q_ref, k_ref, v_ref, ck_ref, cv_ref, bias_ref, o_ref, kk_ref, vv_ref):
    n_cache = ck_ref.shape[1]
    n_new = k_ref.shape[1]
    kk_ref[0:n_cache, :] = ck_ref[0].astype(BF16)
    kk_ref[n_cache:n_cache + n_new, :] = k_ref[0]
    vv_ref[0:n_cache, :] = cv_ref[0].astype(BF16)
    vv_ref[n_cache:n_cache + n_new, :] = v_ref[0]
    _attend_heads(q_ref[0], kk_ref[...], vv_ref[...], bias_ref, None, o_ref.at[0])


def _attn_sample(q, k, v, cache_k, cache_v, rel_table, *, batch, seq):
    n_cache = cache_k.shape[1]
    bias = _relative_bias(rel_table, seq, n_cache + seq, n_cache)
    new = pl.BlockSpec((1, seq, A_WIDTH), lambda b: (b, 0, 0))
    old = pl.BlockSpec((1, n_cache, A_WIDTH), lambda b: (b, 0, 0))
    out = pl.pallas_call(
        _attn_sample_kernel,
        grid=(batch,),
        in_specs=[new, new, new, old, old, _const_spec(bias.shape)],
        out_specs=new,
        out_shape=jax.ShapeDtypeStruct((batch, seq, A_WIDTH), BF16),
        scratch_shapes=[pltpu.VMEM((n_cache + seq, A_WIDTH), BF16),
                        pltpu.VMEM((n_cache + seq, A_WIDTH), BF16)],
        compiler_params=_params(1), name="attn_sample",
    )(q.reshape(batch, seq, A_WIDTH), k.reshape(batch, seq, A_WIDTH), v.reshape(batch, seq, A_WIDTH),
      cache_k.reshape(batch, n_cache, A_WIDTH), cache_v.reshape(batch, n_cache, A_WIDTH), bias)
    return out.reshape(batch * seq, A_WIDTH)


def _first_index(hit, iota, axis, limit):
    return jnp.min(jnp.where(hit, iota, limit), axis=axis, keepdims=True)


def _route(scores, sel):
    n_tok = scores.shape[1]
    neg = -jnp.inf
    grouped = sel.reshape(N_GROUPS, GROUP_SIZE, n_tok)
    member = lax.broadcasted_iota(jnp.int32, grouped.shape, 1)
    best = jnp.max(grouped, axis=1, keepdims=True)
    first = _first_index(grouped == best, member, 1, GROUP_SIZE)
    second = jnp.max(jnp.where(member == first, neg, grouped), axis=1, keepdims=True)
    group_score = best + second

    group_id = lax.broadcasted_iota(jnp.int32, group_score.shape, 0)
    keep = jnp.zeros(group_score.shape, F32)
    for _ in range(TOPK_GROUPS):
        top = jnp.max(group_score, axis=0, keepdims=True)
        hit = group_id == _first_index(group_score == top, group_id, 0, N_GROUPS)
        keep = jnp.where(hit, 1.0, keep)
        group_score = jnp.where(hit, neg, group_score)
    keep = jnp.broadcast_to(keep, grouped.shape).reshape(N_EXPERTS, n_tok)

    cand = jnp.where(keep > 0.0, sel, neg)
    expert_id = lax.broadcasted_iota(jnp.int32, cand.shape, 0)
    ids, weights = [], []
    for _ in range(TOP_K):
        top = jnp.max(cand, axis=0, keepdims=True)
        first = _first_index(cand == top, expert_id, 0, N_EXPERTS)
        hit = expert_id == first
        ids.append(first)
        weights.append(jnp.sum(jnp.where(hit, scores, 0.0), axis=0, keepdims=True))
        cand = jnp.where(hit, neg, cand)
    ids = jnp.concatenate(ids, axis=0)
    weights = jnp.concatenate(weights, axis=0)
    weights = weights / jnp.sum(weights, axis=0, keepdims=True) * ROUTED_SCALE
    return ids, weights


def _store_row_major_tiles(flat_ref, x):
    rows = x.shape[0]
    for j in range(D_TILES):
        flat_ref[pl.ds(j, rows, stride=D_TILES), :] = x[:, j * LANES:(j + 1) * LANES]


def _load_row_major_tiles(flat_ref, rows):
    return jnp.concatenate(
        [flat_ref[pl.ds(j, rows, stride=D_TILES), :] for j in range(D_TILES)], axis=1)


def _store_matmul_tiles(flat_ref, x):
    for a in range(x.shape[0] // SUBLANES):
        for j in range(D_TILES):
            r0 = (a * D_TILES + j) * SUBLANES
            flat_ref[r0:r0 + SUBLANES, :] = x[a * SUBLANES:(a + 1) * SUBLANES, j * LANES:(j + 1) * LANES]


def _load_row_tile(flat_ref, m):
    start = (m // SUBLANES) * SUBLANES * D_TILES + m % SUBLANES
    return flat_ref[pl.ds(start, D_TILES, stride=SUBLANES), :]


def _mix_kernel(x_ref, oa_ref, ob_ref, ga_ref, gb_ref, lng_ref, lnb_ref, wa_ref, wb_ref, wo_ref,
                g1_ref, b1_ref, rw_ref, rb_ref, x1_ref, x1t_ref, ids_ref, wts_ref, *, alpha):
    xn = _layer_norm(x_ref[...], lng_ref[...], lnb_ref[...])
    mix = (ga_ref[...].astype(F32) * _dot(oa_ref[...], wa_ref[...])
           + gb_ref[...].astype(F32) * _dot(ob_ref[...], wb_ref[...]))
    x1 = _layer_norm(alpha * xn + _dot(mix.astype(BF16), wo_ref[...]), g1_ref[...], b1_ref[...])
    x1_ref[...] = x1
    _store_row_major_tiles(x1t_ref, x1)
    scores = jax.nn.sigmoid(_dot_nt(rw_ref[...], x1, precision=lax.Precision.HIGHEST))
    ids, weights = _route(scores, scores + rb_ref[...])
    ids_ref[...] = ids
    wts_ref[...] = weights


def _mix(x2d, oa, ob, ga, gb, ln_g, ln_b, wa, wb, wo, g1, b1, rw_t, rb, *, alpha):
    n_rows = x2d.shape[0]
    tile = ROW_TILE
    row = lambda width: pl.BlockSpec((tile, width), lambda i: (i, 0))
    col = pl.BlockSpec((TOP_K, tile), lambda i: (0, i))
    return pl.pallas_call(
        functools.partial(_mix_kernel, alpha=alpha),
        grid=(n_rows // tile,),
        in_specs=[row(D_MODEL), row(A_WIDTH), row(MLP_WIDTH), row(D_MODEL), row(D_MODEL),
                  _const_spec((1, D_MODEL)), _const_spec((1, D_MODEL)),
                  _const_spec(wa.shape), _const_spec(wb.shape), _const_spec(wo.shape),
                  _const_spec((1, D_MODEL)), _const_spec((1, D_MODEL)),
                  _const_spec(rw_t.shape), _const_spec(rb.shape)],
        out_specs=(row(D_MODEL), pl.BlockSpec((tile * D_TILES, LANES), lambda i: (i, 0)), col, col),
        out_shape=(jax.ShapeDtypeStruct((n_rows, D_MODEL), F32),
                   jax.ShapeDtypeStruct((n_rows * D_TILES, LANES), F32),
                   jax.ShapeDtypeStruct((TOP_K, n_rows), jnp.int32),
                   jax.ShapeDtypeStruct((TOP_K, n_rows), F32)),
        compiler_params=_params(1), name="mix",
    )(x2d, oa, ob, ga, gb, ln_g, ln_b, wa, wb, wo, g1, b1, rw_t, rb)


def _moe_kernel(rows_ref, start_ref, x_ref, wts_ref, w1_ref, w3_ref, w2_ref, *rest):
    out_ref, gat_ref, y_ref, pending_ref = rest[-4:]
    acc = out_ref.at[0]
    e = pl.program_id(0)

    @pl.when(e == 0)
    def _():
        out_ref[...] = jnp.zeros(out_ref.shape, out_ref.dtype)
        y_ref[...] = jnp.zeros(y_ref.shape, y_ref.dtype)
        pending_ref[0] = 0

    first = start_ref[e]
    count = start_ref[e + 1] - first

    def tile_at(ref, row8):
        return ref.at[pl.ds(pl.multiple_of(row8, SUBLANES), SUBLANES), :]

    row_id = lax.broadcasted_iota(jnp.int32, (MOE_ROWS, 2 * LANES), 0)
    lane_id = lax.broadcasted_iota(jnp.int32, (MOE_ROWS, 2 * LANES), 1)

    def scatter_add(base):
        for m0 in range(0, MOE_ROWS, SCATTER_BATCH):
            updates = []
            for m in range(m0, m0 + SCATTER_BATCH):
                dst = tile_at(acc, rows_ref[base + m])
                updates.append((dst, dst[...] + _load_row_tile(y_ref, m)))
            for dst, val in updates:
                dst[...] = val

    def block(b, pending):
        nominal = b * MOE_ROWS
        begin = jnp.minimum(nominal, count - MOE_ROWS)
        redo = nominal - begin
        base = first + begin + MOE_ROWS
        for m in range(MOE_ROWS):
            gat_ref[m * SUBLANES:(m + 1) * SUBLANES, :] = tile_at(x_ref, rows_ref[base + m])[...]
        xb = _load_row_major_tiles(gat_ref, MOE_ROWS).astype(BF16)
        act = jax.nn.silu(_dot(xb, w1_ref[0])) * _dot(xb, w3_ref[0])
        y = _dot(act.astype(BF16), w2_ref[0])

        w_row = base // LANES
        pair = jnp.concatenate([wts_ref[w_row], wts_ref[w_row + 1]], axis=1)
        picked = jnp.where((lane_id == row_id + base % LANES) & (row_id >= redo),
                           jnp.broadcast_to(pair, lane_id.shape), 0.0)
        ys = y * jnp.sum(picked, axis=1, keepdims=True)
        scatter_add(pending)
        _store_matmul_tiles(y_ref, ys)
        return base

    pending = lax.fori_loop(0, pl.cdiv(count, MOE_ROWS), block, pending_ref[0])
    pending_ref[0] = pending

    @pl.when(e == pl.num_programs(0) - 1)
    def _():
        scatter_add(pending)


def _moe(ids, weights, x_tiles, w1, w3, w2):
    n_tok = ids.shape[1]
    g_tok = min(MOE_GROUP_TOKENS, n_tok)
    n_groups = n_tok // g_tok
    assert n_groups * g_tok == n_tok
    per_group = g_tok * TOP_K
    tok = lax.broadcasted_iota(jnp.int32, ids.shape, 1)
    key = ((tok // g_tok) * N_EXPERTS + ids) * g_tok + tok % g_tok
    key, wts = lax.sort((key.reshape(-1), weights.reshape(-1)), num_keys=1)
    rows8 = ((key % g_tok) * D_TILES).reshape(n_groups, per_group)
    wts = wts.reshape(n_groups, per_group)
    experts = jnp.arange(N_EXPERTS, dtype=jnp.int32)[None, None, :, None]
    counts = jnp.sum(ids.reshape(TOP_K, n_groups, 1, g_tok) == experts, axis=(0, 3), dtype=jnp.int32)
    start = jnp.concatenate([jnp.zeros((n_groups, 1), jnp.int32), jnp.cumsum(counts, axis=1)], axis=1)
    pad = ((0, 0), (MOE_ROWS, 2 * LANES))
    rows8 = jnp.pad(rows8, pad)
    wts = jnp.pad(wts, pad).reshape(n_groups, -1, 1, LANES)
    out = None
    for g in range(n_groups):
        out = _moe_group_call(g, n_groups, g_tok, rows8[g], start[g], x_tiles, wts[g], w1, w3, w2, out)
    return out


def _moe_group_call(group, n_groups, g_tok, rows8, start, x_tiles, wts, w1, w3, w2, prev):
    g_rows = g_tok * D_TILES
    assert x_tiles.shape[0] == n_groups * g_rows
    out_block = (1, g_rows, LANES)
    in_specs = [
        pl.BlockSpec((g_rows, LANES), lambda e, *_: (group, 0)),
        pl.BlockSpec(wts.shape, lambda e, *_: (0, 0, 0)),
        pl.BlockSpec((1, D_MODEL, EXPERT_DIM), lambda e, *_: (e, 0, 0)),
        pl.BlockSpec((1, D_MODEL, EXPERT_DIM), lambda e, *_: (e, 0, 0)),
        pl.BlockSpec((1, EXPERT_DIM, D_MODEL), lambda e, *_: (e, 0, 0)),
    ]
    args = [rows8, start, x_tiles, wts, w1, w3, w2]
    aliases = {}
    if prev is not None:
        in_specs.append(pl.BlockSpec(memory_space=pl.ANY))
        aliases = {len(args): 0}
        args.append(prev)
    return pl.pallas_call(
        _moe_kernel,
        grid_spec=pltpu.PrefetchScalarGridSpec(
            num_scalar_prefetch=2, grid=(N_EXPERTS,), in_specs=in_specs,
            out_specs=pl.BlockSpec(out_block, lambda e, *_: (group, 0, 0)),
            scratch_shapes=[pltpu.VMEM((MOE_ROWS * D_TILES, LANES), F32),
                            pltpu.VMEM((MOE_ROWS * D_TILES, LANES), F32),
                            pltpu.SMEM((1,), jnp.int32)]),
        out_shape=jax.ShapeDtypeStruct((n_groups,) + out_block[1:], F32),
        input_output_aliases=aliases,
        compiler_params=_params(1), name="moe",
    )(*args)


def _final_kernel(x1_ref, routed_ref, p_ref, s1_ref, s3_ref, s2_ref, g2_ref, b2_ref, wg_ref, wp_ref,
                  o_ref, *, alpha):
    x1 = x1_ref[...]
    xb = x1.astype(BF16)
    act = jax.nn.silu(_dot(xb, s1_ref[...])) * _dot(xb, s3_ref[...])
    ff = _load_row_major_tiles(routed_ref.at[0], x1.shape[0]) + _dot(act.astype(BF16), s2_ref[...])
    x2 = _layer_norm(alpha * x1 + ff, g2_ref[...], b2_ref[...])
    gate = jax.nn.sigmoid(_dot(x2.astype(BF16), wg_ref[...]))
    o_ref[...] = x2 + gate * _dot(p_ref[...].astype(BF16), wp_ref[...])


def _final(x1, routed, p2d, s1, s3, s2, g2, b2, wg, wp, *, alpha):
    n_rows = x1.shape[0]
    tile = ROW_TILE
    row = lambda width: pl.BlockSpec((tile, width), lambda i: (i, 0))
    tiles_per_group = routed.shape[1] // D_TILES // tile
    routed_spec = pl.BlockSpec((1, tile * D_TILES, LANES),
                               lambda i: (i // tiles_per_group, i % tiles_per_group, 0))
    return pl.pallas_call(
        functools.partial(_final_kernel, alpha=alpha),
        grid=(n_rows // tile,),
        in_specs=[row(D_MODEL), routed_spec,
                  row(p2d.shape[1]),
                  _const_spec(s1.shape), _const_spec(s3.shape), _const_spec(s2.shape),
                  _const_spec((1, D_MODEL)), _const_spec((1, D_MODEL)),
                  _const_spec(wg.shape), _const_spec(wp.shape)],
        out_specs=row(D_MODEL),
        out_shape=jax.ShapeDtypeStruct((n_rows, D_MODEL), F32),
        compiler_params=_params(1), name="final",
    )(x1, routed, p2d, s1, s3, s2, g2, b2, wg, wp)


def _trunk_layer(x, p, cache_k, cache_v, lw, *, alpha, first_layer_ln):
    batch, seq, d = x.shape
    n_rows = batch * seq
    x2d = x.reshape(n_rows, d)
    prompt = cache_k is None
    mlp_len = min(seq, MLP_CHUNK)
    keep = min(PAST_CHUNKS * CHUNK, seq) if prompt else seq
    ln_g, ln_b = first_layer_ln

    w_s = (lw["mlp_w_s"][:, :mlp_len, :mlp_len]
           * jnp.tril(jnp.ones((mlp_len, mlp_len), F32))).astype(BF16)
    b_s = jnp.repeat(lw["mlp_b_s"][:, :mlp_len].T, MLP_WIDTH // MLP_GROUPS, axis=1)
    q, k, v, kf, vf, vn, ga, gb, ob = _proj(
        x2d, ln_g, ln_b, lw["w_in"].astype(BF16), lw["mlp_ln_g"][None], lw["mlp_ln_b"][None],
        w_s, b_s, seq=seq, mlp_len=mlp_len, keep_rows=keep)
    if prompt:
        oa = _attn_prompt(q, k, v, lw["attn_rel_bias"], batch=batch, seq=seq)
    else:
        oa = _attn_sample(q, k, v, cache_k, cache_v, lw["attn_rel_bias"], batch=batch, seq=seq)
    x1, x1_tiles, ids, weights = _mix(
        x2d, oa, ob, ga, gb, ln_g, ln_b, lw["w_branch_a"].astype(BF16), lw["w_branch_b"].astype(BF16),
        lw["w_out"].astype(BF16), lw["ln1_g"][None], lw["ln1_b"][None],
        lw["router_w"].T, lw["router_bias"][:, None], alpha=alpha)
    routed = _moe(ids, weights, x1_tiles, lw["exp_w1"], lw["exp_w3"], lw["exp_w2"])
    y = _final(x1, routed, p.reshape(n_rows, -1), lw["shared_w1"].astype(BF16),
               lw["shared_w3"].astype(BF16), lw["shared_w2"].astype(BF16),
               lw["ln2_g"][None], lw["ln2_b"][None], lw["ple_w_gate"].astype(BF16),
               lw["ple_w_proj"].astype(BF16), alpha=alpha)
    new_k = kf.reshape(batch, keep, HEADS, HEAD_DIM)
    new_v = vf.reshape(batch, keep, HEADS, HEAD_DIM)
    new_mlp_v = vn.reshape(batch, mlp_len, MLP_WIDTH)
    return y.reshape(batch, seq, d), new_k, new_v, new_mlp_v


def kernel(x_prompt, x_sample, cache_attn_k, cache_attn_v, p_prompt, p_sample, ln_in_g, ln_in_b, w_in, attn_rel_bias, mlp_ln_g, mlp_ln_b, mlp_w_s, mlp_b_s, w_branch_a, w_branch_b, w_out, ln1_g, ln1_b, router_w, router_bias, exp_w1, exp_w3, exp_w2, shared_w1, shared_w3, shared_w2, ln2_g, ln2_b, ple_w_gate, ple_w_proj):
    depth = w_in.shape[0]
    assert depth == 1, "the fused proj/mix kernels apply LayerNorm_in themselves: single layer only"
    alpha = (2 * depth) ** 0.25
    stacked = dict(w_in=w_in, attn_rel_bias=attn_rel_bias, mlp_ln_g=mlp_ln_g, mlp_ln_b=mlp_ln_b,
                   mlp_w_s=mlp_w_s, mlp_b_s=mlp_b_s, w_branch_a=w_branch_a, w_branch_b=w_branch_b,
                   w_out=w_out, ln1_g=ln1_g, ln1_b=ln1_b, router_w=router_w, router_bias=router_bias,
                   exp_w1=exp_w1, exp_w3=exp_w3, exp_w2=exp_w2, shared_w1=shared_w1,
                   shared_w3=shared_w3, shared_w2=shared_w2, ln2_g=ln2_g, ln2_b=ln2_b,
                   ple_w_gate=ple_w_gate, ple_w_proj=ple_w_proj)
    lw = {name: value[0] for name, value in stacked.items()}
    for name in ("exp_w1", "exp_w3", "exp_w2"):
        lw[name] = lw[name].astype(BF16)
    ln_in =(ln_in_g[None], ln_in_b[None])
    yp, kp, vp, mp = _trunk_layer(x_prompt, p_prompt[0], None, None, lw, alpha=alpha, first_layer_ln=ln_in)
    ys, ks, vs, ms = _trunk_layer(x_sample, p_sample[0], cache_attn_k[0], cache_attn_v[0], lw,
                                  alpha=alpha, first_layer_ln=ln_in)
    return (yp, ys, kp[None], vp[None], ks[None], vs[None], mp[None], ms[None])
```

```python
import functools

import jax
import jax.numpy as jnp
import numpy as np
from jax import lax
from jax.experimental import pallas as pl
from jax.experimental.pallas import tpu as pltpu

F32 = jnp.float32
BF16 = jnp.bfloat16

D_MODEL = 1024
CHUNK = 64
HEADS = 8
HEAD_DIM = 64
A_WIDTH = HEADS * HEAD_DIM
PAST_CHUNKS = 8
MAX_REL = 256
MLP_CHUNK = 128
MLP_GROUPS = 8
MLP_WIDTH = 512
N_EXPERTS = 64
TOP_K = 8
N_GROUPS = 8
TOPK_GROUPS = 4
GROUP_SIZE = N_EXPERTS // N_GROUPS
EXPERT_DIM = 256
ROUTED_SCALE = 2.5
LN_EPS = 1e-5
MASKED = -1e30

LANES = 128
SUBLANES = 8
VMEM_LIMIT_BYTES = 56 * 1024 * 1024

ROW_TILE = 512
ROUTE_TILE = 2048
Q_CHUNKS = 4
MOE_ROWS = 128
MOE_GROUP_TOKENS = 2048
SCATTER_BATCH = 16
D_TILES = D_MODEL // LANES


def _layer_norm(x, g, b):
    xc = x - jnp.mean(x, axis=-1, keepdims=True)
    var = jnp.mean(xc * xc, axis=-1, keepdims=True)
    return xc * lax.rsqrt(var + LN_EPS) * g + b


def _dot(a, b):
    return jnp.dot(a, b, preferred_element_type=F32)


def _dot_nt(a, b, precision=None):
    return lax.dot_general(a, b, (((1,), (1,)), ((), ())),
                           preferred_element_type=F32, precision=precision)


def _const_spec(shape):
    zeros = (0,) * len(shape)
    return pl.BlockSpec(shape, lambda *_: zeros)


def _params(n_axes):
    return pltpu.CompilerParams(dimension_semantics=("arbitrary",) * n_axes,
                                vmem_limit_bytes=VMEM_LIMIT_BYTES)


def _proj_kernel(x_ref, lng_ref, lnb_ref, w_ref, mg_ref, mb_ref, ws_ref, bs_ref,
                 q_ref, k_ref, v_ref, kf_ref, vf_ref, vn_ref, ga_ref, gb_ref, ob_ref,
                 *, mlp_len, vn_rows):
    rows = x_ref.shape[0]
    xn = _layer_norm(x_ref[...], lng_ref[...], lnb_ref[...]).astype(BF16)

    def section(lo, width):
        return _dot(xn, w_ref[:, lo:lo + width])

    q = section(0, A_WIDTH)
    q_ref[...] = (q * (HEAD_DIM ** -0.5)).astype(BF16)
    k = section(A_WIDTH, A_WIDTH)
    k_ref[...] = k.astype(BF16)
    kf_ref[...] = k
    v = section(2 * A_WIDTH, A_WIDTH)
    v_ref[...] = v.astype(BF16)
    vf_ref[...] = v
    base = 3 * A_WIDTH
    un = jax.nn.gelu(section(base, MLP_WIDTH))
    vn = _layer_norm(jax.nn.gelu(section(base + MLP_WIDTH, MLP_WIDTH)), mg_ref[...], mb_ref[...])
    vn_ref[...] = vn[rows - vn_rows:, :]
    base += 2 * MLP_WIDTH
    ga_ref[...] = jax.nn.sigmoid(section(base, D_MODEL)).astype(BF16)
    gb_ref[...] = jax.nn.sigmoid(section(base + D_MODEL, D_MODEL)).astype(BF16)

    vnb = vn.astype(BF16)
    low_half = lax.broadcasted_iota(jnp.int32, (mlp_len, LANES), 1) < (MLP_WIDTH // MLP_GROUPS)
    for c in range(rows // mlp_len):
        r0 = c * mlp_len
        for p in range(MLP_GROUPS // 2):
            c0 = p * LANES
            slab = vnb[r0:r0 + mlp_len, c0:c0 + LANES]
            mixed = jnp.where(low_half, _dot(ws_ref[2 * p], slab), _dot(ws_ref[2 * p + 1], slab))
            gated = un[r0:r0 + mlp_len, c0:c0 + LANES] * (mixed + bs_ref[:, c0:c0 + LANES])
            ob_ref[r0:r0 + mlp_len, c0:c0 + LANES] = gated.astype(BF16)


def _proj(x2d, ln_g, ln_b, w_in, mlp_g, mlp_b, w_s, b_s, *, seq, mlp_len, keep_rows):
    n_rows = x2d.shape[0]
    tile = ROW_TILE
    assert n_rows % tile == 0 and tile % mlp_len == 0
    n_seq = n_rows // seq
    if seq >= tile:
        assert seq % tile == 0 and keep_rows == tile
        per_seq = seq // tile
        kv_rows, vn_rows = n_seq * tile, mlp_len
        kv_map = lambda i: (i // per_seq, 0)
    else:
        assert keep_rows == seq and mlp_len == seq
        kv_rows, vn_rows = n_rows, tile
        kv_map = lambda i: (i, 0)
    row = lambda width: pl.BlockSpec((tile, width), lambda i: (i, 0))
    n_in = w_in.shape[1]
    out_shape = (
        jax.ShapeDtypeStruct((n_rows, A_WIDTH), BF16),
        jax.ShapeDtypeStruct((n_rows, A_WIDTH), BF16),
        jax.ShapeDtypeStruct((n_rows, A_WIDTH), BF16),
        jax.ShapeDtypeStruct((kv_rows, A_WIDTH), F32),
        jax.ShapeDtypeStruct((kv_rows, A_WIDTH), F32),
        jax.ShapeDtypeStruct((kv_rows // tile * vn_rows, MLP_WIDTH), F32),
        jax.ShapeDtypeStruct((n_rows, D_MODEL), BF16),
        jax.ShapeDtypeStruct((n_rows, D_MODEL), BF16),
        jax.ShapeDtypeStruct((n_rows, MLP_WIDTH), BF16),
    )
    out_specs = (
        row(A_WIDTH), row(A_WIDTH), row(A_WIDTH),
        pl.BlockSpec((tile, A_WIDTH), kv_map), pl.BlockSpec((tile, A_WIDTH), kv_map),
        pl.BlockSpec((vn_rows, MLP_WIDTH), kv_map),
        row(D_MODEL), row(D_MODEL), row(MLP_WIDTH),
    )
    in_specs = [
        row(D_MODEL), _const_spec((1, D_MODEL)), _const_spec((1, D_MODEL)),
        _const_spec((D_MODEL, n_in)), _const_spec((1, MLP_WIDTH)), _const_spec((1, MLP_WIDTH)),
        _const_spec((MLP_GROUPS, mlp_len, mlp_len)), _const_spec((mlp_len, MLP_WIDTH)),
    ]
    return pl.pallas_call(
        functools.partial(_proj_kernel, mlp_len=mlp_len, vn_rows=vn_rows),
        grid=(n_rows // tile,), in_specs=in_specs, out_specs=out_specs, out_shape=out_shape,
        compiler_params=_params(1), name="proj",
    )(x2d, ln_g, ln_b, w_in, mlp_g, mlp_b, w_s, b_s)


def _attend_heads(q, keys, values, bias_ref, extra_mask, o_ref):
    n_q = q.shape[0]
    low_half = lax.broadcasted_iota(jnp.int32, (n_q, LANES), 1) < HEAD_DIM
    zero = jnp.zeros((), BF16)
    for p in range(HEADS // 2):
        c0 = p * LANES
        q2, k2, v2 = q[:, c0:c0 + LANES], keys[:, c0:c0 + LANES], values[:, c0:c0 + LANES]
        qs = jnp.concatenate([jnp.where(low_half, q2, zero), jnp.where(low_half, zero, q2)], axis=0)
        s = _dot_nt(qs, k2) + bias_ref[2 * p:2 * p + 2].reshape(2 * n_q, keys.shape[0])
        if extra_mask is not None:
            s = s + extra_mask
        e = jnp.exp(s - jnp.max(s, axis=-1, keepdims=True))
        denom = jnp.sum(e, axis=-1, keepdims=True)
        o = _dot(e.astype(BF16), v2) / denom
        o_ref[:, c0:c0 + LANES] = jnp.where(low_half, o[:n_q], o[n_q:]).astype(o_ref.dtype)


def _attn_prompt_kernel(q_ref, k_ref, v_ref, bias_ref, o_ref, *, window):
    n_q = q_ref.shape[1]
    start = pl.multiple_of(pl.program_id(1) * n_q, n_q)
    keys = k_ref[0, pl.ds(start, window), :]
    values = v_ref[0, pl.ds(start, window), :]
    col = lax.broadcasted_iota(jnp.int32, (1, window), 1)
    pad_mask = jnp.where(col + start >= PAST_CHUNKS * CHUNK, 0.0, MASKED).astype(F32)
    _attend_heads(q_ref[0], keys, values, bias_ref, pad_mask, o_ref.at[0])


def _relative_bias(rel_table, n_q, n_k, key_offset):
    period = pl.next_power_of_2(n_q + n_k)
    m = np.arange(period)
    diag = np.where(m < n_k, m, m - period)
    idx = np.clip(diag - key_offset, -MAX_REL, MAX_REL) + MAX_REL
    values = rel_table[:, idx].astype(F32)
    rows = jnp.tile(values, (1, n_q))[:, :n_q * (period - 1)]
    return rows.reshape(rel_table.shape[0], n_q, period - 1)[:, :, :n_k]


def _prompt_bias(rel_table):
    n_q, n_k = Q_CHUNKS * CHUNK, (Q_CHUNKS + PAST_CHUNKS) * CHUNK
    lag = np.arange(n_q)[:, None] // CHUNK - (np.arange(n_k)[None, :] // CHUNK - PAST_CHUNKS)
    in_band = (lag >= 0) & (lag <= PAST_CHUNKS)
    bias = _relative_bias(rel_table, n_q, n_k, PAST_CHUNKS * CHUNK)
    return jnp.where(in_band[None], bias, MASKED)


def _attn_prompt(q, k, v, rel_table, *, batch, seq):
    n_q = Q_CHUNKS * CHUNK
    window = (Q_CHUNKS + PAST_CHUNKS) * CHUNK
    pad = PAST_CHUNKS * CHUNK
    q3 = q.reshape(batch, seq, A_WIDTH)
    kp = jnp.pad(k.reshape(batch, seq, A_WIDTH), ((0, 0), (pad, 0), (0, 0)))
    vp = jnp.pad(v.reshape(batch, seq, A_WIDTH), ((0, 0), (pad, 0), (0, 0)))
    bias = _prompt_bias(rel_table)
    qspec = pl.BlockSpec((1, n_q, A_WIDTH), lambda b, i: (b, i, 0))
    kvspec = pl.BlockSpec((1, seq + pad, A_WIDTH), lambda b, i: (b, 0, 0))
    out = pl.pallas_call(
        functools.partial(_attn_prompt_kernel, window=window),
        grid=(batch, seq // n_q),
        in_specs=[qspec, kvspec, kvspec, _const_spec(bias.shape)],
        out_specs=qspec,
        out_shape=jax.ShapeDtypeStruct((batch, seq, A_WIDTH), BF16),
        compiler_params=_params(2), name="attn_prompt",
    )(q3, kp, vp, bias)
    return out.reshape(batch * seq, A_WIDTH)


def _attn_sample_kernel(q_ref, k_ref, v_ref, ck_ref, cv_ref, bias_ref, o_ref, kk_ref, vv_ref):
    n_cache = ck_ref.shape[1]
    n_new = k_ref.shape[1]
    kk_ref[0:n_cache, :] = ck_ref[0].astype(BF16)
    kk_ref[n_cache:n_cache + n_new, :] = k_ref[0]
    vv_ref[0:n_cache, :] = cv_ref[0].astype(BF16)
    vv_ref[n_cache:n_cache + n_new, :] = v_ref[0]
    _attend_heads(q_ref[0], kk_ref[...], vv_ref[...], bias_ref, None, o_ref.at[0])


def _attn_sample(q, k, v, cache_k, cache_v, rel_table, *, batch, seq):
    n_cache = cache_k.shape[1]
    bias = _relative_bias(rel_table, seq, n_cache + seq, n_cache)
    new = pl.BlockSpec((1, seq, A_WIDTH), lambda b: (b, 0, 0))
    old = pl.BlockSpec((1, n_cache, A_WIDTH), lambda b: (b, 0, 0))
    out = pl.pallas_call(
        _attn_sample_kernel,
        grid=(batch,),
        in_specs=[new, new, new, old, old, _const_spec(bias.shape)],
        out_specs=new,
        out_shape=jax.ShapeDtypeStruct((batch, seq, A_WIDTH), BF16),
        scratch_shapes=[pltpu.VMEM((n_cache + seq, A_WIDTH), BF16),
                        pltpu.VMEM((n_cache + seq, A_WIDTH), BF16)],
        compiler_params=_params(1), name="attn_sample",
    )(q.reshape(batch, seq, A_WIDTH), k.reshape(batch, seq, A_WIDTH), v.reshape(batch, seq, A_WIDTH),
      cache_k.reshape(batch, n_cache, A_WIDTH), cache_v.reshape(batch, n_cache, A_WIDTH), bias)
    return out.reshape(batch * seq, A_WIDTH)


def _first_index(hit, iota, axis, limit):
    return jnp.min(jnp.where(hit, iota, limit), axis=axis, keepdims=True)


def _route(scores, sel):
    n_tok = scores.shape[1]
    neg = -jnp.inf
    grouped = sel.reshape(N_GROUPS, GROUP_SIZE, n_tok)
    member = lax.broadcasted_iota(jnp.int32, grouped.shape, 1)
    best = jnp.max(grouped, axis=1, keepdims=True)
    first = _first_index(grouped == best, member, 1, GROUP_SIZE)
    second = jnp.max(jnp.where(member == first, neg, grouped), axis=1, keepdims=True)
    group_score = best + second

    group_id = lax.broadcasted_iota(jnp.int32, group_score.shape, 0)
    keep = jnp.zeros(group_score.shape, F32)
    for _ in range(TOPK_GROUPS):
        top = jnp.max(group_score, axis=0, keepdims=True)
        hit = group_id == _first_index(group_score == top, group_id, 0, N_GROUPS)
        keep = jnp.where(hit, 1.0, keep)
        group_score = jnp.where(hit, neg, group_score)
    keep = jnp.broadcast_to(keep, grouped.shape).reshape(N_EXPERTS, n_tok)

    cand = jnp.where(keep > 0.0, sel, neg)
    expert_id = lax.broadcasted_iota(jnp.int32, cand.shape, 0)
    ids, weights = [], []
    for _ in range(TOP_K):
        top = jnp.max(cand, axis=0, keepdims=True)
        first = _first_index(cand == top, expert_id, 0, N_EXPERTS)
        hit = expert_id == first
        ids.append(first)
        weights.append(jnp.sum(jnp.where(hit, scores, 0.0), axis=0, keepdims=True))
        cand = jnp.where(hit, neg, cand)
    ids = jnp.concatenate(ids, axis=0)
    weights = jnp.concatenate(weights, axis=0)
    weights = weights / jnp.sum(weights, axis=0, keepdims=True) * ROUTED_SCALE
    return ids, weights


def _store_row_major_tiles(flat_ref, x):
    rows = x.shape[0]
    for j in range(D_TILES):
        flat_ref[pl.ds(j, rows, stride=D_TILES), :] = x[:, j * LANES:(j + 1) * LANES]


def _load_row_major_tiles(flat_ref, rows):
    return jnp.concatenate(
        [flat_ref[pl.ds(j, rows, stride=D_TILES), :] for j in range(D_TILES)], axis=1)


def _store_matmul_tiles(flat_ref, x):
    for a in range(x.shape[0] // SUBLANES):
        for j in range(D_TILES):
            r0 = (a * D_TILES + j) * SUBLANES
            flat_ref[r0:r0 + SUBLANES, :] = x[a * SUBLANES:(a + 1) * SUBLANES, j * LANES:(j + 1) * LANES]


def _load_row_tile(flat_ref, m):
    start = (m // SUBLANES) * SUBLANES * D_TILES + m % SUBLANES
    return flat_ref[pl.ds(start, D_TILES, stride=SUBLANES), :]


def _mix_kernel(x_ref, oa_ref, ob_ref, ga_ref, gb_ref, lng_ref, lnb_ref, wa_ref, wb_ref, wo_ref,
                g1_ref, b1_ref, rw_ref, x1_ref, x1t_ref, scores_ref, *, alpha):
    xn = _layer_norm(x_ref[...], lng_ref[...], lnb_ref[...])
    mix = (ga_ref[...].astype(F32) * _dot(oa_ref[...], wa_ref[...])
           + gb_ref[...].astype(F32) * _dot(ob_ref[...], wb_ref[...]))
    x1 = _layer_norm(alpha * xn + _dot(mix.astype(BF16), wo_ref[...]), g1_ref[...], b1_ref[...])
    x1_ref[...] = x1
    _store_row_major_tiles(x1t_ref, x1)
    scores_ref[...] = jax.nn.sigmoid(_dot_nt(rw_ref[...], x1, precision=lax.Precision.HIGHEST))


def _mix(x2d, oa, ob, ga, gb, ln_g, ln_b, wa, wb, wo, g1, b1, rw_t, *, alpha):
    n_rows = x2d.shape[0]
    tile = ROW_TILE
    row = lambda width: pl.BlockSpec((tile, width), lambda i: (i, 0))
    return pl.pallas_call(
        functools.partial(_mix_kernel, alpha=alpha),
        grid=(n_rows // tile,),
        in_specs=[row(D_MODEL), row(A_WIDTH), row(MLP_WIDTH), row(D_MODEL), row(D_MODEL),
                  _const_spec((1, D_MODEL)), _const_spec((1, D_MODEL)),
                  _const_spec(wa.shape), _const_spec(wb.shape), _const_spec(wo.shape),
                  _const_spec((1, D_MODEL)), _const_spec((1, D_MODEL)),
                  _const_spec(rw_t.shape)],
        out_specs=(row(D_MODEL), pl.BlockSpec((tile * D_TILES, LANES), lambda i: (i, 0)),
                   pl.BlockSpec((N_EXPERTS, tile), lambda i: (0, i))),
        out_shape=(jax.ShapeDtypeStruct((n_rows, D_MODEL), F32),
                   jax.ShapeDtypeStruct((n_rows * D_TILES, LANES), F32),
                   jax.ShapeDtypeStruct((N_EXPERTS, n_rows), F32)),
        compiler_params=_params(1), name="mix",
    )(x2d, oa, ob, ga, gb, ln_g, ln_b, wa, wb, wo, g1, b1, rw_t)


def _route_kernel(scores_ref, rb_ref, ids_ref, wts_ref):
    scores = scores_ref[...]
    ids, weights = _route(scores, scores + rb_ref[...])
    ids_ref[...] = ids
    wts_ref[...] = weights


def _route_call(scores, rb):
    n_tok = scores.shape[1]
    tile = min(ROUTE_TILE, n_tok)
    col = lambda rows: pl.BlockSpec((rows, tile), lambda i: (0, i))
    return pl.pallas_call(
        _route_kernel,
        grid=(n_tok // tile,),
        in_specs=[col(N_EXPERTS), _const_spec(rb.shape)],
        out_specs=(col(TOP_K), col(TOP_K)),
        out_shape=(jax.ShapeDtypeStruct((TOP_K, n_tok), jnp.int32),
                   jax.ShapeDtypeStruct((TOP_K, n_tok), F32)),
        compiler_params=_params(1), name="route",
    )(scores, rb)


def _moe_kernel(rows_ref, start_ref, x_ref, wts_ref, w1_ref, w3_ref, w2_ref, *rest):
    out_ref, gat_ref, y_ref, pending_ref = rest[-4:]
    acc = out_ref.at[0]
    e = pl.program_id(0)

    @pl.when(e == 0)
    def _():
        out_ref[...] = jnp.zeros(out_ref.shape, out_ref.dtype)
        y_ref[...] = jnp.zeros(y_ref.shape, y_ref.dtype)
        pending_ref[0] = 0

    first = start_ref[e]
    count = start_ref[e + 1] - first

    def tile_at(ref, row8):
        return ref.at[pl.ds(pl.multiple_of(row8, SUBLANES), SUBLANES), :]

    row_id = lax.broadcasted_iota(jnp.int32, (MOE_ROWS, 2 * LANES), 0)
    lane_id = lax.broadcasted_iota(jnp.int32, (MOE_ROWS, 2 * LANES), 1)

    def scatter_add(base):
        for m0 in range(0, MOE_ROWS, SCATTER_BATCH):
            updates = []
            for m in range(m0, m0 + SCATTER_BATCH):
                dst = tile_at(acc, rows_ref[base + m])
                updates.append((dst, dst[...] + _load_row_tile(y_ref, m)))
            for dst, val in updates:
                dst[...] = val

    def block(b, pending):
        nominal = b * MOE_ROWS
        begin = jnp.minimum(nominal, count - MOE_ROWS)
        redo = nominal - begin
        base = first + begin + MOE_ROWS
        for m in range(MOE_ROWS):
            gat_ref[m * SUBLANES:(m + 1) * SUBLANES, :] = tile_at(x_ref, rows_ref[base + m])[...]
        xb = _load_row_major_tiles(gat_ref, MOE_ROWS).astype(BF16)
        act = jax.nn.silu(_dot(xb, w1_ref[0])) * _dot(xb, w3_ref[0])
        y = _dot(act.astype(BF16), w2_ref[0])

        w_row = base // LANES
        pair = jnp.concatenate([wts_ref[w_row], wts_ref[w_row + 1]], axis=1)
        picked = jnp.where((lane_id == row_id + base % LANES) & (row_id >= redo),
                           jnp.broadcast_to(pair, lane_id.shape), 0.0)
        ys = y * jnp.sum(picked, axis=1, keepdims=True)
        scatter_add(pending)
        _store_matmul_tiles(y_ref, ys)
        return base

    pending = lax.fori_loop(0, pl.cdiv(count, MOE_ROWS), block, pending_ref[0])
    pending_ref[0] = pending

    @pl.when(e == pl.num_programs(0) - 1)
    def _():
        scatter_add(pending)


def _moe(ids, weights, x_tiles, w1, w3, w2):
    n_tok = ids.shape[1]
    g_tok = min(MOE_GROUP_TOKENS, n_tok)
    n_groups = n_tok // g_tok
    assert n_groups * g_tok == n_tok
    per_group = g_tok * TOP_K
    tok = lax.broadcasted_iota(jnp.int32, ids.shape, 1)
    key = ((tok // g_tok) * N_EXPERTS + ids) * g_tok + tok % g_tok
    key, wts = lax.sort((key.reshape(-1), weights.reshape(-1)), num_keys=1)
    rows8 = ((key % g_tok) * D_TILES).reshape(n_groups, per_group)
    wts = wts.reshape(n_groups, per_group)
    experts = jnp.arange(N_EXPERTS, dtype=jnp.int32)[None, None, :, None]
    counts = jnp.sum(ids.reshape(TOP_K, n_groups, 1, g_tok) == experts, axis=(0, 3), dtype=jnp.int32)
    start = jnp.concatenate([jnp.zeros((n_groups, 1), jnp.int32), jnp.cumsum(counts, axis=1)], axis=1)
    pad = ((0, 0), (MOE_ROWS, 2 * LANES))
    rows8 = jnp.pad(rows8, pad)
    wts = jnp.pad(wts, pad).reshape(n_groups, -1, 1, LANES)
    out = None
    for g in range(n_groups):
        out = _moe_group_call(g, n_groups, g_tok, rows8[g], start[g], x_tiles, wts[g], w1, w3, w2, out)
    return out


def _moe_group_call(group, n_groups, g_tok, rows8, start, x_tiles, wts, w1, w3, w2, prev):
    g_rows = g_tok * D_TILES
    assert x_tiles.shape[0] == n_groups * g_rows
    out_block = (1, g_rows, LANES)
    in_specs = [
        pl.BlockSpec((g_rows, LANES), lambda e, *_: (group, 0)),
        pl.BlockSpec(wts.shape, lambda e, *_: (0, 0, 0)),
        pl.BlockSpec((1, D_MODEL, EXPERT_DIM), lambda e, *_: (e, 0, 0)),
        pl.BlockSpec((1, D_MODEL, EXPERT_DIM), lambda e, *_: (e, 0, 0)),
        pl.BlockSpec((1, EXPERT_DIM, D_MODEL), lambda e, *_: (e, 0, 0)),
    ]
    args = [rows8, start, x_tiles, wts, w1, w3, w2]
    aliases = {}
    if prev is not None:
        in_specs.append(pl.BlockSpec(memory_space=pl.ANY))
        aliases = {len(args): 0}
        args.append(prev)
    return pl.pallas_call(
        _moe_kernel,
        grid_spec=pltpu.PrefetchScalarGridSpec(
            num_scalar_prefetch=2, grid=(N_EXPERTS,), in_specs=in_specs,
            out_specs=pl.BlockSpec(out_block, lambda e, *_: (group, 0, 0)),
            scratch_shapes=[pltpu.VMEM((MOE_ROWS * D_TILES, LANES), F32),
                            pltpu.VMEM((MOE_ROWS * D_TILES, LANES), F32),
                            pltpu.SMEM((1,), jnp.int32)]),
        out_shape=jax.ShapeDtypeStruct((n_groups,) + out_block[1:], F32),
        input_output_aliases=aliases,
        compiler_params=_params(1), name="moe",
    )(*args)


def _final_kernel(x1_ref, routed_ref, p_ref, s1_ref, s3_ref, s2_ref, g2_ref, b2_ref, wg_ref, wp_ref,
                  o_ref, *, alpha):
    x1 = x1_ref[...]
    xb = x1.astype(BF16)
    act = jax.nn.silu(_dot(xb, s1_ref[...])) * _dot(xb, s3_ref[...])
    ff = _load_row_major_tiles(routed_ref.at[0], x1.shape[0]) + _dot(act.astype(BF16), s2_ref[...])
    x2 = _layer_norm(alpha * x1 + ff, g2_ref[...], b2_ref[...])
    gate = jax.nn.sigmoid(_dot(x2.astype(BF16), wg_ref[...]))
    o_ref[...] = x2 + gate * _dot(p_ref[...].astype(BF16), wp_ref[...])


def _final(x1, routed, p2d, s1, s3, s2, g2, b2, wg, wp, *, alpha):
    n_rows = x1.shape[0]
    tile = ROW_TILE
    row = lambda width: pl.BlockSpec((tile, width), lambda i: (i, 0))
    tiles_per_group = routed.shape[1] // D_TILES // tile
    routed_spec = pl.BlockSpec((1, tile * D_TILES, LANES),
                               lambda i: (i // tiles_per_group, i % tiles_per_group, 0))
    return pl.pallas_call(
        functools.partial(_final_kernel, alpha=alpha),
        grid=(n_rows // tile,),
        in_specs=[row(D_MODEL), routed_spec,
                  row(p2d.shape[1]),
                  _const_spec(s1.shape), _const_spec(s3.shape), _const_spec(s2.shape),
                  _const_spec((1, D_MODEL)), _const_spec((1, D_MODEL)),
                  _const_spec(wg.shape), _const_spec(wp.shape)],
        out_specs=row(D_MODEL),
        out_shape=jax.ShapeDtypeStruct((n_rows, D_MODEL), F32),
        compiler_params=_params(1), name="final",
    )(x1, routed, p2d, s1, s3, s2, g2, b2, wg, wp)


def _trunk_layer(x, p, cache_k, cache_v, lw, *, alpha, first_layer_ln):
    batch, seq, d = x.shape
    n_rows = batch * seq
    x2d = x.reshape(n_rows, d)
    prompt = cache_k is None
    mlp_len = min(seq, MLP_CHUNK)
    keep = min(PAST_CHUNKS * CHUNK, seq) if prompt else seq
    ln_g, ln_b = first_layer_ln

    w_s = (lw["mlp_w_s"][:, :mlp_len, :mlp_len]
           * jnp.tril(jnp.ones((mlp_len, mlp_len), F32))).astype(BF16)
    b_s = jnp.repeat(lw["mlp_b_s"][:, :mlp_len].T, MLP_WIDTH // MLP_GROUPS, axis=1)
    q, k, v, kf, vf, vn, ga, gb, ob = _proj(
        x2d, ln_g, ln_b, lw["w_in"].astype(BF16), lw["mlp_ln_g"][None], lw["mlp_ln_b"][None],
        w_s, b_s, seq=seq, mlp_len=mlp_len, keep_rows=keep)
    if prompt:
        oa = _attn_prompt(q, k, v, lw["attn_rel_bias"], batch=batch, seq=seq)
    else:
        oa = _attn_sample(q, k, v, cache_k, cache_v, lw["attn_rel_bias"], batch=batch, seq=seq)
    x1, x1_tiles, scores = _mix(
        x2d, oa, ob, ga, gb, ln_g, ln_b, lw["w_branch_a"].astype(BF16), lw["w_branch_b"].astype(BF16),
        lw["w_out"].astype(BF16), lw["ln1_g"][None], lw["ln1_b"][None], lw["router_w"].T, alpha=alpha)
    ids, weights = _route_call(scores, lw["router_bias"][:, None])
    routed = _moe(ids, weights, x1_tiles, lw["exp_w1"], lw["exp_w3"], lw["exp_w2"])
    y = _final(x1, routed, p.reshape(n_rows, -1), lw["shared_w1"].astype(BF16),
               lw["shared_w3"].astype(BF16), lw["shared_w2"].astype(BF16),
               lw["ln2_g"][None], lw["ln2_b"][None], lw["ple_w_gate"].astype(BF16),
               lw["ple_w_proj"].astype(BF16), alpha=alpha)
    new_k = kf.reshape(batch, keep, HEADS, HEAD_DIM)
    new_v = vf.reshape(batch, keep, HEADS, HEAD_DIM)
    new_mlp_v = vn.reshape(batch, mlp_len, MLP_WIDTH)
    return y.reshape(batch, seq, d), new_k, new_v, new_mlp_v


def kernel(x_prompt, x_sample, cache_attn_k, cache_attn_v, p_prompt, p_sample, ln_in_g, ln_in_b, w_in, attn_rel_bias, mlp_ln_g, mlp_ln_b, mlp_w_s, mlp_b_s, w_branch_a, w_branch_b, w_out, ln1_g, ln1_b, router_w, router_bias, exp_w1, exp_w3, exp_w2, shared_w1, shared_w3, shared_w2, ln2_g, ln2_b, ple_w_gate, ple_w_proj):
    depth = w_in.shape[0]
    assert depth == 1, "the fused proj/mix kernels apply LayerNorm_in themselves: single layer only"
    alpha = (2 * depth) ** 0.25
    stacked = dict(w_in=w_in, attn_rel_bias=attn_rel_bias, mlp_ln_g=mlp_ln_g, mlp_ln_b=mlp_ln_b,
                   mlp_w_s=mlp_w_s, mlp_b_s=mlp_b_s, w_branch_a=w_branch_a, w_branch_b=w_branch_b,
                   w_out=w_out, ln1_g=ln1_g, ln1_b=ln1_b, router_w=router_w, router_bias=router_bias,
                   exp_w1=exp_w1, exp_w3=exp_w3, exp_w2=exp_w2, shared_w1=shared_w1,
                   shared_w3=shared_w3, shared_w2=shared_w2, ln2_g=ln2_g, ln2_b=ln2_b,
                   ple_w_gate=ple_w_gate, ple_w_proj=ple_w_proj)
    lw = {name: value[0] for name, value in stacked.items()}
    for name in ("exp_w1", "exp_w3", "exp_w2"):
        lw[name] = lw[name].astype(BF16)
    ln_in =(ln_in_g[None], ln_in_b[None])
    yp, kp, vp, mp = _trunk_layer(x_prompt, p_prompt[0], None, None, lw, alpha=alpha, first_layer_ln=ln_in)
    ys, ks, vs, ms = _trunk_layer(x_sample, p_sample[0], cache_attn_k[0], cache_attn_v[0], lw,
                                  alpha=alpha, first_layer_ln=ln_in)
    return (yp, ys, kp[None], vp[None], ks[None], vs[None], mp[None], ms[None])
```

```python
import functools

import jax
import jax.numpy as jnp
import numpy as np
from jax import lax
from jax.experimental import pallas as pl
from jax.experimental.pallas import tpu as pltpu

F32 = jnp.float32
BF16 = jnp.bfloat16

D_MODEL = 1024
CHUNK = 64
HEADS = 8
HEAD_DIM = 64
A_WIDTH = HEADS * HEAD_DIM
PAST_CHUNKS = 8
MAX_REL = 256
MLP_CHUNK = 128
MLP_GROUPS = 8
MLP_WIDTH = 512
N_EXPERTS = 64
TOP_K = 8
N_GROUPS = 8
TOPK_GROUPS = 4
GROUP_SIZE = N_EXPERTS // N_GROUPS
EXPERT_DIM = 256
ROUTED_SCALE = 2.5
LN_EPS = 1e-5
MASKED = -1e30

LANES = 128
SUBLANES = 8
VMEM_LIMIT_BYTES = 56 * 1024 * 1024

ROW_TILE = 512
ROUTE_TILE = 2048
Q_CHUNKS = 4
MOE_GROUP_TOKENS = 4096
SCATTER_BATCH = 16
D_TILES = D_MODEL // LANES


def _layer_norm(x, g, b):
    xc = x - jnp.mean(x, axis=-1, keepdims=True)
    var = jnp.mean(xc * xc, axis=-1, keepdims=True)
    return xc * lax.rsqrt(var + LN_EPS) * g + b


def _dot(a, b):
    return jnp.dot(a, b, preferred_element_type=F32)


def _dot_nt(a, b, precision=None):
    return lax.dot_general(a, b, (((1,), (1,)), ((), ())),
                           preferred_element_type=F32, precision=precision)


def _const_spec(shape):
    zeros = (0,) * len(shape)
    return pl.BlockSpec(shape, lambda *_: zeros)


def _params(n_axes):
    return pltpu.CompilerParams(dimension_semantics=("arbitrary",) * n_axes,
                                vmem_limit_bytes=VMEM_LIMIT_BYTES)


def _proj_kernel(x_ref, lng_ref, lnb_ref, w_ref, mg_ref, mb_ref, ws_ref, bs_ref,
                 q_ref, k_ref, v_ref, kf_ref, vf_ref, vn_ref, ga_ref, gb_ref, ob_ref,
                 *, mlp_len, vn_rows):
    rows = x_ref.shape[0]
    xn = _layer_norm(x_ref[...], lng_ref[...], lnb_ref[...]).astype(BF16)

    def section(lo, width):
        return _dot(xn, w_ref[:, lo:lo + width])

    q = section(0, A_WIDTH)
    q_ref[...] = (q * (HEAD_DIM ** -0.5)).astype(BF16)
    k = section(A_WIDTH, A_WIDTH)
    k_ref[...] = k.astype(BF16)
    kf_ref[...] = k
    v = section(2 * A_WIDTH, A_WIDTH)
    v_ref[...] = v.astype(BF16)
    vf_ref[...] = v
    base = 3 * A_WIDTH
    un = jax.nn.gelu(section(base, MLP_WIDTH))
    vn = _layer_norm(jax.nn.gelu(section(base + MLP_WIDTH, MLP_WIDTH)), mg_ref[...], mb_ref[...])
    vn_ref[...] = vn[rows - vn_rows:, :]
    base += 2 * MLP_WIDTH
    ga_ref[...] = jax.nn.sigmoid(section(base, D_MODEL)).astype(BF16)
    gb_ref[...] = jax.nn.sigmoid(section(base + D_MODEL, D_MODEL)).astype(BF16)

    vnb = vn.astype(BF16)
    low_half = lax.broadcasted_iota(jnp.int32, (mlp_len, LANES), 1) < (MLP_WIDTH // MLP_GROUPS)
    for c in range(rows // mlp_len):
        r0 = c * mlp_len
        for p in range(MLP_GROUPS // 2):
            c0 = p * LANES
            slab = vnb[r0:r0 + mlp_len, c0:c0 + LANES]
            mixed = jnp.where(low_half, _dot(ws_ref[2 * p], slab), _dot(ws_ref[2 * p + 1], slab))
            gated = un[r0:r0 + mlp_len, c0:c0 + LANES] * (mixed + bs_ref[:, c0:c0 + LANES])
            ob_ref[r0:r0 + mlp_len, c0:c0 + LANES] = gated.astype(BF16)


def _proj(x2d, ln_g, ln_b, w_in, mlp_g, mlp_b, w_s, b_s, *, seq, mlp_len, keep_rows):
    n_rows = x2d.shape[0]
    tile = ROW_TILE
    assert n_rows % tile == 0 and tile % mlp_len == 0
    n_seq = n_rows // seq
    if seq >= tile:
        assert seq % tile == 0 and keep_rows == tile
        per_seq = seq // tile
        kv_rows, vn_rows = n_seq * tile, mlp_len
        kv_map = lambda i: (i // per_seq, 0)
    else:
        assert keep_rows == seq and mlp_len == seq
        kv_rows, vn_rows = n_rows, tile
        kv_map = lambda i: (i, 0)
    row = lambda width: pl.BlockSpec((tile, width), lambda i: (i, 0))
    n_in = w_in.shape[1]
    out_shape = (
        jax.ShapeDtypeStruct((n_rows, A_WIDTH), BF16),
        jax.ShapeDtypeStruct((n_rows, A_WIDTH), BF16),
        jax.ShapeDtypeStruct((n_rows, A_WIDTH), BF16),
        jax.ShapeDtypeStruct((kv_rows, A_WIDTH), F32),
        jax.ShapeDtypeStruct((kv_rows, A_WIDTH), F32),
        jax.ShapeDtypeStruct((kv_rows // tile * vn_rows, MLP_WIDTH), F32),
        jax.ShapeDtypeStruct((n_rows, D_MODEL), BF16),
        jax.ShapeDtypeStruct((n_rows, D_MODEL), BF16),
        jax.ShapeDtypeStruct((n_rows, MLP_WIDTH), BF16),
    )
    out_specs = (
        row(A_WIDTH), row(A_WIDTH), row(A_WIDTH),
        pl.BlockSpec((tile, A_WIDTH), kv_map), pl.BlockSpec((tile, A_WIDTH), kv_map),
        pl.BlockSpec((vn_rows, MLP_WIDTH), kv_map),
        row(D_MODEL), row(D_MODEL), row(MLP_WIDTH),
    )
    in_specs = [
        row(D_MODEL), _const_spec((1, D_MODEL)), _const_spec((1, D_MODEL)),
        _const_spec((D_MODEL, n_in)), _const_spec((1, MLP_WIDTH)), _const_spec((1, MLP_WIDTH)),
        _const_spec((MLP_GROUPS, mlp_len, mlp_len)), _const_spec((mlp_len, MLP_WIDTH)),
    ]
    return pl.pallas_call(
        functools.partial(_proj_kernel, mlp_len=mlp_len, vn_rows=vn_rows),
        grid=(n_rows // tile,), in_specs=in_specs, out_specs=out_specs, out_shape=out_shape,
        compiler_params=_params(1), name="proj",
    )(x2d, ln_g, ln_b, w_in, mlp_g, mlp_b, w_s, b_s)


def _attend_heads(q, keys, values, bias_ref, extra_mask, o_ref):
    n_q = q.shape[0]
    low_half = lax.broadcasted_iota(jnp.int32, (n_q, LANES), 1) < HEAD_DIM
    zero = jnp.zeros((), BF16)
    for p in range(HEADS // 2):
        c0 = p * LANES
        q2, k2, v2 = q[:, c0:c0 + LANES], keys[:, c0:c0 + LANES], values[:, c0:c0 + LANES]
        qs = jnp.concatenate([jnp.where(low_half, q2, zero), jnp.where(low_half, zero, q2)], axis=0)
        s = _dot_nt(qs, k2) + bias_ref[2 * p:2 * p + 2].reshape(2 * n_q, keys.shape[0])
        if extra_mask is not None:
            s = s + extra_mask
        e = jnp.exp(s - jnp.max(s, axis=-1, keepdims=True))
        denom = jnp.sum(e, axis=-1, keepdims=True)
        o = _dot(e.astype(BF16), v2) / denom
        o_ref[:, c0:c0 + LANES] = jnp.where(low_half, o[:n_q], o[n_q:]).astype(o_ref.dtype)


def _attn_prompt_kernel(q_ref, k_ref, v_ref, bias_ref, o_ref, *, window):
    n_q = q_ref.shape[1]
    start = pl.multiple_of(pl.program_id(1) * n_q, n_q)
    keys = k_ref[0, pl.ds(start, window), :]
    values = v_ref[0, pl.ds(start, window), :]
    col = lax.broadcasted_iota(jnp.int32, (1, window), 1)
    pad_mask = jnp.where(col + start >= PAST_CHUNKS * CHUNK, 0.0, MASKED).astype(F32)
    _attend_heads(q_ref[0], keys, values, bias_ref, pad_mask, o_ref.at[0])


def _relative_bias(rel_table, n_q, n_k, key_offset):
    period = pl.next_power_of_2(n_q + n_k)
    m = np.arange(period)
    diag = np.where(m < n_k, m, m - period)
    idx = np.clip(diag - key_offset, -MAX_REL, MAX_REL) + MAX_REL
    values = rel_table[:, idx].astype(F32)
    rows = jnp.tile(values, (1, n_q))[:, :n_q * (period - 1)]
    return rows.reshape(rel_table.shape[0], n_q, period - 1)[:, :, :n_k]


def _prompt_bias(rel_table):
    n_q, n_k = Q_CHUNKS * CHUNK, (Q_CHUNKS + PAST_CHUNKS) * CHUNK
    lag = np.arange(n_q)[:, None] // CHUNK - (np.arange(n_k)[None, :] // CHUNK - PAST_CHUNKS)
    in_band = (lag >= 0) & (lag <= PAST_CHUNKS)
    bias = _relative_bias(rel_table, n_q, n_k, PAST_CHUNKS * CHUNK)
    return jnp.where(in_band[None], bias, MASKED)


def _attn_prompt(q, k, v, rel_table, *, batch, seq):
    n_q = Q_CHUNKS * CHUNK
    window = (Q_CHUNKS + PAST_CHUNKS) * CHUNK
    pad = PAST_CHUNKS * CHUNK
    q3 = q.reshape(batch, seq, A_WIDTH)
    kp = jnp.pad(k.reshape(batch, seq, A_WIDTH), ((0, 0), (pad, 0), (0, 0)))
    vp = jnp.pad(v.reshape(batch, seq, A_WIDTH), ((0, 0), (pad, 0), (0, 0)))
    bias = _prompt_bias(rel_table)
    qspec = pl.BlockSpec((1, n_q, A_WIDTH), lambda b, i: (b, i, 0))
    kvspec = pl.BlockSpec((1, seq + pad, A_WIDTH), lambda b, i: (b, 0, 0))
    out = pl.pallas_call(
        functools.partial(_attn_prompt_kernel, window=window),
        grid=(batch, seq // n_q),
        in_specs=[qspec, kvspec, kvspec, _const_spec(bias.shape)],
        out_specs=qspec,
        out_shape=jax.ShapeDtypeStruct((batch, seq, A_WIDTH), BF16),
        compiler_params=_params(2), name="attn_prompt",
    )(q3, kp, vp, bias)
    return out.reshape(batch * seq, A_WIDTH)


def _attn_sample_kernel(q_ref, k_ref, v_ref, ck_ref, cv_ref, bias_ref, o_ref, kk_ref, vv_ref):
    n_cache = ck_ref.shape[1]
    n_new = k_ref.shape[1]
    kk_ref[0:n_cache, :] = ck_ref[0].astype(BF16)
    kk_ref[n_cache:n_cache + n_new, :] = k_ref[0]
    vv_ref[0:n_cache, :] = cv_ref[0].astype(BF16)
    vv_ref[n_cache:n_cache + n_new, :] = v_ref[0]
    _attend_heads(q_ref[0], kk_ref[...], vv_ref[...], bias_ref, None, o_ref.at[0])


def _attn_sample(q, k, v, cache_k, cache_v, rel_table, *, batch, seq):
    n_cache = cache_k.shape[1]
    bias = _relative_bias(rel_table, seq, n_cache + seq, n_cache)
    new = pl.BlockSpec((1, seq, A_WIDTH), lambda b: (b, 0, 0))
    old = pl.BlockSpec((1, n_cache, A_WIDTH), lambda b: (b, 0, 0))
    out = pl.pallas_call(
        _attn_sample_kernel,
        grid=(batch,),
        in_specs=[new, new, new, old, old, _const_spec(bias.shape)],
        out_specs=new,
        out_shape=jax.ShapeDtypeStruct((batch, seq, A_WIDTH), BF16),
        scratch_shapes=[pltpu.VMEM((n_cache + seq, A_WIDTH), BF16),
                        pltpu.VMEM((n_cache + seq, A_WIDTH), BF16)],
        compiler_params=_params(1), name="attn_sample",
    )(q.reshape(batch, seq, A_WIDTH), k.reshape(batch, seq, A_WIDTH), v.reshape(batch, seq, A_WIDTH),
      cache_k.reshape(batch, n_cache, A_WIDTH), cache_v.reshape(batch, n_cache, A_WIDTH), bias)
    return out.reshape(batch * seq, A_WIDTH)


def _first_index(hit, iota, axis, limit):
    return jnp.min(jnp.where(hit, iota, limit), axis=axis, keepdims=True)


def _route(scores, sel):
    n_tok = scores.shape[1]
    neg = -jnp.inf
    grouped = sel.reshape(N_GROUPS, GROUP_SIZE, n_tok)
    member = lax.broadcasted_iota(jnp.int32, grouped.shape, 1)
    best = jnp.max(grouped, axis=1, keepdims=True)
    first = _first_index(grouped == best, member, 1, GROUP_SIZE)
    second = jnp.max(jnp.where(member == first, neg, grouped), axis=1, keepdims=True)
    group_score = best + second

    group_id = lax.broadcasted_iota(jnp.int32, group_score.shape, 0)
    keep = jnp.zeros(group_score.shape, F32)
    for _ in range(TOPK_GROUPS):
        top = jnp.max(group_score, axis=0, keepdims=True)
        hit = group_id == _first_index(group_score == top, group_id, 0, N_GROUPS)
        keep = jnp.where(hit, 1.0, keep)
        group_score = jnp.where(hit, neg, group_score)
    keep = jnp.broadcast_to(keep, grouped.shape).reshape(N_EXPERTS, n_tok)

    cand = jnp.where(keep > 0.0, sel, neg)
    expert_id = lax.broadcasted_iota(jnp.int32, cand.shape, 0)
    ids, weights = [], []
    for _ in range(TOP_K):
        top = jnp.max(cand, axis=0, keepdims=True)
        first = _first_index(cand == top, expert_id, 0, N_EXPERTS)
        hit = expert_id == first
        ids.append(first)
        weights.append(jnp.sum(jnp.where(hit, scores, 0.0), axis=0, keepdims=True))
        cand = jnp.where(hit, neg, cand)
    ids = jnp.concatenate(ids, axis=0)
    weights = jnp.concatenate(weights, axis=0)
    weights = weights / jnp.sum(weights, axis=0, keepdims=True) * ROUTED_SCALE
    return ids, weights


def _store_row_major_tiles(flat_ref, x):
    rows = x.shape[0]
    for j in range(D_TILES):
        flat_ref[pl.ds(j, rows, stride=D_TILES), :] = x[:, j * LANES:(j + 1) * LANES]


def _load_row_major_tiles(flat_ref, rows):
    return jnp.concatenate(
        [flat_ref[pl.ds(j, rows, stride=D_TILES), :] for j in range(D_TILES)], axis=1)


def _store_matmul_tiles(flat_ref, x):
    for a in range(x.shape[0] // SUBLANES):
        for j in range(D_TILES):
            r0 = (a * D_TILES + j) * SUBLANES
            flat_ref[r0:r0 + SUBLANES, :] = x[a * SUBLANES:(a + 1) * SUBLANES, j * LANES:(j + 1) * LANES]


def _load_row_tile(flat_ref, m):
    start = (m // SUBLANES) * SUBLANES * D_TILES + m % SUBLANES
    return flat_ref[pl.ds(start, D_TILES, stride=SUBLANES), :]


def _mix_kernel(x_ref, oa_ref, ob_ref, ga_ref, gb_ref, lng_ref, lnb_ref, wa_ref, wb_ref, wo_ref,
                g1_ref, b1_ref, rw_ref, x1_ref, x1t_ref, scores_ref, *, alpha):
    xn = _layer_norm(x_ref[...], lng_ref[...], lnb_ref[...])
    mix = (ga_ref[...].astype(F32) * _dot(oa_ref[...], wa_ref[...])
           + gb_ref[...].astype(F32) * _dot(ob_ref[...], wb_ref[...]))
    x1 = _layer_norm(alpha * xn + _dot(mix.astype(BF16), wo_ref[...]), g1_ref[...], b1_ref[...])
    x1_ref[...] = x1
    _store_row_major_tiles(x1t_ref, x1)
    scores_ref[...] = jax.nn.sigmoid(_dot_nt(rw_ref[...], x1, precision=lax.Precision.HIGHEST))


def _mix(x2d, oa, ob, ga, gb, ln_g, ln_b, wa, wb, wo, g1, b1, rw_t, *, alpha):
    n_rows = x2d.shape[0]
    tile = ROW_TILE
    row = lambda width: pl.BlockSpec((tile, width), lambda i: (i, 0))
    return pl.pallas_call(
        functools.partial(_mix_kernel, alpha=alpha),
        grid=(n_rows // tile,),
        in_specs=[row(D_MODEL), row(A_WIDTH), row(MLP_WIDTH), row(D_MODEL), row(D_MODEL),
                  _const_spec((1, D_MODEL)), _const_spec((1, D_MODEL)),
                  _const_spec(wa.shape), _const_spec(wb.shape), _const_spec(wo.shape),
                  _const_spec((1, D_MODEL)), _const_spec((1, D_MODEL)),
                  _const_spec(rw_t.shape)],
        out_specs=(row(D_MODEL), pl.BlockSpec((tile * D_TILES, LANES), lambda i: (i, 0)),
                   pl.BlockSpec((N_EXPERTS, tile), lambda i: (0, i))),
        out_shape=(jax.ShapeDtypeStruct((n_rows, D_MODEL), F32),
                   jax.ShapeDtypeStruct((n_rows * D_TILES, LANES), F32),
                   jax.ShapeDtypeStruct((N_EXPERTS, n_rows), F32)),
        compiler_params=_params(1), name="mix",
    )(x2d, oa, ob, ga, gb, ln_g, ln_b, wa, wb, wo, g1, b1, rw_t)


def _route_kernel(scores_ref, rb_ref, ids_ref, wts_ref):
    scores = scores_ref[...]
    ids, weights = _route(scores, scores + rb_ref[...])
    ids_ref[...] = ids
    wts_ref[...] = weights


def _route_call(scores, rb):
    n_tok = scores.shape[1]
    tile = min(ROUTE_TILE, n_tok)
    col = lambda rows: pl.BlockSpec((rows, tile), lambda i: (0, i))
    return pl.pallas_call(
        _route_kernel,
        grid=(n_tok // tile,),
        in_specs=[col(N_EXPERTS), _const_spec(rb.shape)],
        out_specs=(col(TOP_K), col(TOP_K)),
        out_shape=(jax.ShapeDtypeStruct((TOP_K, n_tok), jnp.int32),
                   jax.ShapeDtypeStruct((TOP_K, n_tok), F32)),
        compiler_params=_params(1), name="route",
    )(scores, rb)


def _moe_kernel(rows_ref, start_ref, x_ref, wts_ref, w1_ref, w3_ref, w2_ref, *rest, n_rows):
    out_ref, gat_ref, y_ref, pending_ref = rest[-4:]
    acc = out_ref.at[0]
    e = pl.program_id(0)

    @pl.when(e == 0)
    def _():
        out_ref[...] = jnp.zeros(out_ref.shape, out_ref.dtype)
        y_ref[...] = jnp.zeros(y_ref.shape, y_ref.dtype)
        pending_ref[0] = 0

    first = start_ref[e]
    count = start_ref[e + 1] - first

    def tile_at(ref, row8):
        return ref.at[pl.ds(pl.multiple_of(row8, SUBLANES), SUBLANES), :]

    w_rows = n_rows // LANES + 1
    row_id = lax.broadcasted_iota(jnp.int32, (n_rows, w_rows * LANES), 0)
    lane_id = lax.broadcasted_iota(jnp.int32, (n_rows, w_rows * LANES), 1)

    def scatter_add(base):
        for m0 in range(0, n_rows, SCATTER_BATCH):
            updates = []
            for m in range(m0, m0 + SCATTER_BATCH):
                dst = tile_at(acc, rows_ref[base + m])
                updates.append((dst, dst[...] + _load_row_tile(y_ref, m)))
            for dst, val in updates:
                dst[...] = val

    def block(b, pending):
        nominal = b * n_rows
        begin = jnp.minimum(nominal, count - n_rows)
        redo = nominal - begin
        base = first + begin + n_rows
        for m in range(n_rows):
            gat_ref[m * SUBLANES:(m + 1) * SUBLANES, :] = tile_at(x_ref, rows_ref[base + m])[...]
        xb = _load_row_major_tiles(gat_ref, n_rows).astype(BF16)
        act = jax.nn.silu(_dot(xb, w1_ref[0])) * _dot(xb, w3_ref[0])
        y = _dot(act.astype(BF16), w2_ref[0])

        w_row = base // LANES
        span = jnp.concatenate([wts_ref[w_row + r] for r in range(w_rows)], axis=1)
        picked = jnp.where((lane_id == row_id + base % LANES) & (row_id >= redo),
                           jnp.broadcast_to(span, lane_id.shape), 0.0)
        ys = y * jnp.sum(picked, axis=1, keepdims=True)
        scatter_add(pending)
        _store_matmul_tiles(y_ref, ys)
        return base

    pending = lax.fori_loop(0, pl.cdiv(count, n_rows), block, pending_ref[0])
    pending_ref[0] = pending

    @pl.when(e == pl.num_programs(0) - 1)
    def _():
        scatter_add(pending)


def _moe(ids, weights, x_tiles, w1, w3, w2):
    n_tok = ids.shape[1]
    g_tok = min(MOE_GROUP_TOKENS, n_tok)
    n_groups = n_tok // g_tok
    assert n_groups * g_tok == n_tok
    per_group = g_tok * TOP_K
    tok = lax.broadcasted_iota(jnp.int32, ids.shape, 1)
    key = ((tok // g_tok) * N_EXPERTS + ids) * g_tok + tok % g_tok
    key, wts = lax.sort((key.reshape(-1), weights.reshape(-1)), num_keys=1)
    rows8 = ((key % g_tok) * D_TILES).reshape(n_groups, per_group)
    wts = wts.reshape(n_groups, per_group)
    experts = jnp.arange(N_EXPERTS, dtype=jnp.int32)[None, None, :, None]
    counts = jnp.sum(ids.reshape(TOP_K, n_groups, 1, g_tok) == experts, axis=(0, 3), dtype=jnp.int32)
    start = jnp.concatenate([jnp.zeros((n_groups, 1), jnp.int32), jnp.cumsum(counts, axis=1)], axis=1)
    n_rows = 256 if per_group // N_EXPERTS >= 512 else 128
    pad = ((0, 0), (n_rows, n_rows + LANES))
    rows8 = jnp.pad(rows8, pad)
    wts = jnp.pad(wts, pad).reshape(n_groups, -1, 1, LANES)
    out = None
    for g in range(n_groups):
        out = _moe_group_call(g, n_groups, g_tok, n_rows, rows8[g], start[g], x_tiles, wts[g], w1, w3, w2, out)
    return out


def _moe_group_call(group, n_groups, g_tok, n_rows, rows8, start, x_tiles, wts, w1, w3, w2, prev):
    g_rows = g_tok * D_TILES
    assert x_tiles.shape[0] == n_groups * g_rows
    out_block = (1, g_rows, LANES)
    in_specs = [
        pl.BlockSpec((g_rows, LANES), lambda e, *_: (group, 0), pipeline_mode=pl.Buffered(1)),
        pl.BlockSpec(wts.shape, lambda e, *_: (0, 0, 0)),
        pl.BlockSpec((1, D_MODEL, EXPERT_DIM), lambda e, *_: (e, 0, 0)),
        pl.BlockSpec((1, D_MODEL, EXPERT_DIM), lambda e, *_: (e, 0, 0)),
        pl.BlockSpec((1, EXPERT_DIM, D_MODEL), lambda e, *_: (e, 0, 0)),
    ]
    args = [rows8, start, x_tiles, wts, w1, w3, w2]
    aliases = {}
    if prev is not None:
        in_specs.append(pl.BlockSpec(memory_space=pl.ANY))
        aliases = {len(args): 0}
        args.append(prev)
    return pl.pallas_call(
        functools.partial(_moe_kernel, n_rows=n_rows),
        grid_spec=pltpu.PrefetchScalarGridSpec(
            num_scalar_prefetch=2, grid=(N_EXPERTS,), in_specs=in_specs,
            out_specs=pl.BlockSpec(out_block, lambda e, *_: (group, 0, 0), pipeline_mode=pl.Buffered(1)),
            scratch_shapes=[pltpu.VMEM((n_rows * D_TILES, LANES), F32),
                            pltpu.VMEM((n_rows * D_TILES, LANES), F32),
                            pltpu.SMEM((1,), jnp.int32)]),
        out_shape=jax.ShapeDtypeStruct((n_groups,) + out_block[1:], F32),
        input_output_aliases=aliases,
        compiler_params=_params(1), name="moe",
    )(*args)


def _final_kernel(x1_ref, routed_ref, p_ref, s1_ref, s3_ref, s2_ref, g2_ref, b2_ref, wg_ref, wp_ref,
                  o_ref, *, alpha):
    x1 = x1_ref[...]
    xb = x1.astype(BF16)
    act = jax.nn.silu(_dot(xb, s1_ref[...])) * _dot(xb, s3_ref[...])
    ff = _load_row_major_tiles(routed_ref.at[0], x1.shape[0]) + _dot(act.astype(BF16), s2_ref[...])
    x2 = _layer_norm(alpha * x1 + ff, g2_ref[...], b2_ref[...])
    gate = jax.nn.sigmoid(_dot(x2.astype(BF16), wg_ref[...]))
    o_ref[...] = x2 + gate * _dot(p_ref[...].astype(BF16), wp_ref[...])


def _final(x1, routed, p2d, s1, s3, s2, g2, b2, wg, wp, *, alpha):
    n_rows = x1.shape[0]
    tile = ROW_TILE
    row = lambda width: pl.BlockSpec((tile, width), lambda i: (i, 0))
    tiles_per_group = routed.shape[1] // D_TILES // tile
    routed_spec = pl.BlockSpec((1, tile * D_TILES, LANES),
                               lambda i: (i // tiles_per_group, i % tiles_per_group, 0))
    return pl.pallas_call(
        functools.partial(_final_kernel, alpha=alpha),
        grid=(n_rows // tile,),
        in_specs=[row(D_MODEL), routed_spec,
                  row(p2d.shape[1]),
                  _const_spec(s1.shape), _const_spec(s3.shape), _const_spec(s2.shape),
                  _const_spec((1, D_MODEL)), _const_spec((1, D_MODEL)),
                  _const_spec(wg.shape), _const_spec(wp.shape)],
        out_specs=row(D_MODEL),
        out_shape=jax.ShapeDtypeStruct((n_rows, D_MODEL), F32),
        compiler_params=_params(1), name="final",
    )(x1, routed, p2d, s1, s3, s2, g2, b2, wg, wp)


def _trunk_layer(x, p, cache_k, cache_v, lw, *, alpha, first_layer_ln):
    batch, seq, d = x.shape
    n_rows = batch * seq
    x2d = x.reshape(n_rows, d)
    prompt = cache_k is None
    mlp_len = min(seq, MLP_CHUNK)
    keep = min(PAST_CHUNKS * CHUNK, seq) if prompt else seq
    ln_g, ln_b = first_layer_ln

    w_s = (lw["mlp_w_s"][:, :mlp_len, :mlp_len]
           * jnp.tril(jnp.ones((mlp_len, mlp_len), F32))).astype(BF16)
    b_s = jnp.repeat(lw["mlp_b_s"][:, :mlp_len].T, MLP_WIDTH // MLP_GROUPS, axis=1)
    q, k, v, kf, vf, vn, ga, gb, ob = _proj(
        x2d, ln_g, ln_b, lw["w_in"].astype(BF16), lw["mlp_ln_g"][None], lw["mlp_ln_b"][None],
        w_s, b_s, seq=seq, mlp_len=mlp_len, keep_rows=keep)
    if prompt:
        oa = _attn_prompt(q, k, v, lw["attn_rel_bias"], batch=batch, seq=seq)
    else:
        oa = _attn_sample(q, k, v, cache_k, cache_v, lw["attn_rel_bias"], batch=batch, seq=seq)
    x1, x1_tiles, scores = _mix(
        x2d, oa, ob, ga, gb, ln_g, ln_b, lw["w_branch_a"].astype(BF16), lw["w_branch_b"].astype(BF16),
        lw["w_out"].astype(BF16), lw["ln1_g"][None], lw["ln1_b"][None], lw["router_w"].T, alpha=alpha)
    ids, weights = _route_call(scores, lw["router_bias"][:, None])
    routed = _moe(ids, weights, x1_tiles, lw["exp_w1"], lw["exp_w3"], lw["exp_w2"])
    y = _final(x1, routed, p.reshape(n_rows, -1), lw["shared_w1"].astype(BF16),
               lw["shared_w3"].astype(BF16), lw["shared_w2"].astype(BF16),
               lw["ln2_g"][None], lw["ln2_b"][None], lw["ple_w_gate"].astype(BF16),
               lw["ple_w_proj"].astype(BF16), alpha=alpha)
    new_k = kf.reshape(batch, keep, HEADS, HEAD_DIM)
    new_v = vf.reshape(batch, keep, HEADS, HEAD_DIM)
    new_mlp_v = vn.reshape(batch, mlp_len, MLP_WIDTH)
    return y.reshape(batch, seq, d), new_k, new_v, new_mlp_v


def kernel(x_prompt, x_sample, cache_attn_k, cache_attn_v, p_prompt, p_sample, ln_in_g, ln_in_b, w_in, attn_rel_bias, mlp_ln_g, mlp_ln_b, mlp_w_s, mlp_b_s, w_branch_a, w_branch_b, w_out, ln1_g, ln1_b, router_w, router_bias, exp_w1, exp_w3, exp_w2, shared_w1, shared_w3, shared_w2, ln2_g, ln2_b, ple_w_gate, ple_w_proj):
    depth = w_in.shape[0]
    assert depth == 1, "the fused proj/mix kernels apply LayerNorm_in themselves: single layer only"
    alpha = (2 * depth) ** 0.25
    stacked = dict(w_in=w_in, attn_rel_bias=attn_rel_bias, mlp_ln_g=mlp_ln_g, mlp_ln_b=mlp_ln_b,
                   mlp_w_s=mlp_w_s, mlp_b_s=mlp_b_s, w_branch_a=w_branch_a, w_branch_b=w_branch_b,
                   w_out=w_out, ln1_g=ln1_g, ln1_b=ln1_b, router_w=router_w, router_bias=router_bias,
                   exp_w1=exp_w1, exp_w3=exp_w3, exp_w2=exp_w2, shared_w1=shared_w1,
                   shared_w3=shared_w3, shared_w2=shared_w2, ln2_g=ln2_g, ln2_b=ln2_b,
                   ple_w_gate=ple_w_gate, ple_w_proj=ple_w_proj)
    lw = {name: value[0] for name, value in stacked.items()}
    for name in ("exp_w1", "exp_w3", "exp_w2"):
        lw[name] = lw[name].astype(BF16)
    ln_in =(ln_in_g[None], ln_in_b[None])
    yp, kp, vp, mp = _trunk_layer(x_prompt, p_prompt[0], None, None, lw, alpha=alpha, first_layer_ln=ln_in)
    ys, ks, vs, ms = _trunk_layer(x_sample, p_sample[0], cache_attn_k[0], cache_attn_v[0], lw,
                                  alpha=alpha, first_layer_ln=ln_in)
    return (yp, ys, kp[None], vp[None], ks[None], vs[None], mp[None], ms[None])
```

```python
import functools

import jax
import jax.numpy as jnp
import numpy as np
from jax import lax
from jax.experimental import pallas as pl
from jax.experimental.pallas import tpu as pltpu

F32 = jnp.float32
BF16 = jnp.bfloat16

D_MODEL = 1024
CHUNK = 64
HEADS = 8
HEAD_DIM = 64
A_WIDTH = HEADS * HEAD_DIM
PAST_CHUNKS = 8
MAX_REL = 256
MLP_CHUNK = 128
MLP_GROUPS = 8
MLP_WIDTH = 512
N_EXPERTS = 64
TOP_K = 8
N_GROUPS = 8
TOPK_GROUPS = 4
GROUP_SIZE = N_EXPERTS // N_GROUPS
EXPERT_DIM = 256
ROUTED_SCALE = 2.5
LN_EPS = 1e-5
MASKED = -1e30

LANES = 128
SUBLANES = 8
VMEM_LIMIT_BYTES = 56 * 1024 * 1024

ROW_TILE = 512
ROUTE_TILE = 2048
Q_CHUNKS = 4
MOE_GROUP_TOKENS = 4096
SCATTER_BATCH = 16
D_TILES = D_MODEL // LANES


def _layer_norm(x, g, b):
    xc = x - jnp.mean(x, axis=-1, keepdims=True)
    var = jnp.mean(xc * xc, axis=-1, keepdims=True)
    return xc * lax.rsqrt(var + LN_EPS) * g + b


def _dot(a, b):
    return jnp.dot(a, b, preferred_element_type=F32)


def _dot_nt(a, b, precision=None):
    return lax.dot_general(a, b, (((1,), (1,)), ((), ())),
                           preferred_element_type=F32, precision=precision)


def _const_spec(shape):
    zeros = (0,) * len(shape)
    return pl.BlockSpec(shape, lambda *_: zeros)


def _params(n_axes):
    return pltpu.CompilerParams(dimension_semantics=("arbitrary",) * n_axes,
                                vmem_limit_bytes=VMEM_LIMIT_BYTES)


def _proj_kernel(x_ref, lng_ref, lnb_ref, w_ref, mg_ref, mb_ref, ws_ref, bs_ref,
                 q_ref, k_ref, v_ref, kf_ref, vf_ref, vn_ref, ga_ref, gb_ref, ob_ref,
                 *, mlp_len, vn_rows, front_pad):
    if front_pad:
        @pl.when(pl.program_id(1) == 0)
        def _():
            k_ref[...] = jnp.zeros(k_ref.shape, k_ref.dtype)
            v_ref[...] = jnp.zeros(v_ref.shape, v_ref.dtype)

        pl.when(pl.program_id(1) > 0)(functools.partial(
            _proj_rows, x_ref, lng_ref, lnb_ref, w_ref, mg_ref, mb_ref, ws_ref, bs_ref, q_ref, k_ref,
            v_ref, kf_ref, vf_ref, vn_ref, ga_ref, gb_ref, ob_ref, mlp_len=mlp_len, vn_rows=vn_rows))
    else:
        _proj_rows(x_ref, lng_ref, lnb_ref, w_ref, mg_ref, mb_ref, ws_ref, bs_ref, q_ref, k_ref,
                   v_ref, kf_ref, vf_ref, vn_ref, ga_ref, gb_ref, ob_ref, mlp_len=mlp_len, vn_rows=vn_rows)


def _proj_rows(x_ref, lng_ref, lnb_ref, w_ref, mg_ref, mb_ref, ws_ref, bs_ref,
               q_ref, k_ref, v_ref, kf_ref, vf_ref, vn_ref, ga_ref, gb_ref, ob_ref, *, mlp_len, vn_rows):
    rows = x_ref.shape[0]
    xn = _layer_norm(x_ref[...], lng_ref[...], lnb_ref[...]).astype(BF16)

    def section(lo, width):
        return _dot(xn, w_ref[:, lo:lo + width])

    q = section(0, A_WIDTH)
    q_ref[...] = (q * (HEAD_DIM ** -0.5)).astype(BF16)
    k = section(A_WIDTH, A_WIDTH)
    k_ref[...] = k.astype(BF16).reshape(k_ref.shape)
    kf_ref[...] = k
    v = section(2 * A_WIDTH, A_WIDTH)
    v_ref[...] = v.astype(BF16).reshape(v_ref.shape)
    vf_ref[...] = v
    base = 3 * A_WIDTH
    un = jax.nn.gelu(section(base, MLP_WIDTH))
    vn = _layer_norm(jax.nn.gelu(section(base + MLP_WIDTH, MLP_WIDTH)), mg_ref[...], mb_ref[...])
    vn_ref[...] = vn[rows - vn_rows:, :]
    base += 2 * MLP_WIDTH
    ga_ref[...] = jax.nn.sigmoid(section(base, D_MODEL)).astype(BF16)
    gb_ref[...] = jax.nn.sigmoid(section(base + D_MODEL, D_MODEL)).astype(BF16)

    vnb = vn.astype(BF16)
    low_half = lax.broadcasted_iota(jnp.int32, (mlp_len, LANES), 1) < (MLP_WIDTH // MLP_GROUPS)
    for c in range(rows // mlp_len):
        r0 = c * mlp_len
        for p in range(MLP_GROUPS // 2):
            c0 = p * LANES
            slab = vnb[r0:r0 + mlp_len, c0:c0 + LANES]
            mixed = jnp.where(low_half, _dot(ws_ref[2 * p], slab), _dot(ws_ref[2 * p + 1], slab))
            gated = un[r0:r0 + mlp_len, c0:c0 + LANES] * (mixed + bs_ref[:, c0:c0 + LANES])
            ob_ref[r0:r0 + mlp_len, c0:c0 + LANES] = gated.astype(BF16)


def _proj(x2d, ln_g, ln_b, w_in, mlp_g, mlp_b, w_s, b_s, *, seq, mlp_len, keep_rows, front_pad):
    n_rows = x2d.shape[0]
    tile = ROW_TILE
    assert n_rows % tile == 0 and tile % mlp_len == 0
    n_seq = n_rows // seq
    if front_pad:
        assert seq % tile == 0 and keep_rows == tile and front_pad == tile
        per_seq = seq // tile
        grid = (n_seq, per_seq + 1)
        kv_rows, vn_rows = n_seq * tile, mlp_len
        kv_map = lambda b, j: (b, 0)
        row_map = lambda b, j: (b * per_seq + jnp.maximum(j - 1, 0), 0)
        kv_shape = jax.ShapeDtypeStruct((n_seq, front_pad + seq, A_WIDTH), BF16)
        kv_spec = pl.BlockSpec((1, tile, A_WIDTH), lambda b, j: (b, j, 0))
    else:
        assert keep_rows == seq and mlp_len == seq and n_rows == tile
        grid = (1,)
        kv_rows, vn_rows = n_rows, tile
        kv_map = row_map = lambda i: (i, 0)
        kv_shape = jax.ShapeDtypeStruct((n_rows, A_WIDTH), BF16)
        kv_spec = pl.BlockSpec((tile, A_WIDTH), row_map)
    row = lambda width: pl.BlockSpec((tile, width), row_map)
    n_in = w_in.shape[1]
    out_shape = (
        jax.ShapeDtypeStruct((n_rows, A_WIDTH), BF16),
        kv_shape,
        kv_shape,
        jax.ShapeDtypeStruct((kv_rows, A_WIDTH), F32),
        jax.ShapeDtypeStruct((kv_rows, A_WIDTH), F32),
        jax.ShapeDtypeStruct((kv_rows // tile * vn_rows, MLP_WIDTH), F32),
        jax.ShapeDtypeStruct((n_rows, D_MODEL), BF16),
        jax.ShapeDtypeStruct((n_rows, D_MODEL), BF16),
        jax.ShapeDtypeStruct((n_rows, MLP_WIDTH), BF16),
    )
    out_specs = (
        row(A_WIDTH), kv_spec, kv_spec,
        pl.BlockSpec((tile, A_WIDTH), kv_map), pl.BlockSpec((tile, A_WIDTH), kv_map),
        pl.BlockSpec((vn_rows, MLP_WIDTH), kv_map),
        row(D_MODEL), row(D_MODEL), row(MLP_WIDTH),
    )
    in_specs = [
        row(D_MODEL), _const_spec((1, D_MODEL)), _const_spec((1, D_MODEL)),
        _const_spec((D_MODEL, n_in)), _const_spec((1, MLP_WIDTH)), _const_spec((1, MLP_WIDTH)),
        _const_spec((MLP_GROUPS, mlp_len, mlp_len)), _const_spec((mlp_len, MLP_WIDTH)),
    ]
    return pl.pallas_call(
        functools.partial(_proj_kernel, mlp_len=mlp_len, vn_rows=vn_rows, front_pad=bool(front_pad)),
        grid=grid, in_specs=in_specs, out_specs=out_specs, out_shape=out_shape,
        compiler_params=_params(len(grid)), name="proj",
    )(x2d, ln_g, ln_b, w_in, mlp_g, mlp_b, w_s, b_s)


def _attend_heads(q, keys, values, bias_ref, extra_mask, o_ref):
    n_q = q.shape[0]
    low_half = lax.broadcasted_iota(jnp.int32, (n_q, LANES), 1) < HEAD_DIM
    zero = jnp.zeros((), BF16)
    for p in range(HEADS // 2):
        c0 = p * LANES
        q2, k2, v2 = q[:, c0:c0 + LANES], keys[:, c0:c0 + LANES], values[:, c0:c0 + LANES]
        qs = jnp.concatenate([jnp.where(low_half, q2, zero), jnp.where(low_half, zero, q2)], axis=0)
        s = _dot_nt(qs, k2) + bias_ref[2 * p:2 * p + 2].reshape(2 * n_q, keys.shape[0])
        if extra_mask is not None:
            s = s + extra_mask
        e = jnp.exp(s - jnp.max(s, axis=-1, keepdims=True))
        denom = jnp.sum(e, axis=-1, keepdims=True)
        o = _dot(e.astype(BF16), v2) / denom
        o_ref[:, c0:c0 + LANES] = jnp.where(low_half, o[:n_q], o[n_q:]).astype(o_ref.dtype)


def _attn_prompt_kernel(q_ref, k_ref, v_ref, bias_ref, o_ref, *, window):
    n_q = q_ref.shape[1]
    start = pl.multiple_of(pl.program_id(1) * n_q, n_q)
    keys = k_ref[0, pl.ds(start, window), :]
    values = v_ref[0, pl.ds(start, window), :]
    col = lax.broadcasted_iota(jnp.int32, (1, window), 1)
    pad_mask = jnp.where(col + start >= PAST_CHUNKS * CHUNK, 0.0, MASKED).astype(F32)
    _attend_heads(q_ref[0], keys, values, bias_ref, pad_mask, o_ref.at[0])


def _relative_bias(rel_table, n_q, n_k, key_offset):
    period = pl.next_power_of_2(n_q + n_k)
    m = np.arange(period)
    diag = np.where(m < n_k, m, m - period)
    idx = np.clip(diag - key_offset, -MAX_REL, MAX_REL) + MAX_REL
    values = rel_table[:, idx].astype(F32)
    rows = jnp.tile(values, (1, n_q))[:, :n_q * (period - 1)]
    return rows.reshape(rel_table.shape[0], n_q, period - 1)[:, :, :n_k]


def _prompt_bias(rel_table):
    n_q, n_k = Q_CHUNKS * CHUNK, (Q_CHUNKS + PAST_CHUNKS) * CHUNK
    lag = np.arange(n_q)[:, None] // CHUNK - (np.arange(n_k)[None, :] // CHUNK - PAST_CHUNKS)
    in_band = (lag >= 0) & (lag <= PAST_CHUNKS)
    bias = _relative_bias(rel_table, n_q, n_k, PAST_CHUNKS * CHUNK)
    return jnp.where(in_band[None], bias, MASKED)


def _attn_prompt(q, kp, vp, rel_table, *, batch, seq):
    n_q = Q_CHUNKS * CHUNK
    window = (Q_CHUNKS + PAST_CHUNKS) * CHUNK
    pad = PAST_CHUNKS * CHUNK
    q3 = q.reshape(batch, seq, A_WIDTH)
    assert kp.shape == vp.shape == (batch, seq + pad, A_WIDTH)
    bias = _prompt_bias(rel_table)
    qspec = pl.BlockSpec((1, n_q, A_WIDTH), lambda b, i: (b, i, 0))
    kvspec = pl.BlockSpec((1, seq + pad, A_WIDTH), lambda b, i: (b, 0, 0))
    out = pl.pallas_call(
        functools.partial(_attn_prompt_kernel, window=window),
        grid=(batch, seq // n_q),
        in_specs=[qspec, kvspec, kvspec, _const_spec(bias.shape)],
        out_specs=qspec,
        out_shape=jax.ShapeDtypeStruct((batch, seq, A_WIDTH), BF16),
        compiler_params=_params(2), name="attn_prompt",
    )(q3, kp, vp, bias)
    return out.reshape(batch * seq, A_WIDTH)


def _attn_sample_kernel(q_ref, k_ref, v_ref, ck_ref, cv_ref, bias_ref, o_ref, kk_ref, vv_ref):
    n_cache = ck_ref.shape[1]
    n_new = k_ref.shape[1]
    kk_ref[0:n_cache, :] = ck_ref[0].astype(BF16)
    kk_ref[n_cache:n_cache + n_new, :] = k_ref[0]
    vv_ref[0:n_cache, :] = cv_ref[0].astype(BF16)
    vv_ref[n_cache:n_cache + n_new, :] = v_ref[0]
    _attend_heads(q_ref[0], kk_ref[...], vv_ref[...], bias_ref, None, o_ref.at[0])


def _attn_sample(q, k, v, cache_k, cache_v, rel_table, *, batch, seq):
    n_cache = cache_k.shape[1]
    bias = _relative_bias(rel_table, seq, n_cache + seq, n_cache)
    new = pl.BlockSpec((1, seq, A_WIDTH), lambda b: (b, 0, 0))
    old = pl.BlockSpec((1, n_cache, A_WIDTH), lambda b: (b, 0, 0))
    out = pl.pallas_call(
        _attn_sample_kernel,
        grid=(batch,),
        in_specs=[new, new, new, old, old, _const_spec(bias.shape)],
        out_specs=new,
        out_shape=jax.ShapeDtypeStruct((batch, seq, A_WIDTH), BF16),
        scratch_shapes=[pltpu.VMEM((n_cache + seq, A_WIDTH), BF16),
                        pltpu.VMEM((n_cache + seq, A_WIDTH), BF16)],
        compiler_params=_params(1), name="attn_sample",
    )(q.reshape(batch, seq, A_WIDTH), k.reshape(batch, seq, A_WIDTH), v.reshape(batch, seq, A_WIDTH),
      cache_k.reshape(batch, n_cache, A_WIDTH), cache_v.reshape(batch, n_cache, A_WIDTH), bias)
    return out.reshape(batch * seq, A_WIDTH)


def _first_index(hit, iota, axis, limit):
    return jnp.min(jnp.where(hit, iota, limit), axis=axis, keepdims=True)


def _route(scores, sel):
    n_tok = scores.shape[1]
    neg = -jnp.inf
    grouped = sel.reshape(N_GROUPS, GROUP_SIZE, n_tok)
    member = lax.broadcasted_iota(jnp.int32, grouped.shape, 1)
    best = jnp.max(grouped, axis=1, keepdims=True)
    first = _first_index(grouped == best, member, 1, GROUP_SIZE)
    second = jnp.max(jnp.where(member == first, neg, grouped), axis=1, keepdims=True)
    group_score = best + second

    group_id = lax.broadcasted_iota(jnp.int32, group_score.shape, 0)
    keep = jnp.zeros(group_score.shape, F32)
    for _ in range(TOPK_GROUPS):
        top = jnp.max(group_score, axis=0, keepdims=True)
        hit = group_id == _first_index(group_score == top, group_id, 0, N_GROUPS)
        keep = jnp.where(hit, 1.0, keep)
        group_score = jnp.where(hit, neg, group_score)
    keep = jnp.broadcast_to(keep, grouped.shape).reshape(N_EXPERTS, n_tok)

    cand = jnp.where(keep > 0.0, sel, neg)
    expert_id = lax.broadcasted_iota(jnp.int32, cand.shape, 0)
    ids, weights = [], []
    for _ in range(TOP_K):
        top = jnp.max(cand, axis=0, keepdims=True)
        first = _first_index(cand == top, expert_id, 0, N_EXPERTS)
        hit = expert_id == first
        ids.append(first)
        weights.append(jnp.sum(jnp.where(hit, scores, 0.0), axis=0, keepdims=True))
        cand = jnp.where(hit, neg, cand)
    ids = jnp.concatenate(ids, axis=0)
    weights = jnp.concatenate(weights, axis=0)
    weights = weights / jnp.sum(weights, axis=0, keepdims=True) * ROUTED_SCALE
    return ids, weights


def _store_row_major_tiles(flat_ref, x):
    rows = x.shape[0]
    for j in range(D_TILES):
        flat_ref[pl.ds(j, rows, stride=D_TILES), :] = x[:, j * LANES:(j + 1) * LANES]


def _load_row_major_tiles(flat_ref, rows):
    return jnp.concatenate(
        [flat_ref[pl.ds(j, rows, stride=D_TILES), :] for j in range(D_TILES)], axis=1)


def _store_matmul_tiles(flat_ref, x):
    for a in range(x.shape[0] // SUBLANES):
        for j in range(D_TILES):
            r0 = (a * D_TILES + j) * SUBLANES
            flat_ref[r0:r0 + SUBLANES, :] = x[a * SUBLANES:(a + 1) * SUBLANES, j * LANES:(j + 1) * LANES]


def _load_row_tile(flat_ref, m):
    start = (m // SUBLANES) * SUBLANES * D_TILES + m % SUBLANES
    return flat_ref[pl.ds(start, D_TILES, stride=SUBLANES), :]


def _mix_kernel(x_ref, oa_ref, ob_ref, ga_ref, gb_ref, lng_ref, lnb_ref, wa_ref, wb_ref, wo_ref,
                g1_ref, b1_ref, rw_ref, x1_ref, x1t_ref, scores_ref, *, alpha):
    xn = _layer_norm(x_ref[...], lng_ref[...], lnb_ref[...])
    mix = (ga_ref[...].astype(F32) * _dot(oa_ref[...], wa_ref[...])
           + gb_ref[...].astype(F32) * _dot(ob_ref[...], wb_ref[...]))
    x1 = _layer_norm(alpha * xn + _dot(mix.astype(BF16), wo_ref[...]), g1_ref[...], b1_ref[...])
    x1_ref[...] = x1
    _store_row_major_tiles(x1t_ref, x1)
    scores_ref[...] = jax.nn.sigmoid(_dot_nt(rw_ref[...], x1, precision=lax.Precision.HIGHEST))


def _mix(x2d, oa, ob, ga, gb, ln_g, ln_b, wa, wb, wo, g1, b1, rw_t, *, alpha):
    n_rows = x2d.shape[0]
    tile = ROW_TILE
    row = lambda width: pl.BlockSpec((tile, width), lambda i: (i, 0))
    return pl.pallas_call(
        functools.partial(_mix_kernel, alpha=alpha),
        grid=(n_rows // tile,),
        in_specs=[row(D_MODEL), row(A_WIDTH), row(MLP_WIDTH), row(D_MODEL), row(D_MODEL),
                  _const_spec((1, D_MODEL)), _const_spec((1, D_MODEL)),
                  _const_spec(wa.shape), _const_spec(wb.shape), _const_spec(wo.shape),
                  _const_spec((1, D_MODEL)), _const_spec((1, D_MODEL)),
                  _const_spec(rw_t.shape)],
        out_specs=(row(D_MODEL), pl.BlockSpec((tile * D_TILES, LANES), lambda i: (i, 0)),
                   pl.BlockSpec((N_EXPERTS, tile), lambda i: (0, i))),
        out_shape=(jax.ShapeDtypeStruct((n_rows, D_MODEL), F32),
                   jax.ShapeDtypeStruct((n_rows * D_TILES, LANES), F32),
                   jax.ShapeDtypeStruct((N_EXPERTS, n_rows), F32)),
        compiler_params=_params(1), name="mix",
    )(x2d, oa, ob, ga, gb, ln_g, ln_b, wa, wb, wo, g1, b1, rw_t)


def _route_kernel(scores_ref, rb_ref, ids_ref, wts_ref):
    scores = scores_ref[...]
    ids, weights = _route(scores, scores + rb_ref[...])
    ids_ref[...] = ids
    wts_ref[...] = weights


def _route_call(scores, rb):
    n_tok = scores.shape[1]
    tile = min(ROUTE_TILE, n_tok)
    col = lambda rows: pl.BlockSpec((rows, tile), lambda i: (0, i))
    return pl.pallas_call(
        _route_kernel,
        grid=(n_tok // tile,),
        in_specs=[col(N_EXPERTS), _const_spec(rb.shape)],
        out_specs=(col(TOP_K), col(TOP_K)),
        out_shape=(jax.ShapeDtypeStruct((TOP_K, n_tok), jnp.int32),
                   jax.ShapeDtypeStruct((TOP_K, n_tok), F32)),
        compiler_params=_params(1), name="route",
    )(scores, rb)


def _moe_kernel(rows_ref, start_ref, x_ref, wts_ref, w1_ref, w3_ref, w2_ref, *rest, n_rows):
    out_ref, gat_ref, y_ref, pending_ref = rest[-4:]
    acc = out_ref.at[0]
    e = pl.program_id(0)

    @pl.when(e == 0)
    def _():
        out_ref[...] = jnp.zeros(out_ref.shape, out_ref.dtype)
        y_ref[...] = jnp.zeros(y_ref.shape, y_ref.dtype)
        pending_ref[0] = 0

    first = start_ref[e]
    count = start_ref[e + 1] - first

    def tile_at(ref, row8):
        return ref.at[pl.ds(pl.multiple_of(row8, SUBLANES), SUBLANES), :]

    w_rows = n_rows // LANES + 1
    row_id = lax.broadcasted_iota(jnp.int32, (n_rows, w_rows * LANES), 0)
    lane_id = lax.broadcasted_iota(jnp.int32, (n_rows, w_rows * LANES), 1)

    def scatter_add(base):
        for m0 in range(0, n_rows, SCATTER_BATCH):
            updates = []
            for m in range(m0, m0 + SCATTER_BATCH):
                dst = tile_at(acc, rows_ref[base + m])
                updates.append((dst, dst[...] + _load_row_tile(y_ref, m)))
            for dst, val in updates:
                dst[...] = val

    def block(b, pending):
        nominal = b * n_rows
        begin = jnp.minimum(nominal, count - n_rows)
        redo = nominal - begin
        base = first + begin + n_rows
        for m in range(n_rows):
            gat_ref[m * SUBLANES:(m + 1) * SUBLANES, :] = tile_at(x_ref, rows_ref[base + m])[...]
        xb = _load_row_major_tiles(gat_ref, n_rows).astype(BF16)
        act = jax.nn.silu(_dot(xb, w1_ref[0])) * _dot(xb, w3_ref[0])
        y = _dot(act.astype(BF16), w2_ref[0])

        w_row = base // LANES
        span = jnp.concatenate([wts_ref[w_row + r] for r in range(w_rows)], axis=1)
        picked = jnp.where((lane_id == row_id + base % LANES) & (row_id >= redo),
                           jnp.broadcast_to(span, lane_id.shape), 0.0)
        ys = y * jnp.sum(picked, axis=1, keepdims=True)
        scatter_add(pending)
        _store_matmul_tiles(y_ref, ys)
        return base

    pending = lax.fori_loop(0, pl.cdiv(count, n_rows), block, pending_ref[0])
    pending_ref[0] = pending

    @pl.when(e == pl.num_programs(0) - 1)
    def _():
        scatter_add(pending)


def _moe(ids, weights, x_tiles, w1, w3, w2):
    n_tok = ids.shape[1]
    g_tok = min(MOE_GROUP_TOKENS, n_tok)
    n_groups = n_tok // g_tok
    assert n_groups * g_tok == n_tok
    per_group = g_tok * TOP_K
    by_group = lambda a: a.reshape(TOP_K, n_groups, g_tok).transpose(1, 0, 2).reshape(n_groups, per_group)
    tok = lax.broadcasted_iota(jnp.int32, ids.shape, 1)
    key, wts = lax.sort((by_group(ids * g_tok + tok % g_tok), by_group(weights)), dimension=1, num_keys=1)
    rows8 = (key % g_tok) * D_TILES
    experts = jnp.arange(N_EXPERTS, dtype=jnp.int32)[None, None, :, None]
    counts = jnp.sum(ids.reshape(TOP_K, n_groups, 1, g_tok) == experts, axis=(0, 3), dtype=jnp.int32)
    start = jnp.concatenate([jnp.zeros((n_groups, 1), jnp.int32), jnp.cumsum(counts, axis=1)], axis=1)
    n_rows = 256 if per_group // N_EXPERTS >= 512 else 128
    pad = ((0, 0), (n_rows, n_rows + LANES))
    rows8 = jnp.pad(rows8, pad)
    wts = jnp.pad(wts, pad).reshape(n_groups, -1, 1, LANES)
    out = None
    for g in range(n_groups):
        out = _moe_group_call(g, n_groups, g_tok, n_rows, rows8[g], start[g], x_tiles, wts[g], w1, w3, w2, out)
    return out


def _moe_group_call(group, n_groups, g_tok, n_rows, rows8, start, x_tiles, wts, w1, w3, w2, prev):
    g_rows = g_tok * D_TILES
    assert x_tiles.shape[0] == n_groups * g_rows
    out_block = (1, g_rows, LANES)
    in_specs = [
        pl.BlockSpec((g_rows, LANES), lambda e, *_: (group, 0), pipeline_mode=pl.Buffered(1)),
        pl.BlockSpec(wts.shape, lambda e, *_: (0, 0, 0)),
        pl.BlockSpec((1, D_MODEL, EXPERT_DIM), lambda e, *_: (e, 0, 0)),
        pl.BlockSpec((1, D_MODEL, EXPERT_DIM), lambda e, *_: (e, 0, 0)),
        pl.BlockSpec((1, EXPERT_DIM, D_MODEL), lambda e, *_: (e, 0, 0)),
    ]
    args = [rows8, start, x_tiles, wts, w1, w3, w2]
    aliases = {}
    if prev is not None:
        in_specs.append(pl.BlockSpec(memory_space=pl.ANY))
        aliases = {len(args): 0}
        args.append(prev)
    return pl.pallas_call(
        functools.partial(_moe_kernel, n_rows=n_rows),
        grid_spec=pltpu.PrefetchScalarGridSpec(
            num_scalar_prefetch=2, grid=(N_EXPERTS,), in_specs=in_specs,
            out_specs=pl.BlockSpec(out_block, lambda e, *_: (group, 0, 0), pipeline_mode=pl.Buffered(1)),
            scratch_shapes=[pltpu.VMEM((n_rows * D_TILES, LANES), F32),
                            pltpu.VMEM((n_rows * D_TILES, LANES), F32),
                            pltpu.SMEM((1,), jnp.int32)]),
        out_shape=jax.ShapeDtypeStruct((n_groups,) + out_block[1:], F32),
        input_output_aliases=aliases,
        compiler_params=_params(1), name="moe",
    )(*args)


def _final_kernel(x1_ref, routed_ref, p_ref, s1_ref, s3_ref, s2_ref, g2_ref, b2_ref, wg_ref, wp_ref,
                  o_ref, *, alpha):
    x1 = x1_ref[...]
    xb = x1.astype(BF16)
    act = jax.nn.silu(_dot(xb, s1_ref[...])) * _dot(xb, s3_ref[...])
    ff = _load_row_major_tiles(routed_ref.at[0], x1.shape[0]) + _dot(act.astype(BF16), s2_ref[...])
    x2 = _layer_norm(alpha * x1 + ff, g2_ref[...], b2_ref[...])
    gate = jax.nn.sigmoid(_dot(x2.astype(BF16), wg_ref[...]))
    o_ref[...] = x2 + gate * _dot(p_ref[...].astype(BF16), wp_ref[...])


def _final(x1, routed, p2d, s1, s3, s2, g2, b2, wg, wp, *, alpha):
    n_rows = x1.shape[0]
    tile = ROW_TILE
    row = lambda width: pl.BlockSpec((tile, width), lambda i: (i, 0))
    tiles_per_group = routed.shape[1] // D_TILES // tile
    routed_spec = pl.BlockSpec((1, tile * D_TILES, LANES),
                               lambda i: (i // tiles_per_group, i % tiles_per_group, 0))
    return pl.pallas_call(
        functools.partial(_final_kernel, alpha=alpha),
        grid=(n_rows // tile,),
        in_specs=[row(D_MODEL), routed_spec,
                  row(p2d.shape[1]),
                  _const_spec(s1.shape), _const_spec(s3.shape), _const_spec(s2.shape),
                  _const_spec((1, D_MODEL)), _const_spec((1, D_MODEL)),
                  _const_spec(wg.shape), _const_spec(wp.shape)],
        out_specs=row(D_MODEL),
        out_shape=jax.ShapeDtypeStruct((n_rows, D_MODEL), F32),
        compiler_params=_params(1), name="final",
    )(x1, routed, p2d, s1, s3, s2, g2, b2, wg, wp)


def _trunk_layer(x, p, cache_k, cache_v, lw, *, alpha, first_layer_ln):
    batch, seq, d = x.shape
    n_rows = batch * seq
    x2d = x.reshape(n_rows, d)
    prompt = cache_k is None
    mlp_len = min(seq, MLP_CHUNK)
    keep = min(PAST_CHUNKS * CHUNK, seq) if prompt else seq
    ln_g, ln_b = first_layer_ln

    w_s = (lw["mlp_w_s"][:, :mlp_len, :mlp_len]
           * jnp.tril(jnp.ones((mlp_len, mlp_len), F32))).astype(BF16)
    b_s = jnp.repeat(lw["mlp_b_s"][:, :mlp_len].T, MLP_WIDTH // MLP_GROUPS, axis=1)
    q, k, v, kf, vf, vn, ga, gb, ob = _proj(
        x2d, ln_g, ln_b, lw["w_in"].astype(BF16), lw["mlp_ln_g"][None], lw["mlp_ln_b"][None],
        w_s, b_s, seq=seq, mlp_len=mlp_len, keep_rows=keep, front_pad=PAST_CHUNKS * CHUNK if prompt else 0)
    if prompt:
        oa = _attn_prompt(q, k, v, lw["attn_rel_bias"], batch=batch, seq=seq)
    else:
        oa = _attn_sample(q, k, v, cache_k, cache_v, lw["attn_rel_bias"], batch=batch, seq=seq)
    x1, x1_tiles, scores = _mix(
        x2d, oa, ob, ga, gb, ln_g, ln_b, lw["w_branch_a"].astype(BF16), lw["w_branch_b"].astype(BF16),
        lw["w_out"].astype(BF16), lw["ln1_g"][None], lw["ln1_b"][None], lw["router_w"].T, alpha=alpha)
    ids, weights = _route_call(scores, lw["router_bias"][:, None])
    routed = _moe(ids, weights, x1_tiles, lw["exp_w1"], lw["exp_w3"], lw["exp_w2"])
    y = _final(x1, routed, p.reshape(n_rows, -1), lw["shared_w1"].astype(BF16),
               lw["shared_w3"].astype(BF16), lw["shared_w2"].astype(BF16),
               lw["ln2_g"][None], lw["ln2_b"][None], lw["ple_w_gate"].astype(BF16),
               lw["ple_w_proj"].astype(BF16), alpha=alpha)
    new_k = kf.reshape(batch, keep, HEADS, HEAD_DIM)
    new_v = vf.reshape(batch, keep, HEADS, HEAD_DIM)
    new_mlp_v = vn.reshape(batch, mlp_len, MLP_WIDTH)
    return y.reshape(batch, seq, d), new_k, new_v, new_mlp_v


def kernel(x_prompt, x_sample, cache_attn_k, cache_attn_v, p_prompt, p_sample, ln_in_g, ln_in_b, w_in, attn_rel_bias, mlp_ln_g, mlp_ln_b, mlp_w_s, mlp_b_s, w_branch_a, w_branch_b, w_out, ln1_g, ln1_b, router_w, router_bias, exp_w1, exp_w3, exp_w2, shared_w1, shared_w3, shared_w2, ln2_g, ln2_b, ple_w_gate, ple_w_proj):
    depth = w_in.shape[0]
    assert depth == 1, "the fused proj/mix kernels apply LayerNorm_in themselves: single layer only"
    alpha = (2 * depth) ** 0.25
    stacked = dict(w_in=w_in, attn_rel_bias=attn_rel_bias, mlp_ln_g=mlp_ln_g, mlp_ln_b=mlp_ln_b,
                   mlp_w_s=mlp_w_s, mlp_b_s=mlp_b_s, w_branch_a=w_branch_a, w_branch_b=w_branch_b,
                   w_out=w_out, ln1_g=ln1_g, ln1_b=ln1_b, router_w=router_w, router_bias=router_bias,
                   exp_w1=exp_w1, exp_w3=exp_w3, exp_w2=exp_w2, shared_w1=shared_w1,
                   shared_w3=shared_w3, shared_w2=shared_w2, ln2_g=ln2_g, ln2_b=ln2_b,
                   ple_w_gate=ple_w_gate, ple_w_proj=ple_w_proj)
    lw = {name: value[0] for name, value in stacked.items()}
    for name in ("exp_w1", "exp_w3", "exp_w2"):
        lw[name] = lw[name].astype(BF16)
    ln_in =(ln_in_g[None], ln_in_b[None])
    yp, kp, vp, mp = _trunk_layer(x_prompt, p_prompt[0], None, None, lw, alpha=alpha, first_layer_ln=ln_in)
    ys, ks, vs, ms = _trunk_layer(x_sample, p_sample[0], cache_attn_k[0], cache_attn_v[0], lw,
                                  alpha=alpha, first_layer_ln=ln_in)
    return (yp, ys, kp[None], vp[None], ks[None], vs[None], mp[None], ms[None])
```

```python
import functools

import jax
import jax.numpy as jnp
import numpy as np
from jax import lax
from jax.experimental import pallas as pl
from jax.experimental.pallas import tpu as pltpu

F32 = jnp.float32
BF16 = jnp.bfloat16

D_MODEL = 1024
CHUNK = 64
HEADS = 8
HEAD_DIM = 64
A_WIDTH = HEADS * HEAD_DIM
PAST_CHUNKS = 8
MAX_REL = 256
MLP_CHUNK = 128
MLP_GROUPS = 8
MLP_WIDTH = 512
N_EXPERTS = 64
TOP_K = 8
N_GROUPS = 8
TOPK_GROUPS = 4
GROUP_SIZE = N_EXPERTS // N_GROUPS
EXPERT_DIM = 256
ROUTED_SCALE = 2.5
LN_EPS = 1e-5
MASKED = -1e30

LANES = 128
SUBLANES = 8
VMEM_LIMIT_BYTES = 56 * 1024 * 1024

ROW_TILE = 512
ROUTE_TILE = 2048
Q_CHUNKS = 4
MOE_GROUP_TOKENS = 4096
SCATTER_BATCH = 8
D_TILES = D_MODEL // LANES


def _layer_norm(x, g, b):
    xc = x - jnp.mean(x, axis=-1, keepdims=True)
    var = jnp.mean(xc * xc, axis=-1, keepdims=True)
    return xc * lax.rsqrt(var + LN_EPS) * g + b


def _dot(a, b):
    return jnp.dot(a, b, preferred_element_type=F32)


def _dot_nt(a, b, precision=None):
    return lax.dot_general(a, b, (((1,), (1,)), ((), ())),
                           preferred_element_type=F32, precision=precision)


def _const_spec(shape):
    zeros = (0,) * len(shape)
    return pl.BlockSpec(shape, lambda *_: zeros)


def _params(n_axes):
    return pltpu.CompilerParams(dimension_semantics=("arbitrary",) * n_axes,
                                vmem_limit_bytes=VMEM_LIMIT_BYTES)


def _proj_kernel(x_ref, lng_ref, lnb_ref, w_ref, mg_ref, mb_ref, ws_ref, bs_ref,
                 q_ref, k_ref, v_ref, kf_ref, vf_ref, vn_ref, ga_ref, gb_ref, ob_ref,
                 *, mlp_len, vn_rows):
    rows = x_ref.shape[0]
    xn = _layer_norm(x_ref[...], lng_ref[...], lnb_ref[...]).astype(BF16)

    def section(lo, width):
        return _dot(xn, w_ref[:, lo:lo + width])

    q = section(0, A_WIDTH)
    q_ref[...] = (q * (HEAD_DIM ** -0.5)).astype(BF16)
    k = section(A_WIDTH, A_WIDTH)
    k_ref[...] = k.astype(BF16)
    kf_ref[...] = k
    v = section(2 * A_WIDTH, A_WIDTH)
    v_ref[...] = v.astype(BF16)
    vf_ref[...] = v
    base = 3 * A_WIDTH
    un = jax.nn.gelu(section(base, MLP_WIDTH))
    vn = _layer_norm(jax.nn.gelu(section(base + MLP_WIDTH, MLP_WIDTH)), mg_ref[...], mb_ref[...])
    vn_ref[...] = vn[rows - vn_rows:, :]
    base += 2 * MLP_WIDTH
    ga_ref[...] = jax.nn.sigmoid(section(base, D_MODEL)).astype(BF16)
    gb_ref[...] = jax.nn.sigmoid(section(base + D_MODEL, D_MODEL)).astype(BF16)

    vnb = vn.astype(BF16)
    low_half = lax.broadcasted_iota(jnp.int32, (mlp_len, LANES), 1) < (MLP_WIDTH // MLP_GROUPS)
    for c in range(rows // mlp_len):
        r0 = c * mlp_len
        for p in range(MLP_GROUPS // 2):
            c0 = p * LANES
            slab = vnb[r0:r0 + mlp_len, c0:c0 + LANES]
            mixed = jnp.where(low_half, _dot(ws_ref[2 * p], slab), _dot(ws_ref[2 * p + 1], slab))
            gated = un[r0:r0 + mlp_len, c0:c0 + LANES] * (mixed + bs_ref[:, c0:c0 + LANES])
            ob_ref[r0:r0 + mlp_len, c0:c0 + LANES] = gated.astype(BF16)


def _proj(x2d, ln_g, ln_b, w_in, mlp_g, mlp_b, w_s, b_s, *, seq, mlp_len, keep_rows):
    n_rows = x2d.shape[0]
    tile = ROW_TILE
    assert n_rows % tile == 0 and tile % mlp_len == 0
    n_seq = n_rows // seq
    if seq >= tile:
        assert seq % tile == 0 and keep_rows == tile
        per_seq = seq // tile
        kv_rows, vn_rows = n_seq * tile, mlp_len
        kv_map = lambda i: (i // per_seq, 0)
    else:
        assert keep_rows == seq and mlp_len == seq
        kv_rows, vn_rows = n_rows, tile
        kv_map = lambda i: (i, 0)
    row = lambda width: pl.BlockSpec((tile, width), lambda i: (i, 0))
    n_in = w_in.shape[1]
    out_shape = (
        jax.ShapeDtypeStruct((n_rows, A_WIDTH), BF16),
        jax.ShapeDtypeStruct((n_rows, A_WIDTH), BF16),
        jax.ShapeDtypeStruct((n_rows, A_WIDTH), BF16),
        jax.ShapeDtypeStruct((kv_rows, A_WIDTH), F32),
        jax.ShapeDtypeStruct((kv_rows, A_WIDTH), F32),
        jax.ShapeDtypeStruct((kv_rows // tile * vn_rows, MLP_WIDTH), F32),
        jax.ShapeDtypeStruct((n_rows, D_MODEL), BF16),
        jax.ShapeDtypeStruct((n_rows, D_MODEL), BF16),
        jax.ShapeDtypeStruct((n_rows, MLP_WIDTH), BF16),
    )
    out_specs = (
        row(A_WIDTH), row(A_WIDTH), row(A_WIDTH),
        pl.BlockSpec((tile, A_WIDTH), kv_map), pl.BlockSpec((tile, A_WIDTH), kv_map),
        pl.BlockSpec((vn_rows, MLP_WIDTH), kv_map),
        row(D_MODEL), row(D_MODEL), row(MLP_WIDTH),
    )
    in_specs = [
        row(D_MODEL), _const_spec((1, D_MODEL)), _const_spec((1, D_MODEL)),
        _const_spec((D_MODEL, n_in)), _const_spec((1, MLP_WIDTH)), _const_spec((1, MLP_WIDTH)),
        _const_spec((MLP_GROUPS, mlp_len, mlp_len)), _const_spec((mlp_len, MLP_WIDTH)),
    ]
    return pl.pallas_call(
        functools.partial(_proj_kernel, mlp_len=mlp_len, vn_rows=vn_rows),
        grid=(n_rows // tile,), in_specs=in_specs, out_specs=out_specs, out_shape=out_shape,
        compiler_params=_params(1), name="proj",
    )(x2d, ln_g, ln_b, w_in, mlp_g, mlp_b, w_s, b_s)


def _attend_heads(q, keys, values, bias_ref, extra_mask, o_ref):
    n_q = q.shape[0]
    low_half = lax.broadcasted_iota(jnp.int32, (n_q, LANES), 1) < HEAD_DIM
    zero = jnp.zeros((), BF16)
    for p in range(HEADS // 2):
        c0 = p * LANES
        q2, k2, v2 = q[:, c0:c0 + LANES], keys[:, c0:c0 + LANES], values[:, c0:c0 + LANES]
        qs = jnp.concatenate([jnp.where(low_half, q2, zero), jnp.where(low_half, zero, q2)], axis=0)
        s = _dot_nt(qs, k2) + bias_ref[2 * p:2 * p + 2].reshape(2 * n_q, keys.shape[0])
        if extra_mask is not None:
            s = s + extra_mask
        e = jnp.exp(s - jnp.max(s, axis=-1, keepdims=True))
        denom = jnp.sum(e, axis=-1, keepdims=True)
        o = _dot(e.astype(BF16), v2) / denom
        o_ref[:, c0:c0 + LANES] = jnp.where(low_half, o[:n_q], o[n_q:]).astype(o_ref.dtype)


def _attn_prompt_kernel(q_ref, k_ref, v_ref, bias_ref, o_ref, *, window):
    n_q = q_ref.shape[1]
    start = pl.multiple_of(pl.program_id(1) * n_q, n_q)
    keys = k_ref[0, pl.ds(start, window), :]
    values = v_ref[0, pl.ds(start, window), :]
    col = lax.broadcasted_iota(jnp.int32, (1, window), 1)
    pad_mask = jnp.where(col + start >= PAST_CHUNKS * CHUNK, 0.0, MASKED).astype(F32)
    _attend_heads(q_ref[0], keys, values, bias_ref, pad_mask, o_ref.at[0])


def _relative_bias(rel_table, n_q, n_k, key_offset):
    period = pl.next_power_of_2(n_q + n_k)
    m = np.arange(period)
    diag = np.where(m < n_k, m, m - period)
    idx = np.clip(diag - key_offset, -MAX_REL, MAX_REL) + MAX_REL
    values = rel_table[:, idx].astype(F32)
    rows = jnp.tile(values, (1, n_q))[:, :n_q * (period - 1)]
    return rows.reshape(rel_table.shape[0], n_q, period - 1)[:, :, :n_k]


def _prompt_bias(rel_table):
    n_q, n_k = Q_CHUNKS * CHUNK, (Q_CHUNKS + PAST_CHUNKS) * CHUNK
    lag = np.arange(n_q)[:, None] // CHUNK - (np.arange(n_k)[None, :] // CHUNK - PAST_CHUNKS)
    in_band = (lag >= 0) & (lag <= PAST_CHUNKS)
    bias = _relative_bias(rel_table, n_q, n_k, PAST_CHUNKS * CHUNK)
    return jnp.where(in_band[None], bias, MASKED)


def _attn_prompt(q, k, v, rel_table, *, batch, seq):
    n_q = Q_CHUNKS * CHUNK
    window = (Q_CHUNKS + PAST_CHUNKS) * CHUNK
    pad = PAST_CHUNKS * CHUNK
    q3 = q.reshape(batch, seq, A_WIDTH)
    kp = jnp.pad(k.reshape(batch, seq, A_WIDTH), ((0, 0), (pad, 0), (0, 0)))
    vp = jnp.pad(v.reshape(batch, seq, A_WIDTH), ((0, 0), (pad, 0), (0, 0)))
    bias = _prompt_bias(rel_table)
    qspec = pl.BlockSpec((1, n_q, A_WIDTH), lambda b, i: (b, i, 0))
    kvspec = pl.BlockSpec((1, seq + pad, A_WIDTH), lambda b, i: (b, 0, 0))
    out = pl.pallas_call(
        functools.partial(_attn_prompt_kernel, window=window),
        grid=(batch, seq // n_q),
        in_specs=[qspec, kvspec, kvspec, _const_spec(bias.shape)],
        out_specs=qspec,
        out_shape=jax.ShapeDtypeStruct((batch, seq, A_WIDTH), BF16),
        compiler_params=_params(2), name="attn_prompt",
    )(q3, kp, vp, bias)
    return out.reshape(batch * seq, A_WIDTH)


def _attn_sample_kernel(q_ref, k_ref, v_ref, ck_ref, cv_ref, bias_ref, o_ref, kk_ref, vv_ref):
    n_cache = ck_ref.shape[1]
    n_new = k_ref.shape[1]
    kk_ref[0:n_cache, :] = ck_ref[0].astype(BF16)
    kk_ref[n_cache:n_cache + n_new, :] = k_ref[0]
    vv_ref[0:n_cache, :] = cv_ref[0].astype(BF16)
    vv_ref[n_cache:n_cache + n_new, :] = v_ref[0]
    _attend_heads(q_ref[0], kk_ref[...], vv_ref[...], bias_ref, None, o_ref.at[0])


def _attn_sample(q, k, v, cache_k, cache_v, rel_table, *, batch, seq):
    n_cache = cache_k.shape[1]
    bias = _relative_bias(rel_table, seq, n_cache + seq, n_cache)
    new = pl.BlockSpec((1, seq, A_WIDTH), lambda b: (b, 0, 0))
    old = pl.BlockSpec((1, n_cache, A_WIDTH), lambda b: (b, 0, 0))
    out = pl.pallas_call(
        _attn_sample_kernel,
        grid=(batch,),
        in_specs=[new, new, new, old, old, _const_spec(bias.shape)],
        out_specs=new,
        out_shape=jax.ShapeDtypeStruct((batch, seq, A_WIDTH), BF16),
        scratch_shapes=[pltpu.VMEM((n_cache + seq, A_WIDTH), BF16),
                        pltpu.VMEM((n_cache + seq, A_WIDTH), BF16)],
        compiler_params=_params(1), name="attn_sample",
    )(q.reshape(batch, seq, A_WIDTH), k.reshape(batch, seq, A_WIDTH), v.reshape(batch, seq, A_WIDTH),
      cache_k.reshape(batch, n_cache, A_WIDTH), cache_v.reshape(batch, n_cache, A_WIDTH), bias)
    return out.reshape(batch * seq, A_WIDTH)


def _first_index(hit, iota, axis, limit):
    return jnp.min(jnp.where(hit, iota, limit), axis=axis, keepdims=True)


def _route(scores, sel):
    n_tok = scores.shape[1]
    neg = -jnp.inf
    grouped = sel.reshape(N_GROUPS, GROUP_SIZE, n_tok)
    member = lax.broadcasted_iota(jnp.int32, grouped.shape, 1)
    best = jnp.max(grouped, axis=1, keepdims=True)
    first = _first_index(grouped == best, member, 1, GROUP_SIZE)
    second = jnp.max(jnp.where(member == first, neg, grouped), axis=1, keepdims=True)
    group_score = best + second

    group_id = lax.broadcasted_iota(jnp.int32, group_score.shape, 0)
    keep = jnp.zeros(group_score.shape, F32)
    for _ in range(TOPK_GROUPS):
        top = jnp.max(group_score, axis=0, keepdims=True)
        hit = group_id == _first_index(group_score == top, group_id, 0, N_GROUPS)
        keep = jnp.where(hit, 1.0, keep)
        group_score = jnp.where(hit, neg, group_score)
    keep = jnp.broadcast_to(keep, grouped.shape).reshape(N_EXPERTS, n_tok)

    cand = jnp.where(keep > 0.0, sel, neg)
    expert_id = lax.broadcasted_iota(jnp.int32, cand.shape, 0)
    ids, weights = [], []
    for _ in range(TOP_K):
        top = jnp.max(cand, axis=0, keepdims=True)
        first = _first_index(cand == top, expert_id, 0, N_EXPERTS)
        hit = expert_id == first
        ids.append(first)
        weights.append(jnp.sum(jnp.where(hit, scores, 0.0), axis=0, keepdims=True))
        cand = jnp.where(hit, neg, cand)
    ids = jnp.concatenate(ids, axis=0)
    weights = jnp.concatenate(weights, axis=0)
    weights = weights / jnp.sum(weights, axis=0, keepdims=True) * ROUTED_SCALE
    return ids, weights


def _store_row_major_tiles(flat_ref, x):
    rows = x.shape[0]
    for j in range(D_TILES):
        flat_ref[pl.ds(j, rows, stride=D_TILES), :] = x[:, j * LANES:(j + 1) * LANES]


def _load_row_major_tiles(flat_ref, rows):
    return jnp.concatenate(
        [flat_ref[pl.ds(j, rows, stride=D_TILES), :] for j in range(D_TILES)], axis=1)


def _mix_kernel(x_ref, oa_ref, ob_ref, ga_ref, gb_ref, lng_ref, lnb_ref, wa_ref, wb_ref, wo_ref,
                g1_ref, b1_ref, rw_ref, x1_ref, x1t_ref, scores_ref, *, alpha):
    xn = _layer_norm(x_ref[...], lng_ref[...], lnb_ref[...])
    mix = (ga_ref[...].astype(F32) * _dot(oa_ref[...], wa_ref[...])
           + gb_ref[...].astype(F32) * _dot(ob_ref[...], wb_ref[...]))
    x1 = _layer_norm(alpha * xn + _dot(mix.astype(BF16), wo_ref[...]), g1_ref[...], b1_ref[...])
    x1_ref[...] = x1
    _store_row_major_tiles(x1t_ref, x1)
    scores_ref[...] = jax.nn.sigmoid(_dot_nt(rw_ref[...], x1, precision=lax.Precision.HIGHEST))


def _mix(x2d, oa, ob, ga, gb, ln_g, ln_b, wa, wb, wo, g1, b1, rw_t, *, alpha):
    n_rows = x2d.shape[0]
    tile = ROW_TILE
    row = lambda width: pl.BlockSpec((tile, width), lambda i: (i, 0))
    return pl.pallas_call(
        functools.partial(_mix_kernel, alpha=alpha),
        grid=(n_rows // tile,),
        in_specs=[row(D_MODEL), row(A_WIDTH), row(MLP_WIDTH), row(D_MODEL), row(D_MODEL),
                  _const_spec((1, D_MODEL)), _const_spec((1, D_MODEL)),
                  _const_spec(wa.shape), _const_spec(wb.shape), _const_spec(wo.shape),
                  _const_spec((1, D_MODEL)), _const_spec((1, D_MODEL)),
                  _const_spec(rw_t.shape)],
        out_specs=(row(D_MODEL), pl.BlockSpec((tile * D_TILES, LANES), lambda i: (i, 0)),
                   pl.BlockSpec((N_EXPERTS, tile), lambda i: (0, i))),
        out_shape=(jax.ShapeDtypeStruct((n_rows, D_MODEL), F32),
                   jax.ShapeDtypeStruct((n_rows * D_TILES, LANES), F32),
                   jax.ShapeDtypeStruct((N_EXPERTS, n_rows), F32)),
        compiler_params=_params(1), name="mix",
    )(x2d, oa, ob, ga, gb, ln_g, ln_b, wa, wb, wo, g1, b1, rw_t)


def _route_kernel(scores_ref, rb_ref, ids_ref, wts_ref):
    scores = scores_ref[...]
    ids, weights = _route(scores, scores + rb_ref[...])
    ids_ref[...] = ids
    wts_ref[...] = weights


def _route_call(scores, rb):
    n_tok = scores.shape[1]
    tile = min(ROUTE_TILE, n_tok)
    col = lambda rows: pl.BlockSpec((rows, tile), lambda i: (0, i))
    return pl.pallas_call(
        _route_kernel,
        grid=(n_tok // tile,),
        in_specs=[col(N_EXPERTS), _const_spec(rb.shape)],
        out_specs=(col(TOP_K), col(TOP_K)),
        out_shape=(jax.ShapeDtypeStruct((TOP_K, n_tok), jnp.int32),
                   jax.ShapeDtypeStruct((TOP_K, n_tok), F32)),
        compiler_params=_params(1), name="route",
    )(scores, rb)


def _moe_kernel(rows_ref, start_ref, x_ref, wts_ref, w1_ref, w3_ref, w2_ref, *rest, n_rows):
    out_ref, gat_ref, y_full_ref, y_half_ref, pending_ref = rest[-5:]
    acc = out_ref.at[0]
    e = pl.program_id(0)
    half = n_rows // 2
    y_refs = {n_rows: y_full_ref, half: y_half_ref}

    @pl.when(e == 0)
    def _():
        out_ref[...] = jnp.zeros(out_ref.shape, out_ref.dtype)
        y_full_ref[...] = jnp.zeros(y_full_ref.shape, y_full_ref.dtype)
        y_half_ref[...] = jnp.zeros(y_half_ref.shape, y_half_ref.dtype)
        pending_ref[0] = 0
        pending_ref[1] = 0

    first = start_ref[e]
    count = start_ref[e + 1] - first

    def tile_at(ref, row8):
        return ref.at[pl.ds(pl.multiple_of(row8, SUBLANES), SUBLANES), :]

    def scatter_add(base, n):
        y_ref = y_refs[n]
        for m0 in range(0, n, SCATTER_BATCH):
            updates = []
            for m in range(m0, m0 + SCATTER_BATCH):
                dst = tile_at(acc, rows_ref[base + m])
                updates.append((dst, dst[...] + y_ref[m * SUBLANES:(m + 1) * SUBLANES, :]))
            for dst, val in updates:
                dst[...] = val

    def run_block(n, nominal, begin, pending):
        redo = nominal - begin
        base = first + begin + n_rows
        for m in range(n):
            gat_ref[m * SUBLANES:(m + 1) * SUBLANES, :] = tile_at(x_ref, rows_ref[base + m])[...]
        xb = _load_row_major_tiles(gat_ref, n).astype(BF16)
        act = jax.nn.silu(_dot(xb, w1_ref[0])) * _dot(xb, w3_ref[0])
        y = _dot(act.astype(BF16), w2_ref[0])

        w_rows = pl.cdiv(n, LANES) + 1
        row_id = lax.broadcasted_iota(jnp.int32, (n, w_rows * LANES), 0)
        lane_id = lax.broadcasted_iota(jnp.int32, (n, w_rows * LANES), 1)
        w_row = base // LANES
        span = jnp.concatenate([wts_ref[w_row + r] for r in range(w_rows)], axis=1)
        picked = jnp.where((lane_id == row_id + base % LANES) & (row_id >= redo),
                           jnp.broadcast_to(span, lane_id.shape), 0.0)
        ys = y * jnp.sum(picked, axis=1, keepdims=True)
        scatter_add(pending, n)
        _store_row_major_tiles(y_refs[n], ys)
        return base

    full_blocks = count // n_rows
    rest_rows = count - full_blocks * n_rows
    main_blocks = full_blocks + (rest_rows > half).astype(jnp.int32)

    def main_block(b, pending):
        nominal = b * n_rows
        return run_block(n_rows, nominal, jnp.minimum(nominal, count - n_rows), pending)

    pending_ref[0] = lax.fori_loop(0, main_blocks, main_block, pending_ref[0])

    @pl.when((rest_rows > 0) & (rest_rows <= half))
    def _():
        pending_ref[1] = run_block(half, full_blocks * n_rows, count - half, pending_ref[1])

    @pl.when(e == pl.num_programs(0) - 1)
    def _():
        scatter_add(pending_ref[0], n_rows)
        scatter_add(pending_ref[1], half)


def _moe(ids, weights, x_tiles, w1, w3, w2):
    n_tok = ids.shape[1]
    g_tok = min(MOE_GROUP_TOKENS, n_tok)
    n_groups = n_tok // g_tok
    assert n_groups * g_tok == n_tok
    per_group = g_tok * TOP_K
    tok = lax.broadcasted_iota(jnp.int32, ids.shape, 1)
    key = ((tok // g_tok) * N_EXPERTS + ids) * g_tok + tok % g_tok
    key, wts = lax.sort((key.reshape(-1), weights.reshape(-1)), num_keys=1)
    rows8 = ((key % g_tok) * D_TILES).reshape(n_groups, per_group)
    wts = wts.reshape(n_groups, per_group)
    experts = jnp.arange(N_EXPERTS, dtype=jnp.int32)[None, None, :, None]
    counts = jnp.sum(ids.reshape(TOP_K, n_groups, 1, g_tok) == experts, axis=(0, 3), dtype=jnp.int32)
    start = jnp.concatenate([jnp.zeros((n_groups, 1), jnp.int32), jnp.cumsum(counts, axis=1)], axis=1)
    n_rows = 256 if per_group // N_EXPERTS >= 512 else 128
    pad = ((0, 0), (n_rows, n_rows + LANES))
    rows8 = jnp.pad(rows8, pad)
    wts = jnp.pad(wts, pad).reshape(n_groups, -1, 1, LANES)
    out = None
    for g in range(n_groups):
        out = _moe_group_call(g, n_groups, g_tok, n_rows, rows8[g], start[g], x_tiles, wts[g], w1, w3, w2, out)
    return out


def _moe_group_call(group, n_groups, g_tok, n_rows, rows8, start, x_tiles, wts, w1, w3, w2, prev):
    g_rows = g_tok * D_TILES
    assert x_tiles.shape[0] == n_groups * g_rows
    out_block = (1, g_rows, LANES)
    in_specs = [
        pl.BlockSpec((g_rows, LANES), lambda e, *_: (group, 0), pipeline_mode=pl.Buffered(1)),
        pl.BlockSpec(wts.shape, lambda e, *_: (0, 0, 0)),
        pl.BlockSpec((1, D_MODEL, EXPERT_DIM), lambda e, *_: (e, 0, 0)),
        pl.BlockSpec((1, D_MODEL, EXPERT_DIM), lambda e, *_: (e, 0, 0)),
        pl.BlockSpec((1, EXPERT_DIM, D_MODEL), lambda e, *_: (e, 0, 0)),
    ]
    args = [rows8, start, x_tiles, wts, w1, w3, w2]
    aliases = {}
    if prev is not None:
        in_specs.append(pl.BlockSpec(memory_space=pl.ANY))
        aliases = {len(args): 0}
        args.append(prev)
    return pl.pallas_call(
        functools.partial(_moe_kernel, n_rows=n_rows),
        grid_spec=pltpu.PrefetchScalarGridSpec(
            num_scalar_prefetch=2, grid=(N_EXPERTS,), in_specs=in_specs,
            out_specs=pl.BlockSpec(out_block, lambda e, *_: (group, 0, 0), pipeline_mode=pl.Buffered(1)),
            scratch_shapes=[pltpu.VMEM((n_rows * D_TILES, LANES), F32),
                            pltpu.VMEM((n_rows * D_TILES, LANES), F32),
                            pltpu.VMEM((n_rows // 2 * D_TILES, LANES), F32),
                            pltpu.SMEM((2,), jnp.int32)]),
        out_shape=jax.ShapeDtypeStruct((n_groups,) + out_block[1:], F32),
        input_output_aliases=aliases,
        compiler_params=_params(1), name="moe",
    )(*args)


def _final_kernel(x1_ref, routed_ref, p_ref, s1_ref, s3_ref, s2_ref, g2_ref, b2_ref, wg_ref, wp_ref,
                  o_ref, *, alpha):
    x1 = x1_ref[...]
    xb = x1.astype(BF16)
    act = jax.nn.silu(_dot(xb, s1_ref[...])) * _dot(xb, s3_ref[...])
    ff = _load_row_major_tiles(routed_ref.at[0], x1.shape[0]) + _dot(act.astype(BF16), s2_ref[...])
    x2 = _layer_norm(alpha * x1 + ff, g2_ref[...], b2_ref[...])
    gate = jax.nn.sigmoid(_dot(x2.astype(BF16), wg_ref[...]))
    o_ref[...] = x2 + gate * _dot(p_ref[...].astype(BF16), wp_ref[...])


def _final(x1, routed, p2d, s1, s3, s2, g2, b2, wg, wp, *, alpha):
    n_rows = x1.shape[0]
    tile = ROW_TILE
    row = lambda width: pl.BlockSpec((tile, width), lambda i: (i, 0))
    tiles_per_group = routed.shape[1] // D_TILES // tile
    routed_spec = pl.BlockSpec((1, tile * D_TILES, LANES),
                               lambda i: (i // tiles_per_group, i % tiles_per_group, 0))
    return pl.pallas_call(
        functools.partial(_final_kernel, alpha=alpha),
        grid=(n_rows // tile,),
        in_specs=[row(D_MODEL), routed_spec,
                  row(p2d.shape[1]),
                  _const_spec(s1.shape), _const_spec(s3.shape), _const_spec(s2.shape),
                  _const_spec((1, D_MODEL)), _const_spec((1, D_MODEL)),
                  _const_spec(wg.shape), _const_spec(wp.shape)],
        out_specs=row(D_MODEL),
        out_shape=jax.ShapeDtypeStruct((n_rows, D_MODEL), F32),
        compiler_params=_params(1), name="final",
    )(x1, routed, p2d, s1, s3, s2, g2, b2, wg, wp)


def _trunk_layer(x, p, cache_k, cache_v, lw, *, alpha, first_layer_ln):
    batch, seq, d = x.shape
    n_rows = batch * seq
    x2d = x.reshape(n_rows, d)
    prompt = cache_k is None
    mlp_len = min(seq, MLP_CHUNK)
    keep = min(PAST_CHUNKS * CHUNK, seq) if prompt else seq
    ln_g, ln_b = first_layer_ln

    w_s = (lw["mlp_w_s"][:, :mlp_len, :mlp_len]
           * jnp.tril(jnp.ones((mlp_len, mlp_len), F32))).astype(BF16)
    b_s = jnp.repeat(lw["mlp_b_s"][:, :mlp_len].T, MLP_WIDTH // MLP_GROUPS, axis=1)
    q, k, v, kf, vf, vn, ga, gb, ob = _proj(
        x2d, ln_g, ln_b, lw["w_in"].astype(BF16), lw["mlp_ln_g"][None], lw["mlp_ln_b"][None],
        w_s, b_s, seq=seq, mlp_len=mlp_len, keep_rows=keep)
    if prompt:
        oa = _attn_prompt(q, k, v, lw["attn_rel_bias"], batch=batch, seq=seq)
    else:
        oa = _attn_sample(q, k, v, cache_k, cache_v, lw["attn_rel_bias"], batch=batch, seq=seq)
    x1, x1_tiles, scores = _mix(
        x2d, oa, ob, ga, gb, ln_g, ln_b, lw["w_branch_a"].astype(BF16), lw["w_branch_b"].astype(BF16),
        lw["w_out"].astype(BF16), lw["ln1_g"][None], lw["ln1_b"][None], lw["router_w"].T, alpha=alpha)
    ids, weights = _route_call(scores, lw["router_bias"][:, None])
    routed = _moe(ids, weights, x1_tiles, lw["exp_w1"], lw["exp_w3"], lw["exp_w2"])
    y = _final(x1, routed, p.reshape(n_rows, -1), lw["shared_w1"].astype(BF16),
               lw["shared_w3"].astype(BF16), lw["shared_w2"].astype(BF16),
               lw["ln2_g"][None], lw["ln2_b"][None], lw["ple_w_gate"].astype(BF16),
               lw["ple_w_proj"].astype(BF16), alpha=alpha)
    new_k = kf.reshape(batch, keep, HEADS, HEAD_DIM)
    new_v = vf.reshape(batch, keep, HEADS, HEAD_DIM)
    new_mlp_v = vn.reshape(batch, mlp_len, MLP_WIDTH)
    return y.reshape(batch, seq, d), new_k, new_v, new_mlp_v


def kernel(x_prompt, x_sample, cache_attn_k, cache_attn_v, p_prompt, p_sample, ln_in_g, ln_in_b, w_in, attn_rel_bias, mlp_ln_g, mlp_ln_b, mlp_w_s, mlp_b_s, w_branch_a, w_branch_b, w_out, ln1_g, ln1_b, router_w, router_bias, exp_w1, exp_w3, exp_w2, shared_w1, shared_w3, shared_w2, ln2_g, ln2_b, ple_w_gate, ple_w_proj):
    depth = w_in.shape[0]
    assert depth == 1, "the fused proj/mix kernels apply LayerNorm_in themselves: single layer only"
    alpha = (2 * depth) ** 0.25
    stacked = dict(w_in=w_in, attn_rel_bias=attn_rel_bias, mlp_ln_g=mlp_ln_g, mlp_ln_b=mlp_ln_b,
                   mlp_w_s=mlp_w_s, mlp_b_s=mlp_b_s, w_branch_a=w_branch_a, w_branch_b=w_branch_b,
                   w_out=w_out, ln1_g=ln1_g, ln1_b=ln1_b, router_w=router_w, router_bias=router_bias,
                   exp_w1=exp_w1, exp_w3=exp_w3, exp_w2=exp_w2, shared_w1=shared_w1,
                   shared_w3=shared_w3, shared_w2=shared_w2, ln2_g=ln2_g, ln2_b=ln2_b,
                   ple_w_gate=ple_w_gate, ple_w_proj=ple_w_proj)
    lw = {name: value[0] for name, value in stacked.items()}
    for name in ("exp_w1", "exp_w3", "exp_w2"):
        lw[name] = lw[name].astype(BF16)
    ln_in =(ln_in_g[None], ln_in_b[None])
    yp, kp, vp, mp = _trunk_layer(x_prompt, p_prompt[0], None, None, lw, alpha=alpha, first_layer_ln=ln_in)
    ys, ks, vs, ms = _trunk_layer(x_sample, p_sample[0], cache_attn_k[0], cache_attn_v[0], lw,
                                  alpha=alpha, first_layer_ln=ln_in)
    return (yp, ys, kp[None], vp[None], ks[None], vs[None], mp[None], ms[None])
```

```python
import functools

import jax
import jax.numpy as jnp
import numpy as np
from jax import lax
from jax.experimental import pallas as pl
from jax.experimental.pallas import tpu as pltpu

F32 = jnp.float32
BF16 = jnp.bfloat16

D_MODEL = 1024
CHUNK = 64
HEADS = 8
HEAD_DIM = 64
A_WIDTH = HEADS * HEAD_DIM
PAST_CHUNKS = 8
MAX_REL = 256
MLP_CHUNK = 128
MLP_GROUPS = 8
MLP_WIDTH = 512
N_EXPERTS = 64
TOP_K = 8
N_GROUPS = 8
TOPK_GROUPS = 4
GROUP_SIZE = N_EXPERTS // N_GROUPS
EXPERT_DIM = 256
ROUTED_SCALE = 2.5
LN_EPS = 1e-5
MASKED = -1e30

LANES = 128
SUBLANES = 8
VMEM_LIMIT_BYTES = 56 * 1024 * 1024

ROW_TILE = 512
ROUTE_TILE = 2048
Q_CHUNKS = 4
MOE_GROUP_TOKENS = 4096
SCATTER_BATCH = 8
D_TILES = D_MODEL // LANES


def _layer_norm(x, g, b):
    xc = x - jnp.mean(x, axis=-1, keepdims=True)
    var = jnp.mean(xc * xc, axis=-1, keepdims=True)
    return xc * lax.rsqrt(var + LN_EPS) * g + b


def _dot(a, b):
    return jnp.dot(a, b, preferred_element_type=F32)


def _dot_nt(a, b, precision=None):
    return lax.dot_general(a, b, (((1,), (1,)), ((), ())),
                           preferred_element_type=F32, precision=precision)


def _const_spec(shape):
    zeros = (0,) * len(shape)
    return pl.BlockSpec(shape, lambda *_: zeros)


def _params(n_axes):
    return pltpu.CompilerParams(dimension_semantics=("arbitrary",) * n_axes,
                                vmem_limit_bytes=VMEM_LIMIT_BYTES)


def _proj_kernel(x_ref, lng_ref, lnb_ref, w_ref, mg_ref, mb_ref, ws_ref, bs_ref,
                 q_ref, k_ref, v_ref, kf_ref, vf_ref, vn_ref, ga_ref, gb_ref, ob_ref,
                 *, mlp_len, vn_rows):
    rows = x_ref.shape[0]
    xn = _layer_norm(x_ref[...], lng_ref[...], lnb_ref[...]).astype(BF16)

    def section(lo, width):
        return _dot(xn, w_ref[:, lo:lo + width])

    q = section(0, A_WIDTH)
    q_ref[...] = (q * (HEAD_DIM ** -0.5)).astype(BF16)
    k = section(A_WIDTH, A_WIDTH)
    k_ref[...] = k.astype(BF16)
    kf_ref[...] = k
    v = section(2 * A_WIDTH, A_WIDTH)
    v_ref[...] = v.astype(BF16)
    vf_ref[...] = v
    base = 3 * A_WIDTH
    un = jax.nn.gelu(section(base, MLP_WIDTH))
    vn = _layer_norm(jax.nn.gelu(section(base + MLP_WIDTH, MLP_WIDTH)), mg_ref[...], mb_ref[...])
    vn_ref[...] = vn[rows - vn_rows:, :]
    base += 2 * MLP_WIDTH
    ga_ref[...] = jax.nn.sigmoid(section(base, D_MODEL)).astype(BF16)
    gb_ref[...] = jax.nn.sigmoid(section(base + D_MODEL, D_MODEL)).astype(BF16)

    vnb = vn.astype(BF16)
    low_half = lax.broadcasted_iota(jnp.int32, (mlp_len, LANES), 1) < (MLP_WIDTH // MLP_GROUPS)
    for c in range(rows // mlp_len):
        r0 = c * mlp_len
        for p in range(MLP_GROUPS // 2):
            c0 = p * LANES
            slab = vnb[r0:r0 + mlp_len, c0:c0 + LANES]
            mixed = jnp.where(low_half, _dot(ws_ref[2 * p], slab), _dot(ws_ref[2 * p + 1], slab))
            gated = un[r0:r0 + mlp_len, c0:c0 + LANES] * (mixed + bs_ref[:, c0:c0 + LANES])
            ob_ref[r0:r0 + mlp_len, c0:c0 + LANES] = gated.astype(BF16)


def _proj(x2d, ln_g, ln_b, w_in, mlp_g, mlp_b, w_s, b_s, *, seq, mlp_len, keep_rows):
    n_rows = x2d.shape[0]
    tile = ROW_TILE
    assert n_rows % tile == 0 and tile % mlp_len == 0
    n_seq = n_rows // seq
    if seq >= tile:
        assert seq % tile == 0 and keep_rows == tile
        per_seq = seq // tile
        kv_rows, vn_rows = n_seq * tile, mlp_len
        kv_map = lambda i: (i // per_seq, 0)
    else:
        assert keep_rows == seq and mlp_len == seq
        kv_rows, vn_rows = n_rows, tile
        kv_map = lambda i: (i, 0)
    row = lambda width: pl.BlockSpec((tile, width), lambda i: (i, 0))
    n_in = w_in.shape[1]
    out_shape = (
        jax.ShapeDtypeStruct((n_rows, A_WIDTH), BF16),
        jax.ShapeDtypeStruct((n_rows, A_WIDTH), BF16),
        jax.ShapeDtypeStruct((n_rows, A_WIDTH), BF16),
        jax.ShapeDtypeStruct((kv_rows, A_WIDTH), F32),
        jax.ShapeDtypeStruct((kv_rows, A_WIDTH), F32),
        jax.ShapeDtypeStruct((kv_rows // tile * vn_rows, MLP_WIDTH), F32),
        jax.ShapeDtypeStruct((n_rows, D_MODEL), BF16),
        jax.ShapeDtypeStruct((n_rows, D_MODEL), BF16),
        jax.ShapeDtypeStruct((n_rows, MLP_WIDTH), BF16),
    )
    out_specs = (
        row(A_WIDTH), row(A_WIDTH), row(A_WIDTH),
        pl.BlockSpec((tile, A_WIDTH), kv_map), pl.BlockSpec((tile, A_WIDTH), kv_map),
        pl.BlockSpec((vn_rows, MLP_WIDTH), kv_map),
        row(D_MODEL), row(D_MODEL), row(MLP_WIDTH),
    )
    in_specs = [
        row(D_MODEL), _const_spec((1, D_MODEL)), _const_spec((1, D_MODEL)),
        _const_spec((D_MODEL, n_in)), _const_spec((1, MLP_WIDTH)), _const_spec((1, MLP_WIDTH)),
        _const_spec((MLP_GROUPS, mlp_len, mlp_len)), _const_spec((mlp_len, MLP_WIDTH)),
    ]
    return pl.pallas_call(
        functools.partial(_proj_kernel, mlp_len=mlp_len, vn_rows=vn_rows),
        grid=(n_rows // tile,), in_specs=in_specs, out_specs=out_specs, out_shape=out_shape,
        compiler_params=_params(1), name="proj",
    )(x2d, ln_g, ln_b, w_in, mlp_g, mlp_b, w_s, b_s)


def _attend_heads(q, keys, values, bias_ref, extra_mask, o_ref):
    n_q = q.shape[0]
    low_half = lax.broadcasted_iota(jnp.int32, (n_q, LANES), 1) < HEAD_DIM
    zero = jnp.zeros((), BF16)
    for p in range(HEADS // 2):
        c0 = p * LANES
        q2, k2, v2 = q[:, c0:c0 + LANES], keys[:, c0:c0 + LANES], values[:, c0:c0 + LANES]
        qs = jnp.concatenate([jnp.where(low_half, q2, zero), jnp.where(low_half, zero, q2)], axis=0)
        s = _dot_nt(qs, k2) + bias_ref[2 * p:2 * p + 2].reshape(2 * n_q, keys.shape[0])
        if extra_mask is not None:
            s = s + extra_mask
        e = jnp.exp(s - jnp.max(s, axis=-1, keepdims=True))
        denom = jnp.sum(e, axis=-1, keepdims=True)
        o = _dot(e.astype(BF16), v2) / denom
        o_ref[:, c0:c0 + LANES] = jnp.where(low_half, o[:n_q], o[n_q:]).astype(o_ref.dtype)


def _attn_prompt_kernel(q_ref, k_ref, v_ref, bias_ref, o_ref, *, window):
    n_q = q_ref.shape[1]
    start = pl.multiple_of(pl.program_id(1) * n_q, n_q)
    keys = k_ref[0, pl.ds(start, window), :]
    values = v_ref[0, pl.ds(start, window), :]
    col = lax.broadcasted_iota(jnp.int32, (1, window), 1)
    pad_mask = jnp.where(col + start >= PAST_CHUNKS * CHUNK, 0.0, MASKED).astype(F32)
    _attend_heads(q_ref[0], keys, values, bias_ref, pad_mask, o_ref.at[0])


def _relative_bias(rel_table, n_q, n_k, key_offset):
    period = pl.next_power_of_2(n_q + n_k)
    m = np.arange(period)
    diag = np.where(m < n_k, m, m - period)
    idx = np.clip(diag - key_offset, -MAX_REL, MAX_REL) + MAX_REL
    values = rel_table[:, idx].astype(F32)
    rows = jnp.tile(values, (1, n_q))[:, :n_q * (period - 1)]
    return rows.reshape(rel_table.shape[0], n_q, period - 1)[:, :, :n_k]


def _prompt_bias(rel_table):
    n_q, n_k = Q_CHUNKS * CHUNK, (Q_CHUNKS + PAST_CHUNKS) * CHUNK
    lag = np.arange(n_q)[:, None] // CHUNK - (np.arange(n_k)[None, :] // CHUNK - PAST_CHUNKS)
    in_band = (lag >= 0) & (lag <= PAST_CHUNKS)
    bias = _relative_bias(rel_table, n_q, n_k, PAST_CHUNKS * CHUNK)
    return jnp.where(in_band[None], bias, MASKED)


def _attn_prompt(q, k, v, rel_table, *, batch, seq):
    n_q = Q_CHUNKS * CHUNK
    window = (Q_CHUNKS + PAST_CHUNKS) * CHUNK
    pad = PAST_CHUNKS * CHUNK
    q3 = q.reshape(batch, seq, A_WIDTH)
    kp = jnp.pad(k.reshape(batch, seq, A_WIDTH), ((0, 0), (pad, 0), (0, 0)))
    vp = jnp.pad(v.reshape(batch, seq, A_WIDTH), ((0, 0), (pad, 0), (0, 0)))
    bias = _prompt_bias(rel_table)
    qspec = pl.BlockSpec((1, n_q, A_WIDTH), lambda b, i: (b, i, 0))
    kvspec = pl.BlockSpec((1, seq + pad, A_WIDTH), lambda b, i: (b, 0, 0))
    out = pl.pallas_call(
        functools.partial(_attn_prompt_kernel, window=window),
        grid=(batch, seq // n_q),
        in_specs=[qspec, kvspec, kvspec, _const_spec(bias.shape)],
        out_specs=qspec,
        out_shape=jax.ShapeDtypeStruct((batch, seq, A_WIDTH), BF16),
        compiler_params=_params(2), name="attn_prompt",
    )(q3, kp, vp, bias)
    return out.reshape(batch * seq, A_WIDTH)


def _attn_sample_kernel(q_ref, k_ref, v_ref, ck_ref, cv_ref, bias_ref, o_ref, kk_ref, vv_ref):
    n_cache = ck_ref.shape[1]
    n_new = k_ref.shape[1]
    kk_ref[0:n_cache, :] = ck_ref[0].astype(BF16)
    kk_ref[n_cache:n_cache + n_new, :] = k_ref[0]
    vv_ref[0:n_cache, :] = cv_ref[0].astype(BF16)
    vv_ref[n_cache:n_cache + n_new, :] = v_ref[0]
    _attend_heads(q_ref[0], kk_ref[...], vv_ref[...], bias_ref, None, o_ref.at[0])


def _attn_sample(q, k, v, cache_k, cache_v, rel_table, *, batch, seq):
    n_cache = cache_k.shape[1]
    bias = _relative_bias(rel_table, seq, n_cache + seq, n_cache)
    new = pl.BlockSpec((1, seq, A_WIDTH), lambda b: (b, 0, 0))
    old = pl.BlockSpec((1, n_cache, A_WIDTH), lambda b: (b, 0, 0))
    out = pl.pallas_call(
        _attn_sample_kernel,
        grid=(batch,),
        in_specs=[new, new, new, old, old, _const_spec(bias.shape)],
        out_specs=new,
        out_shape=jax.ShapeDtypeStruct((batch, seq, A_WIDTH), BF16),
        scratch_shapes=[pltpu.VMEM((n_cache + seq, A_WIDTH), BF16),
                        pltpu.VMEM((n_cache + seq, A_WIDTH), BF16)],
        compiler_params=_params(1), name="attn_sample",
    )(q.reshape(batch, seq, A_WIDTH), k.reshape(batch, seq, A_WIDTH), v.reshape(batch, seq, A_WIDTH),
      cache_k.reshape(batch, n_cache, A_WIDTH), cache_v.reshape(batch, n_cache, A_WIDTH), bias)
    return out.reshape(batch * seq, A_WIDTH)


def _first_index(hit, iota, axis, limit):
    return jnp.min(jnp.where(hit, iota, limit), axis=axis, keepdims=True)


def _route(scores, sel):
    n_tok = scores.shape[1]
    neg = -jnp.inf
    grouped = sel.reshape(N_GROUPS, GROUP_SIZE, n_tok)
    member = lax.broadcasted_iota(jnp.int32, grouped.shape, 1)
    best = jnp.max(grouped, axis=1, keepdims=True)
    first = _first_index(grouped == best, member, 1, GROUP_SIZE)
    second = jnp.max(jnp.where(member == first, neg, grouped), axis=1, keepdims=True)
    group_score = best + second

    group_id = lax.broadcasted_iota(jnp.int32, group_score.shape, 0)
    keep = jnp.zeros(group_score.shape, F32)
    for _ in range(TOPK_GROUPS):
        top = jnp.max(group_score, axis=0, keepdims=True)
        hit = group_id == _first_index(group_score == top, group_id, 0, N_GROUPS)
        keep = jnp.where(hit, 1.0, keep)
        group_score = jnp.where(hit, neg, group_score)
    keep = jnp.broadcast_to(keep, grouped.shape).reshape(N_EXPERTS, n_tok)

    cand = jnp.where(keep > 0.0, sel, neg)
    expert_id = lax.broadcasted_iota(jnp.int32, cand.shape, 0)
    ids, weights = [], []
    for _ in range(TOP_K):
        top = jnp.max(cand, axis=0, keepdims=True)
        first = _first_index(cand == top, expert_id, 0, N_EXPERTS)
        hit = expert_id == first
        ids.append(first)
        weights.append(jnp.sum(jnp.where(hit, scores, 0.0), axis=0, keepdims=True))
        cand = jnp.where(hit, neg, cand)
    ids = jnp.concatenate(ids, axis=0)
    weights = jnp.concatenate(weights, axis=0)
    weights = weights / jnp.sum(weights, axis=0, keepdims=True) * ROUTED_SCALE
    return ids, weights


def _store_row_major_tiles(flat_ref, x):
    rows = x.shape[0]
    for j in range(D_TILES):
        flat_ref[pl.ds(j, rows, stride=D_TILES), :] = x[:, j * LANES:(j + 1) * LANES]


def _load_row_major_tiles(flat_ref, rows):
    return jnp.concatenate(
        [flat_ref[pl.ds(j, rows, stride=D_TILES), :] for j in range(D_TILES)], axis=1)


def _split_bf16(w):
    hi = w.astype(BF16)
    return jnp.concatenate([hi, (w - hi.astype(F32)).astype(BF16)], axis=0)


def _mix_kernel(x_ref, oa_ref, ob_ref, ga_ref, gb_ref, lng_ref, lnb_ref, wa_ref, wb_ref, wo_ref,
                g1_ref, b1_ref, rw_ref, x1_ref, x1t_ref, scores_ref, *, alpha):
    xn = _layer_norm(x_ref[...], lng_ref[...], lnb_ref[...])
    mix = (ga_ref[...].astype(F32) * _dot(oa_ref[...], wa_ref[...])
           + gb_ref[...].astype(F32) * _dot(ob_ref[...], wb_ref[...]))
    x1 = _layer_norm(alpha * xn + _dot(mix.astype(BF16), wo_ref[...]), g1_ref[...], b1_ref[...])
    x1_ref[...] = x1
    _store_row_major_tiles(x1t_ref, x1)
    x1_hi = x1.astype(BF16)
    x1_lo = (x1 - x1_hi.astype(F32)).astype(BF16)
    by_hi = _dot_nt(rw_ref[...], x1_hi)
    logits = by_hi[:N_EXPERTS] + by_hi[N_EXPERTS:] + _dot_nt(rw_ref[:N_EXPERTS, :], x1_lo)
    scores_ref[...] = jax.nn.sigmoid(logits)


def _mix(x2d, oa, ob, ga, gb, ln_g, ln_b, wa, wb, wo, g1, b1, rw_t, *, alpha):
    n_rows = x2d.shape[0]
    tile = ROW_TILE
    row = lambda width: pl.BlockSpec((tile, width), lambda i: (i, 0))
    return pl.pallas_call(
        functools.partial(_mix_kernel, alpha=alpha),
        grid=(n_rows // tile,),
        in_specs=[row(D_MODEL), row(A_WIDTH), row(MLP_WIDTH), row(D_MODEL), row(D_MODEL),
                  _const_spec((1, D_MODEL)), _const_spec((1, D_MODEL)),
                  _const_spec(wa.shape), _const_spec(wb.shape), _const_spec(wo.shape),
                  _const_spec((1, D_MODEL)), _const_spec((1, D_MODEL)),
                  _const_spec(rw_t.shape)],
        out_specs=(row(D_MODEL), pl.BlockSpec((tile * D_TILES, LANES), lambda i: (i, 0)),
                   pl.BlockSpec((N_EXPERTS, tile), lambda i: (0, i))),
        out_shape=(jax.ShapeDtypeStruct((n_rows, D_MODEL), F32),
                   jax.ShapeDtypeStruct((n_rows * D_TILES, LANES), F32),
                   jax.ShapeDtypeStruct((N_EXPERTS, n_rows), F32)),
        compiler_params=_params(1), name="mix",
    )(x2d, oa, ob, ga, gb, ln_g, ln_b, wa, wb, wo, g1, b1, rw_t)


def _route_kernel(scores_ref, rb_ref, ids_ref, wts_ref):
    scores = scores_ref[...]
    ids, weights = _route(scores, scores + rb_ref[...])
    ids_ref[...] = ids
    wts_ref[...] = weights


def _route_call(scores, rb):
    n_tok = scores.shape[1]
    tile = min(ROUTE_TILE, n_tok)
    col = lambda rows: pl.BlockSpec((rows, tile), lambda i: (0, i))
    return pl.pallas_call(
        _route_kernel,
        grid=(n_tok // tile,),
        in_specs=[col(N_EXPERTS), _const_spec(rb.shape)],
        out_specs=(col(TOP_K), col(TOP_K)),
        out_shape=(jax.ShapeDtypeStruct((TOP_K, n_tok), jnp.int32),
                   jax.ShapeDtypeStruct((TOP_K, n_tok), F32)),
        compiler_params=_params(1), name="route",
    )(scores, rb)


def _moe_kernel(rows_ref, start_ref, x_ref, wts_ref, w1_ref, w3_ref, w2_ref, *rest, n_rows):
    out_ref, gat_ref, y_full_ref, y_half_ref, pending_ref = rest[-5:]
    acc = out_ref.at[0]
    e = pl.program_id(0)
    half = n_rows // 2
    y_refs = {n_rows: y_full_ref, half: y_half_ref}

    @pl.when(e == 0)
    def _():
        out_ref[...] = jnp.zeros(out_ref.shape, out_ref.dtype)
        y_full_ref[...] = jnp.zeros(y_full_ref.shape, y_full_ref.dtype)
        y_half_ref[...] = jnp.zeros(y_half_ref.shape, y_half_ref.dtype)
        pending_ref[0] = 0
        pending_ref[1] = 0

    first = start_ref[e]
    count = start_ref[e + 1] - first

    def tile_at(ref, row8):
        return ref.at[pl.ds(pl.multiple_of(row8, SUBLANES), SUBLANES), :]

    def scatter_add(base, n):
        y_ref = y_refs[n]
        for m0 in range(0, n, SCATTER_BATCH):
            updates = []
            for m in range(m0, m0 + SCATTER_BATCH):
                dst = tile_at(acc, rows_ref[base + m])
                updates.append((dst, dst[...] + y_ref[m * SUBLANES:(m + 1) * SUBLANES, :]))
            for dst, val in updates:
                dst[...] = val

    def run_block(n, nominal, begin, pending):
        redo = nominal - begin
        base = first + begin + n_rows
        for m in range(n):
            gat_ref[m * SUBLANES:(m + 1) * SUBLANES, :] = tile_at(x_ref, rows_ref[base + m])[...]
        xb = _load_row_major_tiles(gat_ref, n).astype(BF16)
        act = jax.nn.silu(_dot(xb, w1_ref[0])) * _dot(xb, w3_ref[0])
        y = _dot(act.astype(BF16), w2_ref[0])

        w_rows = pl.cdiv(n, LANES) + 1
        row_id = lax.broadcasted_iota(jnp.int32, (n, w_rows * LANES), 0)
        lane_id = lax.broadcasted_iota(jnp.int32, (n, w_rows * LANES), 1)
        w_row = base // LANES
        span = jnp.concatenate([wts_ref[w_row + r] for r in range(w_rows)], axis=1)
        picked = jnp.where((lane_id == row_id + base % LANES) & (row_id >= redo),
                           jnp.broadcast_to(span, lane_id.shape), 0.0)
        ys = y * jnp.sum(picked, axis=1, keepdims=True)
        scatter_add(pending, n)
        _store_row_major_tiles(y_refs[n], ys)
        return base

    full_blocks = count // n_rows
    rest_rows = count - full_blocks * n_rows
    main_blocks = full_blocks + (rest_rows > half).astype(jnp.int32)

    def main_block(b, pending):
        nominal = b * n_rows
        return run_block(n_rows, nominal, jnp.minimum(nominal, count - n_rows), pending)

    pending_ref[0] = lax.fori_loop(0, main_blocks, main_block, pending_ref[0])

    @pl.when((rest_rows > 0) & (rest_rows <= half))
    def _():
        pending_ref[1] = run_block(half, full_blocks * n_rows, count - half, pending_ref[1])

    @pl.when(e == pl.num_programs(0) - 1)
    def _():
        scatter_add(pending_ref[0], n_rows)
        scatter_add(pending_ref[1], half)


def _moe(ids, weights, x_tiles, w1, w3, w2):
    n_tok = ids.shape[1]
    g_tok = min(MOE_GROUP_TOKENS, n_tok)
    n_groups = n_tok // g_tok
    assert n_groups * g_tok == n_tok
    per_group = g_tok * TOP_K
    tok = lax.broadcasted_iota(jnp.int32, ids.shape, 1)
    key = ((tok // g_tok) * N_EXPERTS + ids) * g_tok + tok % g_tok
    key, wts = lax.sort((key.reshape(-1), weights.reshape(-1)), num_keys=1)
    rows8 = ((key % g_tok) * D_TILES).reshape(n_groups, per_group)
    wts = wts.reshape(n_groups, per_group)
    experts = jnp.arange(N_EXPERTS, dtype=jnp.int32)[None, None, :, None]
    counts = jnp.sum(ids.reshape(TOP_K, n_groups, 1, g_tok) == experts, axis=(0, 3), dtype=jnp.int32)
    start = jnp.concatenate([jnp.zeros((n_groups, 1), jnp.int32), jnp.cumsum(counts, axis=1)], axis=1)
    n_rows = 256 if per_group // N_EXPERTS >= 512 else 128
    pad = ((0, 0), (n_rows, n_rows + LANES))
    rows8 = jnp.pad(rows8, pad)
    wts = jnp.pad(wts, pad).reshape(n_groups, -1, 1, LANES)
    out = None
    for g in range(n_groups):
        out = _moe_group_call(g, n_groups, g_tok, n_rows, rows8[g], start[g], x_tiles, wts[g], w1, w3, w2, out)
    return out


def _moe_group_call(group, n_groups, g_tok, n_rows, rows8, start, x_tiles, wts, w1, w3, w2, prev):
    g_rows = g_tok * D_TILES
    assert x_tiles.shape[0] == n_groups * g_rows
    out_block = (1, g_rows, LANES)
    in_specs = [
        pl.BlockSpec((g_rows, LANES), lambda e, *_: (group, 0), pipeline_mode=pl.Buffered(1)),
        pl.BlockSpec(wts.shape, lambda e, *_: (0, 0, 0)),
        pl.BlockSpec((1, D_MODEL, EXPERT_DIM), lambda e, *_: (e, 0, 0)),
        pl.BlockSpec((1, D_MODEL, EXPERT_DIM), lambda e, *_: (e, 0, 0)),
        pl.BlockSpec((1, EXPERT_DIM, D_MODEL), lambda e, *_: (e, 0, 0)),
    ]
    args = [rows8, start, x_tiles, wts, w1, w3, w2]
    aliases = {}
    if prev is not None:
        in_specs.append(pl.BlockSpec(memory_space=pl.ANY))
        aliases = {len(args): 0}
        args.append(prev)
    return pl.pallas_call(
        functools.partial(_moe_kernel, n_rows=n_rows),
        grid_spec=pltpu.PrefetchScalarGridSpec(
            num_scalar_prefetch=2, grid=(N_EXPERTS,), in_specs=in_specs,
            out_specs=pl.BlockSpec(out_block, lambda e, *_: (group, 0, 0), pipeline_mode=pl.Buffered(1)),
            scratch_shapes=[pltpu.VMEM((n_rows * D_TILES, LANES), F32),
                            pltpu.VMEM((n_rows * D_TILES, LANES), F32),
                            pltpu.VMEM((n_rows // 2 * D_TILES, LANES), F32),
                            pltpu.SMEM((2,), jnp.int32)]),
        out_shape=jax.ShapeDtypeStruct((n_groups,) + out_block[1:], F32),
        input_output_aliases=aliases,
        compiler_params=_params(1), name="moe",
    )(*args)


def _final_kernel(x1_ref, routed_ref, p_ref, s1_ref, s3_ref, s2_ref, g2_ref, b2_ref, wg_ref, wp_ref,
                  o_ref, *, alpha):
    x1 = x1_ref[...]
    xb = x1.astype(BF16)
    act = jax.nn.silu(_dot(xb, s1_ref[...])) * _dot(xb, s3_ref[...])
    ff = _load_row_major_tiles(routed_ref.at[0], x1.shape[0]) + _dot(act.astype(BF16), s2_ref[...])
    x2 = _layer_norm(alpha * x1 + ff, g2_ref[...], b2_ref[...])
    gate = jax.nn.sigmoid(_dot(x2.astype(BF16), wg_ref[...]))
    o_ref[...] = x2 + gate * _dot(p_ref[...].astype(BF16), wp_ref[...])


def _final(x1, routed, p2d, s1, s3, s2, g2, b2, wg, wp, *, alpha):
    n_rows = x1.shape[0]
    tile = ROW_TILE
    row = lambda width: pl.BlockSpec((tile, width), lambda i: (i, 0))
    tiles_per_group = routed.shape[1] // D_TILES // tile
    routed_spec = pl.BlockSpec((1, tile * D_TILES, LANES),
                               lambda i: (i // tiles_per_group, i % tiles_per_group, 0))
    return pl.pallas_call(
        functools.partial(_final_kernel, alpha=alpha),
        grid=(n_rows // tile,),
        in_specs=[row(D_MODEL), routed_spec,
                  row(p2d.shape[1]),
                  _const_spec(s1.shape), _const_spec(s3.shape), _const_spec(s2.shape),
                  _const_spec((1, D_MODEL)), _const_spec((1, D_MODEL)),
                  _const_spec(wg.shape), _const_spec(wp.shape)],
        out_specs=row(D_MODEL),
        out_shape=jax.ShapeDtypeStruct((n_rows, D_MODEL), F32),
        compiler_params=_params(1), name="final",
    )(x1, routed, p2d, s1, s3, s2, g2, b2, wg, wp)


def _trunk_layer(x, p, cache_k, cache_v, lw, *, alpha, first_layer_ln):
    batch, seq, d = x.shape
    n_rows = batch * seq
    x2d = x.reshape(n_rows, d)
    prompt = cache_k is None
    mlp_len = min(seq, MLP_CHUNK)
    keep = min(PAST_CHUNKS * CHUNK, seq) if prompt else seq
    ln_g, ln_b = first_layer_ln

    w_s = (lw["mlp_w_s"][:, :mlp_len, :mlp_len]
           * jnp.tril(jnp.ones((mlp_len, mlp_len), F32))).astype(BF16)
    b_s = jnp.repeat(lw["mlp_b_s"][:, :mlp_len].T, MLP_WIDTH // MLP_GROUPS, axis=1)
    q, k, v, kf, vf, vn, ga, gb, ob = _proj(
        x2d, ln_g, ln_b, lw["w_in"].astype(BF16), lw["mlp_ln_g"][None], lw["mlp_ln_b"][None],
        w_s, b_s, seq=seq, mlp_len=mlp_len, keep_rows=keep)
    if prompt:
        oa = _attn_prompt(q, k, v, lw["attn_rel_bias"], batch=batch, seq=seq)
    else:
        oa = _attn_sample(q, k, v, cache_k, cache_v, lw["attn_rel_bias"], batch=batch, seq=seq)
    x1, x1_tiles, scores = _mix(
        x2d, oa, ob, ga, gb, ln_g, ln_b, lw["w_branch_a"].astype(BF16), lw["w_branch_b"].astype(BF16),
        lw["w_out"].astype(BF16), lw["ln1_g"][None], lw["ln1_b"][None], _split_bf16(lw["router_w"].T),
        alpha=alpha)
    ids, weights = _route_call(scores, lw["router_bias"][:, None])
    routed = _moe(ids, weights, x1_tiles, lw["exp_w1"], lw["exp_w3"], lw["exp_w2"])
    y = _final(x1, routed, p.reshape(n_rows, -1), lw["shared_w1"].astype(BF16),
               lw["shared_w3"].astype(BF16), lw["shared_w2"].astype(BF16),
               lw["ln2_g"][None], lw["ln2_b"][None], lw["ple_w_gate"].astype(BF16),
               lw["ple_w_proj"].astype(BF16), alpha=alpha)
    new_k = kf.reshape(batch, keep, HEADS, HEAD_DIM)
    new_v = vf.reshape(batch, keep, HEADS, HEAD_DIM)
    new_mlp_v = vn.reshape(batch, mlp_len, MLP_WIDTH)
    return y.reshape(batch, seq, d), new_k, new_v, new_mlp_v


def kernel(x_prompt, x_sample, cache_attn_k, cache_attn_v, p_prompt, p_sample, ln_in_g, ln_in_b, w_in, attn_rel_bias, mlp_ln_g, mlp_ln_b, mlp_w_s, mlp_b_s, w_branch_a, w_branch_b, w_out, ln1_g, ln1_b, router_w, router_bias, exp_w1, exp_w3, exp_w2, shared_w1, shared_w3, shared_w2, ln2_g, ln2_b, ple_w_gate, ple_w_proj):
    depth = w_in.shape[0]
    assert depth == 1, "the fused proj/mix kernels apply LayerNorm_in themselves: single layer only"
    alpha = (2 * depth) ** 0.25
    stacked = dict(w_in=w_in, attn_rel_bias=attn_rel_bias, mlp_ln_g=mlp_ln_g, mlp_ln_b=mlp_ln_b,
                   mlp_w_s=mlp_w_s, mlp_b_s=mlp_b_s, w_branch_a=w_branch_a, w_branch_b=w_branch_b,
                   w_out=w_out, ln1_g=ln1_g, ln1_b=ln1_b, router_w=router_w, router_bias=router_bias,
                   exp_w1=exp_w1, exp_w3=exp_w3, exp_w2=exp_w2, shared_w1=shared_w1,
                   shared_w3=shared_w3, shared_w2=shared_w2, ln2_g=ln2_g, ln2_b=ln2_b,
                   ple_w_gate=ple_w_gate, ple_w_proj=ple_w_proj)
    lw = {name: value[0] for name, value in stacked.items()}
    for name in ("exp_w1", "exp_w3", "exp_w2"):
        lw[name] = lw[name].astype(BF16)
    ln_in =(ln_in_g[None], ln_in_b[None])
    yp, kp, vp, mp = _trunk_layer(x_prompt, p_prompt[0], None, None, lw, alpha=alpha, first_layer_ln=ln_in)
    ys, ks, vs, ms = _trunk_layer(x_sample, p_sample[0], cache_attn_k[0], cache_attn_v[0], lw,
                                  alpha=alpha, first_layer_ln=ln_in)
    return (yp, ys, kp[None], vp[None], ks[None], vs[None], mp[None], ms[None])
```

```python
import functools

import jax
import jax.numpy as jnp
import numpy as np
from jax import lax
from jax.experimental import pallas as pl
from jax.experimental.pallas import tpu as pltpu

F32 = jnp.float32
BF16 = jnp.bfloat16

D_MODEL = 1024
CHUNK = 64
HEADS = 8
HEAD_DIM = 64
A_WIDTH = HEADS * HEAD_DIM
PAST_CHUNKS = 8
MAX_REL = 256
MLP_CHUNK = 128
MLP_GROUPS = 8
MLP_WIDTH = 512
N_EXPERTS = 64
TOP_K = 8
N_GROUPS = 8
TOPK_GROUPS = 4
GROUP_SIZE = N_EXPERTS // N_GROUPS
EXPERT_DIM = 256
ROUTED_SCALE = 2.5
LN_EPS = 1e-5
MASKED = -1e30

LANES = 128
SUBLANES = 8
VMEM_LIMIT_BYTES = 56 * 1024 * 1024

ROW_TILE = 512
ROUTE_TILE = 2048
Q_CHUNKS = 4
MOE_GROUP_TOKENS = 4096
SCATTER_BATCH = 8
D_TILES = D_MODEL // LANES


def _layer_norm(x, g, b):
    xc = x - jnp.mean(x, axis=-1, keepdims=True)
    var = jnp.mean(xc * xc, axis=-1, keepdims=True)
    return xc * lax.rsqrt(var + LN_EPS) * g + b


def _dot(a, b):
    return jnp.dot(a, b, preferred_element_type=F32)


def _dot_nt(a, b, precision=None):
    return lax.dot_general(a, b, (((1,), (1,)), ((), ())),
                           preferred_element_type=F32, precision=precision)


def _const_spec(shape):
    zeros = (0,) * len(shape)
    return pl.BlockSpec(shape, lambda *_: zeros)


def _params(n_axes):
    return pltpu.CompilerParams(dimension_semantics=("arbitrary",) * n_axes,
                                vmem_limit_bytes=VMEM_LIMIT_BYTES)


def _proj_kernel(x_ref, lng_ref, lnb_ref, w_ref, mg_ref, mb_ref, ws_ref, bs_ref,
                 q_ref, k_ref, v_ref, kf_ref, vf_ref, vn_ref, ga_ref, gb_ref, ob_ref,
                 *, mlp_len, vn_rows):
    rows = x_ref.shape[0]
    xn = _layer_norm(x_ref[...], lng_ref[...], lnb_ref[...]).astype(BF16)

    def section(lo, width):
        return _dot(xn, w_ref[:, lo:lo + width])

    q = section(0, A_WIDTH)
    q_ref[...] = (q * (HEAD_DIM ** -0.5)).astype(BF16)
    k = section(A_WIDTH, A_WIDTH)
    k_ref[...] = k.astype(BF16)
    kf_ref[...] = k
    v = section(2 * A_WIDTH, A_WIDTH)
    v_ref[...] = v.astype(BF16)
    vf_ref[...] = v
    base = 3 * A_WIDTH
    un = jax.nn.gelu(section(base, MLP_WIDTH))
    vn = _layer_norm(jax.nn.gelu(section(base + MLP_WIDTH, MLP_WIDTH)), mg_ref[...], mb_ref[...])
    vn_ref[...] = vn[rows - vn_rows:, :]
    base += 2 * MLP_WIDTH
    ga_ref[...] = jax.nn.sigmoid(section(base, D_MODEL)).astype(BF16)
    gb_ref[...] = jax.nn.sigmoid(section(base + D_MODEL, D_MODEL)).astype(BF16)

    vnb = vn.astype(BF16)
    low_half = lax.broadcasted_iota(jnp.int32, (mlp_len, LANES), 1) < (MLP_WIDTH // MLP_GROUPS)
    for c in range(rows // mlp_len):
        r0 = c * mlp_len
        for p in range(MLP_GROUPS // 2):
            c0 = p * LANES
            slab = vnb[r0:r0 + mlp_len, c0:c0 + LANES]
            mixed = jnp.where(low_half, _dot(ws_ref[2 * p], slab), _dot(ws_ref[2 * p + 1], slab))
            gated = un[r0:r0 + mlp_len, c0:c0 + LANES] * (mixed + bs_ref[:, c0:c0 + LANES])
            ob_ref[r0:r0 + mlp_len, c0:c0 + LANES] = gated.astype(BF16)


def _proj(x2d, ln_g, ln_b, w_in, mlp_g, mlp_b, w_s, b_s, *, seq, mlp_len, keep_rows):
    n_rows = x2d.shape[0]
    tile = ROW_TILE
    assert n_rows % tile == 0 and tile % mlp_len == 0
    n_seq = n_rows // seq
    if seq >= tile:
        assert seq % tile == 0 and keep_rows == tile
        per_seq = seq // tile
        kv_rows, vn_rows = n_seq * tile, mlp_len
        kv_map = lambda i: (i // per_seq, 0)
    else:
        assert keep_rows == seq and mlp_len == seq
        kv_rows, vn_rows = n_rows, tile
        kv_map = lambda i: (i, 0)
    row = lambda width: pl.BlockSpec((tile, width), lambda i: (i, 0))
    n_in = w_in.shape[1]
    out_shape = (
        jax.ShapeDtypeStruct((n_rows, A_WIDTH), BF16),
        jax.ShapeDtypeStruct((n_rows, A_WIDTH), BF16),
        jax.ShapeDtypeStruct((n_rows, A_WIDTH), BF16),
        jax.ShapeDtypeStruct((kv_rows, A_WIDTH), F32),
        jax.ShapeDtypeStruct((kv_rows, A_WIDTH), F32),
        jax.ShapeDtypeStruct((kv_rows // tile * vn_rows, MLP_WIDTH), F32),
        jax.ShapeDtypeStruct((n_rows, D_MODEL), BF16),
        jax.ShapeDtypeStruct((n_rows, D_MODEL), BF16),
        jax.ShapeDtypeStruct((n_rows, MLP_WIDTH), BF16),
    )
    out_specs = (
        row(A_WIDTH), row(A_WIDTH), row(A_WIDTH),
        pl.BlockSpec((tile, A_WIDTH), kv_map), pl.BlockSpec((tile, A_WIDTH), kv_map),
        pl.BlockSpec((vn_rows, MLP_WIDTH), kv_map),
        row(D_MODEL), row(D_MODEL), row(MLP_WIDTH),
    )
    in_specs = [
        row(D_MODEL), _const_spec((1, D_MODEL)), _const_spec((1, D_MODEL)),
        _const_spec((D_MODEL, n_in)), _const_spec((1, MLP_WIDTH)), _const_spec((1, MLP_WIDTH)),
        _const_spec((MLP_GROUPS, mlp_len, mlp_len)), _const_spec((mlp_len, MLP_WIDTH)),
    ]
    return pl.pallas_call(
        functools.partial(_proj_kernel, mlp_len=mlp_len, vn_rows=vn_rows),
        grid=(n_rows // tile,), in_specs=in_specs, out_specs=out_specs, out_shape=out_shape,
        compiler_params=_params(1), name="proj",
    )(x2d, ln_g, ln_b, w_in, mlp_g, mlp_b, w_s, b_s)


def _attend_heads(q, keys, values, bias_ref, extra_mask, o_ref):
    n_q = q.shape[0]
    low_half = lax.broadcasted_iota(jnp.int32, (n_q, LANES), 1) < HEAD_DIM
    zero = jnp.zeros((), BF16)
    for p in range(HEADS // 2):
        c0 = p * LANES
        q2, k2, v2 = q[:, c0:c0 + LANES], keys[:, c0:c0 + LANES], values[:, c0:c0 + LANES]
        qs = jnp.concatenate([jnp.where(low_half, q2, zero), jnp.where(low_half, zero, q2)], axis=0)
        s = _dot_nt(qs, k2) + bias_ref[2 * p:2 * p + 2].reshape(2 * n_q, keys.shape[0])
        if extra_mask is not None:
            s = s + extra_mask
        e = jnp.exp(s - jnp.max(s, axis=-1, keepdims=True))
        denom = jnp.sum(e, axis=-1, keepdims=True)
        o = _dot(e.astype(BF16), v2) / denom
        o_ref[:, c0:c0 + LANES] = jnp.where(low_half, o[:n_q], o[n_q:]).astype(o_ref.dtype)


def _attn_prompt_kernel(q_ref, k_ref, v_ref, bias_ref, o_ref, *, window, lead_steps):
    n_q = q_ref.shape[1]
    start = pl.multiple_of(jnp.maximum(pl.program_id(1) - lead_steps, 0) * n_q, n_q)
    keys = k_ref[0, pl.ds(start, window), :]
    values = v_ref[0, pl.ds(start, window), :]
    _attend_heads(q_ref[0], keys, values, bias_ref.at[0], None, o_ref.at[0])


def _relative_bias(rel_table, n_q, n_k, key_offset):
    period = pl.next_power_of_2(n_q + n_k)
    m = np.arange(period)
    diag = np.where(m < n_k, m, m - period)
    idx = np.clip(diag - key_offset, -MAX_REL, MAX_REL) + MAX_REL
    values = rel_table[:, idx].astype(F32)
    rows = jnp.tile(values, (1, n_q))[:, :n_q * (period - 1)]
    return rows.reshape(rel_table.shape[0], n_q, period - 1)[:, :, :n_k]


def _prompt_bias(rel_table):
    n_q, n_k = Q_CHUNKS * CHUNK, (Q_CHUNKS + PAST_CHUNKS) * CHUNK
    assert PAST_CHUNKS % Q_CHUNKS == 0
    variants = []
    for v in range(PAST_CHUNKS // Q_CHUNKS + 1):
        lag = np.arange(n_q)[:, None] // CHUNK + v * Q_CHUNKS - np.arange(n_k)[None, :] // CHUNK
        in_band = (lag >= 0) & (lag <= PAST_CHUNKS)
        bias = _relative_bias(rel_table, n_q, n_k, v * n_q)
        variants.append(jnp.where(in_band[None], bias, MASKED))
    return jnp.stack(variants)


def _attn_prompt(q, k, v, rel_table, *, batch, seq):
    n_q = Q_CHUNKS * CHUNK
    window = (Q_CHUNKS + PAST_CHUNKS) * CHUNK
    lead = PAST_CHUNKS // Q_CHUNKS
    assert seq >= window
    bias = _prompt_bias(rel_table)
    qspec = pl.BlockSpec((1, n_q, A_WIDTH), lambda b, i: (b, i, 0))
    kvspec = pl.BlockSpec((1, seq, A_WIDTH), lambda b, i: (b, 0, 0))
    bias_spec = pl.BlockSpec((1,) + bias.shape[1:], lambda b, i: (jnp.minimum(i, lead), 0, 0, 0))
    out = pl.pallas_call(
        functools.partial(_attn_prompt_kernel, window=window, lead_steps=lead),
        grid=(batch, seq // n_q),
        in_specs=[qspec, kvspec, kvspec, bias_spec],
        out_specs=qspec,
        out_shape=jax.ShapeDtypeStruct((batch, seq, A_WIDTH), BF16),
        compiler_params=_params(2), name="attn_prompt",
    )(q.reshape(batch, seq, A_WIDTH), k.reshape(batch, seq, A_WIDTH), v.reshape(batch, seq, A_WIDTH), bias)
    return out.reshape(batch * seq, A_WIDTH)


def _attn_sample_kernel(q_ref, k_ref, v_ref, ck_ref, cv_ref, bias_ref, o_ref, kk_ref, vv_ref):
    n_cache = ck_ref.shape[1]
    n_new = k_ref.shape[1]
    kk_ref[0:n_cache, :] = ck_ref[0].astype(BF16)
    kk_ref[n_cache:n_cache + n_new, :] = k_ref[0]
    vv_ref[0:n_cache, :] = cv_ref[0].astype(BF16)
    vv_ref[n_cache:n_cache + n_new, :] = v_ref[0]
    _attend_heads(q_ref[0], kk_ref[...], vv_ref[...], bias_ref, None, o_ref.at[0])


def _attn_sample(q, k, v, cache_k, cache_v, rel_table, *, batch, seq):
    n_cache = cache_k.shape[1]
    bias = _relative_bias(rel_table, seq, n_cache + seq, n_cache)
    new = pl.BlockSpec((1, seq, A_WIDTH), lambda b: (b, 0, 0))
    old = pl.BlockSpec((1, n_cache, A_WIDTH), lambda b: (b, 0, 0))
    out = pl.pallas_call(
        _attn_sample_kernel,
        grid=(batch,),
        in_specs=[new, new, new, old, old, _const_spec(bias.shape)],
        out_specs=new,
        out_shape=jax.ShapeDtypeStruct((batch, seq, A_WIDTH), BF16),
        scratch_shapes=[pltpu.VMEM((n_cache + seq, A_WIDTH), BF16),
                        pltpu.VMEM((n_cache + seq, A_WIDTH), BF16)],
        compiler_params=_params(1), name="attn_sample",
    )(q.reshape(batch, seq, A_WIDTH), k.reshape(batch, seq, A_WIDTH), v.reshape(batch, seq, A_WIDTH),
      cache_k.reshape(batch, n_cache, A_WIDTH), cache_v.reshape(batch, n_cache, A_WIDTH), bias)
    return out.reshape(batch * seq, A_WIDTH)


def _first_index(hit, iota, axis, limit):
    return jnp.min(jnp.where(hit, iota, limit), axis=axis, keepdims=True)


def _route(scores, sel):
    n_tok = scores.shape[1]
    neg = -jnp.inf
    grouped = sel.reshape(N_GROUPS, GROUP_SIZE, n_tok)
    member = lax.broadcasted_iota(jnp.int32, grouped.shape, 1)
    best = jnp.max(grouped, axis=1, keepdims=True)
    first = _first_index(grouped == best, member, 1, GROUP_SIZE)
    second = jnp.max(jnp.where(member == first, neg, grouped), axis=1, keepdims=True)
    group_score = best + second

    group_id = lax.broadcasted_iota(jnp.int32, group_score.shape, 0)
    keep = jnp.zeros(group_score.shape, F32)
    for _ in range(TOPK_GROUPS):
        top = jnp.max(group_score, axis=0, keepdims=True)
        hit = group_id == _first_index(group_score == top, group_id, 0, N_GROUPS)
        keep = jnp.where(hit, 1.0, keep)
        group_score = jnp.where(hit, neg, group_score)
    keep = jnp.broadcast_to(keep, grouped.shape).reshape(N_EXPERTS, n_tok)

    cand = jnp.where(keep > 0.0, sel, neg)
    expert_id = lax.broadcasted_iota(jnp.int32, cand.shape, 0)
    ids, weights = [], []
    for _ in range(TOP_K):
        top = jnp.max(cand, axis=0, keepdims=True)
        first = _first_index(cand == top, expert_id, 0, N_EXPERTS)
        hit = expert_id == first
        ids.append(first)
        weights.append(jnp.sum(jnp.where(hit, scores, 0.0), axis=0, keepdims=True))
        cand = jnp.where(hit, neg, cand)
    ids = jnp.concatenate(ids, axis=0)
    weights = jnp.concatenate(weights, axis=0)
    weights = weights / jnp.sum(weights, axis=0, keepdims=True) * ROUTED_SCALE
    return ids, weights


def _store_row_major_tiles(flat_ref, x):
    rows = x.shape[0]
    for j in range(D_TILES):
        flat_ref[pl.ds(j, rows, stride=D_TILES), :] = x[:, j * LANES:(j + 1) * LANES]


def _load_row_major_tiles(flat_ref, rows):
    return jnp.concatenate(
        [flat_ref[pl.ds(j, rows, stride=D_TILES), :] for j in range(D_TILES)], axis=1)


def _split_bf16(w):
    hi = w.astype(BF16)
    return jnp.concatenate([hi, (w - hi.astype(F32)).astype(BF16)], axis=0)


def _mix_kernel(x_ref, oa_ref, ob_ref, ga_ref, gb_ref, lng_ref, lnb_ref, wa_ref, wb_ref, wo_ref,
                g1_ref, b1_ref, rw_ref, x1_ref, x1t_ref, scores_ref, *, alpha):
    xn = _layer_norm(x_ref[...], lng_ref[...], lnb_ref[...])
    mix = (ga_ref[...].astype(F32) * _dot(oa_ref[...], wa_ref[...])
           + gb_ref[...].astype(F32) * _dot(ob_ref[...], wb_ref[...]))
    x1 = _layer_norm(alpha * xn + _dot(mix.astype(BF16), wo_ref[...]), g1_ref[...], b1_ref[...])
    x1_ref[...] = x1
    _store_row_major_tiles(x1t_ref, x1)
    x1_hi = x1.astype(BF16)
    x1_lo = (x1 - x1_hi.astype(F32)).astype(BF16)
    by_hi = _dot_nt(rw_ref[...], x1_hi)
    logits = by_hi[:N_EXPERTS] + by_hi[N_EXPERTS:] + _dot_nt(rw_ref[:N_EXPERTS, :], x1_lo)
    scores_ref[...] = jax.nn.sigmoid(logits)


def _mix(x2d, oa, ob, ga, gb, ln_g, ln_b, wa, wb, wo, g1, b1, rw_t, *, alpha):
    n_rows = x2d.shape[0]
    tile = ROW_TILE
    row = lambda width: pl.BlockSpec((tile, width), lambda i: (i, 0))
    return pl.pallas_call(
        functools.partial(_mix_kernel, alpha=alpha),
        grid=(n_rows // tile,),
        in_specs=[row(D_MODEL), row(A_WIDTH), row(MLP_WIDTH), row(D_MODEL), row(D_MODEL),
                  _const_spec((1, D_MODEL)), _const_spec((1, D_MODEL)),
                  _const_spec(wa.shape), _const_spec(wb.shape), _const_spec(wo.shape),
                  _const_spec((1, D_MODEL)), _const_spec((1, D_MODEL)),
                  _const_spec(rw_t.shape)],
        out_specs=(row(D_MODEL), pl.BlockSpec((tile * D_TILES, LANES), lambda i: (i, 0)),
                   pl.BlockSpec((N_EXPERTS, tile), lambda i: (0, i))),
        out_shape=(jax.ShapeDtypeStruct((n_rows, D_MODEL), F32),
                   jax.ShapeDtypeStruct((n_rows * D_TILES, LANES), F32),
                   jax.ShapeDtypeStruct((N_EXPERTS, n_rows), F32)),
        compiler_params=_params(1), name="mix",
    )(x2d, oa, ob, ga, gb, ln_g, ln_b, wa, wb, wo, g1, b1, rw_t)


def _route_kernel(scores_ref, rb_ref, ids_ref, wts_ref):
    scores = scores_ref[...]
    ids, weights = _route(scores, scores + rb_ref[...])
    ids_ref[...] = ids
    wts_ref[...] = weights


def _route_call(scores, rb):
    n_tok = scores.shape[1]
    tile = min(ROUTE_TILE, n_tok)
    col = lambda rows: pl.BlockSpec((rows, tile), lambda i: (0, i))
    return pl.pallas_call(
        _route_kernel,
        grid=(n_tok // tile,),
        in_specs=[col(N_EXPERTS), _const_spec(rb.shape)],
        out_specs=(col(TOP_K), col(TOP_K)),
        out_shape=(jax.ShapeDtypeStruct((TOP_K, n_tok), jnp.int32),
                   jax.ShapeDtypeStruct((TOP_K, n_tok), F32)),
        compiler_params=_params(1), name="route",
    )(scores, rb)


def _moe_kernel(rows_ref, start_ref, x_ref, wts_ref, w1_ref, w3_ref, w2_ref, *rest, n_rows):
    out_ref, gat_ref, y_full_ref, y_half_ref, pending_ref = rest[-5:]
    acc = out_ref.at[0]
    e = pl.program_id(0)
    half = n_rows // 2
    y_refs = {n_rows: y_full_ref, half: y_half_ref}

    @pl.when(e == 0)
    def _():
        out_ref[...] = jnp.zeros(out_ref.shape, out_ref.dtype)
        y_full_ref[...] = jnp.zeros(y_full_ref.shape, y_full_ref.dtype)
        y_half_ref[...] = jnp.zeros(y_half_ref.shape, y_half_ref.dtype)
        pending_ref[0] = 0
        pending_ref[1] = 0

    first = start_ref[e]
    count = start_ref[e + 1] - first

    def tile_at(ref, row8):
        return ref.at[pl.ds(pl.multiple_of(row8, SUBLANES), SUBLANES), :]

    def scatter_add(base, n):
        y_ref = y_refs[n]
        for m0 in range(0, n, SCATTER_BATCH):
            updates = []
            for m in range(m0, m0 + SCATTER_BATCH):
                dst = tile_at(acc, rows_ref[base + m])
                updates.append((dst, dst[...] + y_ref[m * SUBLANES:(m + 1) * SUBLANES, :]))
            for dst, val in updates:
                dst[...] = val

    def run_block(n, nominal, begin, pending):
        redo = nominal - begin
        base = first + begin + n_rows
        for m in range(n):
            gat_ref[m * SUBLANES:(m + 1) * SUBLANES, :] = tile_at(x_ref, rows_ref[base + m])[...]
        xb = _load_row_major_tiles(gat_ref, n).astype(BF16)
        act = jax.nn.silu(_dot(xb, w1_ref[0])) * _dot(xb, w3_ref[0])
        y = _dot(act.astype(BF16), w2_ref[0])

        w_rows = pl.cdiv(n, LANES) + 1
        row_id = lax.broadcasted_iota(jnp.int32, (n, w_rows * LANES), 0)
        lane_id = lax.broadcasted_iota(jnp.int32, (n, w_rows * LANES), 1)
        w_row = base // LANES
        span = jnp.concatenate([wts_ref[w_row + r] for r in range(w_rows)], axis=1)
        picked = jnp.where((lane_id == row_id + base % LANES) & (row_id >= redo),
                           jnp.broadcast_to(span, lane_id.shape), 0.0)
        ys = y * jnp.sum(picked, axis=1, keepdims=True)
        scatter_add(pending, n)
        _store_row_major_tiles(y_refs[n], ys)
        return base

    full_blocks = count // n_rows
    rest_rows = count - full_blocks * n_rows
    main_blocks = full_blocks + (rest_rows > half).astype(jnp.int32)

    def main_block(b, pending):
        nominal = b * n_rows
        return run_block(n_rows, nominal, jnp.minimum(nominal, count - n_rows), pending)

    pending_ref[0] = lax.fori_loop(0, main_blocks, main_block, pending_ref[0])

    @pl.when((rest_rows > 0) & (rest_rows <= half))
    def _():
        pending_ref[1] = run_block(half, full_blocks * n_rows, count - half, pending_ref[1])

    @pl.when(e == pl.num_programs(0) - 1)
    def _():
        scatter_add(pending_ref[0], n_rows)
        scatter_add(pending_ref[1], half)


def _moe(ids, weights, x_tiles, w1, w3, w2):
    n_tok = ids.shape[1]
    g_tok = min(MOE_GROUP_TOKENS, n_tok)
    n_groups = n_tok // g_tok
    assert n_groups * g_tok == n_tok
    per_group = g_tok * TOP_K
    tok = lax.broadcasted_iota(jnp.int32, ids.shape, 1)
    key = ((tok // g_tok) * N_EXPERTS + ids) * g_tok + tok % g_tok
    key, wts = lax.sort((key.reshape(-1), weights.reshape(-1)), num_keys=1)
    rows8 = ((key % g_tok) * D_TILES).reshape(n_groups, per_group)
    wts = wts.reshape(n_groups, per_group)
    experts = jnp.arange(N_EXPERTS, dtype=jnp.int32)[None, None, :, None]
    counts = jnp.sum(ids.reshape(TOP_K, n_groups, 1, g_tok) == experts, axis=(0, 3), dtype=jnp.int32)
    start = jnp.concatenate([jnp.zeros((n_groups, 1), jnp.int32), jnp.cumsum(counts, axis=1)], axis=1)
    n_rows = 256 if per_group // N_EXPERTS >= 512 else 128
    pad = ((0, 0), (n_rows, n_rows + LANES))
    rows8 = jnp.pad(rows8, pad)
    wts = jnp.pad(wts, pad).reshape(n_groups, -1, 1, LANES)
    out = None
    for g in range(n_groups):
        out = _moe_group_call(g, n_groups, g_tok, n_rows, rows8[g], start[g], x_tiles, wts[g], w1, w3, w2, out)
    return out


def _moe_group_call(group, n_groups, g_tok, n_rows, rows8, start, x_tiles, wts, w1, w3, w2, prev):
    g_rows = g_tok * D_TILES
    assert x_tiles.shape[0] == n_groups * g_rows
    out_block = (1, g_rows, LANES)
    in_specs = [
        pl.BlockSpec((g_rows, LANES), lambda e, *_: (group, 0), pipeline_mode=pl.Buffered(1)),
        pl.BlockSpec(wts.shape, lambda e, *_: (0, 0, 0)),
        pl.BlockSpec((1, D_MODEL, EXPERT_DIM), lambda e, *_: (e, 0, 0)),
        pl.BlockSpec((1, D_MODEL, EXPERT_DIM), lambda e, *_: (e, 0, 0)),
        pl.BlockSpec((1, EXPERT_DIM, D_MODEL), lambda e, *_: (e, 0, 0)),
    ]
    args = [rows8, start, x_tiles, wts, w1, w3, w2]
    aliases = {}
    if prev is not None:
        in_specs.append(pl.BlockSpec(memory_space=pl.ANY))
        aliases = {len(args): 0}
        args.append(prev)
    return pl.pallas_call(
        functools.partial(_moe_kernel, n_rows=n_rows),
        grid_spec=pltpu.PrefetchScalarGridSpec(
            num_scalar_prefetch=2, grid=(N_EXPERTS,), in_specs=in_specs,
            out_specs=pl.BlockSpec(out_block, lambda e, *_: (group, 0, 0), pipeline_mode=pl.Buffered(1)),
            scratch_shapes=[pltpu.VMEM((n_rows * D_TILES, LANES), F32),
                            pltpu.VMEM((n_rows * D_TILES, LANES), F32),
                            pltpu.VMEM((n_rows // 2 * D_TILES, LANES), F32),
                            pltpu.SMEM((2,), jnp.int32)]),
        out_shape=jax.ShapeDtypeStruct((n_groups,) + out_block[1:], F32),
        input_output_aliases=aliases,
        compiler_params=_params(1), name="moe",
    )(*args)


def _final_kernel(x1_ref, routed_ref, p_ref, s1_ref, s3_ref, s2_ref, g2_ref, b2_ref, wg_ref, wp_ref,
                  o_ref, *, alpha):
    x1 = x1_ref[...]
    xb = x1.astype(BF16)
    act = jax.nn.silu(_dot(xb, s1_ref[...])) * _dot(xb, s3_ref[...])
    ff = _load_row_major_tiles(routed_ref.at[0], x1.shape[0]) + _dot(act.astype(BF16), s2_ref[...])
    x2 = _layer_norm(alpha * x1 + ff, g2_ref[...], b2_ref[...])
    gate = jax.nn.sigmoid(_dot(x2.astype(BF16), wg_ref[...]))
    o_ref[...] = x2 + gate * _dot(p_ref[...].astype(BF16), wp_ref[...])


def _final(x1, routed, p2d, s1, s3, s2, g2, b2, wg, wp, *, alpha):
    n_rows = x1.shape[0]
    tile = ROW_TILE
    row = lambda width: pl.BlockSpec((tile, width), lambda i: (i, 0))
    tiles_per_group = routed.shape[1] // D_TILES // tile
    routed_spec = pl.BlockSpec((1, tile * D_TILES, LANES),
                               lambda i: (i // tiles_per_group, i % tiles_per_group, 0))
    return pl.pallas_call(
        functools.partial(_final_kernel, alpha=alpha),
        grid=(n_rows // tile,),
        in_specs=[row(D_MODEL), routed_spec,
                  row(p2d.shape[1]),
                  _const_spec(s1.shape), _const_spec(s3.shape), _const_spec(s2.shape),
                  _const_spec((1, D_MODEL)), _const_spec((1, D_MODEL)),
                  _const_spec(wg.shape), _const_spec(wp.shape)],
        out_specs=row(D_MODEL),
        out_shape=jax.ShapeDtypeStruct((n_rows, D_MODEL), F32),
        compiler_params=_params(1), name="final",
    )(x1, routed, p2d, s1, s3, s2, g2, b2, wg, wp)


def _trunk_layer(x, p, cache_k, cache_v, lw, *, alpha, first_layer_ln):
    batch, seq, d = x.shape
    n_rows = batch * seq
    x2d = x.reshape(n_rows, d)
    prompt = cache_k is None
    mlp_len = min(seq, MLP_CHUNK)
    keep = min(PAST_CHUNKS * CHUNK, seq) if prompt else seq
    ln_g, ln_b = first_layer_ln

    w_s = (lw["mlp_w_s"][:, :mlp_len, :mlp_len]
           * jnp.tril(jnp.ones((mlp_len, mlp_len), F32))).astype(BF16)
    b_s = jnp.repeat(lw["mlp_b_s"][:, :mlp_len].T, MLP_WIDTH // MLP_GROUPS, axis=1)
    q, k, v, kf, vf, vn, ga, gb, ob = _proj(
        x2d, ln_g, ln_b, lw["w_in"].astype(BF16), lw["mlp_ln_g"][None], lw["mlp_ln_b"][None],
        w_s, b_s, seq=seq, mlp_len=mlp_len, keep_rows=keep)
    if prompt:
        oa = _attn_prompt(q, k, v, lw["attn_rel_bias"], batch=batch, seq=seq)
    else:
        oa = _attn_sample(q, k, v, cache_k, cache_v, lw["attn_rel_bias"], batch=batch, seq=seq)
    x1, x1_tiles, scores = _mix(
        x2d, oa, ob, ga, gb, ln_g, ln_b, lw["w_branch_a"].astype(BF16), lw["w_branch_b"].astype(BF16),
        lw["w_out"].astype(BF16), lw["ln1_g"][None], lw["ln1_b"][None], _split_bf16(lw["router_w"].T),
        alpha=alpha)
    ids, weights = _route_call(scores, lw["router_bias"][:, None])
    routed = _moe(ids, weights, x1_tiles, lw["exp_w1"], lw["exp_w3"], lw["exp_w2"])
    y = _final(x1, routed, p.reshape(n_rows, -1), lw["shared_w1"].astype(BF16),
               lw["shared_w3"].astype(BF16), lw["shared_w2"].astype(BF16),
               lw["ln2_g"][None], lw["ln2_b"][None], lw["ple_w_gate"].astype(BF16),
               lw["ple_w_proj"].astype(BF16), alpha=alpha)
    new_k = kf.reshape(batch, keep, HEADS, HEAD_DIM)
    new_v = vf.reshape(batch, keep, HEADS, HEAD_DIM)
    new_mlp_v = vn.reshape(batch, mlp_len, MLP_WIDTH)
    return y.reshape(batch, seq, d), new_k, new_v, new_mlp_v


def kernel(x_prompt, x_sample, cache_attn_k, cache_attn_v, p_prompt, p_sample, ln_in_g, ln_in_b, w_in, attn_rel_bias, mlp_ln_g, mlp_ln_b, mlp_w_s, mlp_b_s, w_branch_a, w_branch_b, w_out, ln1_g, ln1_b, router_w, router_bias, exp_w1, exp_w3, exp_w2, shared_w1, shared_w3, shared_w2, ln2_g, ln2_b, ple_w_gate, ple_w_proj):
    depth = w_in.shape[0]
    assert depth == 1, "the fused proj/mix kernels apply LayerNorm_in themselves: single layer only"
    alpha = (2 * depth) ** 0.25
    stacked = dict(w_in=w_in, attn_rel_bias=attn_rel_bias, mlp_ln_g=mlp_ln_g, mlp_ln_b=mlp_ln_b,
                   mlp_w_s=mlp_w_s, mlp_b_s=mlp_b_s, w_branch_a=w_branch_a, w_branch_b=w_branch_b,
                   w_out=w_out, ln1_g=ln1_g, ln1_b=ln1_b, router_w=router_w, router_bias=router_bias,
                   exp_w1=exp_w1, exp_w3=exp_w3, exp_w2=exp_w2, shared_w1=shared_w1,
                   shared_w3=shared_w3, shared_w2=shared_w2, ln2_g=ln2_g, ln2_b=ln2_b,
                   ple_w_gate=ple_w_gate, ple_w_proj=ple_w_proj)
    lw = {name: value[0] for name, value in stacked.items()}
    for name in ("exp_w1", "exp_w3", "exp_w2"):
        lw[name] = lw[name].astype(BF16)
    ln_in =(ln_in_g[None], ln_in_b[None])
    yp, kp, vp, mp = _trunk_layer(x_prompt, p_prompt[0], None, None, lw, alpha=alpha, first_layer_ln=ln_in)
    ys, ks, vs, ms = _trunk_layer(x_sample, p_sample[0], cache_attn_k[0], cache_attn_v[0], lw,
                                  alpha=alpha, first_layer_ln=ln_in)
    return (yp, ys, kp[None], vp[None], ks[None], vs[None], mp[None], ms[None])
```

```python
import functools

import jax
import jax.numpy as jnp
import numpy as np
from jax import lax
from jax.experimental import pallas as pl
from jax.experimental.pallas import tpu as pltpu

F32 = jnp.float32
BF16 = jnp.bfloat16

D_MODEL = 1024
CHUNK = 64
HEADS = 8
HEAD_DIM = 64
A_WIDTH = HEADS * HEAD_DIM
PAST_CHUNKS = 8
MAX_REL = 256
MLP_CHUNK = 128
MLP_GROUPS = 8
MLP_WIDTH = 512
N_EXPERTS = 64
TOP_K = 8
N_GROUPS = 8
TOPK_GROUPS = 4
GROUP_SIZE = N_EXPERTS // N_GROUPS
EXPERT_DIM = 256
ROUTED_SCALE = 2.5
LN_EPS = 1e-5
MASKED = -1e30

LANES = 128
SUBLANES = 8
VMEM_LIMIT_BYTES = 56 * 1024 * 1024

ROW_TILE = 512
ROUTE_TILE = 2048
Q_CHUNKS = 4
MOE_GROUP_TOKENS = 4096
SCATTER_BATCH = 8
D_TILES = D_MODEL // LANES


def _layer_norm(x, g, b):
    xc = x - jnp.mean(x, axis=-1, keepdims=True)
    var = jnp.mean(xc * xc, axis=-1, keepdims=True)
    return xc * lax.rsqrt(var + LN_EPS) * g + b


def _dot(a, b):
    return jnp.dot(a, b, preferred_element_type=F32)


def _dot_nt(a, b, precision=None):
    return lax.dot_general(a, b, (((1,), (1,)), ((), ())),
                           preferred_element_type=F32, precision=precision)


def _const_spec(shape):
    zeros = (0,) * len(shape)
    return pl.BlockSpec(shape, lambda *_: zeros)


def _params(n_axes):
    return pltpu.CompilerParams(dimension_semantics=("arbitrary",) * n_axes,
                                vmem_limit_bytes=VMEM_LIMIT_BYTES)


def _proj_kernel(x_ref, lng_ref, lnb_ref, w_ref, mg_ref, mb_ref, ws_ref, bs_ref,
                 q_ref, k_ref, v_ref, kf_ref, vf_ref, vn_ref, ga_ref, gb_ref, ob_ref,
                 *, mlp_len, vn_rows):
    rows = x_ref.shape[0]
    xn = _layer_norm(x_ref[...], lng_ref[...], lnb_ref[...]).astype(BF16)

    def section(lo, width):
        return _dot(xn, w_ref[:, lo:lo + width])

    q = section(0, A_WIDTH)
    q_ref[...] = (q * (HEAD_DIM ** -0.5)).astype(BF16)
    k = section(A_WIDTH, A_WIDTH)
    k_ref[...] = k.astype(BF16)
    kf_ref[...] = k
    v = section(2 * A_WIDTH, A_WIDTH)
    v_ref[...] = v.astype(BF16)
    vf_ref[...] = v
    base = 3 * A_WIDTH
    un = jax.nn.gelu(section(base, MLP_WIDTH))
    vn = _layer_norm(jax.nn.gelu(section(base + MLP_WIDTH, MLP_WIDTH)), mg_ref[...], mb_ref[...])
    vn_ref[...] = vn[rows - vn_rows:, :]
    base += 2 * MLP_WIDTH
    ga_ref[...] = jax.nn.sigmoid(section(base, D_MODEL)).astype(BF16)
    gb_ref[...] = jax.nn.sigmoid(section(base + D_MODEL, D_MODEL)).astype(BF16)

    vnb = vn.astype(BF16)
    low_half = lax.broadcasted_iota(jnp.int32, (mlp_len, LANES), 1) < (MLP_WIDTH // MLP_GROUPS)
    for c in range(rows // mlp_len):
        r0 = c * mlp_len
        for p in range(MLP_GROUPS // 2):
            c0 = p * LANES
            slab = vnb[r0:r0 + mlp_len, c0:c0 + LANES]
            mixed = jnp.where(low_half, _dot(ws_ref[2 * p], slab), _dot(ws_ref[2 * p + 1], slab))
            gated = un[r0:r0 + mlp_len, c0:c0 + LANES] * (mixed + bs_ref[:, c0:c0 + LANES])
            ob_ref[r0:r0 + mlp_len, c0:c0 + LANES] = gated.astype(BF16)


def _proj(x2d, ln_g, ln_b, w_in, mlp_g, mlp_b, w_s, b_s, *, seq, mlp_len, keep_rows):
    n_rows = x2d.shape[0]
    tile = ROW_TILE
    assert n_rows % tile == 0 and tile % mlp_len == 0
    n_seq = n_rows // seq
    if seq >= tile:
        assert seq % tile == 0 and keep_rows == tile
        per_seq = seq // tile
        kv_rows, vn_rows = n_seq * tile, mlp_len
        kv_map = lambda i: (i // per_seq, 0)
    else:
        assert keep_rows == seq and mlp_len == seq
        kv_rows, vn_rows = n_rows, tile
        kv_map = lambda i: (i, 0)
    row = lambda width: pl.BlockSpec((tile, width), lambda i: (i, 0))
    n_in = w_in.shape[1]
    out_shape = (
        jax.ShapeDtypeStruct((n_rows, A_WIDTH), BF16),
        jax.ShapeDtypeStruct((n_rows, A_WIDTH), BF16),
        jax.ShapeDtypeStruct((n_rows, A_WIDTH), BF16),
        jax.ShapeDtypeStruct((kv_rows, A_WIDTH), F32),
        jax.ShapeDtypeStruct((kv_rows, A_WIDTH), F32),
        jax.ShapeDtypeStruct((kv_rows // tile * vn_rows, MLP_WIDTH), F32),
        jax.ShapeDtypeStruct((n_rows, D_MODEL), BF16),
        jax.ShapeDtypeStruct((n_rows, D_MODEL), BF16),
        jax.ShapeDtypeStruct((n_rows, MLP_WIDTH), BF16),
    )
    out_specs = (
        row(A_WIDTH), row(A_WIDTH), row(A_WIDTH),
        pl.BlockSpec((tile, A_WIDTH), kv_map), pl.BlockSpec((tile, A_WIDTH), kv_map),
        pl.BlockSpec((vn_rows, MLP_WIDTH), kv_map),
        row(D_MODEL), row(D_MODEL), row(MLP_WIDTH),
    )
    in_specs = [
        row(D_MODEL), _const_spec((1, D_MODEL)), _const_spec((1, D_MODEL)),
        _const_spec((D_MODEL, n_in)), _const_spec((1, MLP_WIDTH)), _const_spec((1, MLP_WIDTH)),
        _const_spec((MLP_GROUPS, mlp_len, mlp_len)), _const_spec((mlp_len, MLP_WIDTH)),
    ]
    return pl.pallas_call(
        functools.partial(_proj_kernel, mlp_len=mlp_len, vn_rows=vn_rows),
        grid=(n_rows // tile,), in_specs=in_specs, out_specs=out_specs, out_shape=out_shape,
        compiler_params=_params(1), name="proj",
    )(x2d, ln_g, ln_b, w_in, mlp_g, mlp_b, w_s, b_s)


def _attend_heads(q, keys, values, pair_bias, o_ref):
    n_q = q.shape[0]
    low_half = lax.broadcasted_iota(jnp.int32, (n_q, LANES), 1) < HEAD_DIM
    zero = jnp.zeros((), BF16)
    for p in range(HEADS // 2):
        c0 = p * LANES
        q2, k2, v2 = q[:, c0:c0 + LANES], keys[:, c0:c0 + LANES], values[:, c0:c0 + LANES]
        qs = jnp.concatenate([jnp.where(low_half, q2, zero), jnp.where(low_half, zero, q2)], axis=0)
        s = _dot_nt(qs, k2) + pair_bias(p).reshape(2 * n_q, keys.shape[0])
        e = jnp.exp(s - jnp.max(s, axis=-1, keepdims=True))
        denom = jnp.sum(e, axis=-1, keepdims=True)
        o = _dot(e.astype(BF16), v2) / denom
        o_ref[:, c0:c0 + LANES] = jnp.where(low_half, o[:n_q], o[n_q:]).astype(o_ref.dtype)


def _attn_prompt_kernel(q_ref, k_ref, v_ref, bias_ref, o_ref, *, window, lead_steps):
    n_q = q_ref.shape[0]
    step = jnp.minimum(pl.program_id(1), lead_steps)
    start = pl.multiple_of((pl.program_id(1) - step) * n_q, n_q)
    shift = pl.multiple_of((lead_steps - step) * n_q, n_q)
    keys = k_ref[pl.ds(start, window), :]
    values = v_ref[pl.ds(start, window), :]
    _attend_heads(q_ref[...], keys, values,
                  lambda p: bias_ref[2 * p:2 * p + 2, :, pl.ds(shift, window)], o_ref)


def _relative_bias(rel_table, n_q, n_k, key_offset):
    period = pl.next_power_of_2(n_q + n_k)
    m = np.arange(period)
    diag = np.where(m < n_k, m, m - period)
    idx = np.clip(diag - key_offset, -MAX_REL, MAX_REL) + MAX_REL
    values = rel_table[:, idx].astype(F32)
    rows = jnp.tile(values, (1, n_q))[:, :n_q * (period - 1)]
    return rows.reshape(rel_table.shape[0], n_q, period - 1)[:, :, :n_k]


def _prompt_bias(rel_table):
    n_q = Q_CHUNKS * CHUNK
    n_k = (2 * PAST_CHUNKS + Q_CHUNKS) * CHUNK
    lag = np.arange(n_q)[:, None] // CHUNK + PAST_CHUNKS - np.arange(n_k)[None, :] // CHUNK
    in_band = (lag >= 0) & (lag <= PAST_CHUNKS)
    bias = _relative_bias(rel_table, n_q, n_k, PAST_CHUNKS * CHUNK)
    return jnp.where(in_band[None], bias, MASKED)


def _attn_prompt(q, k, v, rel_table, *, batch, seq):
    n_q = Q_CHUNKS * CHUNK
    window = (Q_CHUNKS + PAST_CHUNKS) * CHUNK
    assert PAST_CHUNKS % Q_CHUNKS == 0 and seq >= window
    lead = PAST_CHUNKS // Q_CHUNKS
    steps = seq // n_q
    bias = _prompt_bias(rel_table)
    qspec = pl.BlockSpec((n_q, A_WIDTH), lambda b, i: (b * steps + i, 0))
    kvspec = pl.BlockSpec((seq, A_WIDTH), lambda b, i: (b, 0))
    return pl.pallas_call(
        functools.partial(_attn_prompt_kernel, window=window, lead_steps=lead),
        grid=(batch, steps),
        in_specs=[qspec, kvspec, kvspec, _const_spec(bias.shape)],
        out_specs=qspec,
        out_shape=jax.ShapeDtypeStruct((batch * seq, A_WIDTH), BF16),
        compiler_params=_params(2), name="attn_prompt",
    )(q, k, v, bias)


def _attn_sample_kernel(q_ref, k_ref, v_ref, ck_ref, cv_ref, bias_ref, o_ref, kk_ref, vv_ref):
    n_cache = ck_ref.shape[1]
    n_new = k_ref.shape[1]
    kk_ref[0:n_cache, :] = ck_ref[0].astype(BF16)
    kk_ref[n_cache:n_cache + n_new, :] = k_ref[0]
    vv_ref[0:n_cache, :] = cv_ref[0].astype(BF16)
    vv_ref[n_cache:n_cache + n_new, :] = v_ref[0]
    _attend_heads(q_ref[0], kk_ref[...], vv_ref[...], lambda p: bias_ref[2 * p:2 * p + 2], o_ref.at[0])


def _attn_sample(q, k, v, cache_k, cache_v, rel_table, *, batch, seq):
    n_cache = cache_k.shape[1]
    bias = _relative_bias(rel_table, seq, n_cache + seq, n_cache)
    new = pl.BlockSpec((1, seq, A_WIDTH), lambda b: (b, 0, 0))
    old = pl.BlockSpec((1, n_cache, A_WIDTH), lambda b: (b, 0, 0))
    out = pl.pallas_call(
        _attn_sample_kernel,
        grid=(batch,),
        in_specs=[new, new, new, old, old, _const_spec(bias.shape)],
        out_specs=new,
        out_shape=jax.ShapeDtypeStruct((batch, seq, A_WIDTH), BF16),
        scratch_shapes=[pltpu.VMEM((n_cache + seq, A_WIDTH), BF16),
                        pltpu.VMEM((n_cache + seq, A_WIDTH), BF16)],
        compiler_params=_params(1), name="attn_sample",
    )(q.reshape(batch, seq, A_WIDTH), k.reshape(batch, seq, A_WIDTH), v.reshape(batch, seq, A_WIDTH),
      cache_k.reshape(batch, n_cache, A_WIDTH), cache_v.reshape(batch, n_cache, A_WIDTH), bias)
    return out.reshape(batch * seq, A_WIDTH)


def _first_index(hit, iota, axis, limit):
    return jnp.min(jnp.where(hit, iota, limit), axis=axis, keepdims=True)


def _route(scores, sel):
    n_tok = scores.shape[1]
    neg = -jnp.inf
    grouped = sel.reshape(N_GROUPS, GROUP_SIZE, n_tok)
    member = lax.broadcasted_iota(jnp.int32, grouped.shape, 1)
    best = jnp.max(grouped, axis=1, keepdims=True)
    first = _first_index(grouped == best, member, 1, GROUP_SIZE)
    second = jnp.max(jnp.where(member == first, neg, grouped), axis=1, keepdims=True)
    group_score = best + second

    group_id = lax.broadcasted_iota(jnp.int32, group_score.shape, 0)
    keep = jnp.zeros(group_score.shape, F32)
    for _ in range(TOPK_GROUPS):
        top = jnp.max(group_score, axis=0, keepdims=True)
        hit = group_id == _first_index(group_score == top, group_id, 0, N_GROUPS)
        keep = jnp.where(hit, 1.0, keep)
        group_score = jnp.where(hit, neg, group_score)
    keep = jnp.broadcast_to(keep, grouped.shape).reshape(N_EXPERTS, n_tok)

    cand = jnp.where(keep > 0.0, sel, neg)
    expert_id = lax.broadcasted_iota(jnp.int32, cand.shape, 0)
    ids, weights = [], []
    for _ in range(TOP_K):
        top = jnp.max(cand, axis=0, keepdims=True)
        first = _first_index(cand == top, expert_id, 0, N_EXPERTS)
        hit = expert_id == first
        ids.append(first)
        weights.append(jnp.sum(jnp.where(hit, scores, 0.0), axis=0, keepdims=True))
        cand = jnp.where(hit, neg, cand)
    ids = jnp.concatenate(ids, axis=0)
    weights = jnp.concatenate(weights, axis=0)
    weights = weights / jnp.sum(weights, axis=0, keepdims=True) * ROUTED_SCALE
    return ids, weights


def _store_row_major_tiles(flat_ref, x):
    rows = x.shape[0]
    for j in range(D_TILES):
        flat_ref[pl.ds(j, rows, stride=D_TILES), :] = x[:, j * LANES:(j + 1) * LANES]


def _load_row_major_tiles(flat_ref, rows):
    return jnp.concatenate(
        [flat_ref[pl.ds(j, rows, stride=D_TILES), :] for j in range(D_TILES)], axis=1)


def _split_bf16(w):
    hi = w.astype(BF16)
    return jnp.concatenate([hi, (w - hi.astype(F32)).astype(BF16)], axis=0)


def _mix_kernel(x_ref, oa_ref, ob_ref, ga_ref, gb_ref, lng_ref, lnb_ref, wa_ref, wb_ref, wo_ref,
                g1_ref, b1_ref, rw_ref, x1_ref, x1t_ref, scores_ref, *, alpha):
    xn = _layer_norm(x_ref[...], lng_ref[...], lnb_ref[...])
    mix = (ga_ref[...].astype(F32) * _dot(oa_ref[...], wa_ref[...])
           + gb_ref[...].astype(F32) * _dot(ob_ref[...], wb_ref[...]))
    x1 = _layer_norm(alpha * xn + _dot(mix.astype(BF16), wo_ref[...]), g1_ref[...], b1_ref[...])
    x1_ref[...] = x1
    _store_row_major_tiles(x1t_ref, x1)
    x1_hi = x1.astype(BF16)
    x1_lo = (x1 - x1_hi.astype(F32)).astype(BF16)
    by_hi = _dot_nt(rw_ref[...], x1_hi)
    logits = by_hi[:N_EXPERTS] + by_hi[N_EXPERTS:] + _dot_nt(rw_ref[:N_EXPERTS, :], x1_lo)
    scores_ref[...] = jax.nn.sigmoid(logits)


def _mix(x2d, oa, ob, ga, gb, ln_g, ln_b, wa, wb, wo, g1, b1, rw_t, *, alpha):
    n_rows = x2d.shape[0]
    tile = ROW_TILE
    row = lambda width: pl.BlockSpec((tile, width), lambda i: (i, 0))
    return pl.pallas_call(
        functools.partial(_mix_kernel, alpha=alpha),
        grid=(n_rows // tile,),
        in_specs=[row(D_MODEL), row(A_WIDTH), row(MLP_WIDTH), row(D_MODEL), row(D_MODEL),
                  _const_spec((1, D_MODEL)), _const_spec((1, D_MODEL)),
                  _const_spec(wa.shape), _const_spec(wb.shape), _const_spec(wo.shape),
                  _const_spec((1, D_MODEL)), _const_spec((1, D_MODEL)),
                  _const_spec(rw_t.shape)],
        out_specs=(row(D_MODEL), pl.BlockSpec((tile * D_TILES, LANES), lambda i: (i, 0)),
                   pl.BlockSpec((N_EXPERTS, tile), lambda i: (0, i))),
        out_shape=(jax.ShapeDtypeStruct((n_rows, D_MODEL), F32),
                   jax.ShapeDtypeStruct((n_rows * D_TILES, LANES), F32),
                   jax.ShapeDtypeStruct((N_EXPERTS, n_rows), F32)),
        compiler_params=_params(1), name="mix",
    )(x2d, oa, ob, ga, gb, ln_g, ln_b, wa, wb, wo, g1, b1, rw_t)


def _route_kernel(scores_ref, rb_ref, ids_ref, wts_ref):
    scores = scores_ref[...]
    ids, weights = _route(scores, scores + rb_ref[...])
    ids_ref[...] = ids
    wts_ref[...] = weights


def _route_call(scores, rb):
    n_tok = scores.shape[1]
    tile = min(ROUTE_TILE, n_tok)
    col = lambda rows: pl.BlockSpec((rows, tile), lambda i: (0, i))
    return pl.pallas_call(
        _route_kernel,
        grid=(n_tok // tile,),
        in_specs=[col(N_EXPERTS), _const_spec(rb.shape)],
        out_specs=(col(TOP_K), col(TOP_K)),
        out_shape=(jax.ShapeDtypeStruct((TOP_K, n_tok), jnp.int32),
                   jax.ShapeDtypeStruct((TOP_K, n_tok), F32)),
        compiler_params=_params(1), name="route",
    )(scores, rb)


def _moe_kernel(rows_ref, start_ref, x_ref, wts_ref, w1_ref, w3_ref, w2_ref, *rest, n_rows):
    out_ref, gat_ref, y_full_ref, y_half_ref, pending_ref = rest[-5:]
    acc = out_ref.at[0]
    e = pl.program_id(0)
    half = n_rows // 2
    y_refs = {n_rows: y_full_ref, half: y_half_ref}

    @pl.when(e == 0)
    def _():
        out_ref[...] = jnp.zeros(out_ref.shape, out_ref.dtype)
        y_full_ref[...] = jnp.zeros(y_full_ref.shape, y_full_ref.dtype)
        y_half_ref[...] = jnp.zeros(y_half_ref.shape, y_half_ref.dtype)
        pending_ref[0] = 0
        pending_ref[1] = 0

    first = start_ref[e]
    count = start_ref[e + 1] - first

    def tile_at(ref, row8):
        return ref.at[pl.ds(pl.multiple_of(row8, SUBLANES), SUBLANES), :]

    def scatter_add(base, n):
        y_ref = y_refs[n]
        for m0 in range(0, n, SCATTER_BATCH):
            updates = []
            for m in range(m0, m0 + SCATTER_BATCH):
                dst = tile_at(acc, rows_ref[base + m])
                updates.append((dst, dst[...] + y_ref[m * SUBLANES:(m + 1) * SUBLANES, :]))
            for dst, val in updates:
                dst[...] = val

    def run_block(n, nominal, begin, pending):
        redo = nominal - begin
        base = first + begin + n_rows
        for m in range(n):
            gat_ref[m * SUBLANES:(m + 1) * SUBLANES, :] = tile_at(x_ref, rows_ref[base + m])[...]
        xb = _load_row_major_tiles(gat_ref, n).astype(BF16)
        act = jax.nn.silu(_dot(xb, w1_ref[0])) * _dot(xb, w3_ref[0])
        y = _dot(act.astype(BF16), w2_ref[0])

        w_rows = pl.cdiv(n, LANES) + 1
        row_id = lax.broadcasted_iota(jnp.int32, (n, w_rows * LANES), 0)
        lane_id = lax.broadcasted_iota(jnp.int32, (n, w_rows * LANES), 1)
        w_row = base // LANES
        span = jnp.concatenate([wts_ref[w_row + r] for r in range(w_rows)], axis=1)
        picked = jnp.where((lane_id == row_id + base % LANES) & (row_id >= redo),
                           jnp.broadcast_to(span, lane_id.shape), 0.0)
        ys = y * jnp.sum(picked, axis=1, keepdims=True)
        scatter_add(pending, n)
        _store_row_major_tiles(y_refs[n], ys)
        return base

    full_blocks = count // n_rows
    rest_rows = count - full_blocks * n_rows
    main_blocks = full_blocks + (rest_rows > half).astype(jnp.int32)

    def main_block(b, pending):
        nominal = b * n_rows
        return run_block(n_rows, nominal, jnp.minimum(nominal, count - n_rows), pending)

    pending_ref[0] = lax.fori_loop(0, main_blocks, main_block, pending_ref[0])

    @pl.when((rest_rows > 0) & (rest_rows <= half))
    def _():
        pending_ref[1] = run_block(half, full_blocks * n_rows, count - half, pending_ref[1])

    @pl.when(e == pl.num_programs(0) - 1)
    def _():
        scatter_add(pending_ref[0], n_rows)
        scatter_add(pending_ref[1], half)


def _moe(ids, weights, x_tiles, w1, w3, w2):
    n_tok = ids.shape[1]
    g_tok = min(MOE_GROUP_TOKENS, n_tok)
    n_groups = n_tok // g_tok
    assert n_groups * g_tok == n_tok
    per_group = g_tok * TOP_K
    tok = lax.broadcasted_iota(jnp.int32, ids.shape, 1)
    key = ((tok // g_tok) * N_EXPERTS + ids) * g_tok + tok % g_tok
    key, wts = lax.sort((key.reshape(-1), weights.reshape(-1)), num_keys=1)
    rows8 = ((key % g_tok) * D_TILES).reshape(n_groups, per_group)
    wts = wts.reshape(n_groups, per_group)
    experts = jnp.arange(N_EXPERTS, dtype=jnp.int32)[None, None, :, None]
    counts = jnp.sum(ids.reshape(TOP_K, n_groups, 1, g_tok) == experts, axis=(0, 3), dtype=jnp.int32)
    start = jnp.concatenate([jnp.zeros((n_groups, 1), jnp.int32), jnp.cumsum(counts, axis=1)], axis=1)
    n_rows = 256 if per_group // N_EXPERTS >= 512 else 128
    pad = ((0, 0), (n_rows, n_rows + LANES))
    rows8 = jnp.pad(rows8, pad)
    wts = jnp.pad(wts, pad).reshape(n_groups, -1, 1, LANES)
    out = None
    for g in range(n_groups):
        out = _moe_group_call(g, n_groups, g_tok, n_rows, rows8[g], start[g], x_tiles, wts[g], w1, w3, w2, out)
    return out


def _moe_group_call(group, n_groups, g_tok, n_rows, rows8, start, x_tiles, wts, w1, w3, w2, prev):
    g_rows = g_tok * D_TILES
    assert x_tiles.shape[0] == n_groups * g_rows
    out_block = (1, g_rows, LANES)
    in_specs = [
        pl.BlockSpec((g_rows, LANES), lambda e, *_: (group, 0), pipeline_mode=pl.Buffered(1)),
        pl.BlockSpec(wts.shape, lambda e, *_: (0, 0, 0)),
        pl.BlockSpec((1, D_MODEL, EXPERT_DIM), lambda e, *_: (e, 0, 0)),
        pl.BlockSpec((1, D_MODEL, EXPERT_DIM), lambda e, *_: (e, 0, 0)),
        pl.BlockSpec((1, EXPERT_DIM, D_MODEL), lambda e, *_: (e, 0, 0)),
    ]
    args = [rows8, start, x_tiles, wts, w1, w3, w2]
    aliases = {}
    if prev is not None:
        in_specs.append(pl.BlockSpec(memory_space=pl.ANY))
        aliases = {len(args): 0}
        args.append(prev)
    return pl.pallas_call(
        functools.partial(_moe_kernel, n_rows=n_rows),
        grid_spec=pltpu.PrefetchScalarGridSpec(
            num_scalar_prefetch=2, grid=(N_EXPERTS,), in_specs=in_specs,
            out_specs=pl.BlockSpec(out_block, lambda e, *_: (group, 0, 0), pipeline_mode=pl.Buffered(1)),
            scratch_shapes=[pltpu.VMEM((n_rows * D_TILES, LANES), F32),
                            pltpu.VMEM((n_rows * D_TILES, LANES), F32),
                            pltpu.VMEM((n_rows // 2 * D_TILES, LANES), F32),
                            pltpu.SMEM((2,), jnp.int32)]),
        out_shape=jax.ShapeDtypeStruct((n_groups,) + out_block[1:], F32),
        input_output_aliases=aliases,
        compiler_params=_params(1), name="moe",
    )(*args)


def _final_kernel(x1_ref, routed_ref, p_ref, s1_ref, s3_ref, s2_ref, g2_ref, b2_ref, wg_ref, wp_ref,
                  o_ref, *, alpha):
    x1 = x1_ref[...]
    xb = x1.astype(BF16)
    act = jax.nn.silu(_dot(xb, s1_ref[...])) * _dot(xb, s3_ref[...])
    ff = _load_row_major_tiles(routed_ref.at[0], x1.shape[0]) + _dot(act.astype(BF16), s2_ref[...])
    x2 = _layer_norm(alpha * x1 + ff, g2_ref[...], b2_ref[...])
    gate = jax.nn.sigmoid(_dot(x2.astype(BF16), wg_ref[...]))
    o_ref[...] = x2 + gate * _dot(p_ref[...].astype(BF16), wp_ref[...])


def _final(x1, routed, p2d, s1, s3, s2, g2, b2, wg, wp, *, alpha):
    n_rows = x1.shape[0]
    tile = ROW_TILE
    row = lambda width: pl.BlockSpec((tile, width), lambda i: (i, 0))
    tiles_per_group = routed.shape[1] // D_TILES // tile
    routed_spec = pl.BlockSpec((1, tile * D_TILES, LANES),
                               lambda i: (i // tiles_per_group, i % tiles_per_group, 0))
    return pl.pallas_call(
        functools.partial(_final_kernel, alpha=alpha),
        grid=(n_rows // tile,),
        in_specs=[row(D_MODEL), routed_spec,
                  row(p2d.shape[1]),
                  _const_spec(s1.shape), _const_spec(s3.shape), _const_spec(s2.shape),
                  _const_spec((1, D_MODEL)), _const_spec((1, D_MODEL)),
                  _const_spec(wg.shape), _const_spec(wp.shape)],
        out_specs=row(D_MODEL),
        out_shape=jax.ShapeDtypeStruct((n_rows, D_MODEL), F32),
        compiler_params=_params(1), name="final",
    )(x1, routed, p2d, s1, s3, s2, g2, b2, wg, wp)


def _trunk_layer(x, p, cache_k, cache_v, lw, *, alpha, first_layer_ln):
    batch, seq, d = x.shape
    n_rows = batch * seq
    x2d = x.reshape(n_rows, d)
    prompt = cache_k is None
    mlp_len = min(seq, MLP_CHUNK)
    keep = min(PAST_CHUNKS * CHUNK, seq) if prompt else seq
    ln_g, ln_b = first_layer_ln

    w_s = (lw["mlp_w_s"][:, :mlp_len, :mlp_len]
           * jnp.tril(jnp.ones((mlp_len, mlp_len), F32))).astype(BF16)
    b_s = jnp.repeat(lw["mlp_b_s"][:, :mlp_len].T, MLP_WIDTH // MLP_GROUPS, axis=1)
    q, k, v, kf, vf, vn, ga, gb, ob = _proj(
        x2d, ln_g, ln_b, lw["w_in"].astype(BF16), lw["mlp_ln_g"][None], lw["mlp_ln_b"][None],
        w_s, b_s, seq=seq, mlp_len=mlp_len, keep_rows=keep)
    if prompt:
        oa = _attn_prompt(q, k, v, lw["attn_rel_bias"], batch=batch, seq=seq)
    else:
        oa = _attn_sample(q, k, v, cache_k, cache_v, lw["attn_rel_bias"], batch=batch, seq=seq)
    x1, x1_tiles, scores = _mix(
        x2d, oa, ob, ga, gb, ln_g, ln_b, lw["w_branch_a"].astype(BF16), lw["w_branch_b"].astype(BF16),
        lw["w_out"].astype(BF16), lw["ln1_g"][None], lw["ln1_b"][None], _split_bf16(lw["router_w"].T),
        alpha=alpha)
    ids, weights = _route_call(scores, lw["router_bias"][:, None])
    routed = _moe(ids, weights, x1_tiles, lw["exp_w1"], lw["exp_w3"], lw["exp_w2"])
    y = _final(x1, routed, p.reshape(n_rows, -1), lw["shared_w1"].astype(BF16),
               lw["shared_w3"].astype(BF16), lw["shared_w2"].astype(BF16),
               lw["ln2_g"][None], lw["ln2_b"][None], lw["ple_w_gate"].astype(BF16),
               lw["ple_w_proj"].astype(BF16), alpha=alpha)
    new_k = kf.reshape(batch, keep, HEADS, HEAD_DIM)
    new_v = vf.reshape(batch, keep, HEADS, HEAD_DIM)
    new_mlp_v = vn.reshape(batch, mlp_len, MLP_WIDTH)
    return y.reshape(batch, seq, d), new_k, new_v, new_mlp_v


def kernel(x_prompt, x_sample, cache_attn_k, cache_attn_v, p_prompt, p_sample, ln_in_g, ln_in_b, w_in, attn_rel_bias, mlp_ln_g, mlp_ln_b, mlp_w_s, mlp_b_s, w_branch_a, w_branch_b, w_out, ln1_g, ln1_b, router_w, router_bias, exp_w1, exp_w3, exp_w2, shared_w1, shared_w3, shared_w2, ln2_g, ln2_b, ple_w_gate, ple_w_proj):
    depth = w_in.shape[0]
    assert depth == 1, "the fused proj/mix kernels apply LayerNorm_in themselves: single layer only"
    alpha = (2 * depth) ** 0.25
    stacked = dict(w_in=w_in, attn_rel_bias=attn_rel_bias, mlp_ln_g=mlp_ln_g, mlp_ln_b=mlp_ln_b,
                   mlp_w_s=mlp_w_s, mlp_b_s=mlp_b_s, w_branch_a=w_branch_a, w_branch_b=w_branch_b,
                   w_out=w_out, ln1_g=ln1_g, ln1_b=ln1_b, router_w=router_w, router_bias=router_bias,
                   exp_w1=exp_w1, exp_w3=exp_w3, exp_w2=exp_w2, shared_w1=shared_w1,
                   shared_w3=shared_w3, shared_w2=shared_w2, ln2_g=ln2_g, ln2_b=ln2_b,
                   ple_w_gate=ple_w_gate, ple_w_proj=ple_w_proj)
    lw = {name: value[0] for name, value in stacked.items()}
    for name in ("exp_w1", "exp_w3", "exp_w2"):
        lw[name] = lw[name].astype(BF16)
    ln_in =(ln_in_g[None], ln_in_b[None])
    yp, kp, vp, mp = _trunk_layer(x_prompt, p_prompt[0], None, None, lw, alpha=alpha, first_layer_ln=ln_in)
    ys, ks, vs, ms = _trunk_layer(x_sample, p_sample[0], cache_attn_k[0], cache_attn_v[0], lw,
                                  alpha=alpha, first_layer_ln=ln_in)
    return (yp, ys, kp[None], vp[None], ks[None], vs[None], mp[None], ms[None])
```

```python
import functools

import jax
import jax.numpy as jnp
import numpy as np
from jax import lax
from jax.experimental import pallas as pl
from jax.experimental.pallas import tpu as pltpu

F32 = jnp.float32
BF16 = jnp.bfloat16

D_MODEL = 1024
CHUNK = 64
HEADS = 8
HEAD_DIM = 64
A_WIDTH = HEADS * HEAD_DIM
PAST_CHUNKS = 8
MAX_REL = 256
MLP_CHUNK = 128
MLP_GROUPS = 8
MLP_WIDTH = 512
N_EXPERTS = 64
TOP_K = 8
N_GROUPS = 8
TOPK_GROUPS = 4
GROUP_SIZE = N_EXPERTS // N_GROUPS
EXPERT_DIM = 256
ROUTED_SCALE = 2.5
LN_EPS = 1e-5
MASKED = -1e30

LANES = 128
SUBLANES = 8
VMEM_LIMIT_BYTES = 56 * 1024 * 1024

ROW_TILE = 512
ROUTE_TILE = 2048
Q_CHUNKS = 4
MOE_GROUP_TOKENS = 4096
SCATTER_BATCH = 8
D_TILES = D_MODEL // LANES


def _layer_norm(x, g, b):
    xc = x - jnp.mean(x, axis=-1, keepdims=True)
    var = jnp.mean(xc * xc, axis=-1, keepdims=True)
    return xc * lax.rsqrt(var + LN_EPS) * g + b


def _dot(a, b):
    return jnp.dot(a, b, preferred_element_type=F32)


def _dot_nt(a, b, precision=None):
    return lax.dot_general(a, b, (((1,), (1,)), ((), ())),
                           preferred_element_type=F32, precision=precision)


def _const_spec(shape):
    zeros = (0,) * len(shape)
    return pl.BlockSpec(shape, lambda *_: zeros)


def _params(n_axes):
    return pltpu.CompilerParams(dimension_semantics=("arbitrary",) * n_axes,
                                vmem_limit_bytes=VMEM_LIMIT_BYTES)


def _proj_kernel(x_ref, lng_ref, lnb_ref, w_ref, mg_ref, mb_ref, ws_ref, bs_ref,
                 q_ref, k_ref, v_ref, kf_ref, vf_ref, vn_ref, ga_ref, gb_ref, ob_ref,
                 *, mlp_len, vn_rows):
    rows = x_ref.shape[0]
    xn = _layer_norm(x_ref[...], lng_ref[...], lnb_ref[...]).astype(BF16)

    def section(lo, width):
        return _dot(xn, w_ref[:, lo:lo + width])

    q = section(0, A_WIDTH)
    q_ref[...] = (q * (HEAD_DIM ** -0.5)).astype(BF16)
    k = section(A_WIDTH, A_WIDTH)
    k_ref[...] = k.astype(BF16)
    kf_ref[...] = k
    v = section(2 * A_WIDTH, A_WIDTH)
    v_ref[...] = v.astype(BF16)
    vf_ref[...] = v
    base = 3 * A_WIDTH
    un = jax.nn.gelu(section(base, MLP_WIDTH))
    vn = _layer_norm(jax.nn.gelu(section(base + MLP_WIDTH, MLP_WIDTH)), mg_ref[...], mb_ref[...])
    vn_ref[...] = vn[rows - vn_rows:, :]
    base += 2 * MLP_WIDTH
    ga_ref[...] = jax.nn.sigmoid(section(base, D_MODEL)).astype(BF16)
    gb_ref[...] = jax.nn.sigmoid(section(base + D_MODEL, D_MODEL)).astype(BF16)

    vnb = vn.astype(BF16)
    low_half = lax.broadcasted_iota(jnp.int32, (mlp_len, LANES), 1) < (MLP_WIDTH // MLP_GROUPS)
    for c in range(rows // mlp_len):
        r0 = c * mlp_len
        for p in range(MLP_GROUPS // 2):
            c0 = p * LANES
            slab = vnb[r0:r0 + mlp_len, c0:c0 + LANES]
            mixed = jnp.where(low_half, _dot(ws_ref[2 * p], slab), _dot(ws_ref[2 * p + 1], slab))
            gated = un[r0:r0 + mlp_len, c0:c0 + LANES] * (mixed + bs_ref[:, c0:c0 + LANES])
            ob_ref[r0:r0 + mlp_len, c0:c0 + LANES] = gated.astype(BF16)


def _proj(x2d, ln_g, ln_b, w_in, mlp_g, mlp_b, w_s, b_s, *, seq, mlp_len, keep_rows):
    n_rows = x2d.shape[0]
    tile = ROW_TILE
    assert n_rows % tile == 0 and tile % mlp_len == 0
    n_seq = n_rows // seq
    if seq >= tile:
        assert seq % tile == 0 and keep_rows == tile
        per_seq = seq // tile
        kv_rows, vn_rows = n_seq * tile, mlp_len
        kv_map = lambda i: (i // per_seq, 0)
    else:
        assert keep_rows == seq and mlp_len == seq
        kv_rows, vn_rows = n_rows, tile
        kv_map = lambda i: (i, 0)
    row = lambda width: pl.BlockSpec((tile, width), lambda i: (i, 0))
    n_in = w_in.shape[1]
    out_shape = (
        jax.ShapeDtypeStruct((n_rows, A_WIDTH), BF16),
        jax.ShapeDtypeStruct((n_rows, A_WIDTH), BF16),
        jax.ShapeDtypeStruct((n_rows, A_WIDTH), BF16),
        jax.ShapeDtypeStruct((kv_rows, A_WIDTH), F32),
        jax.ShapeDtypeStruct((kv_rows, A_WIDTH), F32),
        jax.ShapeDtypeStruct((kv_rows // tile * vn_rows, MLP_WIDTH), F32),
        jax.ShapeDtypeStruct((n_rows, D_MODEL), BF16),
        jax.ShapeDtypeStruct((n_rows, D_MODEL), BF16),
        jax.ShapeDtypeStruct((n_rows, MLP_WIDTH), BF16),
    )
    out_specs = (
        row(A_WIDTH), row(A_WIDTH), row(A_WIDTH),
        pl.BlockSpec((tile, A_WIDTH), kv_map), pl.BlockSpec((tile, A_WIDTH), kv_map),
        pl.BlockSpec((vn_rows, MLP_WIDTH), kv_map),
        row(D_MODEL), row(D_MODEL), row(MLP_WIDTH),
    )
    in_specs = [
        row(D_MODEL), _const_spec((1, D_MODEL)), _const_spec((1, D_MODEL)),
        _const_spec((D_MODEL, n_in)), _const_spec((1, MLP_WIDTH)), _const_spec((1, MLP_WIDTH)),
        _const_spec((MLP_GROUPS, mlp_len, mlp_len)), _const_spec((mlp_len, MLP_WIDTH)),
    ]
    return pl.pallas_call(
        functools.partial(_proj_kernel, mlp_len=mlp_len, vn_rows=vn_rows),
        grid=(n_rows // tile,), in_specs=in_specs, out_specs=out_specs, out_shape=out_shape,
        compiler_params=_params(1), name="proj",
    )(x2d, ln_g, ln_b, w_in, mlp_g, mlp_b, w_s, b_s)


def _attend_heads(q, keys, values, pair_bias, o_ref):
    n_q = q.shape[0]
    low_half = lax.broadcasted_iota(jnp.int32, (n_q, LANES), 1) < HEAD_DIM
    zero = jnp.zeros((), BF16)
    for p in range(HEADS // 2):
        c0 = p * LANES
        q2, k2, v2 = q[:, c0:c0 + LANES], keys[:, c0:c0 + LANES], values[:, c0:c0 + LANES]
        qs = jnp.concatenate([jnp.where(low_half, q2, zero), jnp.where(low_half, zero, q2)], axis=0)
        s = _dot_nt(qs, k2) + pair_bias(p).reshape(2 * n_q, keys.shape[0])
        e = jnp.exp(s - jnp.max(s, axis=-1, keepdims=True))
        denom = jnp.sum(e, axis=-1, keepdims=True)
        o = _dot(e.astype(BF16), v2) / denom
        o_ref[:, c0:c0 + LANES] = jnp.where(low_half, o[:n_q], o[n_q:]).astype(o_ref.dtype)


def _fill_toeplitz_bias(bias_ref, diag_ref, first_key_chunk=None):
    _, n_q, n_k = bias_ref.shape
    if first_key_chunk is not None:
        q_chunk = lax.broadcasted_iota(jnp.int32, (n_q, n_k), 0) // CHUNK
        k_chunk = lax.broadcasted_iota(jnp.int32, (n_q, n_k), 1) // CHUNK + first_key_chunk
        in_band = (k_chunk <= q_chunk) & (k_chunk >= q_chunk - PAST_CHUNKS)
    for h in range(bias_ref.shape[0]):
        rows = jnp.broadcast_to(diag_ref[h:h + 1, :], (n_q, diag_ref.shape[1]))
        rows = pltpu.roll(rows, 0, 1, stride=1, stride_axis=0)[:, :n_k]
        bias_ref[h] = rows if first_key_chunk is None else jnp.where(in_band, rows, MASKED)


def _attn_prompt_kernel(q_ref, k_ref, v_ref, diag_ref, o_ref, bias_ref, *, window, lead_steps):
    n_q = q_ref.shape[0]

    @pl.when((pl.program_id(0) == 0) & (pl.program_id(1) == 0))
    def _():
        _fill_toeplitz_bias(bias_ref, diag_ref, first_key_chunk=-lead_steps * (n_q // CHUNK))

    step = jnp.minimum(pl.program_id(1), lead_steps)
    start = pl.multiple_of((pl.program_id(1) - step) * n_q, n_q)
    shift = pl.multiple_of((lead_steps - step) * n_q, n_q)
    keys = k_ref[pl.ds(start, window), :]
    values = v_ref[pl.ds(start, window), :]
    _attend_heads(q_ref[...], keys, values,
                  lambda p: bias_ref[2 * p:2 * p + 2, :, pl.ds(shift, window)], o_ref)


def _bias_diagonals(rel_table, n_neg, n_pos, key_offset):
    period = pl.next_power_of_2(n_neg + n_pos)
    m = np.arange(period)
    diag = np.where(m < n_pos, m, m - period)
    return rel_table[:, np.clip(diag - key_offset, -MAX_REL, MAX_REL) + MAX_REL].astype(F32), period


def _attn_prompt(q, k, v, rel_table, *, batch, seq):
    n_q = Q_CHUNKS * CHUNK
    window = (Q_CHUNKS + PAST_CHUNKS) * CHUNK
    assert PAST_CHUNKS % Q_CHUNKS == 0 and seq >= window
    lead = PAST_CHUNKS // Q_CHUNKS
    steps = seq // n_q
    n_bias = lead * n_q + window
    diag, _ = _bias_diagonals(rel_table, n_q, n_bias, PAST_CHUNKS * CHUNK)
    qspec = pl.BlockSpec((n_q, A_WIDTH), lambda b, i: (b * steps + i, 0))
    kvspec = pl.BlockSpec((seq, A_WIDTH), lambda b, i: (b, 0))
    return pl.pallas_call(
        functools.partial(_attn_prompt_kernel, window=window, lead_steps=lead),
        grid=(batch, steps),
        in_specs=[qspec, kvspec, kvspec, _const_spec(diag.shape)],
        out_specs=qspec,
        out_shape=jax.ShapeDtypeStruct((batch * seq, A_WIDTH), BF16),
        scratch_shapes=[pltpu.VMEM((HEADS, n_q, n_bias), F32)],
        compiler_params=_params(2), name="attn_prompt",
    )(q, k, v, diag)


def _attn_sample_kernel(q_ref, k_ref, v_ref, ck_ref, cv_ref, diag_ref, o_ref, kk_ref, vv_ref, bias_ref):
    @pl.when(pl.program_id(0) == 0)
    def _():
        _fill_toeplitz_bias(bias_ref, diag_ref)

    n_cache = ck_ref.shape[1]
    n_new = k_ref.shape[1]
    kk_ref[0:n_cache, :] = ck_ref[0].astype(BF16)
    kk_ref[n_cache:n_cache + n_new, :] = k_ref[0]
    vv_ref[0:n_cache, :] = cv_ref[0].astype(BF16)
    vv_ref[n_cache:n_cache + n_new, :] = v_ref[0]
    _attend_heads(q_ref[0], kk_ref[...], vv_ref[...], lambda p: bias_ref[2 * p:2 * p + 2], o_ref.at[0])


def _attn_sample(q, k, v, cache_k, cache_v, rel_table, *, batch, seq):
    n_cache = cache_k.shape[1]
    diag, _ = _bias_diagonals(rel_table, seq, n_cache + seq, n_cache)
    new = pl.BlockSpec((1, seq, A_WIDTH), lambda b: (b, 0, 0))
    old = pl.BlockSpec((1, n_cache, A_WIDTH), lambda b: (b, 0, 0))
    out = pl.pallas_call(
        _attn_sample_kernel,
        grid=(batch,),
        in_specs=[new, new, new, old, old, _const_spec(diag.shape)],
        out_specs=new,
        out_shape=jax.ShapeDtypeStruct((batch, seq, A_WIDTH), BF16),
        scratch_shapes=[pltpu.VMEM((n_cache + seq, A_WIDTH), BF16),
                        pltpu.VMEM((n_cache + seq, A_WIDTH), BF16),
                        pltpu.VMEM((HEADS, seq, n_cache + seq), F32)],
        compiler_params=_params(1), name="attn_sample",
    )(q.reshape(batch, seq, A_WIDTH), k.reshape(batch, seq, A_WIDTH), v.reshape(batch, seq, A_WIDTH),
      cache_k.reshape(batch, n_cache, A_WIDTH), cache_v.reshape(batch, n_cache, A_WIDTH), diag)
    return out.reshape(batch * seq, A_WIDTH)


def _first_index(hit, iota, axis, limit):
    return jnp.min(jnp.where(hit, iota, limit), axis=axis, keepdims=True)


def _route(scores, sel):
    n_tok = scores.shape[1]
    neg = -jnp.inf
    grouped = sel.reshape(N_GROUPS, GROUP_SIZE, n_tok)
    member = lax.broadcasted_iota(jnp.int32, grouped.shape, 1)
    best = jnp.max(grouped, axis=1, keepdims=True)
    first = _first_index(grouped == best, member, 1, GROUP_SIZE)
    second = jnp.max(jnp.where(member == first, neg, grouped), axis=1, keepdims=True)
    group_score = best + second

    group_id = lax.broadcasted_iota(jnp.int32, group_score.shape, 0)
    keep = jnp.zeros(group_score.shape, F32)
    for _ in range(TOPK_GROUPS):
        top = jnp.max(group_score, axis=0, keepdims=True)
        hit = group_id == _first_index(group_score == top, group_id, 0, N_GROUPS)
        keep = jnp.where(hit, 1.0, keep)
        group_score = jnp.where(hit, neg, group_score)
    keep = jnp.broadcast_to(keep, grouped.shape).reshape(N_EXPERTS, n_tok)

    cand = jnp.where(keep > 0.0, sel, neg)
    expert_id = lax.broadcasted_iota(jnp.int32, cand.shape, 0)
    ids, weights = [], []
    for _ in range(TOP_K):
        top = jnp.max(cand, axis=0, keepdims=True)
        first = _first_index(cand == top, expert_id, 0, N_EXPERTS)
        hit = expert_id == first
        ids.append(first)
        weights.append(jnp.sum(jnp.where(hit, scores, 0.0), axis=0, keepdims=True))
        cand = jnp.where(hit, neg, cand)
    ids = jnp.concatenate(ids, axis=0)
    weights = jnp.concatenate(weights, axis=0)
    weights = weights / jnp.sum(weights, axis=0, keepdims=True) * ROUTED_SCALE
    return ids, weights


def _store_row_major_tiles(flat_ref, x):
    rows = x.shape[0]
    for j in range(D_TILES):
        flat_ref[pl.ds(j, rows, stride=D_TILES), :] = x[:, j * LANES:(j + 1) * LANES]


def _load_row_major_tiles(flat_ref, rows):
    return jnp.concatenate(
        [flat_ref[pl.ds(j, rows, stride=D_TILES), :] for j in range(D_TILES)], axis=1)


def _split_bf16(w):
    hi = w.astype(BF16)
    return jnp.concatenate([hi, (w - hi.astype(F32)).astype(BF16)], axis=0)


def _mix_kernel(x_ref, oa_ref, ob_ref, ga_ref, gb_ref, lng_ref, lnb_ref, wa_ref, wb_ref, wo_ref,
                g1_ref, b1_ref, rw_ref, x1_ref, x1t_ref, scores_ref, *, alpha):
    xn = _layer_norm(x_ref[...], lng_ref[...], lnb_ref[...])
    mix = (ga_ref[...].astype(F32) * _dot(oa_ref[...], wa_ref[...])
           + gb_ref[...].astype(F32) * _dot(ob_ref[...], wb_ref[...]))
    x1 = _layer_norm(alpha * xn + _dot(mix.astype(BF16), wo_ref[...]), g1_ref[...], b1_ref[...])
    x1_ref[...] = x1
    _store_row_major_tiles(x1t_ref, x1)
    x1_hi = x1.astype(BF16)
    x1_lo = (x1 - x1_hi.astype(F32)).astype(BF16)
    by_hi = _dot_nt(rw_ref[...], x1_hi)
    logits = by_hi[:N_EXPERTS] + by_hi[N_EXPERTS:] + _dot_nt(rw_ref[:N_EXPERTS, :], x1_lo)
    scores_ref[...] = jax.nn.sigmoid(logits)


def _mix(x2d, oa, ob, ga, gb, ln_g, ln_b, wa, wb, wo, g1, b1, rw_t, *, alpha):
    n_rows = x2d.shape[0]
    tile = ROW_TILE
    row = lambda width: pl.BlockSpec((tile, width), lambda i: (i, 0))
    return pl.pallas_call(
        functools.partial(_mix_kernel, alpha=alpha),
        grid=(n_rows // tile,),
        in_specs=[row(D_MODEL), row(A_WIDTH), row(MLP_WIDTH), row(D_MODEL), row(D_MODEL),
                  _const_spec((1, D_MODEL)), _const_spec((1, D_MODEL)),
                  _const_spec(wa.shape), _const_spec(wb.shape), _const_spec(wo.shape),
                  _const_spec((1, D_MODEL)), _const_spec((1, D_MODEL)),
                  _const_spec(rw_t.shape)],
        out_specs=(row(D_MODEL), pl.BlockSpec((tile * D_TILES, LANES), lambda i: (i, 0)),
                   pl.BlockSpec((N_EXPERTS, tile), lambda i: (0, i))),
        out_shape=(jax.ShapeDtypeStruct((n_rows, D_MODEL), F32),
                   jax.ShapeDtypeStruct((n_rows * D_TILES, LANES), F32),
                   jax.ShapeDtypeStruct((N_EXPERTS, n_rows), F32)),
        compiler_params=_params(1), name="mix",
    )(x2d, oa, ob, ga, gb, ln_g, ln_b, wa, wb, wo, g1, b1, rw_t)


def _route_kernel(scores_ref, rb_ref, ids_ref, wts_ref):
    scores = scores_ref[...]
    ids, weights = _route(scores, scores + rb_ref[...])
    ids_ref[...] = ids
    wts_ref[...] = weights


def _route_call(scores, rb):
    n_tok = scores.shape[1]
    tile = min(ROUTE_TILE, n_tok)
    col = lambda rows: pl.BlockSpec((rows, tile), lambda i: (0, i))
    return pl.pallas_call(
        _route_kernel,
        grid=(n_tok // tile,),
        in_specs=[col(N_EXPERTS), _const_spec(rb.shape)],
        out_specs=(col(TOP_K), col(TOP_K)),
        out_shape=(jax.ShapeDtypeStruct((TOP_K, n_tok), jnp.int32),
                   jax.ShapeDtypeStruct((TOP_K, n_tok), F32)),
        compiler_params=_params(1), name="route",
    )(scores, rb)


def _moe_kernel(rows_ref, start_ref, x_ref, wts_ref, w1_ref, w3_ref, w2_ref, *rest, n_rows):
    out_ref, gat_ref, y_full_ref, y_half_ref, pending_ref = rest[-5:]
    acc = out_ref.at[0]
    e = pl.program_id(0)
    half = n_rows // 2
    y_refs = {n_rows: y_full_ref, half: y_half_ref}

    @pl.when(e == 0)
    def _():
        out_ref[...] = jnp.zeros(out_ref.shape, out_ref.dtype)
        y_full_ref[...] = jnp.zeros(y_full_ref.shape, y_full_ref.dtype)
        y_half_ref[...] = jnp.zeros(y_half_ref.shape, y_half_ref.dtype)
        pending_ref[0] = 0
        pending_ref[1] = 0

    first = start_ref[e]
    count = start_ref[e + 1] - first

    def tile_at(ref, row8):
        return ref.at[pl.ds(pl.multiple_of(row8, SUBLANES), SUBLANES), :]

    def scatter_add(base, n):
        y_ref = y_refs[n]
        for m0 in range(0, n, SCATTER_BATCH):
            updates = []
            for m in range(m0, m0 + SCATTER_BATCH):
                dst = tile_at(acc, rows_ref[base + m])
                updates.append((dst, dst[...] + y_ref[m * SUBLANES:(m + 1) * SUBLANES, :]))
            for dst, val in updates:
                dst[...] = val

    def run_block(n, nominal, begin, pending):
        redo = nominal - begin
        base = first + begin + n_rows
        for m in range(n):
            gat_ref[m * SUBLANES:(m + 1) * SUBLANES, :] = tile_at(x_ref, rows_ref[base + m])[...]
        xb = _load_row_major_tiles(gat_ref, n).astype(BF16)
        act = jax.nn.silu(_dot(xb, w1_ref[0])) * _dot(xb, w3_ref[0])
        y = _dot(act.astype(BF16), w2_ref[0])

        w_rows = pl.cdiv(n, LANES) + 1
        row_id = lax.broadcasted_iota(jnp.int32, (n, w_rows * LANES), 0)
        lane_id = lax.broadcasted_iota(jnp.int32, (n, w_rows * LANES), 1)
        w_row = base // LANES
        span = jnp.concatenate([wts_ref[w_row + r] for r in range(w_rows)], axis=1)
        picked = jnp.where((lane_id == row_id + base % LANES) & (row_id >= redo),
                           jnp.broadcast_to(span, lane_id.shape), 0.0)
        ys = y * jnp.sum(picked, axis=1, keepdims=True)
        scatter_add(pending, n)
        _store_row_major_tiles(y_refs[n], ys)
        return base

    full_blocks = count // n_rows
    rest_rows = count - full_blocks * n_rows
    main_blocks = full_blocks + (rest_rows > half).astype(jnp.int32)

    def main_block(b, pending):
        nominal = b * n_rows
        return run_block(n_rows, nominal, jnp.minimum(nominal, count - n_rows), pending)

    pending_ref[0] = lax.fori_loop(0, main_blocks, main_block, pending_ref[0])

    @pl.when((rest_rows > 0) & (rest_rows <= half))
    def _():
        pending_ref[1] = run_block(half, full_blocks * n_rows, count - half, pending_ref[1])

    @pl.when(e == pl.num_programs(0) - 1)
    def _():
        scatter_add(pending_ref[0], n_rows)
        scatter_add(pending_ref[1], half)


def _moe(ids, weights, x_tiles, w1, w3, w2):
    n_tok = ids.shape[1]
    g_tok = min(MOE_GROUP_TOKENS, n_tok)
    n_groups = n_tok // g_tok
    assert n_groups * g_tok == n_tok
    per_group = g_tok * TOP_K
    tok = lax.broadcasted_iota(jnp.int32, ids.shape, 1)
    key = ((tok // g_tok) * N_EXPERTS + ids) * g_tok + tok % g_tok
    key, wts = lax.sort((key.reshape(-1), weights.reshape(-1)), num_keys=1)
    rows8 = ((key % g_tok) * D_TILES).reshape(n_groups, per_group)
    wts = wts.reshape(n_groups, per_group)
    experts = jnp.arange(N_EXPERTS, dtype=jnp.int32)[None, None, :, None]
    counts = jnp.sum(ids.reshape(TOP_K, n_groups, 1, g_tok) == experts, axis=(0, 3), dtype=jnp.int32)
    start = jnp.concatenate([jnp.zeros((n_groups, 1), jnp.int32), jnp.cumsum(counts, axis=1)], axis=1)
    n_rows = 256 if per_group // N_EXPERTS >= 512 else 128
    pad = ((0, 0), (n_rows, n_rows + LANES))
    rows8 = jnp.pad(rows8, pad)
    wts = jnp.pad(wts, pad).reshape(n_groups, -1, 1, LANES)
    out = None
    for g in range(n_groups):
        out = _moe_group_call(g, n_groups, g_tok, n_rows, rows8[g], start[g], x_tiles, wts[g], w1, w3, w2, out)
    return out


def _moe_group_call(group, n_groups, g_tok, n_rows, rows8, start, x_tiles, wts, w1, w3, w2, prev):
    g_rows = g_tok * D_TILES
    assert x_tiles.shape[0] == n_groups * g_rows
    out_block = (1, g_rows, LANES)
    in_specs = [
        pl.BlockSpec((g_rows, LANES), lambda e, *_: (group, 0), pipeline_mode=pl.Buffered(1)),
        pl.BlockSpec(wts.shape, lambda e, *_: (0, 0, 0)),
        pl.BlockSpec((1, D_MODEL, EXPERT_DIM), lambda e, *_: (e, 0, 0)),
        pl.BlockSpec((1, D_MODEL, EXPERT_DIM), lambda e, *_: (e, 0, 0)),
        pl.BlockSpec((1, EXPERT_DIM, D_MODEL), lambda e, *_: (e, 0, 0)),
    ]
    args = [rows8, start, x_tiles, wts, w1, w3, w2]
    aliases = {}
    if prev is not None:
        in_specs.append(pl.BlockSpec(memory_space=pl.ANY))
        aliases = {len(args): 0}
        args.append(prev)
    return pl.pallas_call(
        functools.partial(_moe_kernel, n_rows=n_rows),
        grid_spec=pltpu.PrefetchScalarGridSpec(
            num_scalar_prefetch=2, grid=(N_EXPERTS,), in_specs=in_specs,
            out_specs=pl.BlockSpec(out_block, lambda e, *_: (group, 0, 0), pipeline_mode=pl.Buffered(1)),
            scratch_shapes=[pltpu.VMEM((n_rows * D_TILES, LANES), F32),
                            pltpu.VMEM((n_rows * D_TILES, LANES), F32),
                            pltpu.VMEM((n_rows // 2 * D_TILES, LANES), F32),
                            pltpu.SMEM((2,), jnp.int32)]),
        out_shape=jax.ShapeDtypeStruct((n_groups,) + out_block[1:], F32),
        input_output_aliases=aliases,
        compiler_params=_params(1), name="moe",
    )(*args)


def _final_kernel(x1_ref, routed_ref, p_ref, s1_ref, s3_ref, s2_ref, g2_ref, b2_ref, wg_ref, wp_ref,
                  o_ref, *, alpha):
    x1 = x1_ref[...]
    xb = x1.astype(BF16)
    act = jax.nn.silu(_dot(xb, s1_ref[...])) * _dot(xb, s3_ref[...])
    ff = _load_row_major_tiles(routed_ref.at[0], x1.shape[0]) + _dot(act.astype(BF16), s2_ref[...])
    x2 = _layer_norm(alpha * x1 + ff, g2_ref[...], b2_ref[...])
    gate = jax.nn.sigmoid(_dot(x2.astype(BF16), wg_ref[...]))
    o_ref[...] = x2 + gate * _dot(p_ref[...].astype(BF16), wp_ref[...])


def _final(x1, routed, p2d, s1, s3, s2, g2, b2, wg, wp, *, alpha):
    n_rows = x1.shape[0]
    tile = ROW_TILE
    row = lambda width: pl.BlockSpec((tile, width), lambda i: (i, 0))
    tiles_per_group = routed.shape[1] // D_TILES // tile
    routed_spec = pl.BlockSpec((1, tile * D_TILES, LANES),
                               lambda i: (i // tiles_per_group, i % tiles_per_group, 0))
    return pl.pallas_call(
        functools.partial(_final_kernel, alpha=alpha),
        grid=(n_rows // tile,),
        in_specs=[row(D_MODEL), routed_spec,
                  row(p2d.shape[1]),
                  _const_spec(s1.shape), _const_spec(s3.shape), _const_spec(s2.shape),
                  _const_spec((1, D_MODEL)), _const_spec((1, D_MODEL)),
                  _const_spec(wg.shape), _const_spec(wp.shape)],
        out_specs=row(D_MODEL),
        out_shape=jax.ShapeDtypeStruct((n_rows, D_MODEL), F32),
        compiler_params=_params(1), name="final",
    )(x1, routed, p2d, s1, s3, s2, g2, b2, wg, wp)


def _trunk_layer(x, p, cache_k, cache_v, lw, *, alpha, first_layer_ln):
    batch, seq, d = x.shape
    n_rows = batch * seq
    x2d = x.reshape(n_rows, d)
    prompt = cache_k is None
    mlp_len = min(seq, MLP_CHUNK)
    keep = min(PAST_CHUNKS * CHUNK, seq) if prompt else seq
    ln_g, ln_b = first_layer_ln

    w_s = (lw["mlp_w_s"][:, :mlp_len, :mlp_len]
           * jnp.tril(jnp.ones((mlp_len, mlp_len), F32))).astype(BF16)
    b_s = jnp.repeat(lw["mlp_b_s"][:, :mlp_len].T, MLP_WIDTH // MLP_GROUPS, axis=1)
    q, k, v, kf, vf, vn, ga, gb, ob = _proj(
        x2d, ln_g, ln_b, lw["w_in"].astype(BF16), lw["mlp_ln_g"][None], lw["mlp_ln_b"][None],
        w_s, b_s, seq=seq, mlp_len=mlp_len, keep_rows=keep)
    if prompt:
        oa = _attn_prompt(q, k, v, lw["attn_rel_bias"], batch=batch, seq=seq)
    else:
        oa = _attn_sample(q, k, v, cache_k, cache_v, lw["attn_rel_bias"], batch=batch, seq=seq)
    x1, x1_tiles, scores = _mix(
        x2d, oa, ob, ga, gb, ln_g, ln_b, lw["w_branch_a"].astype(BF16), lw["w_branch_b"].astype(BF16),
        lw["w_out"].astype(BF16), lw["ln1_g"][None], lw["ln1_b"][None], _split_bf16(lw["router_w"].T),
        alpha=alpha)
    ids, weights = _route_call(scores, lw["router_bias"][:, None])
    routed = _moe(ids, weights, x1_tiles, lw["exp_w1"], lw["exp_w3"], lw["exp_w2"])
    y = _final(x1, routed, p.reshape(n_rows, -1), lw["shared_w1"].astype(BF16),
               lw["shared_w3"].astype(BF16), lw["shared_w2"].astype(BF16),
               lw["ln2_g"][None], lw["ln2_b"][None], lw["ple_w_gate"].astype(BF16),
               lw["ple_w_proj"].astype(BF16), alpha=alpha)
    new_k = kf.reshape(batch, keep, HEADS, HEAD_DIM)
    new_v = vf.reshape(batch, keep, HEADS, HEAD_DIM)
    new_mlp_v = vn.reshape(batch, mlp_len, MLP_WIDTH)
    return y.reshape(batch, seq, d), new_k, new_v, new_mlp_v


def kernel(x_prompt, x_sample, cache_attn_k, cache_attn_v, p_prompt, p_sample, ln_in_g, ln_in_b, w_in, attn_rel_bias, mlp_ln_g, mlp_ln_b, mlp_w_s, mlp_b_s, w_branch_a, w_branch_b, w_out, ln1_g, ln1_b, router_w, router_bias, exp_w1, exp_w3, exp_w2, shared_w1, shared_w3, shared_w2, ln2_g, ln2_b, ple_w_gate, ple_w_proj):
    depth = w_in.shape[0]
    assert depth == 1, "the fused proj/mix kernels apply LayerNorm_in themselves: single layer only"
    alpha = (2 * depth) ** 0.25
    stacked = dict(w_in=w_in, attn_rel_bias=attn_rel_bias, mlp_ln_g=mlp_ln_g, mlp_ln_b=mlp_ln_b,
                   mlp_w_s=mlp_w_s, mlp_b_s=mlp_b_s, w_branch_a=w_branch_a, w_branch_b=w_branch_b,
                   w_out=w_out, ln1_g=ln1_g, ln1_b=ln1_b, router_w=router_w, router_bias=router_bias,
                   exp_w1=exp_w1, exp_w3=exp_w3, exp_w2=exp_w2, shared_w1=shared_w1,
                   shared_w3=shared_w3, shared_w2=shared_w2, ln2_g=ln2_g, ln2_b=ln2_b,
                   ple_w_gate=ple_w_gate, ple_w_proj=ple_w_proj)
    lw = {name: value[0] for name, value in stacked.items()}
    for name in ("exp_w1", "exp_w3", "exp_w2"):
        lw[name] = lw[name].astype(BF16)
    ln_in =(ln_in_g[None], ln_in_b[None])
    yp, kp, vp, mp = _trunk_layer(x_prompt, p_prompt[0], None, None, lw, alpha=alpha, first_layer_ln=ln_in)
    ys, ks, vs, ms = _trunk_layer(x_sample, p_sample[0], cache_attn_k[0], cache_attn_v[0], lw,
                                  alpha=alpha, first_layer_ln=ln_in)
    return (yp, ys, kp[None], vp[None], ks[None], vs[None], mp[None], ms[None])
```

```python
import functools

import jax
import jax.numpy as jnp
import numpy as np
from jax import lax
from jax.experimental import pallas as pl
from jax.experimental.pallas import tpu as pltpu

F32 = jnp.float32
BF16 = jnp.bfloat16

D_MODEL = 1024
CHUNK = 64
HEADS = 8
HEAD_DIM = 64
A_WIDTH = HEADS * HEAD_DIM
PAST_CHUNKS = 8
MAX_REL = 256
MLP_CHUNK = 128
MLP_GROUPS = 8
MLP_WIDTH = 512
N_EXPERTS = 64
TOP_K = 8
N_GROUPS = 8
TOPK_GROUPS = 4
GROUP_SIZE = N_EXPERTS // N_GROUPS
EXPERT_DIM = 256
ROUTED_SCALE = 2.5
LN_EPS = 1e-5
MASKED = -1e30

LANES = 128
SUBLANES = 8
VMEM_LIMIT_BYTES = 56 * 1024 * 1024

ROW_TILE = 512
ROUTE_TILE = 2048
Q_CHUNKS = 4
MOE_GROUP_TOKENS = 4096
MOE_BLOCK_ROWS = 256
MOE_SMALL_BLOCK_ROWS = 128
CAST_COLS = 512
SCATTER_BATCH = 8
D_TILES = D_MODEL // LANES


def _layer_norm(x, g, b):
    xc = x - jnp.mean(x, axis=-1, keepdims=True)
    var = jnp.mean(xc * xc, axis=-1, keepdims=True)
    return xc * lax.rsqrt(var + LN_EPS) * g + b


def _dot(a, b):
    return jnp.dot(a, b, preferred_element_type=F32)


def _dot_nt(a, b, precision=None):
    return lax.dot_general(a, b, (((1,), (1,)), ((), ())),
                           preferred_element_type=F32, precision=precision)


def _const_spec(shape):
    zeros = (0,) * len(shape)
    return pl.BlockSpec(shape, lambda *_: zeros)


def _resident_spec(shape):
    zeros = (0,) * len(shape)
    return pl.BlockSpec(shape, lambda *_: zeros, pipeline_mode=pl.Buffered(1))


def _cast_weights_once(pairs):
    @pl.when(pl.program_id(0) == 0)
    def _():
        for src, dst in pairs:
            cols = src.shape[1]
            for c in range(0, cols, CAST_COLS):
                dst[:, c:min(c + CAST_COLS, cols)] = src[:, c:min(c + CAST_COLS, cols)].astype(BF16)


def _bf16_scratch(*weights):
    return [pltpu.VMEM(w.shape, BF16) for w in weights]


def _params(n_axes):
    return pltpu.CompilerParams(dimension_semantics=("arbitrary",) * n_axes,
                                vmem_limit_bytes=VMEM_LIMIT_BYTES)


def _proj_kernel(x_ref, lng_ref, lnb_ref, w32_ref, mg_ref, mb_ref, ws_ref, bs_ref,
                 q_ref, k_ref, v_ref, kf_ref, vf_ref, vn_ref, ga_ref, gb_ref, ob_ref, w_ref,
                 *, mlp_len, vn_rows):
    _cast_weights_once([(w32_ref, w_ref)])
    rows = x_ref.shape[0]
    xn = _layer_norm(x_ref[...], lng_ref[...], lnb_ref[...]).astype(BF16)

    def section(lo, width):
        return _dot(xn, w_ref[:, lo:lo + width])

    q = section(0, A_WIDTH)
    q_ref[...] = (q * (HEAD_DIM ** -0.5)).astype(BF16)
    k = section(A_WIDTH, A_WIDTH)
    k_ref[...] = k.astype(BF16)
    kf_ref[...] = k
    v = section(2 * A_WIDTH, A_WIDTH)
    v_ref[...] = v.astype(BF16)
    vf_ref[...] = v
    base = 3 * A_WIDTH
    un = jax.nn.gelu(section(base, MLP_WIDTH))
    vn = _layer_norm(jax.nn.gelu(section(base + MLP_WIDTH, MLP_WIDTH)), mg_ref[...], mb_ref[...])
    vn_ref[...] = vn[rows - vn_rows:, :]
    base += 2 * MLP_WIDTH
    ga_ref[...] = jax.nn.sigmoid(section(base, D_MODEL)).astype(BF16)
    gb_ref[...] = jax.nn.sigmoid(section(base + D_MODEL, D_MODEL)).astype(BF16)

    vnb = vn.astype(BF16)
    low_half = lax.broadcasted_iota(jnp.int32, (mlp_len, LANES), 1) < (MLP_WIDTH // MLP_GROUPS)
    for c in range(rows // mlp_len):
        r0 = c * mlp_len
        for p in range(MLP_GROUPS // 2):
            c0 = p * LANES
            slab = vnb[r0:r0 + mlp_len, c0:c0 + LANES]
            mixed = jnp.where(low_half, _dot(ws_ref[2 * p], slab), _dot(ws_ref[2 * p + 1], slab))
            gated = un[r0:r0 + mlp_len, c0:c0 + LANES] * (mixed + bs_ref[:, c0:c0 + LANES])
            ob_ref[r0:r0 + mlp_len, c0:c0 + LANES] = gated.astype(BF16)


def _proj(x2d, ln_g, ln_b, w_in, mlp_g, mlp_b, w_s, b_s, *, seq, mlp_len, keep_rows):
    n_rows = x2d.shape[0]
    tile = ROW_TILE
    assert n_rows % tile == 0 and tile % mlp_len == 0
    n_seq = n_rows // seq
    if seq >= tile:
        assert seq % tile == 0 and keep_rows == tile
        per_seq = seq // tile
        kv_rows, vn_rows = n_seq * tile, mlp_len
        kv_map = lambda i: (i // per_seq, 0)
    else:
        assert keep_rows == seq and mlp_len == seq
        kv_rows, vn_rows = n_rows, tile
        kv_map = lambda i: (i, 0)
    row = lambda width: pl.BlockSpec((tile, width), lambda i: (i, 0))
    n_in = w_in.shape[1]
    out_shape = (
        jax.ShapeDtypeStruct((n_rows, A_WIDTH), BF16),
        jax.ShapeDtypeStruct((n_rows, A_WIDTH), BF16),
        jax.ShapeDtypeStruct((n_rows, A_WIDTH), BF16),
        jax.ShapeDtypeStruct((kv_rows, A_WIDTH), F32),
        jax.ShapeDtypeStruct((kv_rows, A_WIDTH), F32),
        jax.ShapeDtypeStruct((kv_rows // tile * vn_rows, MLP_WIDTH), F32),
        jax.ShapeDtypeStruct((n_rows, D_MODEL), BF16),
        jax.ShapeDtypeStruct((n_rows, D_MODEL), BF16),
        jax.ShapeDtypeStruct((n_rows, MLP_WIDTH), BF16),
    )
    out_specs = (
        row(A_WIDTH), row(A_WIDTH), row(A_WIDTH),
        pl.BlockSpec((tile, A_WIDTH), kv_map), pl.BlockSpec((tile, A_WIDTH), kv_map),
        pl.BlockSpec((vn_rows, MLP_WIDTH), kv_map),
        row(D_MODEL), row(D_MODEL), row(MLP_WIDTH),
    )
    in_specs = [
        row(D_MODEL), _const_spec((1, D_MODEL)), _const_spec((1, D_MODEL)),
        _resident_spec((D_MODEL, n_in)), _const_spec((1, MLP_WIDTH)), _const_spec((1, MLP_WIDTH)),
        _const_spec((MLP_GROUPS, mlp_len, mlp_len)), _const_spec((mlp_len, MLP_WIDTH)),
    ]
    return pl.pallas_call(
        functools.partial(_proj_kernel, mlp_len=mlp_len, vn_rows=vn_rows),
        grid=(n_rows // tile,), in_specs=in_specs, out_specs=out_specs, out_shape=out_shape,
        scratch_shapes=_bf16_scratch(w_in), compiler_params=_params(1), name="proj",
    )(x2d, ln_g, ln_b, w_in, mlp_g, mlp_b, w_s, b_s)


def _attend_heads(q, keys, values, pair_bias, o_ref):
    n_q = q.shape[0]
    low_half = lax.broadcasted_iota(jnp.int32, (n_q, LANES), 1) < HEAD_DIM
    zero = jnp.zeros((), BF16)
    for p in range(HEADS // 2):
        c0 = p * LANES
        q2, k2, v2 = q[:, c0:c0 + LANES], keys[:, c0:c0 + LANES], values[:, c0:c0 + LANES]
        qs = jnp.concatenate([jnp.where(low_half, q2, zero), jnp.where(low_half, zero, q2)], axis=0)
        s = _dot_nt(qs, k2) + pair_bias(p).reshape(2 * n_q, keys.shape[0])
        e = jnp.exp(s - jnp.max(s, axis=-1, keepdims=True))
        denom = jnp.sum(e, axis=-1, keepdims=True)
        o = _dot(e.astype(BF16), v2) / denom
        o_ref[:, c0:c0 + LANES] = jnp.where(low_half, o[:n_q], o[n_q:]).astype(o_ref.dtype)


def _fill_toeplitz_bias(bias_ref, diag_ref, first_key_chunk=None):
    _, n_q, n_k = bias_ref.shape
    if first_key_chunk is not None:
        q_chunk = lax.broadcasted_iota(jnp.int32, (n_q, n_k), 0) // CHUNK
        k_chunk = lax.broadcasted_iota(jnp.int32, (n_q, n_k), 1) // CHUNK + first_key_chunk
        in_band = (k_chunk <= q_chunk) & (k_chunk >= q_chunk - PAST_CHUNKS)
    for h in range(bias_ref.shape[0]):
        rows = jnp.broadcast_to(diag_ref[h:h + 1, :], (n_q, diag_ref.shape[1]))
        rows = pltpu.roll(rows, 0, 1, stride=1, stride_axis=0)[:, :n_k]
        bias_ref[h] = rows if first_key_chunk is None else jnp.where(in_band, rows, MASKED)


def _attn_prompt_kernel(q_ref, k_ref, v_ref, diag_ref, o_ref, bias_ref, *, window, lead_steps):
    n_q = q_ref.shape[0]

    @pl.when((pl.program_id(0) == 0) & (pl.program_id(1) == 0))
    def _():
        _fill_toeplitz_bias(bias_ref, diag_ref, first_key_chunk=-lead_steps * (n_q // CHUNK))

    step = jnp.minimum(pl.program_id(1), lead_steps)
    start = pl.multiple_of((pl.program_id(1) - step) * n_q, n_q)
    shift = pl.multiple_of((lead_steps - step) * n_q, n_q)
    keys = k_ref[pl.ds(start, window), :]
    values = v_ref[pl.ds(start, window), :]
    _attend_heads(q_ref[...], keys, values,
                  lambda p: bias_ref[2 * p:2 * p + 2, :, pl.ds(shift, window)], o_ref)


def _bias_diagonals(rel_table, n_neg, n_pos, key_offset):
    period = pl.next_power_of_2(n_neg + n_pos)
    m = np.arange(period)
    diag = np.where(m < n_pos, m, m - period)
    return rel_table[:, np.clip(diag - key_offset, -MAX_REL, MAX_REL) + MAX_REL].astype(F32), period


def _attn_prompt(q, k, v, rel_table, *, batch, seq):
    n_q = Q_CHUNKS * CHUNK
    window = (Q_CHUNKS + PAST_CHUNKS) * CHUNK
    assert PAST_CHUNKS % Q_CHUNKS == 0 and seq >= window
    lead = PAST_CHUNKS // Q_CHUNKS
    steps = seq // n_q
    n_bias = lead * n_q + window
    diag, _ = _bias_diagonals(rel_table, n_q, n_bias, PAST_CHUNKS * CHUNK)
    qspec = pl.BlockSpec((n_q, A_WIDTH), lambda b, i: (b * steps + i, 0))
    kvspec = pl.BlockSpec((seq, A_WIDTH), lambda b, i: (b, 0))
    return pl.pallas_call(
        functools.partial(_attn_prompt_kernel, window=window, lead_steps=lead),
        grid=(batch, steps),
        in_specs=[qspec, kvspec, kvspec, _const_spec(diag.shape)],
        out_specs=qspec,
        out_shape=jax.ShapeDtypeStruct((batch * seq, A_WIDTH), BF16),
        scratch_shapes=[pltpu.VMEM((HEADS, n_q, n_bias), F32)],
        compiler_params=_params(2), name="attn_prompt",
    )(q, k, v, diag)


def _attn_sample_kernel(q_ref, k_ref, v_ref, ck_ref, cv_ref, diag_ref, o_ref, kk_ref, vv_ref, bias_ref):
    @pl.when(pl.program_id(0) == 0)
    def _():
        _fill_toeplitz_bias(bias_ref, diag_ref)

    n_cache = ck_ref.shape[1]
    n_new = k_ref.shape[1]
    kk_ref[0:n_cache, :] = ck_ref[0].astype(BF16)
    kk_ref[n_cache:n_cache + n_new, :] = k_ref[0]
    vv_ref[0:n_cache, :] = cv_ref[0].astype(BF16)
    vv_ref[n_cache:n_cache + n_new, :] = v_ref[0]
    _attend_heads(q_ref[0], kk_ref[...], vv_ref[...], lambda p: bias_ref[2 * p:2 * p + 2], o_ref.at[0])


def _attn_sample(q, k, v, cache_k, cache_v, rel_table, *, batch, seq):
    n_cache = cache_k.shape[1]
    diag, _ = _bias_diagonals(rel_table, seq, n_cache + seq, n_cache)
    new = pl.BlockSpec((1, seq, A_WIDTH), lambda b: (b, 0, 0))
    old = pl.BlockSpec((1, n_cache, A_WIDTH), lambda b: (b, 0, 0))
    out = pl.pallas_call(
        _attn_sample_kernel,
        grid=(batch,),
        in_specs=[new, new, new, old, old, _const_spec(diag.shape)],
        out_specs=new,
        out_shape=jax.ShapeDtypeStruct((batch, seq, A_WIDTH), BF16),
        scratch_shapes=[pltpu.VMEM((n_cache + seq, A_WIDTH), BF16),
                        pltpu.VMEM((n_cache + seq, A_WIDTH), BF16),
                        pltpu.VMEM((HEADS, seq, n_cache + seq), F32)],
        compiler_params=_params(1), name="attn_sample",
    )(q.reshape(batch, seq, A_WIDTH), k.reshape(batch, seq, A_WIDTH), v.reshape(batch, seq, A_WIDTH),
      cache_k.reshape(batch, n_cache, A_WIDTH), cache_v.reshape(batch, n_cache, A_WIDTH), diag)
    return out.reshape(batch * seq, A_WIDTH)


def _first_index(hit, iota, axis, limit):
    return jnp.min(jnp.where(hit, iota, limit), axis=axis, keepdims=True)


def _route(scores, sel):
    n_tok = scores.shape[1]
    neg = -jnp.inf
    grouped = sel.reshape(N_GROUPS, GROUP_SIZE, n_tok)
    member = lax.broadcasted_iota(jnp.int32, grouped.shape, 1)
    best = jnp.max(grouped, axis=1, keepdims=True)
    first = _first_index(grouped == best, member, 1, GROUP_SIZE)
    second = jnp.max(jnp.where(member == first, neg, grouped), axis=1, keepdims=True)
    group_score = best + second

    group_id = lax.broadcasted_iota(jnp.int32, group_score.shape, 0)
    keep = jnp.zeros(group_score.shape, F32)
    for _ in range(TOPK_GROUPS):
        top = jnp.max(group_score, axis=0, keepdims=True)
        hit = group_id == _first_index(group_score == top, group_id, 0, N_GROUPS)
        keep = jnp.where(hit, 1.0, keep)
        group_score = jnp.where(hit, neg, group_score)
    keep = jnp.broadcast_to(keep, grouped.shape).reshape(N_EXPERTS, n_tok)

    cand = jnp.where(keep > 0.0, sel, neg)
    expert_id = lax.broadcasted_iota(jnp.int32, cand.shape, 0)
    ids, weights = [], []
    for _ in range(TOP_K):
        top = jnp.max(cand, axis=0, keepdims=True)
        first = _first_index(cand == top, expert_id, 0, N_EXPERTS)
        hit = expert_id == first
        ids.append(first)
        weights.append(jnp.sum(jnp.where(hit, scores, 0.0), axis=0, keepdims=True))
        cand = jnp.where(hit, neg, cand)
    ids = jnp.concatenate(ids, axis=0)
    weights = jnp.concatenate(weights, axis=0)
    weights = weights / jnp.sum(weights, axis=0, keepdims=True) * ROUTED_SCALE
    return ids, weights


def _store_row_major_tiles(flat_ref, x):
    rows = x.shape[0]
    for j in range(D_TILES):
        flat_ref[pl.ds(j, rows, stride=D_TILES), :] = x[:, j * LANES:(j + 1) * LANES]


def _load_row_major_tiles(flat_ref, rows):
    return jnp.concatenate(
        [flat_ref[pl.ds(j, rows, stride=D_TILES), :] for j in range(D_TILES)], axis=1)


def _split_bf16(w):
    hi = w.astype(BF16)
    return jnp.concatenate([hi, (w - hi.astype(F32)).astype(BF16)], axis=0)


def _mix_kernel(x_ref, oa_ref, ob_ref, ga_ref, gb_ref, lng_ref, lnb_ref, wa32_ref, wb32_ref, wo32_ref,
                g1_ref, b1_ref, rw_ref, x1_ref, x1t_ref, scores_ref, wa_ref, wb_ref, wo_ref, *, alpha):
    _cast_weights_once([(wa32_ref, wa_ref), (wb32_ref, wb_ref), (wo32_ref, wo_ref)])
    xn = _layer_norm(x_ref[...], lng_ref[...], lnb_ref[...])
    mix = (ga_ref[...].astype(F32) * _dot(oa_ref[...], wa_ref[...])
           + gb_ref[...].astype(F32) * _dot(ob_ref[...], wb_ref[...]))
    x1 = _layer_norm(alpha * xn + _dot(mix.astype(BF16), wo_ref[...]), g1_ref[...], b1_ref[...])
    x1_ref[...] = x1
    _store_row_major_tiles(x1t_ref, x1)
    x1_hi = x1.astype(BF16)
    x1_lo = (x1 - x1_hi.astype(F32)).astype(BF16)
    by_hi = _dot_nt(rw_ref[...], x1_hi)
    logits = by_hi[:N_EXPERTS] + by_hi[N_EXPERTS:] + _dot_nt(rw_ref[:N_EXPERTS, :], x1_lo)
    scores_ref[...] = jax.nn.sigmoid(logits)


def _mix(x2d, oa, ob, ga, gb, ln_g, ln_b, wa, wb, wo, g1, b1, rw_t, *, alpha):
    n_rows = x2d.shape[0]
    tile = ROW_TILE
    row = lambda width: pl.BlockSpec((tile, width), lambda i: (i, 0))
    return pl.pallas_call(
        functools.partial(_mix_kernel, alpha=alpha),
        grid=(n_rows // tile,),
        in_specs=[row(D_MODEL), row(A_WIDTH), row(MLP_WIDTH), row(D_MODEL), row(D_MODEL),
                  _const_spec((1, D_MODEL)), _const_spec((1, D_MODEL)),
                  _resident_spec(wa.shape), _resident_spec(wb.shape), _resident_spec(wo.shape),
                  _const_spec((1, D_MODEL)), _const_spec((1, D_MODEL)),
                  _const_spec(rw_t.shape)],
        out_specs=(row(D_MODEL), pl.BlockSpec((tile * D_TILES, LANES), lambda i: (i, 0)),
                   pl.BlockSpec((N_EXPERTS, tile), lambda i: (0, i))),
        out_shape=(jax.ShapeDtypeStruct((n_rows, D_MODEL), F32),
                   jax.ShapeDtypeStruct((n_rows * D_TILES, LANES), F32),
                   jax.ShapeDtypeStruct((N_EXPERTS, n_rows), F32)),
        scratch_shapes=_bf16_scratch(wa, wb, wo), compiler_params=_params(1), name="mix",
    )(x2d, oa, ob, ga, gb, ln_g, ln_b, wa, wb, wo, g1, b1, rw_t)


def _route_kernel(scores_ref, rb_ref, ids_ref, wts_ref):
    scores = scores_ref[...]
    ids, weights = _route(scores, scores + rb_ref[...])
    ids_ref[...] = ids
    wts_ref[...] = weights


def _route_call(scores, rb):
    n_tok = scores.shape[1]
    tile = min(ROUTE_TILE, n_tok)
    col = lambda rows: pl.BlockSpec((rows, tile), lambda i: (0, i))
    return pl.pallas_call(
        _route_kernel,
        grid=(n_tok // tile,),
        in_specs=[col(N_EXPERTS), _const_spec(rb.shape)],
        out_specs=(col(TOP_K), col(TOP_K)),
        out_shape=(jax.ShapeDtypeStruct((TOP_K, n_tok), jnp.int32),
                   jax.ShapeDtypeStruct((TOP_K, n_tok), F32)),
        compiler_params=_params(1), name="route",
    )(scores, rb)


def _moe_kernel(rows_ref, start_ref, x_ref, wts_ref, w1_ref, w3_ref, w2_ref, *rest, n_rows):
    out_ref, gat_ref, y_full_ref, y_half_ref, pending_ref = rest[-5:]
    acc = out_ref.at[0]
    e = pl.program_id(0)
    half = n_rows // 2
    y_refs = {n_rows: y_full_ref, half: y_half_ref}

    @pl.when(e == 0)
    def _():
        out_ref[...] = jnp.zeros(out_ref.shape, out_ref.dtype)
        y_full_ref[...] = jnp.zeros(y_full_ref.shape, y_full_ref.dtype)
        y_half_ref[...] = jnp.zeros(y_half_ref.shape, y_half_ref.dtype)
        pending_ref[0] = 0
        pending_ref[1] = 0

    first = start_ref[e]
    count = start_ref[e + 1] - first

    def tile_at(ref, row8):
        return ref.at[pl.ds(pl.multiple_of(row8, SUBLANES), SUBLANES), :]

    def scatter_add(base, n):
        y_ref = y_refs[n]
        for m0 in range(0, n, SCATTER_BATCH):
            updates = []
            for m in range(m0, m0 + SCATTER_BATCH):
                dst = tile_at(acc, rows_ref[base + m])
                updates.append((dst, dst[...] + y_ref[m * SUBLANES:(m + 1) * SUBLANES, :]))
            for dst, val in updates:
                dst[...] = val

    def run_block(n, nominal, begin, pending):
        redo = nominal - begin
        base = first + begin + n_rows
        for m in range(n):
            gat_ref[m * SUBLANES:(m + 1) * SUBLANES, :] = tile_at(x_ref, rows_ref[base + m])[...]
        xb = _load_row_major_tiles(gat_ref, n).astype(BF16)
        act = jax.nn.silu(_dot(xb, w1_ref[0])) * _dot(xb, w3_ref[0])
        y = _dot(act.astype(BF16), w2_ref[0])

        w_rows = pl.cdiv(n, LANES) + 1
        row_id = lax.broadcasted_iota(jnp.int32, (n, w_rows * LANES), 0)
        lane_id = lax.broadcasted_iota(jnp.int32, (n, w_rows * LANES), 1)
        w_row = base // LANES
        span = jnp.concatenate([wts_ref[w_row + r] for r in range(w_rows)], axis=1)
        picked = jnp.where((lane_id == row_id + base % LANES) & (row_id >= redo),
                           jnp.broadcast_to(span, lane_id.shape), 0.0)
        ys = y * jnp.sum(picked, axis=1, keepdims=True)
        scatter_add(pending, n)
        _store_row_major_tiles(y_refs[n], ys)
        return base

    full_blocks = count // n_rows
    rest_rows = count - full_blocks * n_rows
    main_blocks = full_blocks + (rest_rows > half).astype(jnp.int32)

    def main_block(b, pending):
        nominal = b * n_rows
        return run_block(n_rows, nominal, jnp.minimum(nominal, count - n_rows), pending)

    pending_ref[0] = lax.fori_loop(0, main_blocks, main_block, pending_ref[0])

    @pl.when((rest_rows > 0) & (rest_rows <= half))
    def _():
        pending_ref[1] = run_block(half, full_blocks * n_rows, count - half, pending_ref[1])

    @pl.when(e == pl.num_programs(0) - 1)
    def _():
        scatter_add(pending_ref[0], n_rows)
        scatter_add(pending_ref[1], half)


def _moe(ids, weights, x_tiles, w1, w3, w2):
    n_tok = ids.shape[1]
    g_tok = min(MOE_GROUP_TOKENS, n_tok)
    n_groups = n_tok // g_tok
    assert n_groups * g_tok == n_tok
    per_group = g_tok * TOP_K
    tok = lax.broadcasted_iota(jnp.int32, ids.shape, 1)
    key = ((tok // g_tok) * N_EXPERTS + ids) * g_tok + tok % g_tok
    key, wts = lax.sort((key.reshape(-1), weights.reshape(-1)), num_keys=1)
    rows8 = ((key % g_tok) * D_TILES).reshape(n_groups, per_group)
    wts = wts.reshape(n_groups, per_group)
    experts = jnp.arange(N_EXPERTS, dtype=jnp.int32)[None, None, :, None]
    counts = jnp.sum(ids.reshape(TOP_K, n_groups, 1, g_tok) == experts, axis=(0, 3), dtype=jnp.int32)
    start = jnp.concatenate([jnp.zeros((n_groups, 1), jnp.int32), jnp.cumsum(counts, axis=1)], axis=1)
    many_rows = per_group // N_EXPERTS >= 2 * MOE_BLOCK_ROWS
    n_rows = MOE_BLOCK_ROWS if many_rows else MOE_SMALL_BLOCK_ROWS
    pad = ((0, 0), (n_rows, n_rows + LANES))
    rows8 = jnp.pad(rows8, pad)
    wts = jnp.pad(wts, pad).reshape(n_groups, -1, 1, LANES)
    out = None
    for g in range(n_groups):
        out = _moe_group_call(g, n_groups, g_tok, n_rows, rows8[g], start[g], x_tiles, wts[g], w1, w3, w2, out)
    return out


def _moe_group_call(group, n_groups, g_tok, n_rows, rows8, start, x_tiles, wts, w1, w3, w2, prev):
    g_rows = g_tok * D_TILES
    assert x_tiles.shape[0] == n_groups * g_rows
    out_block = (1, g_rows, LANES)
    in_specs = [
        pl.BlockSpec((g_rows, LANES), lambda e, *_: (group, 0), pipeline_mode=pl.Buffered(1)),
        pl.BlockSpec(wts.shape, lambda e, *_: (0, 0, 0)),
        pl.BlockSpec((1, D_MODEL, EXPERT_DIM), lambda e, *_: (e, 0, 0)),
        pl.BlockSpec((1, D_MODEL, EXPERT_DIM), lambda e, *_: (e, 0, 0)),
        pl.BlockSpec((1, EXPERT_DIM, D_MODEL), lambda e, *_: (e, 0, 0)),
    ]
    args = [rows8, start, x_tiles, wts, w1, w3, w2]
    aliases = {}
    if prev is not None:
        in_specs.append(pl.BlockSpec(memory_space=pl.ANY))
        aliases = {len(args): 0}
        args.append(prev)
    return pl.pallas_call(
        functools.partial(_moe_kernel, n_rows=n_rows),
        grid_spec=pltpu.PrefetchScalarGridSpec(
            num_scalar_prefetch=2, grid=(N_EXPERTS,), in_specs=in_specs,
            out_specs=pl.BlockSpec(out_block, lambda e, *_: (group, 0, 0), pipeline_mode=pl.Buffered(1)),
            scratch_shapes=[pltpu.VMEM((n_rows * D_TILES, LANES), F32),
                            pltpu.VMEM((n_rows * D_TILES, LANES), F32),
                            pltpu.VMEM((n_rows // 2 * D_TILES, LANES), F32),
                            pltpu.SMEM((2,), jnp.int32)]),
        out_shape=jax.ShapeDtypeStruct((n_groups,) + out_block[1:], F32),
        input_output_aliases=aliases,
        compiler_params=_params(1), name="moe",
    )(*args)


def _final_kernel(x1_ref, routed_ref, p_ref, s1_32_ref, s3_32_ref, s2_32_ref, g2_ref, b2_ref, wg32_ref,
                  wp32_ref, o_ref, s1_ref, s3_ref, s2_ref, wg_ref, wp_ref, *, alpha):
    _cast_weights_once([(s1_32_ref, s1_ref), (s3_32_ref, s3_ref), (s2_32_ref, s2_ref),
                        (wg32_ref, wg_ref), (wp32_ref, wp_ref)])
    x1 = x1_ref[...]
    xb = x1.astype(BF16)
    act = jax.nn.silu(_dot(xb, s1_ref[...])) * _dot(xb, s3_ref[...])
    ff = _load_row_major_tiles(routed_ref.at[0], x1.shape[0]) + _dot(act.astype(BF16), s2_ref[...])
    x2 = _layer_norm(alpha * x1 + ff, g2_ref[...], b2_ref[...])
    gate = jax.nn.sigmoid(_dot(x2.astype(BF16), wg_ref[...]))
    o_ref[...] = x2 + gate * _dot(p_ref[...].astype(BF16), wp_ref[...])


def _final(x1, routed, p2d, s1, s3, s2, g2, b2, wg, wp, *, alpha):
    n_rows = x1.shape[0]
    tile = ROW_TILE
    row = lambda width: pl.BlockSpec((tile, width), lambda i: (i, 0))
    tiles_per_group = routed.shape[1] // D_TILES // tile
    routed_spec = pl.BlockSpec((1, tile * D_TILES, LANES),
                               lambda i: (i // tiles_per_group, i % tiles_per_group, 0))
    return pl.pallas_call(
        functools.partial(_final_kernel, alpha=alpha),
        grid=(n_rows // tile,),
        in_specs=[row(D_MODEL), routed_spec,
                  row(p2d.shape[1]),
                  _resident_spec(s1.shape), _resident_spec(s3.shape), _resident_spec(s2.shape),
                  _const_spec((1, D_MODEL)), _const_spec((1, D_MODEL)),
                  _resident_spec(wg.shape), _resident_spec(wp.shape)],
        out_specs=row(D_MODEL),
        out_shape=jax.ShapeDtypeStruct((n_rows, D_MODEL), F32),
        scratch_shapes=_bf16_scratch(s1, s3, s2, wg, wp), compiler_params=_params(1), name="final",
    )(x1, routed, p2d, s1, s3, s2, g2, b2, wg, wp)


def _trunk_layer(x, p, cache_k, cache_v, lw, *, alpha, first_layer_ln):
    batch, seq, d = x.shape
    n_rows = batch * seq
    x2d = x.reshape(n_rows, d)
    prompt = cache_k is None
    mlp_len = min(seq, MLP_CHUNK)
    keep = min(PAST_CHUNKS * CHUNK, seq) if prompt else seq
    ln_g, ln_b = first_layer_ln

    w_s = (lw["mlp_w_s"][:, :mlp_len, :mlp_len]
           * jnp.tril(jnp.ones((mlp_len, mlp_len), F32))).astype(BF16)
    b_s = jnp.repeat(lw["mlp_b_s"][:, :mlp_len].T, MLP_WIDTH // MLP_GROUPS, axis=1)
    q, k, v, kf, vf, vn, ga, gb, ob = _proj(
        x2d, ln_g, ln_b, lw["w_in"], lw["mlp_ln_g"][None], lw["mlp_ln_b"][None],
        w_s, b_s, seq=seq, mlp_len=mlp_len, keep_rows=keep)
    if prompt:
        oa = _attn_prompt(q, k, v, lw["attn_rel_bias"], batch=batch, seq=seq)
    else:
        oa = _attn_sample(q, k, v, cache_k, cache_v, lw["attn_rel_bias"], batch=batch, seq=seq)
    x1, x1_tiles, scores = _mix(
        x2d, oa, ob, ga, gb, ln_g, ln_b, lw["w_branch_a"], lw["w_branch_b"], lw["w_out"],
        lw["ln1_g"][None], lw["ln1_b"][None], _split_bf16(lw["router_w"].T), alpha=alpha)
    ids, weights = _route_call(scores, lw["router_bias"][:, None])
    routed = _moe(ids, weights, x1_tiles, lw["exp_w1"], lw["exp_w3"], lw["exp_w2"])
    y = _final(x1, routed, p.reshape(n_rows, -1), lw["shared_w1"], lw["shared_w3"], lw["shared_w2"],
               lw["ln2_g"][None], lw["ln2_b"][None], lw["ple_w_gate"], lw["ple_w_proj"], alpha=alpha)
    new_k = kf.reshape(batch, keep, HEADS, HEAD_DIM)
    new_v = vf.reshape(batch, keep, HEADS, HEAD_DIM)
    new_mlp_v = vn.reshape(batch, mlp_len, MLP_WIDTH)
    return y.reshape(batch, seq, d), new_k, new_v, new_mlp_v


def kernel(x_prompt, x_sample, cache_attn_k, cache_attn_v, p_prompt, p_sample, ln_in_g, ln_in_b, w_in, attn_rel_bias, mlp_ln_g, mlp_ln_b, mlp_w_s, mlp_b_s, w_branch_a, w_branch_b, w_out, ln1_g, ln1_b, router_w, router_bias, exp_w1, exp_w3, exp_w2, shared_w1, shared_w3, shared_w2, ln2_g, ln2_b, ple_w_gate, ple_w_proj):
    depth = w_in.shape[0]
    assert depth == 1, "the fused proj/mix kernels apply LayerNorm_in themselves: single layer only"
    alpha = (2 * depth) ** 0.25
    stacked = dict(w_in=w_in, attn_rel_bias=attn_rel_bias, mlp_ln_g=mlp_ln_g, mlp_ln_b=mlp_ln_b,
                   mlp_w_s=mlp_w_s, mlp_b_s=mlp_b_s, w_branch_a=w_branch_a, w_branch_b=w_branch_b,
                   w_out=w_out, ln1_g=ln1_g, ln1_b=ln1_b, router_w=router_w, router_bias=router_bias,
                   exp_w1=exp_w1, exp_w3=exp_w3, exp_w2=exp_w2, shared_w1=shared_w1,
                   shared_w3=shared_w3, shared_w2=shared_w2, ln2_g=ln2_g, ln2_b=ln2_b,
                   ple_w_gate=ple_w_gate, ple_w_proj=ple_w_proj)
    lw = {name: value[0] for name, value in stacked.items()}
    for name in ("exp_w1", "exp_w3", "exp_w2"):
        lw[name] = lw[name].astype(BF16)
    ln_in =(ln_in_g[None], ln_in_b[None])
    yp, kp, vp, mp = _trunk_layer(x_prompt, p_prompt[0], None, None, lw, alpha=alpha, first_layer_ln=ln_in)
    ys, ks, vs, ms = _trunk_layer(x_sample, p_sample[0], cache_attn_k[0], cache_attn_v[0], lw,
                                  alpha=alpha, first_layer_ln=ln_in)
    return (yp, ys, kp[None], vp[None], ks[None], vs[None], mp[None], ms[None])
```

```python
import functools

import jax
import jax.numpy as jnp
import numpy as np
from jax import lax
from jax.experimental import pallas as pl
from jax.experimental.pallas import tpu as pltpu

F32 = jnp.float32
BF16 = jnp.bfloat16

D_MODEL = 1024
CHUNK = 64
HEADS = 8
HEAD_DIM = 64
A_WIDTH = HEADS * HEAD_DIM
PAST_CHUNKS = 8
MAX_REL = 256
MLP_CHUNK = 128
MLP_GROUPS = 8
MLP_WIDTH = 512
N_EXPERTS = 64
TOP_K = 8
N_GROUPS = 8
TOPK_GROUPS = 4
GROUP_SIZE = N_EXPERTS // N_GROUPS
EXPERT_DIM = 256
ROUTED_SCALE = 2.5
LN_EPS = 1e-5
MASKED = -1e30

LANES = 128
SUBLANES = 8
VMEM_LIMIT_BYTES = 56 * 1024 * 1024

ROW_TILE = 512
ROUTE_TILE = 2048
Q_CHUNKS = 4
MOE_GROUP_TOKENS = 4096
MOE_BLOCK_ROWS = 256
MOE_SMALL_BLOCK_ROWS = 128
CAST_COLS = 512
SCATTER_BATCH = 8
D_TILES = D_MODEL // LANES


def _layer_norm(x, g, b):
    xc = x - jnp.mean(x, axis=-1, keepdims=True)
    var = jnp.mean(xc * xc, axis=-1, keepdims=True)
    return xc * lax.rsqrt(var + LN_EPS) * g + b


def _dot(a, b):
    return jnp.dot(a, b, preferred_element_type=F32)


def _dot_nt(a, b, precision=None):
    return lax.dot_general(a, b, (((1,), (1,)), ((), ())),
                           preferred_element_type=F32, precision=precision)


def _const_spec(shape):
    zeros = (0,) * len(shape)
    return pl.BlockSpec(shape, lambda *_: zeros)


def _resident_spec(shape):
    zeros = (0,) * len(shape)
    return pl.BlockSpec(shape, lambda *_: zeros, pipeline_mode=pl.Buffered(1))


def _cast_weights_once(pairs):
    @pl.when(pl.program_id(0) == 0)
    def _():
        for src, dst in pairs:
            cols = src.shape[1]
            for c in range(0, cols, CAST_COLS):
                dst[:, c:min(c + CAST_COLS, cols)] = src[:, c:min(c + CAST_COLS, cols)].astype(BF16)


def _bf16_scratch(*weights):
    return [pltpu.VMEM(w.shape, BF16) for w in weights]


def _params(n_axes):
    return pltpu.CompilerParams(dimension_semantics=("arbitrary",) * n_axes,
                                vmem_limit_bytes=VMEM_LIMIT_BYTES)


def _proj_kernel(x_ref, lng_ref, lnb_ref, w32_ref, mg_ref, mb_ref, ws_ref, bs_ref,
                 q_ref, k_ref, v_ref, kf_ref, vf_ref, vn_ref, ga_ref, gb_ref, ob_ref, w_ref,
                 *, mlp_len, vn_rows):
    _cast_weights_once([(w32_ref, w_ref)])
    rows = x_ref.shape[0]
    xn = _layer_norm(x_ref[...], lng_ref[...], lnb_ref[...]).astype(BF16)

    def section(lo, width):
        return _dot(xn, w_ref[:, lo:lo + width])

    q = section(0, A_WIDTH)
    q_ref[...] = (q * (HEAD_DIM ** -0.5)).astype(BF16)
    k = section(A_WIDTH, A_WIDTH)
    k_ref[...] = k.astype(BF16)
    kf_ref[...] = k
    v = section(2 * A_WIDTH, A_WIDTH)
    v_ref[...] = v.astype(BF16)
    vf_ref[...] = v
    base = 3 * A_WIDTH
    un = jax.nn.gelu(section(base, MLP_WIDTH))
    vn = _layer_norm(jax.nn.gelu(section(base + MLP_WIDTH, MLP_WIDTH)), mg_ref[...], mb_ref[...])
    vn_ref[...] = vn[rows - vn_rows:, :]
    base += 2 * MLP_WIDTH
    ga_ref[...] = jax.nn.sigmoid(section(base, D_MODEL)).astype(BF16)
    gb_ref[...] = jax.nn.sigmoid(section(base + D_MODEL, D_MODEL)).astype(BF16)

    vnb = vn.astype(BF16)
    low_half = lax.broadcasted_iota(jnp.int32, (mlp_len, LANES), 1) < (MLP_WIDTH // MLP_GROUPS)
    for c in range(rows // mlp_len):
        r0 = c * mlp_len
        for p in range(MLP_GROUPS // 2):
            c0 = p * LANES
            slab = vnb[r0:r0 + mlp_len, c0:c0 + LANES]
            mixed = jnp.where(low_half, _dot(ws_ref[2 * p], slab), _dot(ws_ref[2 * p + 1], slab))
            gated = un[r0:r0 + mlp_len, c0:c0 + LANES] * (mixed + bs_ref[:, c0:c0 + LANES])
            ob_ref[r0:r0 + mlp_len, c0:c0 + LANES] = gated.astype(BF16)


def _proj(x2d, ln_g, ln_b, w_in, mlp_g, mlp_b, w_s, b_s, *, seq, mlp_len, keep_rows):
    n_rows = x2d.shape[0]
    tile = ROW_TILE
    assert n_rows % tile == 0 and tile % mlp_len == 0
    n_seq = n_rows // seq
    if seq >= tile:
        assert seq % tile == 0 and keep_rows == tile
        per_seq = seq // tile
        kv_rows, vn_rows = n_seq * tile, mlp_len
        kv_map = lambda i: (i // per_seq, 0)
    else:
        assert keep_rows == seq and mlp_len == seq
        kv_rows, vn_rows = n_rows, tile
        kv_map = lambda i: (i, 0)
    row = lambda width: pl.BlockSpec((tile, width), lambda i: (i, 0))
    n_in = w_in.shape[1]
    out_shape = (
        jax.ShapeDtypeStruct((n_rows, A_WIDTH), BF16),
        jax.ShapeDtypeStruct((n_rows, A_WIDTH), BF16),
        jax.ShapeDtypeStruct((n_rows, A_WIDTH), BF16),
        jax.ShapeDtypeStruct((kv_rows, A_WIDTH), F32),
        jax.ShapeDtypeStruct((kv_rows, A_WIDTH), F32),
        jax.ShapeDtypeStruct((kv_rows // tile * vn_rows, MLP_WIDTH), F32),
        jax.ShapeDtypeStruct((n_rows, D_MODEL), BF16),
        jax.ShapeDtypeStruct((n_rows, D_MODEL), BF16),
        jax.ShapeDtypeStruct((n_rows, MLP_WIDTH), BF16),
    )
    out_specs = (
        row(A_WIDTH), row(A_WIDTH), row(A_WIDTH),
        pl.BlockSpec((tile, A_WIDTH), kv_map), pl.BlockSpec((tile, A_WIDTH), kv_map),
        pl.BlockSpec((vn_rows, MLP_WIDTH), kv_map),
        row(D_MODEL), row(D_MODEL), row(MLP_WIDTH),
    )
    in_specs = [
        row(D_MODEL), _const_spec((1, D_MODEL)), _const_spec((1, D_MODEL)),
        _resident_spec((D_MODEL, n_in)), _const_spec((1, MLP_WIDTH)), _const_spec((1, MLP_WIDTH)),
        _const_spec((MLP_GROUPS, mlp_len, mlp_len)), _const_spec((mlp_len, MLP_WIDTH)),
    ]
    return pl.pallas_call(
        functools.partial(_proj_kernel, mlp_len=mlp_len, vn_rows=vn_rows),
        grid=(n_rows // tile,), in_specs=in_specs, out_specs=out_specs, out_shape=out_shape,
        scratch_shapes=_bf16_scratch(w_in), compiler_params=_params(1), name="proj",
    )(x2d, ln_g, ln_b, w_in, mlp_g, mlp_b, w_s, b_s)


def _attend_heads(q, keys, values, pair_bias, o_ref):
    n_q = q.shape[0]
    low_half = lax.broadcasted_iota(jnp.int32, (n_q, LANES), 1) < HEAD_DIM
    zero = jnp.zeros((), BF16)
    for p in range(HEADS // 2):
        c0 = p * LANES
        q2, k2, v2 = q[:, c0:c0 + LANES], keys[:, c0:c0 + LANES], values[:, c0:c0 + LANES]
        qs = jnp.concatenate([jnp.where(low_half, q2, zero), jnp.where(low_half, zero, q2)], axis=0)
        s = _dot_nt(qs, k2) + pair_bias(p).reshape(2 * n_q, keys.shape[0])
        e = jnp.exp(s - jnp.max(s, axis=-1, keepdims=True))
        denom = jnp.sum(e, axis=-1, keepdims=True)
        o = _dot(e.astype(BF16), v2) / denom
        o_ref[:, c0:c0 + LANES] = jnp.where(low_half, o[:n_q], o[n_q:]).astype(o_ref.dtype)


def _fill_toeplitz_bias(bias_ref, diag_ref, first_key_chunk=None):
    _, n_q, n_k = bias_ref.shape
    if first_key_chunk is not None:
        q_chunk = lax.broadcasted_iota(jnp.int32, (n_q, n_k), 0) // CHUNK
        k_chunk = lax.broadcasted_iota(jnp.int32, (n_q, n_k), 1) // CHUNK + first_key_chunk
        in_band = (k_chunk <= q_chunk) & (k_chunk >= q_chunk - PAST_CHUNKS)
    for h in range(bias_ref.shape[0]):
        rows = jnp.broadcast_to(diag_ref[h:h + 1, :], (n_q, diag_ref.shape[1]))
        rows = pltpu.roll(rows, 0, 1, stride=1, stride_axis=0)[:, :n_k]
        bias_ref[h] = rows if first_key_chunk is None else jnp.where(in_band, rows, MASKED)


def _attn_prompt_kernel(q_ref, k_ref, v_ref, diag_ref, o_ref, bias_ref, *, window, lead_steps):
    n_q = q_ref.shape[0]

    @pl.when((pl.program_id(0) == 0) & (pl.program_id(1) == 0))
    def _():
        _fill_toeplitz_bias(bias_ref, diag_ref, first_key_chunk=-lead_steps * (n_q // CHUNK))

    step = jnp.minimum(pl.program_id(1), lead_steps)
    start = pl.multiple_of((pl.program_id(1) - step) * n_q, n_q)
    shift = pl.multiple_of((lead_steps - step) * n_q, n_q)
    keys = k_ref[pl.ds(start, window), :]
    values = v_ref[pl.ds(start, window), :]
    _attend_heads(q_ref[...], keys, values,
                  lambda p: bias_ref[2 * p:2 * p + 2, :, pl.ds(shift, window)], o_ref)


def _bias_diagonals(rel_table, n_neg, n_pos, key_offset):
    period = pl.next_power_of_2(n_neg + n_pos)
    m = np.arange(period)
    diag = np.where(m < n_pos, m, m - period)
    return rel_table[:, np.clip(diag - key_offset, -MAX_REL, MAX_REL) + MAX_REL].astype(F32), period


def _attn_prompt(q, k, v, rel_table, *, batch, seq):
    n_q = Q_CHUNKS * CHUNK
    window = (Q_CHUNKS + PAST_CHUNKS) * CHUNK
    assert PAST_CHUNKS % Q_CHUNKS == 0 and seq >= window
    lead = PAST_CHUNKS // Q_CHUNKS
    steps = seq // n_q
    n_bias = lead * n_q + window
    diag, _ = _bias_diagonals(rel_table, n_q, n_bias, PAST_CHUNKS * CHUNK)
    qspec = pl.BlockSpec((n_q, A_WIDTH), lambda b, i: (b * steps + i, 0))
    kvspec = pl.BlockSpec((seq, A_WIDTH), lambda b, i: (b, 0))
    return pl.pallas_call(
        functools.partial(_attn_prompt_kernel, window=window, lead_steps=lead),
        grid=(batch, steps),
        in_specs=[qspec, kvspec, kvspec, _const_spec(diag.shape)],
        out_specs=qspec,
        out_shape=jax.ShapeDtypeStruct((batch * seq, A_WIDTH), BF16),
        scratch_shapes=[pltpu.VMEM((HEADS, n_q, n_bias), F32)],
        compiler_params=_params(2), name="attn_prompt",
    )(q, k, v, diag)


def _attn_sample_kernel(q_ref, k_ref, v_ref, ck_ref, cv_ref, diag_ref, o_ref, kk_ref, vv_ref, bias_ref):
    @pl.when(pl.program_id(0) == 0)
    def _():
        _fill_toeplitz_bias(bias_ref, diag_ref)

    n_cache = ck_ref.shape[1]
    n_new = k_ref.shape[1]
    kk_ref[0:n_cache, :] = ck_ref[0].astype(BF16)
    kk_ref[n_cache:n_cache + n_new, :] = k_ref[0]
    vv_ref[0:n_cache, :] = cv_ref[0].astype(BF16)
    vv_ref[n_cache:n_cache + n_new, :] = v_ref[0]
    _attend_heads(q_ref[0], kk_ref[...], vv_ref[...], lambda p: bias_ref[2 * p:2 * p + 2], o_ref.at[0])


def _attn_sample(q, k, v, cache_k, cache_v, rel_table, *, batch, seq):
    n_cache = cache_k.shape[1]
    diag, _ = _bias_diagonals(rel_table, seq, n_cache + seq, n_cache)
    new = pl.BlockSpec((1, seq, A_WIDTH), lambda b: (b, 0, 0))
    old = pl.BlockSpec((1, n_cache, A_WIDTH), lambda b: (b, 0, 0))
    out = pl.pallas_call(
        _attn_sample_kernel,
        grid=(batch,),
        in_specs=[new, new, new, old, old, _const_spec(diag.shape)],
        out_specs=new,
        out_shape=jax.ShapeDtypeStruct((batch, seq, A_WIDTH), BF16),
        scratch_shapes=[pltpu.VMEM((n_cache + seq, A_WIDTH), BF16),
                        pltpu.VMEM((n_cache + seq, A_WIDTH), BF16),
                        pltpu.VMEM((HEADS, seq, n_cache + seq), F32)],
        compiler_params=_params(1), name="attn_sample",
    )(q.reshape(batch, seq, A_WIDTH), k.reshape(batch, seq, A_WIDTH), v.reshape(batch, seq, A_WIDTH),
      cache_k.reshape(batch, n_cache, A_WIDTH), cache_v.reshape(batch, n_cache, A_WIDTH), diag)
    return out.reshape(batch * seq, A_WIDTH)


def _first_index(hit, iota, axis, limit):
    return jnp.min(jnp.where(hit, iota, limit), axis=axis, keepdims=True)


def _route(scores, sel):
    n_tok = scores.shape[1]
    neg = -jnp.inf
    grouped = sel.reshape(N_GROUPS, GROUP_SIZE, n_tok)
    member = lax.broadcasted_iota(jnp.int32, grouped.shape, 1)
    best = jnp.max(grouped, axis=1, keepdims=True)
    first = _first_index(grouped == best, member, 1, GROUP_SIZE)
    second = jnp.max(jnp.where(member == first, neg, grouped), axis=1, keepdims=True)
    group_score = best + second

    group_id = lax.broadcasted_iota(jnp.int32, group_score.shape, 0)
    keep = jnp.zeros(group_score.shape, F32)
    for _ in range(TOPK_GROUPS):
        top = jnp.max(group_score, axis=0, keepdims=True)
        hit = group_id == _first_index(group_score == top, group_id, 0, N_GROUPS)
        keep = jnp.where(hit, 1.0, keep)
        group_score = jnp.where(hit, neg, group_score)
    keep = jnp.broadcast_to(keep, grouped.shape).reshape(N_EXPERTS, n_tok)

    cand = jnp.where(keep > 0.0, sel, neg)
    expert_id = lax.broadcasted_iota(jnp.int32, cand.shape, 0)
    ids, weights = [], []
    for _ in range(TOP_K):
        top = jnp.max(cand, axis=0, keepdims=True)
        first = _first_index(cand == top, expert_id, 0, N_EXPERTS)
        hit = expert_id == first
        ids.append(first)
        weights.append(jnp.sum(jnp.where(hit, scores, 0.0), axis=0, keepdims=True))
        cand = jnp.where(hit, neg, cand)
    ids = jnp.concatenate(ids, axis=0)
    weights = jnp.concatenate(weights, axis=0)
    weights = weights / jnp.sum(weights, axis=0, keepdims=True) * ROUTED_SCALE
    return ids, weights


def _store_row_major_tiles(flat_ref, x):
    rows = x.shape[0]
    for j in range(D_TILES):
        flat_ref[pl.ds(j, rows, stride=D_TILES), :] = x[:, j * LANES:(j + 1) * LANES]


def _load_row_major_tiles(flat_ref, rows):
    return jnp.concatenate(
        [flat_ref[pl.ds(j, rows, stride=D_TILES), :] for j in range(D_TILES)], axis=1)


def _split_bf16(w):
    hi = w.astype(BF16)
    return jnp.concatenate([hi, (w - hi.astype(F32)).astype(BF16)], axis=0)


def _mix_kernel(x_ref, oa_ref, ob_ref, ga_ref, gb_ref, lng_ref, lnb_ref, wa32_ref, wb32_ref, wo32_ref,
                g1_ref, b1_ref, rw_ref, x1_ref, x1t_ref, scores_ref, wa_ref, wb_ref, wo_ref, *, alpha):
    _cast_weights_once([(wa32_ref, wa_ref), (wb32_ref, wb_ref), (wo32_ref, wo_ref)])
    xn = _layer_norm(x_ref[...], lng_ref[...], lnb_ref[...])
    mix = (ga_ref[...].astype(F32) * _dot(oa_ref[...], wa_ref[...])
           + gb_ref[...].astype(F32) * _dot(ob_ref[...], wb_ref[...]))
    x1 = _layer_norm(alpha * xn + _dot(mix.astype(BF16), wo_ref[...]), g1_ref[...], b1_ref[...])
    x1_ref[...] = x1
    _store_row_major_tiles(x1t_ref, x1)
    x1_hi = x1.astype(BF16)
    x1_lo = (x1 - x1_hi.astype(F32)).astype(BF16)
    by_hi = _dot_nt(rw_ref[...], x1_hi)
    logits = by_hi[:N_EXPERTS] + by_hi[N_EXPERTS:] + _dot_nt(rw_ref[:N_EXPERTS, :], x1_lo)
    scores_ref[...] = jax.nn.sigmoid(logits)


def _mix(x2d, oa, ob, ga, gb, ln_g, ln_b, wa, wb, wo, g1, b1, rw_t, *, alpha):
    n_rows = x2d.shape[0]
    tile = ROW_TILE
    row = lambda width: pl.BlockSpec((tile, width), lambda i: (i, 0))
    return pl.pallas_call(
        functools.partial(_mix_kernel, alpha=alpha),
        grid=(n_rows // tile,),
        in_specs=[row(D_MODEL), row(A_WIDTH), row(MLP_WIDTH), row(D_MODEL), row(D_MODEL),
                  _const_spec((1, D_MODEL)), _const_spec((1, D_MODEL)),
                  _resident_spec(wa.shape), _resident_spec(wb.shape), _resident_spec(wo.shape),
                  _const_spec((1, D_MODEL)), _const_spec((1, D_MODEL)),
                  _const_spec(rw_t.shape)],
        out_specs=(row(D_MODEL), pl.BlockSpec((tile * D_TILES, LANES), lambda i: (i, 0)),
                   pl.BlockSpec((N_EXPERTS, tile), lambda i: (0, i))),
        out_shape=(jax.ShapeDtypeStruct((n_rows, D_MODEL), F32),
                   jax.ShapeDtypeStruct((n_rows * D_TILES, LANES), F32),
                   jax.ShapeDtypeStruct((N_EXPERTS, n_rows), F32)),
        scratch_shapes=_bf16_scratch(wa, wb, wo), compiler_params=_params(1), name="mix",
    )(x2d, oa, ob, ga, gb, ln_g, ln_b, wa, wb, wo, g1, b1, rw_t)


def _route_kernel(scores_ref, rb_ref, ids_ref, wts_ref):
    scores = scores_ref[...]
    ids, weights = _route(scores, scores + rb_ref[...])
    ids_ref[...] = ids
    wts_ref[...] = weights


def _route_call(scores, rb):
    n_tok = scores.shape[1]
    tile = min(ROUTE_TILE, n_tok)
    col = lambda rows: pl.BlockSpec((rows, tile), lambda i: (0, i))
    return pl.pallas_call(
        _route_kernel,
        grid=(n_tok // tile,),
        in_specs=[col(N_EXPERTS), _const_spec(rb.shape)],
        out_specs=(col(TOP_K), col(TOP_K)),
        out_shape=(jax.ShapeDtypeStruct((TOP_K, n_tok), jnp.int32),
                   jax.ShapeDtypeStruct((TOP_K, n_tok), F32)),
        compiler_params=_params(1), name="route",
    )(scores, rb)


def _moe_kernel(rows_ref, start_ref, x_ref, wts_ref, w1_ref, w3_ref, w2_ref, *rest, n_rows, emit_bf16):
    gat_ref, y_full_ref, y_half_ref, pending_ref = rest[-4:]
    if emit_bf16:
        out_ref, w1b_ref, w3b_ref, w2b_ref = rest[-8:-4]
        w1b_ref[...] = w1_ref[...].astype(BF16)
        w3b_ref[...] = w3_ref[...].astype(BF16)
        w2b_ref[...] = w2_ref[...].astype(BF16)
        w1_ref, w3_ref, w2_ref = w1b_ref, w3b_ref, w2b_ref
    else:
        out_ref = rest[-5]
    acc = out_ref.at[0]
    e = pl.program_id(0)
    half = n_rows // 2
    y_refs = {n_rows: y_full_ref, half: y_half_ref}

    @pl.when(e == 0)
    def _():
        out_ref[...] = jnp.zeros(out_ref.shape, out_ref.dtype)
        y_full_ref[...] = jnp.zeros(y_full_ref.shape, y_full_ref.dtype)
        y_half_ref[...] = jnp.zeros(y_half_ref.shape, y_half_ref.dtype)
        pending_ref[0] = 0
        pending_ref[1] = 0

    first = start_ref[e]
    count = start_ref[e + 1] - first

    def tile_at(ref, row8):
        return ref.at[pl.ds(pl.multiple_of(row8, SUBLANES), SUBLANES), :]

    def scatter_add(base, n):
        y_ref = y_refs[n]
        for m0 in range(0, n, SCATTER_BATCH):
            updates = []
            for m in range(m0, m0 + SCATTER_BATCH):
                dst = tile_at(acc, rows_ref[base + m])
                updates.append((dst, dst[...] + y_ref[m * SUBLANES:(m + 1) * SUBLANES, :]))
            for dst, val in updates:
                dst[...] = val

    def run_block(n, nominal, begin, pending):
        redo = nominal - begin
        base = first + begin + n_rows
        for m in range(n):
            gat_ref[m * SUBLANES:(m + 1) * SUBLANES, :] = tile_at(x_ref, rows_ref[base + m])[...]
        xb = _load_row_major_tiles(gat_ref, n).astype(BF16)
        act = jax.nn.silu(_dot(xb, w1_ref[0])) * _dot(xb, w3_ref[0])
        y = _dot(act.astype(BF16), w2_ref[0])

        w_rows = pl.cdiv(n, LANES) + 1
        row_id = lax.broadcasted_iota(jnp.int32, (n, w_rows * LANES), 0)
        lane_id = lax.broadcasted_iota(jnp.int32, (n, w_rows * LANES), 1)
        w_row = base // LANES
        span = jnp.concatenate([wts_ref[w_row + r] for r in range(w_rows)], axis=1)
        picked = jnp.where((lane_id == row_id + base % LANES) & (row_id >= redo),
                           jnp.broadcast_to(span, lane_id.shape), 0.0)
        ys = y * jnp.sum(picked, axis=1, keepdims=True)
        scatter_add(pending, n)
        _store_row_major_tiles(y_refs[n], ys)
        return base

    full_blocks = count // n_rows
    rest_rows = count - full_blocks * n_rows
    main_blocks = full_blocks + (rest_rows > half).astype(jnp.int32)

    def main_block(b, pending):
        nominal = b * n_rows
        return run_block(n_rows, nominal, jnp.minimum(nominal, count - n_rows), pending)

    pending_ref[0] = lax.fori_loop(0, main_blocks, main_block, pending_ref[0])

    @pl.when((rest_rows > 0) & (rest_rows <= half))
    def _():
        pending_ref[1] = run_block(half, full_blocks * n_rows, count - half, pending_ref[1])

    @pl.when(e == pl.num_programs(0) - 1)
    def _():
        scatter_add(pending_ref[0], n_rows)
        scatter_add(pending_ref[1], half)


def _moe(ids, weights, x_tiles, w1, w3, w2):
    n_tok = ids.shape[1]
    g_tok = min(MOE_GROUP_TOKENS, n_tok)
    n_groups = n_tok // g_tok
    assert n_groups * g_tok == n_tok
    per_group = g_tok * TOP_K
    tok = lax.broadcasted_iota(jnp.int32, ids.shape, 1)
    key = ((tok // g_tok) * N_EXPERTS + ids) * g_tok + tok % g_tok
    key, wts = lax.sort((key.reshape(-1), weights.reshape(-1)), num_keys=1)
    rows8 = ((key % g_tok) * D_TILES).reshape(n_groups, per_group)
    wts = wts.reshape(n_groups, per_group)
    experts = jnp.arange(N_EXPERTS, dtype=jnp.int32)[None, None, :, None]
    counts = jnp.sum(ids.reshape(TOP_K, n_groups, 1, g_tok) == experts, axis=(0, 3), dtype=jnp.int32)
    start = jnp.concatenate([jnp.zeros((n_groups, 1), jnp.int32), jnp.cumsum(counts, axis=1)], axis=1)
    many_rows = per_group // N_EXPERTS >= 2 * MOE_BLOCK_ROWS
    n_rows = MOE_BLOCK_ROWS if many_rows else MOE_SMALL_BLOCK_ROWS
    pad = ((0, 0), (n_rows, n_rows + LANES))
    rows8 = jnp.pad(rows8, pad)
    wts = jnp.pad(wts, pad).reshape(n_groups, -1, 1, LANES)
    out = None
    for g in range(n_groups):
        out, (w1, w3, w2) = _moe_group_call(g, n_groups, g_tok, n_rows, rows8[g], start[g], x_tiles, wts[g],
                                            w1, w3, w2, out)
    return out, (w1, w3, w2)


def _moe_group_call(group, n_groups, g_tok, n_rows, rows8, start, x_tiles, wts, w1, w3, w2, prev):
    g_rows = g_tok * D_TILES
    assert x_tiles.shape[0] == n_groups * g_rows
    emit_bf16 = w1.dtype != BF16
    assert not (emit_bf16 and prev is not None)
    out_block = (1, g_rows, LANES)
    per_expert = lambda w: pl.BlockSpec((1,) + w.shape[1:], lambda e, *_: (e, 0, 0))
    in_specs = [
        pl.BlockSpec((g_rows, LANES), lambda e, *_: (group, 0), pipeline_mode=pl.Buffered(1)),
        pl.BlockSpec(wts.shape, lambda e, *_: (0, 0, 0)),
        per_expert(w1), per_expert(w3), per_expert(w2),
    ]
    args = [rows8, start, x_tiles, wts, w1, w3, w2]
    aliases = {}
    if prev is not None:
        in_specs.append(pl.BlockSpec(memory_space=pl.ANY))
        aliases = {len(args): 0}
        args.append(prev)
    out_specs = [pl.BlockSpec(out_block, lambda e, *_: (group, 0, 0), pipeline_mode=pl.Buffered(1))]
    out_shape = [jax.ShapeDtypeStruct((n_groups,) + out_block[1:], F32)]
    if emit_bf16:
        out_specs += [per_expert(w1), per_expert(w3), per_expert(w2)]
        out_shape += [jax.ShapeDtypeStruct(w.shape, BF16) for w in (w1, w3, w2)]
    results = pl.pallas_call(
        functools.partial(_moe_kernel, n_rows=n_rows, emit_bf16=emit_bf16),
        grid_spec=pltpu.PrefetchScalarGridSpec(
            num_scalar_prefetch=2, grid=(N_EXPERTS,), in_specs=in_specs, out_specs=out_specs,
            scratch_shapes=[pltpu.VMEM((n_rows * D_TILES, LANES), F32),
                            pltpu.VMEM((n_rows * D_TILES, LANES), F32),
                            pltpu.VMEM((n_rows // 2 * D_TILES, LANES), F32),
                            pltpu.SMEM((2,), jnp.int32)]),
        out_shape=out_shape,
        input_output_aliases=aliases,
        compiler_params=_params(1), name="moe",
    )(*args)
    return results[0], (tuple(results[1:]) if emit_bf16 else (w1, w3, w2))


def _final_kernel(x1_ref, routed_ref, p_ref, s1_32_ref, s3_32_ref, s2_32_ref, g2_ref, b2_ref, wg32_ref,
                  wp32_ref, o_ref, s1_ref, s3_ref, s2_ref, wg_ref, wp_ref, *, alpha):
    _cast_weights_once([(s1_32_ref, s1_ref), (s3_32_ref, s3_ref), (s2_32_ref, s2_ref),
                        (wg32_ref, wg_ref), (wp32_ref, wp_ref)])
    x1 = x1_ref[...]
    xb = x1.astype(BF16)
    act = jax.nn.silu(_dot(xb, s1_ref[...])) * _dot(xb, s3_ref[...])
    ff = _load_row_major_tiles(routed_ref.at[0], x1.shape[0]) + _dot(act.astype(BF16), s2_ref[...])
    x2 = _layer_norm(alpha * x1 + ff, g2_ref[...], b2_ref[...])
    gate = jax.nn.sigmoid(_dot(x2.astype(BF16), wg_ref[...]))
    o_ref[...] = x2 + gate * _dot(p_ref[...].astype(BF16), wp_ref[...])


def _final(x1, routed, p2d, s1, s3, s2, g2, b2, wg, wp, *, alpha):
    n_rows = x1.shape[0]
    tile = ROW_TILE
    row = lambda width: pl.BlockSpec((tile, width), lambda i: (i, 0))
    tiles_per_group = routed.shape[1] // D_TILES // tile
    routed_spec = pl.BlockSpec((1, tile * D_TILES, LANES),
                               lambda i: (i // tiles_per_group, i % tiles_per_group, 0))
    return pl.pallas_call(
        functools.partial(_final_kernel, alpha=alpha),
        grid=(n_rows // tile,),
        in_specs=[row(D_MODEL), routed_spec,
                  row(p2d.shape[1]),
                  _resident_spec(s1.shape), _resident_spec(s3.shape), _resident_spec(s2.shape),
                  _const_spec((1, D_MODEL)), _const_spec((1, D_MODEL)),
                  _resident_spec(wg.shape), _resident_spec(wp.shape)],
        out_specs=row(D_MODEL),
        out_shape=jax.ShapeDtypeStruct((n_rows, D_MODEL), F32),
        scratch_shapes=_bf16_scratch(s1, s3, s2, wg, wp), compiler_params=_params(1), name="final",
    )(x1, routed, p2d, s1, s3, s2, g2, b2, wg, wp)


def _trunk_layer(x, p, cache_k, cache_v, lw, *, alpha, first_layer_ln):
    batch, seq, d = x.shape
    n_rows = batch * seq
    x2d = x.reshape(n_rows, d)
    prompt = cache_k is None
    mlp_len = min(seq, MLP_CHUNK)
    keep = min(PAST_CHUNKS * CHUNK, seq) if prompt else seq
    ln_g, ln_b = first_layer_ln

    w_s = (lw["mlp_w_s"][:, :mlp_len, :mlp_len]
           * jnp.tril(jnp.ones((mlp_len, mlp_len), F32))).astype(BF16)
    b_s = jnp.repeat(lw["mlp_b_s"][:, :mlp_len].T, MLP_WIDTH // MLP_GROUPS, axis=1)
    q, k, v, kf, vf, vn, ga, gb, ob = _proj(
        x2d, ln_g, ln_b, lw["w_in"], lw["mlp_ln_g"][None], lw["mlp_ln_b"][None],
        w_s, b_s, seq=seq, mlp_len=mlp_len, keep_rows=keep)
    if prompt:
        oa = _attn_prompt(q, k, v, lw["attn_rel_bias"], batch=batch, seq=seq)
    else:
        oa = _attn_sample(q, k, v, cache_k, cache_v, lw["attn_rel_bias"], batch=batch, seq=seq)
    x1, x1_tiles, scores = _mix(
        x2d, oa, ob, ga, gb, ln_g, ln_b, lw["w_branch_a"], lw["w_branch_b"], lw["w_out"],
        lw["ln1_g"][None], lw["ln1_b"][None], _split_bf16(lw["router_w"].T), alpha=alpha)
    ids, weights = _route_call(scores, lw["router_bias"][:, None])
    routed, expert_bf16 = _moe(ids, weights, x1_tiles, lw["exp_w1"], lw["exp_w3"], lw["exp_w2"])
    y = _final(x1, routed, p.reshape(n_rows, -1), lw["shared_w1"], lw["shared_w3"], lw["shared_w2"],
               lw["ln2_g"][None], lw["ln2_b"][None], lw["ple_w_gate"], lw["ple_w_proj"], alpha=alpha)
    new_k = kf.reshape(batch, keep, HEADS, HEAD_DIM)
    new_v = vf.reshape(batch, keep, HEADS, HEAD_DIM)
    new_mlp_v = vn.reshape(batch, mlp_len, MLP_WIDTH)
    return y.reshape(batch, seq, d), new_k, new_v, new_mlp_v, expert_bf16


def kernel(x_prompt, x_sample, cache_attn_k, cache_attn_v, p_prompt, p_sample, ln_in_g, ln_in_b, w_in, attn_rel_bias, mlp_ln_g, mlp_ln_b, mlp_w_s, mlp_b_s, w_branch_a, w_branch_b, w_out, ln1_g, ln1_b, router_w, router_bias, exp_w1, exp_w3, exp_w2, shared_w1, shared_w3, shared_w2, ln2_g, ln2_b, ple_w_gate, ple_w_proj):
    depth = w_in.shape[0]
    assert depth == 1, "the fused proj/mix kernels apply LayerNorm_in themselves: single layer only"
    alpha = (2 * depth) ** 0.25
    stacked = dict(w_in=w_in, attn_rel_bias=attn_rel_bias, mlp_ln_g=mlp_ln_g, mlp_ln_b=mlp_ln_b,
                   mlp_w_s=mlp_w_s, mlp_b_s=mlp_b_s, w_branch_a=w_branch_a, w_branch_b=w_branch_b,
                   w_out=w_out, ln1_g=ln1_g, ln1_b=ln1_b, router_w=router_w, router_bias=router_bias,
                   exp_w1=exp_w1, exp_w3=exp_w3, exp_w2=exp_w2, shared_w1=shared_w1,
                   shared_w3=shared_w3, shared_w2=shared_w2, ln2_g=ln2_g, ln2_b=ln2_b,
                   ple_w_gate=ple_w_gate, ple_w_proj=ple_w_proj)
    lw = {name: value[0] for name, value in stacked.items()}
    ln_in =(ln_in_g[None], ln_in_b[None])
    yp, kp, vp, mp, expert_bf16 = _trunk_layer(x_prompt, p_prompt[0], None, None, lw, alpha=alpha,
                                               first_layer_ln=ln_in)
    lw = dict(lw, **dict(zip(("exp_w1", "exp_w3", "exp_w2"), expert_bf16)))
    ys, ks, vs, ms, _ = _trunk_layer(x_sample, p_sample[0], cache_attn_k[0], cache_attn_v[0], lw,
                                     alpha=alpha, first_layer_ln=ln_in)
    return (yp, ys, kp[None], vp[None], ks[None], vs[None], mp[None], ms[None])
```

```python
import functools

import jax
import jax.numpy as jnp
import numpy as np
from jax import lax
from jax.experimental import pallas as pl
from jax.experimental.pallas import tpu as pltpu

F32 = jnp.float32
BF16 = jnp.bfloat16

D_MODEL = 1024
CHUNK = 64
HEADS = 8
HEAD_DIM = 64
A_WIDTH = HEADS * HEAD_DIM
PAST_CHUNKS = 8
MAX_REL = 256
MLP_CHUNK = 128
MLP_GROUPS = 8
MLP_WIDTH = 512
N_EXPERTS = 64
TOP_K = 8
N_GROUPS = 8
TOPK_GROUPS = 4
GROUP_SIZE = N_EXPERTS // N_GROUPS
EXPERT_DIM = 256
ROUTED_SCALE = 2.5
LN_EPS = 1e-5
MASKED = -1e30

LANES = 128
SUBLANES = 8
VMEM_LIMIT_BYTES = 56 * 1024 * 1024

ROW_TILE = 512
ROUTE_TILE = 2048
Q_CHUNKS = 4
MOE_GROUP_TOKENS = 4096
MOE_BLOCK_ROWS = 256
MOE_SMALL_BLOCK_ROWS = 128
MOE_EXPERTS_PER_STEP = 2
CAST_COLS = 512
SCATTER_BATCH = 8
D_TILES = D_MODEL // LANES


def _layer_norm(x, g, b):
    xc = x - jnp.mean(x, axis=-1, keepdims=True)
    var = jnp.mean(xc * xc, axis=-1, keepdims=True)
    return xc * lax.rsqrt(var + LN_EPS) * g + b


def _dot(a, b):
    return jnp.dot(a, b, preferred_element_type=F32)


def _dot_nt(a, b, precision=None):
    return lax.dot_general(a, b, (((1,), (1,)), ((), ())),
                           preferred_element_type=F32, precision=precision)


def _const_spec(shape):
    zeros = (0,) * len(shape)
    return pl.BlockSpec(shape, lambda *_: zeros)


def _resident_spec(shape):
    zeros = (0,) * len(shape)
    return pl.BlockSpec(shape, lambda *_: zeros, pipeline_mode=pl.Buffered(1))


def _cast_weights_once(pairs):
    @pl.when(pl.program_id(0) == 0)
    def _():
        for src, dst in pairs:
            cols = src.shape[1]
            for c in range(0, cols, CAST_COLS):
                dst[:, c:min(c + CAST_COLS, cols)] = src[:, c:min(c + CAST_COLS, cols)].astype(BF16)


def _bf16_scratch(*weights):
    return [pltpu.VMEM(w.shape, BF16) for w in weights]


def _params(n_axes):
    return pltpu.CompilerParams(dimension_semantics=("arbitrary",) * n_axes,
                                vmem_limit_bytes=VMEM_LIMIT_BYTES)


def _proj_kernel(x_ref, lng_ref, lnb_ref, w32_ref, mg_ref, mb_ref, ws_ref, bs_ref,
                 q_ref, k_ref, v_ref, kf_ref, vf_ref, vn_ref, ga_ref, gb_ref, ob_ref, w_ref,
                 *, mlp_len, vn_rows):
    _cast_weights_once([(w32_ref, w_ref)])
    rows = x_ref.shape[0]
    xn = _layer_norm(x_ref[...], lng_ref[...], lnb_ref[...]).astype(BF16)

    def section(lo, width):
        return _dot(xn, w_ref[:, lo:lo + width])

    q = section(0, A_WIDTH)
    q_ref[...] = (q * (HEAD_DIM ** -0.5)).astype(BF16)
    k = section(A_WIDTH, A_WIDTH)
    k_ref[...] = k.astype(BF16)
    kf_ref[...] = k
    v = section(2 * A_WIDTH, A_WIDTH)
    v_ref[...] = v.astype(BF16)
    vf_ref[...] = v
    base = 3 * A_WIDTH
    un = jax.nn.gelu(section(base, MLP_WIDTH))
    vn = _layer_norm(jax.nn.gelu(section(base + MLP_WIDTH, MLP_WIDTH)), mg_ref[...], mb_ref[...])
    vn_ref[...] = vn[rows - vn_rows:, :]
    base += 2 * MLP_WIDTH
    ga_ref[...] = jax.nn.sigmoid(section(base, D_MODEL)).astype(BF16)
    gb_ref[...] = jax.nn.sigmoid(section(base + D_MODEL, D_MODEL)).astype(BF16)

    vnb = vn.astype(BF16)
    low_half = lax.broadcasted_iota(jnp.int32, (mlp_len, LANES), 1) < (MLP_WIDTH // MLP_GROUPS)
    for c in range(rows // mlp_len):
        r0 = c * mlp_len
        for p in range(MLP_GROUPS // 2):
            c0 = p * LANES
            slab = vnb[r0:r0 + mlp_len, c0:c0 + LANES]
            mixed = jnp.where(low_half, _dot(ws_ref[2 * p], slab), _dot(ws_ref[2 * p + 1], slab))
            gated = un[r0:r0 + mlp_len, c0:c0 + LANES] * (mixed + bs_ref[:, c0:c0 + LANES])
            ob_ref[r0:r0 + mlp_len, c0:c0 + LANES] = gated.astype(BF16)


def _proj(x2d, ln_g, ln_b, w_in, mlp_g, mlp_b, w_s, b_s, *, seq, mlp_len, keep_rows):
    n_rows = x2d.shape[0]
    tile = ROW_TILE
    assert n_rows % tile == 0 and tile % mlp_len == 0
    n_seq = n_rows // seq
    if seq >= tile:
        assert seq % tile == 0 and keep_rows == tile
        per_seq = seq // tile
        kv_rows, vn_rows = n_seq * tile, mlp_len
        kv_map = lambda i: (i // per_seq, 0)
    else:
        assert keep_rows == seq and mlp_len == seq
        kv_rows, vn_rows = n_rows, tile
        kv_map = lambda i: (i, 0)
    row = lambda width: pl.BlockSpec((tile, width), lambda i: (i, 0))
    n_in = w_in.shape[1]
    out_shape = (
        jax.ShapeDtypeStruct((n_rows, A_WIDTH), BF16),
        jax.ShapeDtypeStruct((n_rows, A_WIDTH), BF16),
        jax.ShapeDtypeStruct((n_rows, A_WIDTH), BF16),
        jax.ShapeDtypeStruct((kv_rows, A_WIDTH), F32),
        jax.ShapeDtypeStruct((kv_rows, A_WIDTH), F32),
        jax.ShapeDtypeStruct((kv_rows // tile * vn_rows, MLP_WIDTH), F32),
        jax.ShapeDtypeStruct((n_rows, D_MODEL), BF16),
        jax.ShapeDtypeStruct((n_rows, D_MODEL), BF16),
        jax.ShapeDtypeStruct((n_rows, MLP_WIDTH), BF16),
    )
    out_specs = (
        row(A_WIDTH), row(A_WIDTH), row(A_WIDTH),
        pl.BlockSpec((tile, A_WIDTH), kv_map), pl.BlockSpec((tile, A_WIDTH), kv_map),
        pl.BlockSpec((vn_rows, MLP_WIDTH), kv_map),
        row(D_MODEL), row(D_MODEL), row(MLP_WIDTH),
    )
    in_specs = [
        row(D_MODEL), _const_spec((1, D_MODEL)), _const_spec((1, D_MODEL)),
        _resident_spec((D_MODEL, n_in)), _const_spec((1, MLP_WIDTH)), _const_spec((1, MLP_WIDTH)),
        _const_spec((MLP_GROUPS, mlp_len, mlp_len)), _const_spec((mlp_len, MLP_WIDTH)),
    ]
    return pl.pallas_call(
        functools.partial(_proj_kernel, mlp_len=mlp_len, vn_rows=vn_rows),
        grid=(n_rows // tile,), in_specs=in_specs, out_specs=out_specs, out_shape=out_shape,
        scratch_shapes=_bf16_scratch(w_in), compiler_params=_params(1), name="proj",
    )(x2d, ln_g, ln_b, w_in, mlp_g, mlp_b, w_s, b_s)


def _attend_heads(q, keys, values, pair_bias, o_ref):
    n_q = q.shape[0]
    low_half = lax.broadcasted_iota(jnp.int32, (n_q, LANES), 1) < HEAD_DIM
    zero = jnp.zeros((), BF16)
    for p in range(HEADS // 2):
        c0 = p * LANES
        q2, k2, v2 = q[:, c0:c0 + LANES], keys[:, c0:c0 + LANES], values[:, c0:c0 + LANES]
        qs = jnp.concatenate([jnp.where(low_half, q2, zero), jnp.where(low_half, zero, q2)], axis=0)
        s = _dot_nt(qs, k2) + pair_bias(p).reshape(2 * n_q, keys.shape[0])
        e = jnp.exp(s - jnp.max(s, axis=-1, keepdims=True))
        denom = jnp.sum(e, axis=-1, keepdims=True)
        o = _dot(e.astype(BF16), v2) / denom
        o_ref[:, c0:c0 + LANES] = jnp.where(low_half, o[:n_q], o[n_q:]).astype(o_ref.dtype)


def _fill_toeplitz_bias(bias_ref, diag_ref, first_key_chunk=None):
    _, n_q, n_k = bias_ref.shape
    if first_key_chunk is not None:
        q_chunk = lax.broadcasted_iota(jnp.int32, (n_q, n_k), 0) // CHUNK
        k_chunk = lax.broadcasted_iota(jnp.int32, (n_q, n_k), 1) // CHUNK + first_key_chunk
        in_band = (k_chunk <= q_chunk) & (k_chunk >= q_chunk - PAST_CHUNKS)
    for h in range(bias_ref.shape[0]):
        rows = jnp.broadcast_to(diag_ref[h:h + 1, :], (n_q, diag_ref.shape[1]))
        rows = pltpu.roll(rows, 0, 1, stride=1, stride_axis=0)[:, :n_k]
        bias_ref[h] = rows if first_key_chunk is None else jnp.where(in_band, rows, MASKED)


def _attn_prompt_kernel(q_ref, k_ref, v_ref, diag_ref, o_ref, bias_ref, *, window, lead_steps):
    n_q = q_ref.shape[0]

    @pl.when((pl.program_id(0) == 0) & (pl.program_id(1) == 0))
    def _():
        _fill_toeplitz_bias(bias_ref, diag_ref, first_key_chunk=-lead_steps * (n_q // CHUNK))

    step = jnp.minimum(pl.program_id(1), lead_steps)
    start = pl.multiple_of((pl.program_id(1) - step) * n_q, n_q)
    shift = pl.multiple_of((lead_steps - step) * n_q, n_q)
    keys = k_ref[pl.ds(start, window), :]
    values = v_ref[pl.ds(start, window), :]
    _attend_heads(q_ref[...], keys, values,
                  lambda p: bias_ref[2 * p:2 * p + 2, :, pl.ds(shift, window)], o_ref)


def _bias_diagonals(rel_table, n_neg, n_pos, key_offset):
    period = pl.next_power_of_2(n_neg + n_pos)
    m = np.arange(period)
    diag = np.where(m < n_pos, m, m - period)
    return rel_table[:, np.clip(diag - key_offset, -MAX_REL, MAX_REL) + MAX_REL].astype(F32), period


def _attn_prompt(q, k, v, rel_table, *, batch, seq):
    n_q = Q_CHUNKS * CHUNK
    window = (Q_CHUNKS + PAST_CHUNKS) * CHUNK
    assert PAST_CHUNKS % Q_CHUNKS == 0 and seq >= window
    lead = PAST_CHUNKS // Q_CHUNKS
    steps = seq // n_q
    n_bias = lead * n_q + window
    diag, _ = _bias_diagonals(rel_table, n_q, n_bias, PAST_CHUNKS * CHUNK)
    qspec = pl.BlockSpec((n_q, A_WIDTH), lambda b, i: (b * steps + i, 0))
    kvspec = pl.BlockSpec((seq, A_WIDTH), lambda b, i: (b, 0))
    return pl.pallas_call(
        functools.partial(_attn_prompt_kernel, window=window, lead_steps=lead),
        grid=(batch, steps),
        in_specs=[qspec, kvspec, kvspec, _const_spec(diag.shape)],
        out_specs=qspec,
        out_shape=jax.ShapeDtypeStruct((batch * seq, A_WIDTH), BF16),
        scratch_shapes=[pltpu.VMEM((HEADS, n_q, n_bias), F32)],
        compiler_params=_params(2), name="attn_prompt",
    )(q, k, v, diag)


def _attn_sample_kernel(q_ref, k_ref, v_ref, ck_ref, cv_ref, diag_ref, o_ref, kk_ref, vv_ref, bias_ref):
    @pl.when(pl.program_id(0) == 0)
    def _():
        _fill_toeplitz_bias(bias_ref, diag_ref)

    n_cache = ck_ref.shape[1]
    n_new = k_ref.shape[1]
    kk_ref[0:n_cache, :] = ck_ref[0].astype(BF16)
    kk_ref[n_cache:n_cache + n_new, :] = k_ref[0]
    vv_ref[0:n_cache, :] = cv_ref[0].astype(BF16)
    vv_ref[n_cache:n_cache + n_new, :] = v_ref[0]
    _attend_heads(q_ref[0], kk_ref[...], vv_ref[...], lambda p: bias_ref[2 * p:2 * p + 2], o_ref.at[0])


def _attn_sample(q, k, v, cache_k, cache_v, rel_table, *, batch, seq):
    n_cache = cache_k.shape[1]
    diag, _ = _bias_diagonals(rel_table, seq, n_cache + seq, n_cache)
    new = pl.BlockSpec((1, seq, A_WIDTH), lambda b: (b, 0, 0))
    old = pl.BlockSpec((1, n_cache, A_WIDTH), lambda b: (b, 0, 0))
    out = pl.pallas_call(
        _attn_sample_kernel,
        grid=(batch,),
        in_specs=[new, new, new, old, old, _const_spec(diag.shape)],
        out_specs=new,
        out_shape=jax.ShapeDtypeStruct((batch, seq, A_WIDTH), BF16),
        scratch_shapes=[pltpu.VMEM((n_cache + seq, A_WIDTH), BF16),
                        pltpu.VMEM((n_cache + seq, A_WIDTH), BF16),
                        pltpu.VMEM((HEADS, seq, n_cache + seq), F32)],
        compiler_params=_params(1), name="attn_sample",
    )(q.reshape(batch, seq, A_WIDTH), k.reshape(batch, seq, A_WIDTH), v.reshape(batch, seq, A_WIDTH),
      cache_k.reshape(batch, n_cache, A_WIDTH), cache_v.reshape(batch, n_cache, A_WIDTH), diag)
    return out.reshape(batch * seq, A_WIDTH)


def _first_index(hit, iota, axis, limit):
    return jnp.min(jnp.where(hit, iota, limit), axis=axis, keepdims=True)


def _route(scores, sel):
    n_tok = scores.shape[1]
    neg = -jnp.inf
    grouped = sel.reshape(N_GROUPS, GROUP_SIZE, n_tok)
    member = lax.broadcasted_iota(jnp.int32, grouped.shape, 1)
    best = jnp.max(grouped, axis=1, keepdims=True)
    first = _first_index(grouped == best, member, 1, GROUP_SIZE)
    second = jnp.max(jnp.where(member == first, neg, grouped), axis=1, keepdims=True)
    group_score = best + second

    group_id = lax.broadcasted_iota(jnp.int32, group_score.shape, 0)
    keep = jnp.zeros(group_score.shape, F32)
    for _ in range(TOPK_GROUPS):
        top = jnp.max(group_score, axis=0, keepdims=True)
        hit = group_id == _first_index(group_score == top, group_id, 0, N_GROUPS)
        keep = jnp.where(hit, 1.0, keep)
        group_score = jnp.where(hit, neg, group_score)
    keep = jnp.broadcast_to(keep, grouped.shape).reshape(N_EXPERTS, n_tok)

    cand = jnp.where(keep > 0.0, sel, neg)
    expert_id = lax.broadcasted_iota(jnp.int32, cand.shape, 0)
    ids, weights = [], []
    for _ in range(TOP_K):
        top = jnp.max(cand, axis=0, keepdims=True)
        first = _first_index(cand == top, expert_id, 0, N_EXPERTS)
        hit = expert_id == first
        ids.append(first)
        weights.append(jnp.sum(jnp.where(hit, scores, 0.0), axis=0, keepdims=True))
        cand = jnp.where(hit, neg, cand)
    ids = jnp.concatenate(ids, axis=0)
    weights = jnp.concatenate(weights, axis=0)
    weights = weights / jnp.sum(weights, axis=0, keepdims=True) * ROUTED_SCALE
    return ids, weights


def _store_row_major_tiles(flat_ref, x):
    rows = x.shape[0]
    for j in range(D_TILES):
        flat_ref[pl.ds(j, rows, stride=D_TILES), :] = x[:, j * LANES:(j + 1) * LANES]


def _load_row_major_tiles(flat_ref, rows):
    return jnp.concatenate(
        [flat_ref[pl.ds(j, rows, stride=D_TILES), :] for j in range(D_TILES)], axis=1)


def _split_bf16(w):
    hi = w.astype(BF16)
    return jnp.concatenate([hi, (w - hi.astype(F32)).astype(BF16)], axis=0)


def _mix_kernel(x_ref, oa_ref, ob_ref, ga_ref, gb_ref, lng_ref, lnb_ref, wa32_ref, wb32_ref, wo32_ref,
                g1_ref, b1_ref, rw_ref, x1_ref, x1t_ref, scores_ref, wa_ref, wb_ref, wo_ref, *, alpha):
    _cast_weights_once([(wa32_ref, wa_ref), (wb32_ref, wb_ref), (wo32_ref, wo_ref)])
    xn = _layer_norm(x_ref[...], lng_ref[...], lnb_ref[...])
    mix = (ga_ref[...].astype(F32) * _dot(oa_ref[...], wa_ref[...])
           + gb_ref[...].astype(F32) * _dot(ob_ref[...], wb_ref[...]))
    x1 = _layer_norm(alpha * xn + _dot(mix.astype(BF16), wo_ref[...]), g1_ref[...], b1_ref[...])
    x1_ref[...] = x1
    _store_row_major_tiles(x1t_ref, x1)
    x1_hi = x1.astype(BF16)
    x1_lo = (x1 - x1_hi.astype(F32)).astype(BF16)
    by_hi = _dot_nt(rw_ref[...], x1_hi)
    logits = by_hi[:N_EXPERTS] + by_hi[N_EXPERTS:] + _dot_nt(rw_ref[:N_EXPERTS, :], x1_lo)
    scores_ref[...] = jax.nn.sigmoid(logits)


def _mix(x2d, oa, ob, ga, gb, ln_g, ln_b, wa, wb, wo, g1, b1, rw_t, *, alpha):
    n_rows = x2d.shape[0]
    tile = ROW_TILE
    row = lambda width: pl.BlockSpec((tile, width), lambda i: (i, 0))
    return pl.pallas_call(
        functools.partial(_mix_kernel, alpha=alpha),
        grid=(n_rows // tile,),
        in_specs=[row(D_MODEL), row(A_WIDTH), row(MLP_WIDTH), row(D_MODEL), row(D_MODEL),
                  _const_spec((1, D_MODEL)), _const_spec((1, D_MODEL)),
                  _resident_spec(wa.shape), _resident_spec(wb.shape), _resident_spec(wo.shape),
                  _const_spec((1, D_MODEL)), _const_spec((1, D_MODEL)),
                  _const_spec(rw_t.shape)],
        out_specs=(row(D_MODEL), pl.BlockSpec((tile * D_TILES, LANES), lambda i: (i, 0)),
                   pl.BlockSpec((N_EXPERTS, tile), lambda i: (0, i))),
        out_shape=(jax.ShapeDtypeStruct((n_rows, D_MODEL), F32),
                   jax.ShapeDtypeStruct((n_rows * D_TILES, LANES), F32),
                   jax.ShapeDtypeStruct((N_EXPERTS, n_rows), F32)),
        scratch_shapes=_bf16_scratch(wa, wb, wo), compiler_params=_params(1), name="mix",
    )(x2d, oa, ob, ga, gb, ln_g, ln_b, wa, wb, wo, g1, b1, rw_t)


def _route_kernel(scores_ref, rb_ref, ids_ref, wts_ref):
    scores = scores_ref[...]
    ids, weights = _route(scores, scores + rb_ref[...])
    ids_ref[...] = ids
    wts_ref[...] = weights


def _route_call(scores, rb):
    n_tok = scores.shape[1]
    tile = min(ROUTE_TILE, n_tok)
    col = lambda rows: pl.BlockSpec((rows, tile), lambda i: (0, i))
    return pl.pallas_call(
        _route_kernel,
        grid=(n_tok // tile,),
        in_specs=[col(N_EXPERTS), _const_spec(rb.shape)],
        out_specs=(col(TOP_K), col(TOP_K)),
        out_shape=(jax.ShapeDtypeStruct((TOP_K, n_tok), jnp.int32),
                   jax.ShapeDtypeStruct((TOP_K, n_tok), F32)),
        compiler_params=_params(1), name="route",
    )(scores, rb)


def _moe_kernel(rows_ref, start_ref, x_ref, wts_ref, w1_ref, w3_ref, w2_ref, *rest, n_rows, emit_bf16,
                experts_per_step):
    gat_ref, y_full_ref, y_half_ref, pending_ref = rest[-4:]
    if emit_bf16:
        out_ref, w1b_ref, w3b_ref, w2b_ref = rest[-8:-4]
        w1b_ref[...] = w1_ref[...].astype(BF16)
        w3b_ref[...] = w3_ref[...].astype(BF16)
        w2b_ref[...] = w2_ref[...].astype(BF16)
        w1_ref, w3_ref, w2_ref = w1b_ref, w3b_ref, w2b_ref
    else:
        out_ref = rest[-5]
    acc = out_ref.at[0]
    step = pl.program_id(0)
    half = n_rows // 2
    y_refs = {n_rows: y_full_ref, half: y_half_ref}

    @pl.when(step == 0)
    def _():
        out_ref[...] = jnp.zeros(out_ref.shape, out_ref.dtype)
        y_full_ref[...] = jnp.zeros(y_full_ref.shape, y_full_ref.dtype)
        y_half_ref[...] = jnp.zeros(y_half_ref.shape, y_half_ref.dtype)
        pending_ref[0] = 0
        pending_ref[1] = 0

    def tile_at(ref, row8):
        return ref.at[pl.ds(pl.multiple_of(row8, SUBLANES), SUBLANES), :]

    def scatter_add(base, n):
        y_ref = y_refs[n]
        for m0 in range(0, n, SCATTER_BATCH):
            updates = []
            for m in range(m0, m0 + SCATTER_BATCH):
                dst = tile_at(acc, rows_ref[base + m])
                updates.append((dst, dst[...] + y_ref[m * SUBLANES:(m + 1) * SUBLANES, :]))
            for dst, val in updates:
                dst[...] = val

    def run_block(j, first, n, nominal, begin, pending):
        redo = nominal - begin
        base = first + begin + n_rows
        for m in range(n):
            gat_ref[m * SUBLANES:(m + 1) * SUBLANES, :] = tile_at(x_ref, rows_ref[base + m])[...]
        xb = _load_row_major_tiles(gat_ref, n).astype(BF16)
        act = jax.nn.silu(_dot(xb, w1_ref[j])) * _dot(xb, w3_ref[j])
        y = _dot(act.astype(BF16), w2_ref[j])

        w_rows = pl.cdiv(n, LANES) + 1
        row_id = lax.broadcasted_iota(jnp.int32, (n, w_rows * LANES), 0)
        lane_id = lax.broadcasted_iota(jnp.int32, (n, w_rows * LANES), 1)
        w_row = base // LANES
        span = jnp.concatenate([wts_ref[w_row + r] for r in range(w_rows)], axis=1)
        picked = jnp.where((lane_id == row_id + base % LANES) & (row_id >= redo),
                           jnp.broadcast_to(span, lane_id.shape), 0.0)
        ys = y * jnp.sum(picked, axis=1, keepdims=True)
        scatter_add(pending, n)
        _store_row_major_tiles(y_refs[n], ys)
        return base

    def one_expert(j):
        e = step * experts_per_step + j
        first = start_ref[e]
        count = start_ref[e + 1] - first
        full_blocks = count // n_rows
        rest_rows = count - full_blocks * n_rows
        main_blocks = full_blocks + (rest_rows > half).astype(jnp.int32)

        def main_block(b, pending):
            nominal = b * n_rows
            return run_block(j, first, n_rows, nominal, jnp.minimum(nominal, count - n_rows), pending)

        pending_ref[0] = lax.fori_loop(0, main_blocks, main_block, pending_ref[0])

        @pl.when((rest_rows > 0) & (rest_rows <= half))
        def _():
            pending_ref[1] = run_block(j, first, half, full_blocks * n_rows, count - half, pending_ref[1])

    for j in range(experts_per_step):
        one_expert(j)

    @pl.when(step == pl.num_programs(0) - 1)
    def _():
        scatter_add(pending_ref[0], n_rows)
        scatter_add(pending_ref[1], half)


def _moe(ids, weights, x_tiles, w1, w3, w2):
    n_tok = ids.shape[1]
    g_tok = min(MOE_GROUP_TOKENS, n_tok)
    n_groups = n_tok // g_tok
    assert n_groups * g_tok == n_tok
    per_group = g_tok * TOP_K
    tok = lax.broadcasted_iota(jnp.int32, ids.shape, 1)
    key = ((tok // g_tok) * N_EXPERTS + ids) * g_tok + tok % g_tok
    key, wts = lax.sort((key.reshape(-1), weights.reshape(-1)), num_keys=1)
    rows8 = ((key % g_tok) * D_TILES).reshape(n_groups, per_group)
    wts = wts.reshape(n_groups, per_group)
    experts = jnp.arange(N_EXPERTS, dtype=jnp.int32)[None, None, :, None]
    counts = jnp.sum(ids.reshape(TOP_K, n_groups, 1, g_tok) == experts, axis=(0, 3), dtype=jnp.int32)
    start = jnp.concatenate([jnp.zeros((n_groups, 1), jnp.int32), jnp.cumsum(counts, axis=1)], axis=1)
    many_rows = per_group // N_EXPERTS >= 2 * MOE_BLOCK_ROWS
    n_rows = MOE_BLOCK_ROWS if many_rows else MOE_SMALL_BLOCK_ROWS
    pad = ((0, 0), (n_rows, n_rows + LANES))
    rows8 = jnp.pad(rows8, pad)
    wts = jnp.pad(wts, pad).reshape(n_groups, -1, 1, LANES)
    out = None
    for g in range(n_groups):
        out, (w1, w3, w2) = _moe_group_call(g, n_groups, g_tok, n_rows, rows8[g], start[g], x_tiles, wts[g],
                                            w1, w3, w2, out)
    return out, (w1, w3, w2)


def _moe_group_call(group, n_groups, g_tok, n_rows, rows8, start, x_tiles, wts, w1, w3, w2, prev):
    g_rows = g_tok * D_TILES
    assert x_tiles.shape[0] == n_groups * g_rows
    emit_bf16 = w1.dtype != BF16
    assert not (emit_bf16 and prev is not None)
    out_block = (1, g_rows, LANES)
    eps = 1 if emit_bf16 else MOE_EXPERTS_PER_STEP
    per_expert = lambda w: pl.BlockSpec((eps,) + w.shape[1:], lambda s, *_: (s, 0, 0))
    in_specs = [
        pl.BlockSpec((g_rows, LANES), lambda e, *_: (group, 0), pipeline_mode=pl.Buffered(1)),
        pl.BlockSpec(wts.shape, lambda e, *_: (0, 0, 0)),
        per_expert(w1), per_expert(w3), per_expert(w2),
    ]
    args = [rows8, start, x_tiles, wts, w1, w3, w2]
    aliases = {}
    if prev is not None:
        in_specs.append(pl.BlockSpec(memory_space=pl.ANY))
        aliases = {len(args): 0}
        args.append(prev)
    out_specs = [pl.BlockSpec(out_block, lambda e, *_: (group, 0, 0), pipeline_mode=pl.Buffered(1))]
    out_shape = [jax.ShapeDtypeStruct((n_groups,) + out_block[1:], F32)]
    if emit_bf16:
        out_specs += [per_expert(w1), per_expert(w3), per_expert(w2)]
        out_shape += [jax.ShapeDtypeStruct(w.shape, BF16) for w in (w1, w3, w2)]
    results = pl.pallas_call(
        functools.partial(_moe_kernel, n_rows=n_rows, emit_bf16=emit_bf16, experts_per_step=eps),
        grid_spec=pltpu.PrefetchScalarGridSpec(
            num_scalar_prefetch=2, grid=(N_EXPERTS // eps,), in_specs=in_specs, out_specs=out_specs,
            scratch_shapes=[pltpu.VMEM((n_rows * D_TILES, LANES), F32),
                            pltpu.VMEM((n_rows * D_TILES, LANES), F32),
                            pltpu.VMEM((n_rows // 2 * D_TILES, LANES), F32),
                            pltpu.SMEM((2,), jnp.int32)]),
        out_shape=out_shape,
        input_output_aliases=aliases,
        compiler_params=_params(1), name="moe",
    )(*args)
    return results[0], (tuple(results[1:]) if emit_bf16 else (w1, w3, w2))


def _final_kernel(x1_ref, routed_ref, p_ref, s1_32_ref, s3_32_ref, s2_32_ref, g2_ref, b2_ref, wg32_ref,
                  wp32_ref, o_ref, s1_ref, s3_ref, s2_ref, wg_ref, wp_ref, *, alpha):
    _cast_weights_once([(s1_32_ref, s1_ref), (s3_32_ref, s3_ref), (s2_32_ref, s2_ref),
                        (wg32_ref, wg_ref), (wp32_ref, wp_ref)])
    x1 = x1_ref[...]
    xb = x1.astype(BF16)
    act = jax.nn.silu(_dot(xb, s1_ref[...])) * _dot(xb, s3_ref[...])
    ff = _load_row_major_tiles(routed_ref.at[0], x1.shape[0]) + _dot(act.astype(BF16), s2_ref[...])
    x2 = _layer_norm(alpha * x1 + ff, g2_ref[...], b2_ref[...])
    gate = jax.nn.sigmoid(_dot(x2.astype(BF16), wg_ref[...]))
    o_ref[...] = x2 + gate * _dot(p_ref[...].astype(BF16), wp_ref[...])


def _final(x1, routed, p2d, s1, s3, s2, g2, b2, wg, wp, *, alpha):
    n_rows = x1.shape[0]
    tile = ROW_TILE
    row = lambda width: pl.BlockSpec((tile, width), lambda i: (i, 0))
    tiles_per_group = routed.shape[1] // D_TILES // tile
    routed_spec = pl.BlockSpec((1, tile * D_TILES, LANES),
                               lambda i: (i // tiles_per_group, i % tiles_per_group, 0))
    return pl.pallas_call(
        functools.partial(_final_kernel, alpha=alpha),
        grid=(n_rows // tile,),
        in_specs=[row(D_MODEL), routed_spec,
                  row(p2d.shape[1]),
                  _resident_spec(s1.shape), _resident_spec(s3.shape), _resident_spec(s2.shape),
                  _const_spec((1, D_MODEL)), _const_spec((1, D_MODEL)),
                  _resident_spec(wg.shape), _resident_spec(wp.shape)],
        out_specs=row(D_MODEL),
        out_shape=jax.ShapeDtypeStruct((n_rows, D_MODEL), F32),
        scratch_shapes=_bf16_scratch(s1, s3, s2, wg, wp), compiler_params=_params(1), name="final",
    )(x1, routed, p2d, s1, s3, s2, g2, b2, wg, wp)


def _trunk_layer(x, p, cache_k, cache_v, lw, *, alpha, first_layer_ln):
    batch, seq, d = x.shape
    n_rows = batch * seq
    x2d = x.reshape(n_rows, d)
    prompt = cache_k is None
    mlp_len = min(seq, MLP_CHUNK)
    keep = min(PAST_CHUNKS * CHUNK, seq) if prompt else seq
    ln_g, ln_b = first_layer_ln

    w_s = (lw["mlp_w_s"][:, :mlp_len, :mlp_len]
           * jnp.tril(jnp.ones((mlp_len, mlp_len), F32))).astype(BF16)
    b_s = jnp.repeat(lw["mlp_b_s"][:, :mlp_len].T, MLP_WIDTH // MLP_GROUPS, axis=1)
    q, k, v, kf, vf, vn, ga, gb, ob = _proj(
        x2d, ln_g, ln_b, lw["w_in"], lw["mlp_ln_g"][None], lw["mlp_ln_b"][None],
        w_s, b_s, seq=seq, mlp_len=mlp_len, keep_rows=keep)
    if prompt:
        oa = _attn_prompt(q, k, v, lw["attn_rel_bias"], batch=batch, seq=seq)
    else:
        oa = _attn_sample(q, k, v, cache_k, cache_v, lw["attn_rel_bias"], batch=batch, seq=seq)
    x1, x1_tiles, scores = _mix(
        x2d, oa, ob, ga, gb, ln_g, ln_b, lw["w_branch_a"], lw["w_branch_b"], lw["w_out"],
        lw["ln1_g"][None], lw["ln1_b"][None], _split_bf16(lw["router_w"].T), alpha=alpha)
    ids, weights = _route_call(scores, lw["router_bias"][:, None])
    routed, expert_bf16 = _moe(ids, weights, x1_tiles, lw["exp_w1"], lw["exp_w3"], lw["exp_w2"])
    y = _final(x1, routed, p.reshape(n_rows, -1), lw["shared_w1"], lw["shared_w3"], lw["shared_w2"],
               lw["ln2_g"][None], lw["ln2_b"][None], lw["ple_w_gate"], lw["ple_w_proj"], alpha=alpha)
    new_k = kf.reshape(batch, keep, HEADS, HEAD_DIM)
    new_v = vf.reshape(batch, keep, HEADS, HEAD_DIM)
    new_mlp_v = vn.reshape(batch, mlp_len, MLP_WIDTH)
    return y.reshape(batch, seq, d), new_k, new_v, new_mlp_v, expert_bf16


def kernel(x_prompt, x_sample, cache_attn_k, cache_attn_v, p_prompt, p_sample, ln_in_g, ln_in_b, w_in, attn_rel_bias, mlp_ln_g, mlp_ln_b, mlp_w_s, mlp_b_s, w_branch_a, w_branch_b, w_out, ln1_g, ln1_b, router_w, router_bias, exp_w1, exp_w3, exp_w2, shared_w1, shared_w3, shared_w2, ln2_g, ln2_b, ple_w_gate, ple_w_proj):
    depth = w_in.shape[0]
    assert depth == 1, "the fused proj/mix kernels apply LayerNorm_in themselves: single layer only"
    alpha = (2 * depth) ** 0.25
    stacked = dict(w_in=w_in, attn_rel_bias=attn_rel_bias, mlp_ln_g=mlp_ln_g, mlp_ln_b=mlp_ln_b,
                   mlp_w_s=mlp_w_s, mlp_b_s=mlp_b_s, w_branch_a=w_branch_a, w_branch_b=w_branch_b,
                   w_out=w_out, ln1_g=ln1_g, ln1_b=ln1_b, router_w=router_w, router_bias=router_bias,
                   exp_w1=exp_w1, exp_w3=exp_w3, exp_w2=exp_w2, shared_w1=shared_w1,
                   shared_w3=shared_w3, shared_w2=shared_w2, ln2_g=ln2_g, ln2_b=ln2_b,
                   ple_w_gate=ple_w_gate, ple_w_proj=ple_w_proj)
    lw = {name: value[0] for name, value in stacked.items()}
    ln_in =(ln_in_g[None], ln_in_b[None])
    yp, kp, vp, mp, expert_bf16 = _trunk_layer(x_prompt, p_prompt[0], None, None, lw, alpha=alpha,
                                               first_layer_ln=ln_in)
    lw = dict(lw, **dict(zip(("exp_w1", "exp_w3", "exp_w2"), expert_bf16)))
    ys, ks, vs, ms, _ = _trunk_layer(x_sample, p_sample[0], cache_attn_k[0], cache_attn_v[0], lw,
                                     alpha=alpha, first_layer_ln=ln_in)
    return (yp, ys, kp[None], vp[None], ks[None], vs[None], mp[None], ms[None])
```

```python
import functools

import jax
import jax.numpy as jnp
import numpy as np
from jax import lax
from jax.experimental import pallas as pl
from jax.experimental.pallas import tpu as pltpu

F32 = jnp.float32
BF16 = jnp.bfloat16

D_MODEL = 1024
CHUNK = 64
HEADS = 8
HEAD_DIM = 64
A_WIDTH = HEADS * HEAD_DIM
PAST_CHUNKS = 8
MAX_REL = 256
MLP_CHUNK = 128
MLP_GROUPS = 8
MLP_WIDTH = 512
N_EXPERTS = 64
TOP_K = 8
N_GROUPS = 8
TOPK_GROUPS = 4
GROUP_SIZE = N_EXPERTS // N_GROUPS
EXPERT_DIM = 256
ROUTED_SCALE = 2.5
LN_EPS = 1e-5
MASKED = -1e30

LANES = 128
SUBLANES = 8
VMEM_LIMIT_BYTES = 56 * 1024 * 1024

ROW_TILE = 512
ROUTE_TILE = 2048
Q_CHUNKS = 4
MOE_GROUP_TOKENS = 4096
MOE_BLOCK_ROWS = 256
MOE_SMALL_BLOCK_ROWS = 128
MOE_EXPERTS_PER_STEP = {MOE_BLOCK_ROWS: 2, MOE_SMALL_BLOCK_ROWS: 4}
CAST_COLS = 512
SCATTER_BATCH = 8
D_TILES = D_MODEL // LANES


def _layer_norm(x, g, b):
    xc = x - jnp.mean(x, axis=-1, keepdims=True)
    var = jnp.mean(xc * xc, axis=-1, keepdims=True)
    return xc * lax.rsqrt(var + LN_EPS) * g + b


def _dot(a, b):
    return jnp.dot(a, b, preferred_element_type=F32)


def _dot_nt(a, b, precision=None):
    return lax.dot_general(a, b, (((1,), (1,)), ((), ())),
                           preferred_element_type=F32, precision=precision)


def _const_spec(shape):
    zeros = (0,) * len(shape)
    return pl.BlockSpec(shape, lambda *_: zeros)


def _resident_spec(shape):
    zeros = (0,) * len(shape)
    return pl.BlockSpec(shape, lambda *_: zeros, pipeline_mode=pl.Buffered(1))


def _cast_weights_once(pairs):
    @pl.when(pl.program_id(0) == 0)
    def _():
        for src, dst in pairs:
            cols = src.shape[1]
            for c in range(0, cols, CAST_COLS):
                dst[:, c:min(c + CAST_COLS, cols)] = src[:, c:min(c + CAST_COLS, cols)].astype(BF16)


def _bf16_scratch(*weights):
    return [pltpu.VMEM(w.shape, BF16) for w in weights]


def _params(n_axes):
    return pltpu.CompilerParams(dimension_semantics=("arbitrary",) * n_axes,
                                vmem_limit_bytes=VMEM_LIMIT_BYTES)


def _proj_kernel(x_ref, lng_ref, lnb_ref, w32_ref, mg_ref, mb_ref, ws_ref, bs_ref,
                 q_ref, k_ref, v_ref, kf_ref, vf_ref, vn_ref, ga_ref, gb_ref, ob_ref, w_ref,
                 *, mlp_len, vn_rows):
    _cast_weights_once([(w32_ref, w_ref)])
    rows = x_ref.shape[0]
    xn = _layer_norm(x_ref[...], lng_ref[...], lnb_ref[...]).astype(BF16)

    def section(lo, width):
        return _dot(xn, w_ref[:, lo:lo + width])

    q = section(0, A_WIDTH)
    q_ref[...] = (q * (HEAD_DIM ** -0.5)).astype(BF16)
    k = section(A_WIDTH, A_WIDTH)
    k_ref[...] = k.astype(BF16)
    kf_ref[...] = k
    v = section(2 * A_WIDTH, A_WIDTH)
    v_ref[...] = v.astype(BF16)
    vf_ref[...] = v
    base = 3 * A_WIDTH
    un = jax.nn.gelu(section(base, MLP_WIDTH))
    vn = _layer_norm(jax.nn.gelu(section(base + MLP_WIDTH, MLP_WIDTH)), mg_ref[...], mb_ref[...])
    vn_ref[...] = vn[rows - vn_rows:, :]
    base += 2 * MLP_WIDTH
    ga_ref[...] = jax.nn.sigmoid(section(base, D_MODEL)).astype(BF16)
    gb_ref[...] = jax.nn.sigmoid(section(base + D_MODEL, D_MODEL)).astype(BF16)

    vnb = vn.astype(BF16)
    low_half = lax.broadcasted_iota(jnp.int32, (mlp_len, LANES), 1) < (MLP_WIDTH // MLP_GROUPS)
    for c in range(rows // mlp_len):
        r0 = c * mlp_len
        for p in range(MLP_GROUPS // 2):
            c0 = p * LANES
            slab = vnb[r0:r0 + mlp_len, c0:c0 + LANES]
            mixed = jnp.where(low_half, _dot(ws_ref[2 * p], slab), _dot(ws_ref[2 * p + 1], slab))
            gated = un[r0:r0 + mlp_len, c0:c0 + LANES] * (mixed + bs_ref[:, c0:c0 + LANES])
            ob_ref[r0:r0 + mlp_len, c0:c0 + LANES] = gated.astype(BF16)


def _proj(x2d, ln_g, ln_b, w_in, mlp_g, mlp_b, w_s, b_s, *, seq, mlp_len, keep_rows):
    n_rows = x2d.shape[0]
    tile = ROW_TILE
    assert n_rows % tile == 0 and tile % mlp_len == 0
    n_seq = n_rows // seq
    if seq >= tile:
        assert seq % tile == 0 and keep_rows == tile
        per_seq = seq // tile
        kv_rows, vn_rows = n_seq * tile, mlp_len
        kv_map = lambda i: (i // per_seq, 0)
    else:
        assert keep_rows == seq and mlp_len == seq
        kv_rows, vn_rows = n_rows, tile
        kv_map = lambda i: (i, 0)
    row = lambda width: pl.BlockSpec((tile, width), lambda i: (i, 0))
    n_in = w_in.shape[1]
    out_shape = (
        jax.ShapeDtypeStruct((n_rows, A_WIDTH), BF16),
        jax.ShapeDtypeStruct((n_rows, A_WIDTH), BF16),
        jax.ShapeDtypeStruct((n_rows, A_WIDTH), BF16),
        jax.ShapeDtypeStruct((kv_rows, A_WIDTH), F32),
        jax.ShapeDtypeStruct((kv_rows, A_WIDTH), F32),
        jax.ShapeDtypeStruct((kv_rows // tile * vn_rows, MLP_WIDTH), F32),
        jax.ShapeDtypeStruct((n_rows, D_MODEL), BF16),
        jax.ShapeDtypeStruct((n_rows, D_MODEL), BF16),
        jax.ShapeDtypeStruct((n_rows, MLP_WIDTH), BF16),
    )
    out_specs = (
        row(A_WIDTH), row(A_WIDTH), row(A_WIDTH),
        pl.BlockSpec((tile, A_WIDTH), kv_map), pl.BlockSpec((tile, A_WIDTH), kv_map),
        pl.BlockSpec((vn_rows, MLP_WIDTH), kv_map),
        row(D_MODEL), row(D_MODEL), row(MLP_WIDTH),
    )
    in_specs = [
        row(D_MODEL), _const_spec((1, D_MODEL)), _const_spec((1, D_MODEL)),
        _resident_spec((D_MODEL, n_in)), _const_spec((1, MLP_WIDTH)), _const_spec((1, MLP_WIDTH)),
        _const_spec((MLP_GROUPS, mlp_len, mlp_len)), _const_spec((mlp_len, MLP_WIDTH)),
    ]
    return pl.pallas_call(
        functools.partial(_proj_kernel, mlp_len=mlp_len, vn_rows=vn_rows),
        grid=(n_rows // tile,), in_specs=in_specs, out_specs=out_specs, out_shape=out_shape,
        scratch_shapes=_bf16_scratch(w_in), compiler_params=_params(1), name="proj",
    )(x2d, ln_g, ln_b, w_in, mlp_g, mlp_b, w_s, b_s)


def _attend_heads(q, keys, values, pair_bias, o_ref):
    n_q = q.shape[0]
    low_half = lax.broadcasted_iota(jnp.int32, (n_q, LANES), 1) < HEAD_DIM
    zero = jnp.zeros((), BF16)
    for p in range(HEADS // 2):
        c0 = p * LANES
        q2, k2, v2 = q[:, c0:c0 + LANES], keys[:, c0:c0 + LANES], values[:, c0:c0 + LANES]
        qs = jnp.concatenate([jnp.where(low_half, q2, zero), jnp.where(low_half, zero, q2)], axis=0)
        s = _dot_nt(qs, k2) + pair_bias(p).reshape(2 * n_q, keys.shape[0])
        e = jnp.exp(s - jnp.max(s, axis=-1, keepdims=True))
        denom = jnp.sum(e, axis=-1, keepdims=True)
        o = _dot(e.astype(BF16), v2) / denom
        o_ref[:, c0:c0 + LANES] = jnp.where(low_half, o[:n_q], o[n_q:]).astype(o_ref.dtype)


def _fill_toeplitz_bias(bias_ref, diag_ref, first_key_chunk=None):
    _, n_q, n_k = bias_ref.shape
    if first_key_chunk is not None:
        q_chunk = lax.broadcasted_iota(jnp.int32, (n_q, n_k), 0) // CHUNK
        k_chunk = lax.broadcasted_iota(jnp.int32, (n_q, n_k), 1) // CHUNK + first_key_chunk
        in_band = (k_chunk <= q_chunk) & (k_chunk >= q_chunk - PAST_CHUNKS)
    for h in range(bias_ref.shape[0]):
        rows = jnp.broadcast_to(diag_ref[h:h + 1, :], (n_q, diag_ref.shape[1]))
        rows = pltpu.roll(rows, 0, 1, stride=1, stride_axis=0)[:, :n_k]
        bias_ref[h] = rows if first_key_chunk is None else jnp.where(in_band, rows, MASKED)


def _attn_prompt_kernel(q_ref, k_ref, v_ref, diag_ref, o_ref, bias_ref, *, window, lead_steps):
    n_q = q_ref.shape[0]

    @pl.when((pl.program_id(0) == 0) & (pl.program_id(1) == 0))
    def _():
        _fill_toeplitz_bias(bias_ref, diag_ref, first_key_chunk=-lead_steps * (n_q // CHUNK))

    step = jnp.minimum(pl.program_id(1), lead_steps)
    start = pl.multiple_of((pl.program_id(1) - step) * n_q, n_q)
    shift = pl.multiple_of((lead_steps - step) * n_q, n_q)
    keys = k_ref[pl.ds(start, window), :]
    values = v_ref[pl.ds(start, window), :]
    _attend_heads(q_ref[...], keys, values,
                  lambda p: bias_ref[2 * p:2 * p + 2, :, pl.ds(shift, window)], o_ref)


def _bias_diagonals(rel_table, n_neg, n_pos, key_offset):
    period = pl.next_power_of_2(n_neg + n_pos)
    m = np.arange(period)
    diag = np.where(m < n_pos, m, m - period)
    return rel_table[:, np.clip(diag - key_offset, -MAX_REL, MAX_REL) + MAX_REL].astype(F32), period


def _attn_prompt(q, k, v, rel_table, *, batch, seq):
    n_q = Q_CHUNKS * CHUNK
    window = (Q_CHUNKS + PAST_CHUNKS) * CHUNK
    assert PAST_CHUNKS % Q_CHUNKS == 0 and seq >= window
    lead = PAST_CHUNKS // Q_CHUNKS
    steps = seq // n_q
    n_bias = lead * n_q + window
    diag, _ = _bias_diagonals(rel_table, n_q, n_bias, PAST_CHUNKS * CHUNK)
    qspec = pl.BlockSpec((n_q, A_WIDTH), lambda b, i: (b * steps + i, 0))
    kvspec = pl.BlockSpec((seq, A_WIDTH), lambda b, i: (b, 0))
    return pl.pallas_call(
        functools.partial(_attn_prompt_kernel, window=window, lead_steps=lead),
        grid=(batch, steps),
        in_specs=[qspec, kvspec, kvspec, _const_spec(diag.shape)],
        out_specs=qspec,
        out_shape=jax.ShapeDtypeStruct((batch * seq, A_WIDTH), BF16),
        scratch_shapes=[pltpu.VMEM((HEADS, n_q, n_bias), F32)],
        compiler_params=_params(2), name="attn_prompt",
    )(q, k, v, diag)


def _attn_sample_kernel(q_ref, k_ref, v_ref, ck_ref, cv_ref, diag_ref, o_ref, kk_ref, vv_ref, bias_ref):
    @pl.when(pl.program_id(0) == 0)
    def _():
        _fill_toeplitz_bias(bias_ref, diag_ref)

    n_cache = ck_ref.shape[1]
    n_new = k_ref.shape[1]
    kk_ref[0:n_cache, :] = ck_ref[0].astype(BF16)
    kk_ref[n_cache:n_cache + n_new, :] = k_ref[0]
    vv_ref[0:n_cache, :] = cv_ref[0].astype(BF16)
    vv_ref[n_cache:n_cache + n_new, :] = v_ref[0]
    _attend_heads(q_ref[0], kk_ref[...], vv_ref[...], lambda p: bias_ref[2 * p:2 * p + 2], o_ref.at[0])


def _attn_sample(q, k, v, cache_k, cache_v, rel_table, *, batch, seq):
    n_cache = cache_k.shape[1]
    diag, _ = _bias_diagonals(rel_table, seq, n_cache + seq, n_cache)
    new = pl.BlockSpec((1, seq, A_WIDTH), lambda b: (b, 0, 0))
    old = pl.BlockSpec((1, n_cache, A_WIDTH), lambda b: (b, 0, 0))
    out = pl.pallas_call(
        _attn_sample_kernel,
        grid=(batch,),
        in_specs=[new, new, new, old, old, _const_spec(diag.shape)],
        out_specs=new,
        out_shape=jax.ShapeDtypeStruct((batch, seq, A_WIDTH), BF16),
        scratch_shapes=[pltpu.VMEM((n_cache + seq, A_WIDTH), BF16),
                        pltpu.VMEM((n_cache + seq, A_WIDTH), BF16),
                        pltpu.VMEM((HEADS, seq, n_cache + seq), F32)],
        compiler_params=_params(1), name="attn_sample",
    )(q.reshape(batch, seq, A_WIDTH), k.reshape(batch, seq, A_WIDTH), v.reshape(batch, seq, A_WIDTH),
      cache_k.reshape(batch, n_cache, A_WIDTH), cache_v.reshape(batch, n_cache, A_WIDTH), diag)
    return out.reshape(batch * seq, A_WIDTH)


def _first_index(hit, iota, axis, limit):
    return jnp.min(jnp.where(hit, iota, limit), axis=axis, keepdims=True)


def _route(scores, sel):
    n_tok = scores.shape[1]
    neg = -jnp.inf
    grouped = sel.reshape(N_GROUPS, GROUP_SIZE, n_tok)
    member = lax.broadcasted_iota(jnp.int32, grouped.shape, 1)
    best = jnp.max(grouped, axis=1, keepdims=True)
    first = _first_index(grouped == best, member, 1, GROUP_SIZE)
    second = jnp.max(jnp.where(member == first, neg, grouped), axis=1, keepdims=True)
    group_score = best + second

    group_id = lax.broadcasted_iota(jnp.int32, group_score.shape, 0)
    keep = jnp.zeros(group_score.shape, F32)
    for _ in range(TOPK_GROUPS):
        top = jnp.max(group_score, axis=0, keepdims=True)
        hit = group_id == _first_index(group_score == top, group_id, 0, N_GROUPS)
        keep = jnp.where(hit, 1.0, keep)
        group_score = jnp.where(hit, neg, group_score)
    keep = jnp.broadcast_to(keep, grouped.shape).reshape(N_EXPERTS, n_tok)

    cand = jnp.where(keep > 0.0, sel, neg)
    expert_id = lax.broadcasted_iota(jnp.int32, cand.shape, 0)
    ids, weights = [], []
    for _ in range(TOP_K):
        top = jnp.max(cand, axis=0, keepdims=True)
        first = _first_index(cand == top, expert_id, 0, N_EXPERTS)
        hit = expert_id == first
        ids.append(first)
        weights.append(jnp.sum(jnp.where(hit, scores, 0.0), axis=0, keepdims=True))
        cand = jnp.where(hit, neg, cand)
    ids = jnp.concatenate(ids, axis=0)
    weights = jnp.concatenate(weights, axis=0)
    weights = weights / jnp.sum(weights, axis=0, keepdims=True) * ROUTED_SCALE
    return ids, weights


def _store_row_major_tiles(flat_ref, x):
    rows = x.shape[0]
    for j in range(D_TILES):
        flat_ref[pl.ds(j, rows, stride=D_TILES), :] = x[:, j * LANES:(j + 1) * LANES]


def _load_row_major_tiles(flat_ref, rows):
    return jnp.concatenate(
        [flat_ref[pl.ds(j, rows, stride=D_TILES), :] for j in range(D_TILES)], axis=1)


def _split_bf16(w):
    hi = w.astype(BF16)
    return jnp.concatenate([hi, (w - hi.astype(F32)).astype(BF16)], axis=0)


def _mix_kernel(x_ref, oa_ref, ob_ref, ga_ref, gb_ref, lng_ref, lnb_ref, wa32_ref, wb32_ref, wo32_ref,
                g1_ref, b1_ref, rw_ref, x1_ref, x1t_ref, scores_ref, wa_ref, wb_ref, wo_ref, *, alpha):
    _cast_weights_once([(wa32_ref, wa_ref), (wb32_ref, wb_ref), (wo32_ref, wo_ref)])
    xn = _layer_norm(x_ref[...], lng_ref[...], lnb_ref[...])
    mix = (ga_ref[...].astype(F32) * _dot(oa_ref[...], wa_ref[...])
           + gb_ref[...].astype(F32) * _dot(ob_ref[...], wb_ref[...]))
    x1 = _layer_norm(alpha * xn + _dot(mix.astype(BF16), wo_ref[...]), g1_ref[...], b1_ref[...])
    x1_ref[...] = x1
    _store_row_major_tiles(x1t_ref, x1)
    x1_hi = x1.astype(BF16)
    x1_lo = (x1 - x1_hi.astype(F32)).astype(BF16)
    by_hi = _dot_nt(rw_ref[...], x1_hi)
    logits = by_hi[:N_EXPERTS] + by_hi[N_EXPERTS:] + _dot_nt(rw_ref[:N_EXPERTS, :], x1_lo)
    scores_ref[...] = jax.nn.sigmoid(logits)


def _mix(x2d, oa, ob, ga, gb, ln_g, ln_b, wa, wb, wo, g1, b1, rw_t, *, alpha):
    n_rows = x2d.shape[0]
    tile = ROW_TILE
    row = lambda width: pl.BlockSpec((tile, width), lambda i: (i, 0))
    return pl.pallas_call(
        functools.partial(_mix_kernel, alpha=alpha),
        grid=(n_rows // tile,),
        in_specs=[row(D_MODEL), row(A_WIDTH), row(MLP_WIDTH), row(D_MODEL), row(D_MODEL),
                  _const_spec((1, D_MODEL)), _const_spec((1, D_MODEL)),
                  _resident_spec(wa.shape), _resident_spec(wb.shape), _resident_spec(wo.shape),
                  _const_spec((1, D_MODEL)), _const_spec((1, D_MODEL)),
                  _const_spec(rw_t.shape)],
        out_specs=(row(D_MODEL), pl.BlockSpec((tile * D_TILES, LANES), lambda i: (i, 0)),
                   pl.BlockSpec((N_EXPERTS, tile), lambda i: (0, i))),
        out_shape=(jax.ShapeDtypeStruct((n_rows, D_MODEL), F32),
                   jax.ShapeDtypeStruct((n_rows * D_TILES, LANES), F32),
                   jax.ShapeDtypeStruct((N_EXPERTS, n_rows), F32)),
        scratch_shapes=_bf16_scratch(wa, wb, wo), compiler_params=_params(1), name="mix",
    )(x2d, oa, ob, ga, gb, ln_g, ln_b, wa, wb, wo, g1, b1, rw_t)


def _route_kernel(scores_ref, rb_ref, ids_ref, wts_ref):
    scores = scores_ref[...]
    ids, weights = _route(scores, scores + rb_ref[...])
    ids_ref[...] = ids
    wts_ref[...] = weights


def _route_call(scores, rb):
    n_tok = scores.shape[1]
    tile = min(ROUTE_TILE, n_tok)
    col = lambda rows: pl.BlockSpec((rows, tile), lambda i: (0, i))
    return pl.pallas_call(
        _route_kernel,
        grid=(n_tok // tile,),
        in_specs=[col(N_EXPERTS), _const_spec(rb.shape)],
        out_specs=(col(TOP_K), col(TOP_K)),
        out_shape=(jax.ShapeDtypeStruct((TOP_K, n_tok), jnp.int32),
                   jax.ShapeDtypeStruct((TOP_K, n_tok), F32)),
        compiler_params=_params(1), name="route",
    )(scores, rb)


def _moe_kernel(rows_ref, start_ref, x_ref, wts_ref, w1_ref, w3_ref, w2_ref, *rest, n_rows, emit_bf16,
                experts_per_step):
    gat_ref, y_full_ref, y_half_ref, pending_ref = rest[-4:]
    if emit_bf16:
        out_ref, w1b_ref, w3b_ref, w2b_ref = rest[-8:-4]
        w1b_ref[...] = w1_ref[...].astype(BF16)
        w3b_ref[...] = w3_ref[...].astype(BF16)
        w2b_ref[...] = w2_ref[...].astype(BF16)
        w1_ref, w3_ref, w2_ref = w1b_ref, w3b_ref, w2b_ref
    else:
        out_ref = rest[-5]
    acc = out_ref.at[0]
    step = pl.program_id(0)
    half = n_rows // 2
    y_refs = {n_rows: y_full_ref, half: y_half_ref}

    @pl.when(step == 0)
    def _():
        out_ref[...] = jnp.zeros(out_ref.shape, out_ref.dtype)
        y_full_ref[...] = jnp.zeros(y_full_ref.shape, y_full_ref.dtype)
        y_half_ref[...] = jnp.zeros(y_half_ref.shape, y_half_ref.dtype)
        pending_ref[0] = 0
        pending_ref[1] = 0

    def tile_at(ref, row8):
        return ref.at[pl.ds(pl.multiple_of(row8, SUBLANES), SUBLANES), :]

    def scatter_add(base, n):
        y_ref = y_refs[n]
        for m0 in range(0, n, SCATTER_BATCH):
            updates = []
            for m in range(m0, m0 + SCATTER_BATCH):
                dst = tile_at(acc, rows_ref[base + m])
                updates.append((dst, dst[...] + y_ref[m * SUBLANES:(m + 1) * SUBLANES, :]))
            for dst, val in updates:
                dst[...] = val

    def run_block(j, first, n, nominal, begin, pending):
        redo = nominal - begin
        base = first + begin + n_rows
        for m in range(n):
            gat_ref[m * SUBLANES:(m + 1) * SUBLANES, :] = tile_at(x_ref, rows_ref[base + m])[...]
        xb = _load_row_major_tiles(gat_ref, n).astype(BF16)
        act = jax.nn.silu(_dot(xb, w1_ref[j])) * _dot(xb, w3_ref[j])
        y = _dot(act.astype(BF16), w2_ref[j])

        w_rows = pl.cdiv(n, LANES) + 1
        row_id = lax.broadcasted_iota(jnp.int32, (n, w_rows * LANES), 0)
        lane_id = lax.broadcasted_iota(jnp.int32, (n, w_rows * LANES), 1)
        w_row = base // LANES
        span = jnp.concatenate([wts_ref[w_row + r] for r in range(w_rows)], axis=1)
        picked = jnp.where((lane_id == row_id + base % LANES) & (row_id >= redo),
                           jnp.broadcast_to(span, lane_id.shape), 0.0)
        ys = y * jnp.sum(picked, axis=1, keepdims=True)
        scatter_add(pending, n)
        _store_row_major_tiles(y_refs[n], ys)
        return base

    def one_expert(j):
        e = step * experts_per_step + j
        first = start_ref[e]
        count = start_ref[e + 1] - first
        full_blocks = count // n_rows
        rest_rows = count - full_blocks * n_rows
        main_blocks = full_blocks + (rest_rows > half).astype(jnp.int32)

        def main_block(b, pending):
            nominal = b * n_rows
            return run_block(j, first, n_rows, nominal, jnp.minimum(nominal, count - n_rows), pending)

        pending_ref[0] = lax.fori_loop(0, main_blocks, main_block, pending_ref[0])

        @pl.when((rest_rows > 0) & (rest_rows <= half))
        def _():
            pending_ref[1] = run_block(j, first, half, full_blocks * n_rows, count - half, pending_ref[1])

    for j in range(experts_per_step):
        one_expert(j)

    @pl.when(step == pl.num_programs(0) - 1)
    def _():
        scatter_add(pending_ref[0], n_rows)
        scatter_add(pending_ref[1], half)


def _moe(ids, weights, x_tiles, w1, w3, w2):
    n_tok = ids.shape[1]
    g_tok = min(MOE_GROUP_TOKENS, n_tok)
    n_groups = n_tok // g_tok
    assert n_groups * g_tok == n_tok
    per_group = g_tok * TOP_K
    tok = lax.broadcasted_iota(jnp.int32, ids.shape, 1)
    key = ((tok // g_tok) * N_EXPERTS + ids) * g_tok + tok % g_tok
    key, wts = lax.sort((key.reshape(-1), weights.reshape(-1)), num_keys=1)
    rows8 = ((key % g_tok) * D_TILES).reshape(n_groups, per_group)
    wts = wts.reshape(n_groups, per_group)
    experts = jnp.arange(N_EXPERTS, dtype=jnp.int32)[None, None, :, None]
    counts = jnp.sum(ids.reshape(TOP_K, n_groups, 1, g_tok) == experts, axis=(0, 3), dtype=jnp.int32)
    start = jnp.concatenate([jnp.zeros((n_groups, 1), jnp.int32), jnp.cumsum(counts, axis=1)], axis=1)
    many_rows = per_group // N_EXPERTS >= 2 * MOE_BLOCK_ROWS
    n_rows = MOE_BLOCK_ROWS if many_rows else MOE_SMALL_BLOCK_ROWS
    pad = ((0, 0), (n_rows, n_rows + LANES))
    rows8 = jnp.pad(rows8, pad)
    wts = jnp.pad(wts, pad).reshape(n_groups, -1, 1, LANES)
    out = None
    for g in range(n_groups):
        out, (w1, w3, w2) = _moe_group_call(g, n_groups, g_tok, n_rows, rows8[g], start[g], x_tiles, wts[g],
                                            w1, w3, w2, out)
    return out, (w1, w3, w2)


def _moe_group_call(group, n_groups, g_tok, n_rows, rows8, start, x_tiles, wts, w1, w3, w2, prev):
    g_rows = g_tok * D_TILES
    assert x_tiles.shape[0] == n_groups * g_rows
    emit_bf16 = w1.dtype != BF16
    assert not (emit_bf16 and prev is not None)
    out_block = (1, g_rows, LANES)
    eps = 1 if emit_bf16 else MOE_EXPERTS_PER_STEP[n_rows]
    per_expert = lambda w: pl.BlockSpec((eps,) + w.shape[1:], lambda s, *_: (s, 0, 0))
    in_specs = [
        pl.BlockSpec((g_rows, LANES), lambda e, *_: (group, 0), pipeline_mode=pl.Buffered(1)),
        pl.BlockSpec(wts.shape, lambda e, *_: (0, 0, 0)),
        per_expert(w1), per_expert(w3), per_expert(w2),
    ]
    args = [rows8, start, x_tiles, wts, w1, w3, w2]
    aliases = {}
    if prev is not None:
        in_specs.append(pl.BlockSpec(memory_space=pl.ANY))
        aliases = {len(args): 0}
        args.append(prev)
    out_specs = [pl.BlockSpec(out_block, lambda e, *_: (group, 0, 0), pipeline_mode=pl.Buffered(1))]
    out_shape = [jax.ShapeDtypeStruct((n_groups,) + out_block[1:], F32)]
    if emit_bf16:
        out_specs += [per_expert(w1), per_expert(w3), per_expert(w2)]
        out_shape += [jax.ShapeDtypeStruct(w.shape, BF16) for w in (w1, w3, w2)]
    results = pl.pallas_call(
        functools.partial(_moe_kernel, n_rows=n_rows, emit_bf16=emit_bf16, experts_per_step=eps),
        grid_spec=pltpu.PrefetchScalarGridSpec(
            num_scalar_prefetch=2, grid=(N_EXPERTS // eps,), in_specs=in_specs, out_specs=out_specs,
            scratch_shapes=[pltpu.VMEM((n_rows * D_TILES, LANES), F32),
                            pltpu.VMEM((n_rows * D_TILES, LANES), F32),
                            pltpu.VMEM((n_rows // 2 * D_TILES, LANES), F32),
                            pltpu.SMEM((2,), jnp.int32)]),
        out_shape=out_shape,
        input_output_aliases=aliases,
        compiler_params=_params(1), name="moe",
    )(*args)
    return results[0], (tuple(results[1:]) if emit_bf16 else (w1, w3, w2))


def _final_kernel(x1_ref, routed_ref, p_ref, s1_32_ref, s3_32_ref, s2_32_ref, g2_ref, b2_ref, wg32_ref,
                  wp32_ref, o_ref, s1_ref, s3_ref, s2_ref, wg_ref, wp_ref, *, alpha):
    _cast_weights_once([(s1_32_ref, s1_ref), (s3_32_ref, s3_ref), (s2_32_ref, s2_ref),
                        (wg32_ref, wg_ref), (wp32_ref, wp_ref)])
    x1 = x1_ref[...]
    xb = x1.astype(BF16)
    act = jax.nn.silu(_dot(xb, s1_ref[...])) * _dot(xb, s3_ref[...])
    ff = _load_row_major_tiles(routed_ref.at[0], x1.shape[0]) + _dot(act.astype(BF16), s2_ref[...])
    x2 = _layer_norm(alpha * x1 + ff, g2_ref[...], b2_ref[...])
    gate = jax.nn.sigmoid(_dot(x2.astype(BF16), wg_ref[...]))
    o_ref[...] = x2 + gate * _dot(p_ref[...].astype(BF16), wp_ref[...])


def _final(x1, routed, p2d, s1, s3, s2, g2, b2, wg, wp, *, alpha):
    n_rows = x1.shape[0]
    tile = ROW_TILE
    row = lambda width: pl.BlockSpec((tile, width), lambda i: (i, 0))
    tiles_per_group = routed.shape[1] // D_TILES // tile
    routed_spec = pl.BlockSpec((1, tile * D_TILES, LANES),
                               lambda i: (i // tiles_per_group, i % tiles_per_group, 0))
    return pl.pallas_call(
        functools.partial(_final_kernel, alpha=alpha),
        grid=(n_rows // tile,),
        in_specs=[row(D_MODEL), routed_spec,
                  row(p2d.shape[1]),
                  _resident_spec(s1.shape), _resident_spec(s3.shape), _resident_spec(s2.shape),
                  _const_spec((1, D_MODEL)), _const_spec((1, D_MODEL)),
                  _resident_spec(wg.shape), _resident_spec(wp.shape)],
        out_specs=row(D_MODEL),
        out_shape=jax.ShapeDtypeStruct((n_rows, D_MODEL), F32),
        scratch_shapes=_bf16_scratch(s1, s3, s2, wg, wp), compiler_params=_params(1), name="final",
    )(x1, routed, p2d, s1, s3, s2, g2, b2, wg, wp)


def _trunk_layer(x, p, cache_k, cache_v, lw, *, alpha, first_layer_ln):
    batch, seq, d = x.shape
    n_rows = batch * seq
    x2d = x.reshape(n_rows, d)
    prompt = cache_k is None
    mlp_len = min(seq, MLP_CHUNK)
    keep = min(PAST_CHUNKS * CHUNK, seq) if prompt else seq
    ln_g, ln_b = first_layer_ln

    w_s = (lw["mlp_w_s"][:, :mlp_len, :mlp_len]
           * jnp.tril(jnp.ones((mlp_len, mlp_len), F32))).astype(BF16)
    b_s = jnp.repeat(lw["mlp_b_s"][:, :mlp_len].T, MLP_WIDTH // MLP_GROUPS, axis=1)
    q, k, v, kf, vf, vn, ga, gb, ob = _proj(
        x2d, ln_g, ln_b, lw["w_in"], lw["mlp_ln_g"][None], lw["mlp_ln_b"][None],
        w_s, b_s, seq=seq, mlp_len=mlp_len, keep_rows=keep)
    if prompt:
        oa = _attn_prompt(q, k, v, lw["attn_rel_bias"], batch=batch, seq=seq)
    else:
        oa = _attn_sample(q, k, v, cache_k, cache_v, lw["attn_rel_bias"], batch=batch, seq=seq)
    x1, x1_tiles, scores = _mix(
        x2d, oa, ob, ga, gb, ln_g, ln_b, lw["w_branch_a"], lw["w_branch_b"], lw["w_out"],
        lw["ln1_g"][None], lw["ln1_b"][None], _split_bf16(lw["router_w"].T), alpha=alpha)
    ids, weights = _route_call(scores, lw["router_bias"][:, None])
    routed, expert_bf16 = _moe(ids, weights, x1_tiles, lw["exp_w1"], lw["exp_w3"], lw["exp_w2"])
    y = _final(x1, routed, p.reshape(n_rows, -1), lw["shared_w1"], lw["shared_w3"], lw["shared_w2"],
               lw["ln2_g"][None], lw["ln2_b"][None], lw["ple_w_gate"], lw["ple_w_proj"], alpha=alpha)
    new_k = kf.reshape(batch, keep, HEADS, HEAD_DIM)
    new_v = vf.reshape(batch, keep, HEADS, HEAD_DIM)
    new_mlp_v = vn.reshape(batch, mlp_len, MLP_WIDTH)
    return y.reshape(batch, seq, d), new_k, new_v, new_mlp_v, expert_bf16


def kernel(x_prompt, x_sample, cache_attn_k, cache_attn_v, p_prompt, p_sample, ln_in_g, ln_in_b, w_in, attn_rel_bias, mlp_ln_g, mlp_ln_b, mlp_w_s, mlp_b_s, w_branch_a, w_branch_b, w_out, ln1_g, ln1_b, router_w, router_bias, exp_w1, exp_w3, exp_w2, shared_w1, shared_w3, shared_w2, ln2_g, ln2_b, ple_w_gate, ple_w_proj):
    depth = w_in.shape[0]
    assert depth == 1, "the fused proj/mix kernels apply LayerNorm_in themselves: single layer only"
    alpha = (2 * depth) ** 0.25
    stacked = dict(w_in=w_in, attn_rel_bias=attn_rel_bias, mlp_ln_g=mlp_ln_g, mlp_ln_b=mlp_ln_b,
                   mlp_w_s=mlp_w_s, mlp_b_s=mlp_b_s, w_branch_a=w_branch_a, w_branch_b=w_branch_b,
                   w_out=w_out, ln1_g=ln1_g, ln1_b=ln1_b, router_w=router_w, router_bias=router_bias,
                   exp_w1=exp_w1, exp_w3=exp_w3, exp_w2=exp_w2, shared_w1=shared_w1,
                   shared_w3=shared_w3, shared_w2=shared_w2, ln2_g=ln2_g, ln2_b=ln2_b,
                   ple_w_gate=ple_w_gate, ple_w_proj=ple_w_proj)
    lw = {name: value[0] for name, value in stacked.items()}
    ln_in =(ln_in_g[None], ln_in_b[None])
    yp, kp, vp, mp, expert_bf16 = _trunk_layer(x_prompt, p_prompt[0], None, None, lw, alpha=alpha,
                                               first_layer_ln=ln_in)
    lw = dict(lw, **dict(zip(("exp_w1", "exp_w3", "exp_w2"), expert_bf16)))
    ys, ks, vs, ms, _ = _trunk_layer(x_sample, p_sample[0], cache_attn_k[0], cache_attn_v[0], lw,
                                     alpha=alpha, first_layer_ln=ln_in)
    return (yp, ys, kp[None], vp[None], ks[None], vs[None], mp[None], ms[None])
```

```python
import functools

import jax
import jax.numpy as jnp
import numpy as np
from jax import lax
from jax.experimental import pallas as pl
from jax.experimental.pallas import tpu as pltpu

F32 = jnp.float32
BF16 = jnp.bfloat16

D_MODEL = 1024
CHUNK = 64
HEADS = 8
HEAD_DIM = 64
A_WIDTH = HEADS * HEAD_DIM
PAST_CHUNKS = 8
MAX_REL = 256
MLP_CHUNK = 128
MLP_GROUPS = 8
MLP_WIDTH = 512
N_EXPERTS = 64
TOP_K = 8
N_GROUPS = 8
TOPK_GROUPS = 4
GROUP_SIZE = N_EXPERTS // N_GROUPS
EXPERT_DIM = 256
ROUTED_SCALE = 2.5
LN_EPS = 1e-5
MASKED = -1e30

LANES = 128
SUBLANES = 8
VMEM_LIMIT_BYTES = 56 * 1024 * 1024

ROW_TILE = 512
ROUTE_TILE = 2048
Q_CHUNKS = 4
MOE_GROUP_TOKENS = 4096
MOE_BLOCK_ROWS = 256
MOE_SMALL_BLOCK_ROWS = 128
MOE_EXPERTS_PER_STEP = {MOE_BLOCK_ROWS: 2, MOE_SMALL_BLOCK_ROWS: 4}
CAST_COLS = 512
SCATTER_BATCH = 8
D_TILES = D_MODEL // LANES


def _layer_norm(x, g, b):
    xc = x - jnp.mean(x, axis=-1, keepdims=True)
    var = jnp.mean(xc * xc, axis=-1, keepdims=True)
    return xc * lax.rsqrt(var + LN_EPS) * g + b


def _dot(a, b):
    return jnp.dot(a, b, preferred_element_type=F32)


def _dot_nt(a, b, precision=None):
    return lax.dot_general(a, b, (((1,), (1,)), ((), ())),
                           preferred_element_type=F32, precision=precision)


def _const_spec(shape):
    zeros = (0,) * len(shape)
    return pl.BlockSpec(shape, lambda *_: zeros)


def _resident_spec(shape):
    zeros = (0,) * len(shape)
    return pl.BlockSpec(shape, lambda *_: zeros, pipeline_mode=pl.Buffered(1))


def _cast_weights_once(pairs):
    @pl.when(pl.program_id(0) == 0)
    def _():
        for src, dst in pairs:
            cols = src.shape[1]
            for c in range(0, cols, CAST_COLS):
                dst[:, c:min(c + CAST_COLS, cols)] = src[:, c:min(c + CAST_COLS, cols)].astype(BF16)


def _bf16_scratch(*weights):
    return [pltpu.VMEM(w.shape, BF16) for w in weights]


def _params(n_axes):
    return pltpu.CompilerParams(dimension_semantics=("arbitrary",) * n_axes,
                                vmem_limit_bytes=VMEM_LIMIT_BYTES)


def _proj_kernel(x_ref, lng_ref, lnb_ref, w32_ref, mg_ref, mb_ref, ws_ref, bs_ref,
                 q_ref, k_ref, v_ref, kf_ref, vf_ref, vn_ref, ga_ref, gb_ref, ob_ref, w_ref,
                 *, mlp_len, vn_rows):
    _cast_weights_once([(w32_ref, w_ref)])
    rows = x_ref.shape[0]
    xn = _layer_norm(x_ref[...], lng_ref[...], lnb_ref[...]).astype(BF16)

    def section(lo, width):
        return _dot(xn, w_ref[:, lo:lo + width])

    q = section(0, A_WIDTH)
    q_ref[...] = (q * (HEAD_DIM ** -0.5)).astype(BF16)
    k = section(A_WIDTH, A_WIDTH)
    k_ref[...] = k.astype(BF16)
    kf_ref[...] = k
    v = section(2 * A_WIDTH, A_WIDTH)
    v_ref[...] = v.astype(BF16)
    vf_ref[...] = v
    base = 3 * A_WIDTH
    un = jax.nn.gelu(section(base, MLP_WIDTH))
    vn = _layer_norm(jax.nn.gelu(section(base + MLP_WIDTH, MLP_WIDTH)), mg_ref[...], mb_ref[...])
    vn_ref[...] = vn[rows - vn_rows:, :]
    base += 2 * MLP_WIDTH
    ga_ref[...] = jax.nn.sigmoid(section(base, D_MODEL)).astype(BF16)
    gb_ref[...] = jax.nn.sigmoid(section(base + D_MODEL, D_MODEL)).astype(BF16)

    vnb = vn.astype(BF16)
    low_half = lax.broadcasted_iota(jnp.int32, (mlp_len, LANES), 1) < (MLP_WIDTH // MLP_GROUPS)
    for c in range(rows // mlp_len):
        r0 = c * mlp_len
        for p in range(MLP_GROUPS // 2):
            c0 = p * LANES
            slab = vnb[r0:r0 + mlp_len, c0:c0 + LANES]
            mixed = jnp.where(low_half, _dot(ws_ref[2 * p], slab), _dot(ws_ref[2 * p + 1], slab))
            gated = un[r0:r0 + mlp_len, c0:c0 + LANES] * (mixed + bs_ref[:, c0:c0 + LANES])
            ob_ref[r0:r0 + mlp_len, c0:c0 + LANES] = gated.astype(BF16)


def _proj(x2d, ln_g, ln_b, w_in, mlp_g, mlp_b, w_s, b_s, *, seq, mlp_len, keep_rows):
    n_rows = x2d.shape[0]
    tile = ROW_TILE
    assert n_rows % tile == 0 and tile % mlp_len == 0
    n_seq = n_rows // seq
    if seq >= tile:
        assert seq % tile == 0 and keep_rows == tile
        per_seq = seq // tile
        kv_rows, vn_rows = n_seq * tile, mlp_len
        kv_map = lambda i: (i // per_seq, 0)
    else:
        assert keep_rows == seq and mlp_len == seq
        kv_rows, vn_rows = n_rows, tile
        kv_map = lambda i: (i, 0)
    row = lambda width: pl.BlockSpec((tile, width), lambda i: (i, 0))
    n_in = w_in.shape[1]
    out_shape = (
        jax.ShapeDtypeStruct((n_rows, A_WIDTH), BF16),
        jax.ShapeDtypeStruct((n_rows, A_WIDTH), BF16),
        jax.ShapeDtypeStruct((n_rows, A_WIDTH), BF16),
        jax.ShapeDtypeStruct((kv_rows, A_WIDTH), F32),
        jax.ShapeDtypeStruct((kv_rows, A_WIDTH), F32),
        jax.ShapeDtypeStruct((kv_rows // tile * vn_rows, MLP_WIDTH), F32),
        jax.ShapeDtypeStruct((n_rows, D_MODEL), BF16),
        jax.ShapeDtypeStruct((n_rows, D_MODEL), BF16),
        jax.ShapeDtypeStruct((n_rows, MLP_WIDTH), BF16),
    )
    out_specs = (
        row(A_WIDTH), row(A_WIDTH), row(A_WIDTH),
        pl.BlockSpec((tile, A_WIDTH), kv_map), pl.BlockSpec((tile, A_WIDTH), kv_map),
        pl.BlockSpec((vn_rows, MLP_WIDTH), kv_map),
        row(D_MODEL), row(D_MODEL), row(MLP_WIDTH),
    )
    in_specs = [
        row(D_MODEL), _const_spec((1, D_MODEL)), _const_spec((1, D_MODEL)),
        _resident_spec((D_MODEL, n_in)), _const_spec((1, MLP_WIDTH)), _const_spec((1, MLP_WIDTH)),
        _const_spec((MLP_GROUPS, mlp_len, mlp_len)), _const_spec((mlp_len, MLP_WIDTH)),
    ]
    return pl.pallas_call(
        functools.partial(_proj_kernel, mlp_len=mlp_len, vn_rows=vn_rows),
        grid=(n_rows // tile,), in_specs=in_specs, out_specs=out_specs, out_shape=out_shape,
        scratch_shapes=_bf16_scratch(w_in), compiler_params=_params(1), name="proj",
    )(x2d, ln_g, ln_b, w_in, mlp_g, mlp_b, w_s, b_s)


def _attend_heads(q, keys, values, pair_bias, o_ref):
    n_q = q.shape[0]
    low_half = lax.broadcasted_iota(jnp.int32, (n_q, LANES), 1) < HEAD_DIM
    zero = jnp.zeros((), BF16)
    for p in range(HEADS // 2):
        c0 = p * LANES
        q2, k2, v2 = q[:, c0:c0 + LANES], keys[:, c0:c0 + LANES], values[:, c0:c0 + LANES]
        qs = jnp.concatenate([jnp.where(low_half, q2, zero), jnp.where(low_half, zero, q2)], axis=0)
        s = _dot_nt(qs, k2) + pair_bias(p).reshape(2 * n_q, keys.shape[0])
        e = jnp.exp(s - jnp.max(s, axis=-1, keepdims=True))
        denom = jnp.sum(e, axis=-1, keepdims=True)
        o = _dot(e.astype(BF16), v2) / denom
        o_ref[:, c0:c0 + LANES] = jnp.where(low_half, o[:n_q], o[n_q:]).astype(o_ref.dtype)


def _fill_toeplitz_bias(bias_ref, diag_ref, first_key_chunk=None):
    _, n_q, n_k = bias_ref.shape
    if first_key_chunk is not None:
        q_chunk = lax.broadcasted_iota(jnp.int32, (n_q, n_k), 0) // CHUNK
        k_chunk = lax.broadcasted_iota(jnp.int32, (n_q, n_k), 1) // CHUNK + first_key_chunk
        in_band = (k_chunk <= q_chunk) & (k_chunk >= q_chunk - PAST_CHUNKS)
    for h in range(bias_ref.shape[0]):
        rows = jnp.broadcast_to(diag_ref[h:h + 1, :], (n_q, diag_ref.shape[1]))
        rows = pltpu.roll(rows, 0, 1, stride=1, stride_axis=0)[:, :n_k]
        bias_ref[h] = rows if first_key_chunk is None else jnp.where(in_band, rows, MASKED)


def _attn_prompt_kernel(q_ref, k_ref, v_ref, diag_ref, o_ref, bias_ref, *, window, lead_steps):
    n_q = q_ref.shape[0]

    @pl.when((pl.program_id(0) == 0) & (pl.program_id(1) == 0))
    def _():
        _fill_toeplitz_bias(bias_ref, diag_ref, first_key_chunk=-lead_steps * (n_q // CHUNK))

    step = jnp.minimum(pl.program_id(1), lead_steps)
    start = pl.multiple_of((pl.program_id(1) - step) * n_q, n_q)
    shift = pl.multiple_of((lead_steps - step) * n_q, n_q)
    keys = k_ref[pl.ds(start, window), :]
    values = v_ref[pl.ds(start, window), :]
    _attend_heads(q_ref[...], keys, values,
                  lambda p: bias_ref[2 * p:2 * p + 2, :, pl.ds(shift, window)], o_ref)


def _bias_diagonals(rel_table, n_neg, n_pos, key_offset):
    period = pl.next_power_of_2(n_neg + n_pos)
    m = np.arange(period)
    diag = np.where(m < n_pos, m, m - period)
    return rel_table[:, np.clip(diag - key_offset, -MAX_REL, MAX_REL) + MAX_REL].astype(F32), period


def _attn_prompt(q, k, v, rel_table, *, batch, seq):
    n_q = Q_CHUNKS * CHUNK
    window = (Q_CHUNKS + PAST_CHUNKS) * CHUNK
    assert PAST_CHUNKS % Q_CHUNKS == 0 and seq >= window
    lead = PAST_CHUNKS // Q_CHUNKS
    steps = seq // n_q
    n_bias = lead * n_q + window
    diag, _ = _bias_diagonals(rel_table, n_q, n_bias, PAST_CHUNKS * CHUNK)
    qspec = pl.BlockSpec((n_q, A_WIDTH), lambda b, i: (b * steps + i, 0))
    kvspec = pl.BlockSpec((seq, A_WIDTH), lambda b, i: (b, 0))
    return pl.pallas_call(
        functools.partial(_attn_prompt_kernel, window=window, lead_steps=lead),
        grid=(batch, steps),
        in_specs=[qspec, kvspec, kvspec, _const_spec(diag.shape)],
        out_specs=qspec,
        out_shape=jax.ShapeDtypeStruct((batch * seq, A_WIDTH), BF16),
        scratch_shapes=[pltpu.VMEM((HEADS, n_q, n_bias), F32)],
        compiler_params=_params(2), name="attn_prompt",
    )(q, k, v, diag)


def _attn_sample_kernel(q_ref, k_ref, v_ref, ck_ref, cv_ref, diag_ref, o_ref, kk_ref, vv_ref, bias_ref):
    @pl.when(pl.program_id(0) == 0)
    def _():
        _fill_toeplitz_bias(bias_ref, diag_ref)

    n_cache = ck_ref.shape[1]
    n_new = k_ref.shape[1]
    kk_ref[0:n_cache, :] = ck_ref[0].astype(BF16)
    kk_ref[n_cache:n_cache + n_new, :] = k_ref[0]
    vv_ref[0:n_cache, :] = cv_ref[0].astype(BF16)
    vv_ref[n_cache:n_cache + n_new, :] = v_ref[0]
    _attend_heads(q_ref[0], kk_ref[...], vv_ref[...], lambda p: bias_ref[2 * p:2 * p + 2], o_ref.at[0])


def _attn_sample(q, k, v, cache_k, cache_v, rel_table, *, batch, seq):
    n_cache = cache_k.shape[1]
    diag, _ = _bias_diagonals(rel_table, seq, n_cache + seq, n_cache)
    new = pl.BlockSpec((1, seq, A_WIDTH), lambda b: (b, 0, 0))
    old = pl.BlockSpec((1, n_cache, A_WIDTH), lambda b: (b, 0, 0))
    out = pl.pallas_call(
        _attn_sample_kernel,
        grid=(batch,),
        in_specs=[new, new, new, old, old, _const_spec(diag.shape)],
        out_specs=new,
        out_shape=jax.ShapeDtypeStruct((batch, seq, A_WIDTH), BF16),
        scratch_shapes=[pltpu.VMEM((n_cache + seq, A_WIDTH), BF16),
                        pltpu.VMEM((n_cache + seq, A_WIDTH), BF16),
                        pltpu.VMEM((HEADS, seq, n_cache + seq), F32)],
        compiler_params=_params(1), name="attn_sample",
    )(q.reshape(batch, seq, A_WIDTH), k.reshape(batch, seq, A_WIDTH), v.reshape(batch, seq, A_WIDTH),
      cache_k.reshape(batch, n_cache, A_WIDTH), cache_v.reshape(batch, n_cache, A_WIDTH), diag)
    return out.reshape(batch * seq, A_WIDTH)


def _first_index(hit, iota, axis, limit):
    return jnp.min(jnp.where(hit, iota, limit), axis=axis, keepdims=True)


def _route(scores, sel):
    n_tok = scores.shape[1]
    neg = -jnp.inf
    grouped = sel.reshape(N_GROUPS, GROUP_SIZE, n_tok)
    member = lax.broadcasted_iota(jnp.int32, grouped.shape, 1)
    best = jnp.max(grouped, axis=1, keepdims=True)
    first = _first_index(grouped == best, member, 1, GROUP_SIZE)
    second = jnp.max(jnp.where(member == first, neg, grouped), axis=1, keepdims=True)
    group_score = best + second

    group_id = lax.broadcasted_iota(jnp.int32, group_score.shape, 0)
    keep = jnp.zeros(group_score.shape, F32)
    for _ in range(TOPK_GROUPS):
        top = jnp.max(group_score, axis=0, keepdims=True)
        hit = group_id == _first_index(group_score == top, group_id, 0, N_GROUPS)
        keep = jnp.where(hit, 1.0, keep)
        group_score = jnp.where(hit, neg, group_score)
    keep = jnp.broadcast_to(keep, grouped.shape).reshape(N_EXPERTS, n_tok)

    cand = jnp.where(keep > 0.0, sel, neg)
    expert_id = lax.broadcasted_iota(jnp.int32, cand.shape, 0)
    ids, weights = [], []
    for _ in range(TOP_K):
        top = jnp.max(cand, axis=0, keepdims=True)
        first = _first_index(cand == top, expert_id, 0, N_EXPERTS)
        hit = expert_id == first
        ids.append(first)
        weights.append(jnp.sum(jnp.where(hit, scores, 0.0), axis=0, keepdims=True))
        cand = jnp.where(hit, neg, cand)
    ids = jnp.concatenate(ids, axis=0)
    weights = jnp.concatenate(weights, axis=0)
    weights = weights / jnp.sum(weights, axis=0, keepdims=True) * ROUTED_SCALE
    return ids, weights


def _store_row_major_tiles(flat_ref, x):
    rows = x.shape[0]
    for j in range(D_TILES):
        flat_ref[pl.ds(j, rows, stride=D_TILES), :] = x[:, j * LANES:(j + 1) * LANES]


def _load_row_major_tiles(flat_ref, rows):
    return jnp.concatenate(
        [flat_ref[pl.ds(j, rows, stride=D_TILES), :] for j in range(D_TILES)], axis=1)


def _split_bf16(w):
    hi = w.astype(BF16)
    return jnp.concatenate([hi, (w - hi.astype(F32)).astype(BF16)], axis=0)


def _mix_kernel(x_ref, oa_ref, ob_ref, ga_ref, gb_ref, lng_ref, lnb_ref, wa32_ref, wb32_ref, wo32_ref,
                g1_ref, b1_ref, rw_ref, x1_ref, x1t_ref, scores_ref, wa_ref, wb_ref, wo_ref, *, alpha):
    _cast_weights_once([(wa32_ref, wa_ref), (wb32_ref, wb_ref), (wo32_ref, wo_ref)])
    xn = _layer_norm(x_ref[...], lng_ref[...], lnb_ref[...])
    mix = (ga_ref[...].astype(F32) * _dot(oa_ref[...], wa_ref[...])
           + gb_ref[...].astype(F32) * _dot(ob_ref[...], wb_ref[...]))
    x1 = _layer_norm(alpha * xn + _dot(mix.astype(BF16), wo_ref[...]), g1_ref[...], b1_ref[...])
    x1_ref[...] = x1
    _store_row_major_tiles(x1t_ref, x1)
    x1_hi = x1.astype(BF16)
    x1_lo = (x1 - x1_hi.astype(F32)).astype(BF16)
    by_hi = _dot_nt(rw_ref[...], x1_hi)
    logits = by_hi[:N_EXPERTS] + by_hi[N_EXPERTS:] + _dot_nt(rw_ref[:N_EXPERTS, :], x1_lo)
    scores_ref[...] = jax.nn.sigmoid(logits)


def _mix(x2d, oa, ob, ga, gb, ln_g, ln_b, wa, wb, wo, g1, b1, rw_t, *, alpha):
    n_rows = x2d.shape[0]
    tile = ROW_TILE
    row = lambda width: pl.BlockSpec((tile, width), lambda i: (i, 0))
    return pl.pallas_call(
        functools.partial(_mix_kernel, alpha=alpha),
        grid=(n_rows // tile,),
        in_specs=[row(D_MODEL), row(A_WIDTH), row(MLP_WIDTH), row(D_MODEL), row(D_MODEL),
                  _const_spec((1, D_MODEL)), _const_spec((1, D_MODEL)),
                  _resident_spec(wa.shape), _resident_spec(wb.shape), _resident_spec(wo.shape),
                  _const_spec((1, D_MODEL)), _const_spec((1, D_MODEL)),
                  _const_spec(rw_t.shape)],
        out_specs=(row(D_MODEL), pl.BlockSpec((tile * D_TILES, LANES), lambda i: (i, 0)),
                   pl.BlockSpec((N_EXPERTS, tile), lambda i: (0, i))),
        out_shape=(jax.ShapeDtypeStruct((n_rows, D_MODEL), F32),
                   jax.ShapeDtypeStruct((n_rows * D_TILES, LANES), F32),
                   jax.ShapeDtypeStruct((N_EXPERTS, n_rows), F32)),
        scratch_shapes=_bf16_scratch(wa, wb, wo), compiler_params=_params(1), name="mix",
    )(x2d, oa, ob, ga, gb, ln_g, ln_b, wa, wb, wo, g1, b1, rw_t)


def _route_kernel(scores_ref, rb_ref, ids_ref, wts_ref):
    scores = scores_ref[...]
    ids, weights = _route(scores, scores + rb_ref[...])
    ids_ref[...] = ids
    wts_ref[...] = weights


def _route_call(scores, rb):
    n_tok = scores.shape[1]
    tile = min(ROUTE_TILE, n_tok)
    col = lambda rows: pl.BlockSpec((rows, tile), lambda i: (0, i))
    return pl.pallas_call(
        _route_kernel,
        grid=(n_tok // tile,),
        in_specs=[col(N_EXPERTS), _const_spec(rb.shape)],
        out_specs=(col(TOP_K), col(TOP_K)),
        out_shape=(jax.ShapeDtypeStruct((TOP_K, n_tok), jnp.int32),
                   jax.ShapeDtypeStruct((TOP_K, n_tok), F32)),
        compiler_params=_params(1), name="route",
    )(scores, rb)


def _moe_kernel(rows_ref, start_ref, x_ref, wts_ref, w1_ref, w3_ref, w2_ref, *rest, n_rows, emit_bf16,
                experts_per_step):
    gat_ref, y_full_ref, y_half_ref, pending_ref = rest[-4:]
    if emit_bf16:
        out_ref, w1b_ref, w3b_ref, w2b_ref = rest[-8:-4]
        w1b_ref[...] = w1_ref[...].astype(BF16)
        w3b_ref[...] = w3_ref[...].astype(BF16)
        w2b_ref[...] = w2_ref[...].astype(BF16)
        w1_ref, w3_ref, w2_ref = w1b_ref, w3b_ref, w2b_ref
    else:
        out_ref = rest[-5]
    acc = out_ref.at[0]
    step = pl.program_id(0)
    half = n_rows // 2
    y_refs = {n_rows: y_full_ref, half: y_half_ref}

    @pl.when(step == 0)
    def _():
        out_ref[...] = jnp.zeros(out_ref.shape, out_ref.dtype)
        y_full_ref[...] = jnp.zeros(y_full_ref.shape, y_full_ref.dtype)
        y_half_ref[...] = jnp.zeros(y_half_ref.shape, y_half_ref.dtype)
        pending_ref[0] = 0
        pending_ref[1] = 0

    def tile_at(ref, row8):
        return ref.at[pl.ds(pl.multiple_of(row8, SUBLANES), SUBLANES), :]

    def scatter_add(base, n):
        y_ref = y_refs[n]
        for m0 in range(0, n, SCATTER_BATCH):
            updates = []
            for m in range(m0, m0 + SCATTER_BATCH):
                dst = tile_at(acc, rows_ref[base + m])
                updates.append((dst, dst[...] + y_ref[m * SUBLANES:(m + 1) * SUBLANES, :]))
            for dst, val in updates:
                dst[...] = val

    def run_block(j, first, n, nominal, begin, pending):
        redo = nominal - begin
        base = first + begin + n_rows
        for m in range(n):
            gat_ref[m * SUBLANES:(m + 1) * SUBLANES, :] = tile_at(x_ref, rows_ref[base + m])[...]
        xb = _load_row_major_tiles(gat_ref, n).astype(BF16)
        act = jax.nn.silu(_dot(xb, w1_ref[j])) * _dot(xb, w3_ref[j])
        y = _dot(act.astype(BF16), w2_ref[j])

        w_rows = pl.cdiv(n, LANES) + 1
        row_id = lax.broadcasted_iota(jnp.int32, (n, w_rows * LANES), 0)
        lane_id = lax.broadcasted_iota(jnp.int32, (n, w_rows * LANES), 1)
        w_row = base // LANES
        span = jnp.concatenate([wts_ref[w_row + r] for r in range(w_rows)], axis=1)
        picked = jnp.where((lane_id == row_id + base % LANES) & (row_id >= redo),
                           jnp.broadcast_to(span, lane_id.shape), 0.0)
        ys = y * jnp.sum(picked, axis=1, keepdims=True)
        scatter_add(pending, n)
        _store_row_major_tiles(y_refs[n], ys)
        return base

    def one_expert(j):
        e = step * experts_per_step + j
        first = start_ref[e]
        count = start_ref[e + 1] - first
        full_blocks = count // n_rows
        rest_rows = count - full_blocks * n_rows
        main_blocks = full_blocks + (rest_rows > half).astype(jnp.int32)

        def main_block(b, pending):
            nominal = b * n_rows
            return run_block(j, first, n_rows, nominal, jnp.minimum(nominal, count - n_rows), pending)

        pending_ref[0] = lax.fori_loop(0, main_blocks, main_block, pending_ref[0])

        @pl.when((rest_rows > 0) & (rest_rows <= half))
        def _():
            pending_ref[1] = run_block(j, first, half, full_blocks * n_rows, count - half, pending_ref[1])

    for j in range(experts_per_step):
        one_expert(j)

    @pl.when(step == pl.num_programs(0) - 1)
    def _():
        scatter_add(pending_ref[0], n_rows)
        scatter_add(pending_ref[1], half)


def _moe(ids, weights, x_tiles, w1, w3, w2):
    n_tok = ids.shape[1]
    g_tok = min(MOE_GROUP_TOKENS, n_tok)
    n_groups = n_tok // g_tok
    assert n_groups * g_tok == n_tok
    per_group = g_tok * TOP_K
    tok = lax.broadcasted_iota(jnp.int32, ids.shape, 1)
    key = ((tok // g_tok) * N_EXPERTS + ids) * g_tok + tok % g_tok
    key, wts = lax.sort((key.reshape(-1), weights.reshape(-1)), num_keys=1, is_stable=False)
    rows8 = ((key % g_tok) * D_TILES).reshape(n_groups, per_group)
    wts = wts.reshape(n_groups, per_group)
    experts = jnp.arange(N_EXPERTS, dtype=jnp.int32)[None, None, :, None]
    counts = jnp.sum(ids.reshape(TOP_K, n_groups, 1, g_tok) == experts, axis=(0, 3), dtype=jnp.int32)
    start = jnp.concatenate([jnp.zeros((n_groups, 1), jnp.int32), jnp.cumsum(counts, axis=1)], axis=1)
    many_rows = per_group // N_EXPERTS >= 2 * MOE_BLOCK_ROWS
    n_rows = MOE_BLOCK_ROWS if many_rows else MOE_SMALL_BLOCK_ROWS
    pad = ((0, 0), (n_rows, n_rows + LANES))
    rows8 = jnp.pad(rows8, pad)
    wts = jnp.pad(wts, pad).reshape(n_groups, -1, 1, LANES)
    out = None
    for g in range(n_groups):
        out, (w1, w3, w2) = _moe_group_call(g, n_groups, g_tok, n_rows, rows8[g], start[g], x_tiles, wts[g],
                                            w1, w3, w2, out)
    return out, (w1, w3, w2)


def _moe_group_call(group, n_groups, g_tok, n_rows, rows8, start, x_tiles, wts, w1, w3, w2, prev):
    g_rows = g_tok * D_TILES
    assert x_tiles.shape[0] == n_groups * g_rows
    emit_bf16 = w1.dtype != BF16
    assert not (emit_bf16 and prev is not None)
    out_block = (1, g_rows, LANES)
    eps = 1 if emit_bf16 else MOE_EXPERTS_PER_STEP[n_rows]
    per_expert = lambda w: pl.BlockSpec((eps,) + w.shape[1:], lambda s, *_: (s, 0, 0))
    in_specs = [
        pl.BlockSpec((g_rows, LANES), lambda e, *_: (group, 0), pipeline_mode=pl.Buffered(1)),
        pl.BlockSpec(wts.shape, lambda e, *_: (0, 0, 0)),
        per_expert(w1), per_expert(w3), per_expert(w2),
    ]
    args = [rows8, start, x_tiles, wts, w1, w3, w2]
    aliases = {}
    if prev is not None:
        in_specs.append(pl.BlockSpec(memory_space=pl.ANY))
        aliases = {len(args): 0}
        args.append(prev)
    out_specs = [pl.BlockSpec(out_block, lambda e, *_: (group, 0, 0), pipeline_mode=pl.Buffered(1))]
    out_shape = [jax.ShapeDtypeStruct((n_groups,) + out_block[1:], F32)]
    if emit_bf16:
        out_specs += [per_expert(w1), per_expert(w3), per_expert(w2)]
        out_shape += [jax.ShapeDtypeStruct(w.shape, BF16) for w in (w1, w3, w2)]
    results = pl.pallas_call(
        functools.partial(_moe_kernel, n_rows=n_rows, emit_bf16=emit_bf16, experts_per_step=eps),
        grid_spec=pltpu.PrefetchScalarGridSpec(
            num_scalar_prefetch=2, grid=(N_EXPERTS // eps,), in_specs=in_specs, out_specs=out_specs,
            scratch_shapes=[pltpu.VMEM((n_rows * D_TILES, LANES), F32),
                            pltpu.VMEM((n_rows * D_TILES, LANES), F32),
                            pltpu.VMEM((n_rows // 2 * D_TILES, LANES), F32),
                            pltpu.SMEM((2,), jnp.int32)]),
        out_shape=out_shape,
        input_output_aliases=aliases,
        compiler_params=_params(1), name="moe",
    )(*args)
    return results[0], (tuple(results[1:]) if emit_bf16 else (w1, w3, w2))


def _final_kernel(x1_ref, routed_ref, p_ref, s1_32_ref, s3_32_ref, s2_32_ref, g2_ref, b2_ref, wg32_ref,
                  wp32_ref, o_ref, s1_ref, s3_ref, s2_ref, wg_ref, wp_ref, *, alpha):
    _cast_weights_once([(s1_32_ref, s1_ref), (s3_32_ref, s3_ref), (s2_32_ref, s2_ref),
                        (wg32_ref, wg_ref), (wp32_ref, wp_ref)])
    x1 = x1_ref[...]
    xb = x1.astype(BF16)
    act = jax.nn.silu(_dot(xb, s1_ref[...])) * _dot(xb, s3_ref[...])
    ff = _load_row_major_tiles(routed_ref.at[0], x1.shape[0]) + _dot(act.astype(BF16), s2_ref[...])
    x2 = _layer_norm(alpha * x1 + ff, g2_ref[...], b2_ref[...])
    gate = jax.nn.sigmoid(_dot(x2.astype(BF16), wg_ref[...]))
    o_ref[...] = x2 + gate * _dot(p_ref[...].astype(BF16), wp_ref[...])


def _final(x1, routed, p2d, s1, s3, s2, g2, b2, wg, wp, *, alpha):
    n_rows = x1.shape[0]
    tile = ROW_TILE
    row = lambda width: pl.BlockSpec((tile, width), lambda i: (i, 0))
    tiles_per_group = routed.shape[1] // D_TILES // tile
    routed_spec = pl.BlockSpec((1, tile * D_TILES, LANES),
                               lambda i: (i // tiles_per_group, i % tiles_per_group, 0))
    return pl.pallas_call(
        functools.partial(_final_kernel, alpha=alpha),
        grid=(n_rows // tile,),
        in_specs=[row(D_MODEL), routed_spec,
                  row(p2d.shape[1]),
                  _resident_spec(s1.shape), _resident_spec(s3.shape), _resident_spec(s2.shape),
                  _const_spec((1, D_MODEL)), _const_spec((1, D_MODEL)),
                  _resident_spec(wg.shape), _resident_spec(wp.shape)],
        out_specs=row(D_MODEL),
        out_shape=jax.ShapeDtypeStruct((n_rows, D_MODEL), F32),
        scratch_shapes=_bf16_scratch(s1, s3, s2, wg, wp), compiler_params=_params(1), name="final",
    )(x1, routed, p2d, s1, s3, s2, g2, b2, wg, wp)


def _trunk_layer(x, p, cache_k, cache_v, lw, *, alpha, first_layer_ln):
    batch, seq, d = x.shape
    n_rows = batch * seq
    x2d = x.reshape(n_rows, d)
    prompt = cache_k is None
    mlp_len = min(seq, MLP_CHUNK)
    keep = min(PAST_CHUNKS * CHUNK, seq) if prompt else seq
    ln_g, ln_b = first_layer_ln

    w_s = (lw["mlp_w_s"][:, :mlp_len, :mlp_len]
           * jnp.tril(jnp.ones((mlp_len, mlp_len), F32))).astype(BF16)
    b_s = jnp.repeat(lw["mlp_b_s"][:, :mlp_len].T, MLP_WIDTH // MLP_GROUPS, axis=1)
    q, k, v, kf, vf, vn, ga, gb, ob = _proj(
        x2d, ln_g, ln_b, lw["w_in"], lw["mlp_ln_g"][None], lw["mlp_ln_b"][None],
        w_s, b_s, seq=seq, mlp_len=mlp_len, keep_rows=keep)
    if prompt:
        oa = _attn_prompt(q, k, v, lw["attn_rel_bias"], batch=batch, seq=seq)
    else:
        oa = _attn_sample(q, k, v, cache_k, cache_v, lw["attn_rel_bias"], batch=batch, seq=seq)
    x1, x1_tiles, scores = _mix(
        x2d, oa, ob, ga, gb, ln_g, ln_b, lw["w_branch_a"], lw["w_branch_b"], lw["w_out"],
        lw["ln1_g"][None], lw["ln1_b"][None], _split_bf16(lw["router_w"].T), alpha=alpha)
    ids, weights = _route_call(scores, lw["router_bias"][:, None])
    routed, expert_bf16 = _moe(ids, weights, x1_tiles, lw["exp_w1"], lw["exp_w3"], lw["exp_w2"])
    y = _final(x1, routed, p.reshape(n_rows, -1), lw["shared_w1"], lw["shared_w3"], lw["shared_w2"],
               lw["ln2_g"][None], lw["ln2_b"][None], lw["ple_w_gate"], lw["ple_w_proj"], alpha=alpha)
    new_k = kf.reshape(batch, keep, HEADS, HEAD_DIM)
    new_v = vf.reshape(batch, keep, HEADS, HEAD_DIM)
    new_mlp_v = vn.reshape(batch, mlp_len, MLP_WIDTH)
    return y.reshape(batch, seq, d), new_k, new_v, new_mlp_v, expert_bf16


def kernel(x_prompt, x_sample, cache_attn_k, cache_attn_v, p_prompt, p_sample, ln_in_g, ln_in_b, w_in, attn_rel_bias, mlp_ln_g, mlp_ln_b, mlp_w_s, mlp_b_s, w_branch_a, w_branch_b, w_out, ln1_g, ln1_b, router_w, router_bias, exp_w1, exp_w3, exp_w2, shared_w1, shared_w3, shared_w2, ln2_g, ln2_b, ple_w_gate, ple_w_proj):
    depth = w_in.shape[0]
    assert depth == 1, "the fused proj/mix kernels apply LayerNorm_in themselves: single layer only"
    alpha = (2 * depth) ** 0.25
    stacked = dict(w_in=w_in, attn_rel_bias=attn_rel_bias, mlp_ln_g=mlp_ln_g, mlp_ln_b=mlp_ln_b,
                   mlp_w_s=mlp_w_s, mlp_b_s=mlp_b_s, w_branch_a=w_branch_a, w_branch_b=w_branch_b,
                   w_out=w_out, ln1_g=ln1_g, ln1_b=ln1_b, router_w=router_w, router_bias=router_bias,
                   exp_w1=exp_w1, exp_w3=exp_w3, exp_w2=exp_w2, shared_w1=shared_w1,
                   shared_w3=shared_w3, shared_w2=shared_w2, ln2_g=ln2_g, ln2_b=ln2_b,
                   ple_w_gate=ple_w_gate, ple_w_proj=ple_w_proj)
    lw = {name: value[0] for name, value in stacked.items()}
    ln_in =(ln_in_g[None], ln_in_b[None])
    yp, kp, vp, mp, expert_bf16 = _trunk_layer(x_prompt, p_prompt[0], None, None, lw, alpha=alpha,
                                               first_layer_ln=ln_in)
    lw = dict(lw, **dict(zip(("exp_w1", "exp_w3", "exp_w2"), expert_bf16)))
    ys, ks, vs, ms, _ = _trunk_layer(x_sample, p_sample[0], cache_attn_k[0], cache_attn_v[0], lw,
                                     alpha=alpha, first_layer_ln=ln_in)
    return (yp, ys, kp[None], vp[None], ks[None], vs[None], mp[None], ms[None])
```

```python
import functools

import jax
import jax.numpy as jnp
import numpy as np
from jax import lax
from jax.experimental import pallas as pl
from jax.experimental.pallas import tpu as pltpu

F32 = jnp.float32
BF16 = jnp.bfloat16

D_MODEL = 1024
CHUNK = 64
HEADS = 8
HEAD_DIM = 64
A_WIDTH = HEADS * HEAD_DIM
PAST_CHUNKS = 8
MAX_REL = 256
MLP_CHUNK = 128
MLP_GROUPS = 8
MLP_WIDTH = 512
N_EXPERTS = 64
TOP_K = 8
N_GROUPS = 8
TOPK_GROUPS = 4
GROUP_SIZE = N_EXPERTS // N_GROUPS
EXPERT_DIM = 256
ROUTED_SCALE = 2.5
LN_EPS = 1e-5
MASKED = -1e30

LANES = 128
SUBLANES = 8
VMEM_LIMIT_BYTES = 56 * 1024 * 1024

ROW_TILE = 512
ROUTE_TILE = 2048
Q_CHUNKS = 4
MOE_GROUP_TOKENS = 4096
MOE_BLOCK_ROWS = 256
MOE_SMALL_BLOCK_ROWS = 128
MOE_EXPERTS_PER_STEP = {MOE_BLOCK_ROWS: 2, MOE_SMALL_BLOCK_ROWS: 4}
GATHER_PIECE = 128
CAST_COLS = 512
SCATTER_BATCH = 8
D_TILES = D_MODEL // LANES


def _layer_norm(x, g, b):
    xc = x - jnp.mean(x, axis=-1, keepdims=True)
    var = jnp.mean(xc * xc, axis=-1, keepdims=True)
    return xc * lax.rsqrt(var + LN_EPS) * g + b


def _dot(a, b):
    return jnp.dot(a, b, preferred_element_type=F32)


def _dot_nt(a, b, precision=None):
    return lax.dot_general(a, b, (((1,), (1,)), ((), ())),
                           preferred_element_type=F32, precision=precision)


def _const_spec(shape):
    zeros = (0,) * len(shape)
    return pl.BlockSpec(shape, lambda *_: zeros)


def _resident_spec(shape):
    zeros = (0,) * len(shape)
    return pl.BlockSpec(shape, lambda *_: zeros, pipeline_mode=pl.Buffered(1))


def _cast_weights_once(pairs):
    @pl.when(pl.program_id(0) == 0)
    def _():
        for src, dst in pairs:
            cols = src.shape[1]
            for c in range(0, cols, CAST_COLS):
                dst[:, c:min(c + CAST_COLS, cols)] = src[:, c:min(c + CAST_COLS, cols)].astype(BF16)


def _bf16_scratch(*weights):
    return [pltpu.VMEM(w.shape, BF16) for w in weights]


def _params(n_axes):
    return pltpu.CompilerParams(dimension_semantics=("arbitrary",) * n_axes,
                                vmem_limit_bytes=VMEM_LIMIT_BYTES)


def _proj_kernel(x_ref, lng_ref, lnb_ref, w32_ref, mg_ref, mb_ref, ws_ref, bs_ref,
                 q_ref, k_ref, v_ref, kf_ref, vf_ref, vn_ref, ga_ref, gb_ref, ob_ref, w_ref,
                 *, mlp_len, vn_rows):
    _cast_weights_once([(w32_ref, w_ref)])
    rows = x_ref.shape[0]
    xn = _layer_norm(x_ref[...], lng_ref[...], lnb_ref[...]).astype(BF16)

    def section(lo, width):
        return _dot(xn, w_ref[:, lo:lo + width])

    q = section(0, A_WIDTH)
    q_ref[...] = (q * (HEAD_DIM ** -0.5)).astype(BF16)
    k = section(A_WIDTH, A_WIDTH)
    k_ref[...] = k.astype(BF16)
    kf_ref[...] = k
    v = section(2 * A_WIDTH, A_WIDTH)
    v_ref[...] = v.astype(BF16)
    vf_ref[...] = v
    base = 3 * A_WIDTH
    un = jax.nn.gelu(section(base, MLP_WIDTH))
    vn = _layer_norm(jax.nn.gelu(section(base + MLP_WIDTH, MLP_WIDTH)), mg_ref[...], mb_ref[...])
    vn_ref[...] = vn[rows - vn_rows:, :]
    base += 2 * MLP_WIDTH
    ga_ref[...] = jax.nn.sigmoid(section(base, D_MODEL)).astype(BF16)
    gb_ref[...] = jax.nn.sigmoid(section(base + D_MODEL, D_MODEL)).astype(BF16)

    vnb = vn.astype(BF16)
    low_half = lax.broadcasted_iota(jnp.int32, (mlp_len, LANES), 1) < (MLP_WIDTH // MLP_GROUPS)
    for c in range(rows // mlp_len):
        r0 = c * mlp_len
        for p in range(MLP_GROUPS // 2):
            c0 = p * LANES
            slab = vnb[r0:r0 + mlp_len, c0:c0 + LANES]
            mixed = jnp.where(low_half, _dot(ws_ref[2 * p], slab), _dot(ws_ref[2 * p + 1], slab))
            gated = un[r0:r0 + mlp_len, c0:c0 + LANES] * (mixed + bs_ref[:, c0:c0 + LANES])
            ob_ref[r0:r0 + mlp_len, c0:c0 + LANES] = gated.astype(BF16)


def _proj(x2d, ln_g, ln_b, w_in, mlp_g, mlp_b, w_s, b_s, *, seq, mlp_len, keep_rows):
    n_rows = x2d.shape[0]
    tile = ROW_TILE
    assert n_rows % tile == 0 and tile % mlp_len == 0
    n_seq = n_rows // seq
    if seq >= tile:
        assert seq % tile == 0 and keep_rows == tile
        per_seq = seq // tile
        kv_rows, vn_rows = n_seq * tile, mlp_len
        kv_map = lambda i: (i // per_seq, 0)
    else:
        assert keep_rows == seq and mlp_len == seq
        kv_rows, vn_rows = n_rows, tile
        kv_map = lambda i: (i, 0)
    row = lambda width: pl.BlockSpec((tile, width), lambda i: (i, 0))
    n_in = w_in.shape[1]
    out_shape = (
        jax.ShapeDtypeStruct((n_rows, A_WIDTH), BF16),
        jax.ShapeDtypeStruct((n_rows, A_WIDTH), BF16),
        jax.ShapeDtypeStruct((n_rows, A_WIDTH), BF16),
        jax.ShapeDtypeStruct((kv_rows, A_WIDTH), F32),
        jax.ShapeDtypeStruct((kv_rows, A_WIDTH), F32),
        jax.ShapeDtypeStruct((kv_rows // tile * vn_rows, MLP_WIDTH), F32),
        jax.ShapeDtypeStruct((n_rows, D_MODEL), BF16),
        jax.ShapeDtypeStruct((n_rows, D_MODEL), BF16),
        jax.ShapeDtypeStruct((n_rows, MLP_WIDTH), BF16),
    )
    out_specs = (
        row(A_WIDTH), row(A_WIDTH), row(A_WIDTH),
        pl.BlockSpec((tile, A_WIDTH), kv_map), pl.BlockSpec((tile, A_WIDTH), kv_map),
        pl.BlockSpec((vn_rows, MLP_WIDTH), kv_map),
        row(D_MODEL), row(D_MODEL), row(MLP_WIDTH),
    )
    in_specs = [
        row(D_MODEL), _const_spec((1, D_MODEL)), _const_spec((1, D_MODEL)),
        _resident_spec((D_MODEL, n_in)), _const_spec((1, MLP_WIDTH)), _const_spec((1, MLP_WIDTH)),
        _const_spec((MLP_GROUPS, mlp_len, mlp_len)), _const_spec((mlp_len, MLP_WIDTH)),
    ]
    return pl.pallas_call(
        functools.partial(_proj_kernel, mlp_len=mlp_len, vn_rows=vn_rows),
        grid=(n_rows // tile,), in_specs=in_specs, out_specs=out_specs, out_shape=out_shape,
        scratch_shapes=_bf16_scratch(w_in), compiler_params=_params(1), name="proj",
    )(x2d, ln_g, ln_b, w_in, mlp_g, mlp_b, w_s, b_s)


def _attend_heads(q, keys, values, pair_bias, o_ref):
    n_q = q.shape[0]
    low_half = lax.broadcasted_iota(jnp.int32, (n_q, LANES), 1) < HEAD_DIM
    zero = jnp.zeros((), BF16)
    for p in range(HEADS // 2):
        c0 = p * LANES
        q2, k2, v2 = q[:, c0:c0 + LANES], keys[:, c0:c0 + LANES], values[:, c0:c0 + LANES]
        qs = jnp.concatenate([jnp.where(low_half, q2, zero), jnp.where(low_half, zero, q2)], axis=0)
        s = _dot_nt(qs, k2) + pair_bias(p).reshape(2 * n_q, keys.shape[0])
        e = jnp.exp(s - jnp.max(s, axis=-1, keepdims=True))
        denom = jnp.sum(e, axis=-1, keepdims=True)
        o = _dot(e.astype(BF16), v2) / denom
        o_ref[:, c0:c0 + LANES] = jnp.where(low_half, o[:n_q], o[n_q:]).astype(o_ref.dtype)


def _fill_toeplitz_bias(bias_ref, diag_ref, first_key_chunk=None):
    _, n_q, n_k = bias_ref.shape
    if first_key_chunk is not None:
        q_chunk = lax.broadcasted_iota(jnp.int32, (n_q, n_k), 0) // CHUNK
        k_chunk = lax.broadcasted_iota(jnp.int32, (n_q, n_k), 1) // CHUNK + first_key_chunk
        in_band = (k_chunk <= q_chunk) & (k_chunk >= q_chunk - PAST_CHUNKS)
    for h in range(bias_ref.shape[0]):
        rows = jnp.broadcast_to(diag_ref[h:h + 1, :], (n_q, diag_ref.shape[1]))
        rows = pltpu.roll(rows, 0, 1, stride=1, stride_axis=0)[:, :n_k]
        bias_ref[h] = rows if first_key_chunk is None else jnp.where(in_band, rows, MASKED)


def _attn_prompt_kernel(q_ref, k_ref, v_ref, diag_ref, o_ref, bias_ref, *, window, lead_steps):
    n_q = q_ref.shape[0]

    @pl.when((pl.program_id(0) == 0) & (pl.program_id(1) == 0))
    def _():
        _fill_toeplitz_bias(bias_ref, diag_ref, first_key_chunk=-lead_steps * (n_q // CHUNK))

    step = jnp.minimum(pl.program_id(1), lead_steps)
    start = pl.multiple_of((pl.program_id(1) - step) * n_q, n_q)
    shift = pl.multiple_of((lead_steps - step) * n_q, n_q)
    keys = k_ref[pl.ds(start, window), :]
    values = v_ref[pl.ds(start, window), :]
    _attend_heads(q_ref[...], keys, values,
                  lambda p: bias_ref[2 * p:2 * p + 2, :, pl.ds(shift, window)], o_ref)


def _bias_diagonals(rel_table, n_neg, n_pos, key_offset):
    period = pl.next_power_of_2(n_neg + n_pos)
    m = np.arange(period)
    diag = np.where(m < n_pos, m, m - period)
    return rel_table[:, np.clip(diag - key_offset, -MAX_REL, MAX_REL) + MAX_REL].astype(F32), period


def _attn_prompt(q, k, v, rel_table, *, batch, seq):
    n_q = Q_CHUNKS * CHUNK
    window = (Q_CHUNKS + PAST_CHUNKS) * CHUNK
    assert PAST_CHUNKS % Q_CHUNKS == 0 and seq >= window
    lead = PAST_CHUNKS // Q_CHUNKS
    steps = seq // n_q
    n_bias = lead * n_q + window
    diag, _ = _bias_diagonals(rel_table, n_q, n_bias, PAST_CHUNKS * CHUNK)
    qspec = pl.BlockSpec((n_q, A_WIDTH), lambda b, i: (b * steps + i, 0))
    kvspec = pl.BlockSpec((seq, A_WIDTH), lambda b, i: (b, 0))
    return pl.pallas_call(
        functools.partial(_attn_prompt_kernel, window=window, lead_steps=lead),
        grid=(batch, steps),
        in_specs=[qspec, kvspec, kvspec, _const_spec(diag.shape)],
        out_specs=qspec,
        out_shape=jax.ShapeDtypeStruct((batch * seq, A_WIDTH), BF16),
        scratch_shapes=[pltpu.VMEM((HEADS, n_q, n_bias), F32)],
        compiler_params=_params(2), name="attn_prompt",
    )(q, k, v, diag)


def _attn_sample_kernel(q_ref, k_ref, v_ref, ck_ref, cv_ref, diag_ref, o_ref, kk_ref, vv_ref, bias_ref):
    @pl.when(pl.program_id(0) == 0)
    def _():
        _fill_toeplitz_bias(bias_ref, diag_ref)

    n_cache = ck_ref.shape[1]
    n_new = k_ref.shape[1]
    kk_ref[0:n_cache, :] = ck_ref[0].astype(BF16)
    kk_ref[n_cache:n_cache + n_new, :] = k_ref[0]
    vv_ref[0:n_cache, :] = cv_ref[0].astype(BF16)
    vv_ref[n_cache:n_cache + n_new, :] = v_ref[0]
    _attend_heads(q_ref[0], kk_ref[...], vv_ref[...], lambda p: bias_ref[2 * p:2 * p + 2], o_ref.at[0])


def _attn_sample(q, k, v, cache_k, cache_v, rel_table, *, batch, seq):
    n_cache = cache_k.shape[1]
    diag, _ = _bias_diagonals(rel_table, seq, n_cache + seq, n_cache)
    new = pl.BlockSpec((1, seq, A_WIDTH), lambda b: (b, 0, 0))
    old = pl.BlockSpec((1, n_cache, A_WIDTH), lambda b: (b, 0, 0))
    out = pl.pallas_call(
        _attn_sample_kernel,
        grid=(batch,),
        in_specs=[new, new, new, old, old, _const_spec(diag.shape)],
        out_specs=new,
        out_shape=jax.ShapeDtypeStruct((batch, seq, A_WIDTH), BF16),
        scratch_shapes=[pltpu.VMEM((n_cache + seq, A_WIDTH), BF16),
                        pltpu.VMEM((n_cache + seq, A_WIDTH), BF16),
                        pltpu.VMEM((HEADS, seq, n_cache + seq), F32)],
        compiler_params=_params(1), name="attn_sample",
    )(q.reshape(batch, seq, A_WIDTH), k.reshape(batch, seq, A_WIDTH), v.reshape(batch, seq, A_WIDTH),
      cache_k.reshape(batch, n_cache, A_WIDTH), cache_v.reshape(batch, n_cache, A_WIDTH), diag)
    return out.reshape(batch * seq, A_WIDTH)


def _first_index(hit, iota, axis, limit):
    return jnp.min(jnp.where(hit, iota, limit), axis=axis, keepdims=True)


def _route(scores, sel):
    n_tok = scores.shape[1]
    neg = -jnp.inf
    grouped = sel.reshape(N_GROUPS, GROUP_SIZE, n_tok)
    member = lax.broadcasted_iota(jnp.int32, grouped.shape, 1)
    best = jnp.max(grouped, axis=1, keepdims=True)
    first = _first_index(grouped == best, member, 1, GROUP_SIZE)
    second = jnp.max(jnp.where(member == first, neg, grouped), axis=1, keepdims=True)
    group_score = best + second

    group_id = lax.broadcasted_iota(jnp.int32, group_score.shape, 0)
    keep = jnp.zeros(group_score.shape, F32)
    for _ in range(TOPK_GROUPS):
        top = jnp.max(group_score, axis=0, keepdims=True)
        hit = group_id == _first_index(group_score == top, group_id, 0, N_GROUPS)
        keep = jnp.where(hit, 1.0, keep)
        group_score = jnp.where(hit, neg, group_score)
    keep = jnp.broadcast_to(keep, grouped.shape).reshape(N_EXPERTS, n_tok)

    cand = jnp.where(keep > 0.0, sel, neg)
    expert_id = lax.broadcasted_iota(jnp.int32, cand.shape, 0)
    ids, weights = [], []
    for _ in range(TOP_K):
        top = jnp.max(cand, axis=0, keepdims=True)
        first = _first_index(cand == top, expert_id, 0, N_EXPERTS)
        hit = expert_id == first
        ids.append(first)
        weights.append(jnp.sum(jnp.where(hit, scores, 0.0), axis=0, keepdims=True))
        cand = jnp.where(hit, neg, cand)
    ids = jnp.concatenate(ids, axis=0)
    weights = jnp.concatenate(weights, axis=0)
    weights = weights / jnp.sum(weights, axis=0, keepdims=True) * ROUTED_SCALE
    return ids, weights


def _store_row_major_tiles(flat_ref, x):
    rows = x.shape[0]
    for j in range(D_TILES):
        flat_ref[pl.ds(j, rows, stride=D_TILES), :] = x[:, j * LANES:(j + 1) * LANES]


def _load_row_major_tiles(flat_ref, rows):
    return jnp.concatenate(
        [flat_ref[pl.ds(j, rows, stride=D_TILES), :] for j in range(D_TILES)], axis=1)


def _split_bf16(w):
    hi = w.astype(BF16)
    return jnp.concatenate([hi, (w - hi.astype(F32)).astype(BF16)], axis=0)


def _mix_kernel(x_ref, oa_ref, ob_ref, ga_ref, gb_ref, lng_ref, lnb_ref, wa32_ref, wb32_ref, wo32_ref,
                g1_ref, b1_ref, rw_ref, x1_ref, x1t_ref, scores_ref, wa_ref, wb_ref, wo_ref, *, alpha):
    _cast_weights_once([(wa32_ref, wa_ref), (wb32_ref, wb_ref), (wo32_ref, wo_ref)])
    xn = _layer_norm(x_ref[...], lng_ref[...], lnb_ref[...])
    mix = (ga_ref[...].astype(F32) * _dot(oa_ref[...], wa_ref[...])
           + gb_ref[...].astype(F32) * _dot(ob_ref[...], wb_ref[...]))
    x1 = _layer_norm(alpha * xn + _dot(mix.astype(BF16), wo_ref[...]), g1_ref[...], b1_ref[...])
    x1_ref[...] = x1
    _store_row_major_tiles(x1t_ref, x1)
    x1_hi = x1.astype(BF16)
    x1_lo = (x1 - x1_hi.astype(F32)).astype(BF16)
    by_hi = _dot_nt(rw_ref[...], x1_hi)
    logits = by_hi[:N_EXPERTS] + by_hi[N_EXPERTS:] + _dot_nt(rw_ref[:N_EXPERTS, :], x1_lo)
    scores_ref[...] = jax.nn.sigmoid(logits)


def _mix(x2d, oa, ob, ga, gb, ln_g, ln_b, wa, wb, wo, g1, b1, rw_t, *, alpha):
    n_rows = x2d.shape[0]
    tile = ROW_TILE
    row = lambda width: pl.BlockSpec((tile, width), lambda i: (i, 0))
    return pl.pallas_call(
        functools.partial(_mix_kernel, alpha=alpha),
        grid=(n_rows // tile,),
        in_specs=[row(D_MODEL), row(A_WIDTH), row(MLP_WIDTH), row(D_MODEL), row(D_MODEL),
                  _const_spec((1, D_MODEL)), _const_spec((1, D_MODEL)),
                  _resident_spec(wa.shape), _resident_spec(wb.shape), _resident_spec(wo.shape),
                  _const_spec((1, D_MODEL)), _const_spec((1, D_MODEL)),
                  _const_spec(rw_t.shape)],
        out_specs=(row(D_MODEL), pl.BlockSpec((tile * D_TILES, LANES), lambda i: (i, 0)),
                   pl.BlockSpec((N_EXPERTS, tile), lambda i: (0, i))),
        out_shape=(jax.ShapeDtypeStruct((n_rows, D_MODEL), F32),
                   jax.ShapeDtypeStruct((n_rows * D_TILES, LANES), F32),
                   jax.ShapeDtypeStruct((N_EXPERTS, n_rows), F32)),
        scratch_shapes=_bf16_scratch(wa, wb, wo), compiler_params=_params(1), name="mix",
    )(x2d, oa, ob, ga, gb, ln_g, ln_b, wa, wb, wo, g1, b1, rw_t)


def _route_kernel(scores_ref, rb_ref, ids_ref, wts_ref):
    scores = scores_ref[...]
    ids, weights = _route(scores, scores + rb_ref[...])
    ids_ref[...] = ids
    wts_ref[...] = weights


def _route_call(scores, rb):
    n_tok = scores.shape[1]
    tile = min(ROUTE_TILE, n_tok)
    col = lambda rows: pl.BlockSpec((rows, tile), lambda i: (0, i))
    return pl.pallas_call(
        _route_kernel,
        grid=(n_tok // tile,),
        in_specs=[col(N_EXPERTS), _const_spec(rb.shape)],
        out_specs=(col(TOP_K), col(TOP_K)),
        out_shape=(jax.ShapeDtypeStruct((TOP_K, n_tok), jnp.int32),
                   jax.ShapeDtypeStruct((TOP_K, n_tok), F32)),
        compiler_params=_params(1), name="route",
    )(scores, rb)


def _moe_kernel(rows_ref, start_ref, x_ref, wts_ref, w1_ref, w3_ref, w2_ref, *rest, n_rows, emit_bf16,
                experts_per_step):
    gat_ref, y_full_ref, y_half_ref, pending_ref = rest[-4:]
    if emit_bf16:
        out_ref, w1b_ref, w3b_ref, w2b_ref = rest[-8:-4]
        w1b_ref[...] = w1_ref[...].astype(BF16)
        w3b_ref[...] = w3_ref[...].astype(BF16)
        w2b_ref[...] = w2_ref[...].astype(BF16)
        w1_ref, w3_ref, w2_ref = w1b_ref, w3b_ref, w2b_ref
    else:
        out_ref = rest[-5]
    acc = out_ref.at[0]
    step = pl.program_id(0)
    half = n_rows // 2
    y_refs = {n_rows: y_full_ref, half: y_half_ref}

    @pl.when(step == 0)
    def _():
        out_ref[...] = jnp.zeros(out_ref.shape, out_ref.dtype)
        y_full_ref[...] = jnp.zeros(y_full_ref.shape, y_full_ref.dtype)
        y_half_ref[...] = jnp.zeros(y_half_ref.shape, y_half_ref.dtype)
        pending_ref[0] = 0
        pending_ref[1] = 0

    def tile_at(ref, row8):
        return ref.at[pl.ds(pl.multiple_of(row8, SUBLANES), SUBLANES), :]

    def scatter_add(base, n):
        y_ref = y_refs[n]
        for m0 in range(0, n, SCATTER_BATCH):
            updates = []
            for m in range(m0, m0 + SCATTER_BATCH):
                dst = tile_at(acc, rows_ref[base + m])
                updates.append((dst, dst[...] + y_ref[m * SUBLANES:(m + 1) * SUBLANES, :]))
            for dst, val in updates:
                dst[...] = val

    def run_block(j, first, n, nominal, begin, pending):
        redo = nominal - begin
        base = first + begin + n_rows
        piece = min(n, GATHER_PIECE)
        acts = []
        for r0 in range(0, n, piece):
            gat = gat_ref.at[pl.ds(r0 * D_TILES, piece * D_TILES), :]
            for m in range(piece):
                gat[m * SUBLANES:(m + 1) * SUBLANES, :] = tile_at(x_ref, rows_ref[base + r0 + m])[...]
            xb = _load_row_major_tiles(gat, piece).astype(BF16)
            acts.append((jax.nn.silu(_dot(xb, w1_ref[j])) * _dot(xb, w3_ref[j])).astype(BF16))
        y = _dot(acts[0] if len(acts) == 1 else jnp.concatenate(acts, axis=0), w2_ref[j])

        w_rows = pl.cdiv(n, LANES) + 1
        row_id = lax.broadcasted_iota(jnp.int32, (n, w_rows * LANES), 0)
        lane_id = lax.broadcasted_iota(jnp.int32, (n, w_rows * LANES), 1)
        w_row = base // LANES
        span = jnp.concatenate([wts_ref[w_row + r] for r in range(w_rows)], axis=1)
        picked = jnp.where((lane_id == row_id + base % LANES) & (row_id >= redo),
                           jnp.broadcast_to(span, lane_id.shape), 0.0)
        ys = y * jnp.sum(picked, axis=1, keepdims=True)
        scatter_add(pending, n)
        _store_row_major_tiles(y_refs[n], ys)
        return base

    def one_expert(j):
        e = step * experts_per_step + j
        first = start_ref[e]
        count = start_ref[e + 1] - first
        full_blocks = count // n_rows
        rest_rows = count - full_blocks * n_rows
        main_blocks = full_blocks + (rest_rows > half).astype(jnp.int32)

        def main_block(b, pending):
            nominal = b * n_rows
            return run_block(j, first, n_rows, nominal, jnp.minimum(nominal, count - n_rows), pending)

        pending_ref[0] = lax.fori_loop(0, main_blocks, main_block, pending_ref[0])

        @pl.when((rest_rows > 0) & (rest_rows <= half))
        def _():
            pending_ref[1] = run_block(j, first, half, full_blocks * n_rows, count - half, pending_ref[1])

    for j in range(experts_per_step):
        one_expert(j)

    @pl.when(step == pl.num_programs(0) - 1)
    def _():
        scatter_add(pending_ref[0], n_rows)
        scatter_add(pending_ref[1], half)


def _moe(ids, weights, x_tiles, w1, w3, w2):
    n_tok = ids.shape[1]
    g_tok = min(MOE_GROUP_TOKENS, n_tok)
    n_groups = n_tok // g_tok
    assert n_groups * g_tok == n_tok
    per_group = g_tok * TOP_K
    tok = lax.broadcasted_iota(jnp.int32, ids.shape, 1)
    key = ((tok // g_tok) * N_EXPERTS + ids) * g_tok + tok % g_tok
    key, wts = lax.sort((key.reshape(-1), weights.reshape(-1)), num_keys=1, is_stable=False)
    rows8 = ((key % g_tok) * D_TILES).reshape(n_groups, per_group)
    wts = wts.reshape(n_groups, per_group)
    experts = jnp.arange(N_EXPERTS, dtype=jnp.int32)[None, None, :, None]
    counts = jnp.sum(ids.reshape(TOP_K, n_groups, 1, g_tok) == experts, axis=(0, 3), dtype=jnp.int32)
    start = jnp.concatenate([jnp.zeros((n_groups, 1), jnp.int32), jnp.cumsum(counts, axis=1)], axis=1)
    many_rows = per_group // N_EXPERTS >= 2 * MOE_BLOCK_ROWS
    n_rows = MOE_BLOCK_ROWS if many_rows else MOE_SMALL_BLOCK_ROWS
    pad = ((0, 0), (n_rows, n_rows + LANES))
    rows8 = jnp.pad(rows8, pad)
    wts = jnp.pad(wts, pad).reshape(n_groups, -1, 1, LANES)
    out = None
    for g in range(n_groups):
        out, (w1, w3, w2) = _moe_group_call(g, n_groups, g_tok, n_rows, rows8[g], start[g], x_tiles, wts[g],
                                            w1, w3, w2, out)
    return out, (w1, w3, w2)


def _moe_group_call(group, n_groups, g_tok, n_rows, rows8, start, x_tiles, wts, w1, w3, w2, prev):
    g_rows = g_tok * D_TILES
    assert x_tiles.shape[0] == n_groups * g_rows
    emit_bf16 = w1.dtype != BF16
    assert not (emit_bf16 and prev is not None)
    out_block = (1, g_rows, LANES)
    eps = 1 if emit_bf16 else MOE_EXPERTS_PER_STEP[n_rows]
    per_expert = lambda w: pl.BlockSpec((eps,) + w.shape[1:], lambda s, *_: (s, 0, 0))
    in_specs = [
        pl.BlockSpec((g_rows, LANES), lambda e, *_: (group, 0), pipeline_mode=pl.Buffered(1)),
        pl.BlockSpec(wts.shape, lambda e, *_: (0, 0, 0)),
        per_expert(w1), per_expert(w3), per_expert(w2),
    ]
    args = [rows8, start, x_tiles, wts, w1, w3, w2]
    aliases = {}
    if prev is not None:
        in_specs.append(pl.BlockSpec(memory_space=pl.ANY))
        aliases = {len(args): 0}
        args.append(prev)
    out_specs = [pl.BlockSpec(out_block, lambda e, *_: (group, 0, 0), pipeline_mode=pl.Buffered(1))]
    out_shape = [jax.ShapeDtypeStruct((n_groups,) + out_block[1:], F32)]
    if emit_bf16:
        out_specs += [per_expert(w1), per_expert(w3), per_expert(w2)]
        out_shape += [jax.ShapeDtypeStruct(w.shape, BF16) for w in (w1, w3, w2)]
    results = pl.pallas_call(
        functools.partial(_moe_kernel, n_rows=n_rows, emit_bf16=emit_bf16, experts_per_step=eps),
        grid_spec=pltpu.PrefetchScalarGridSpec(
            num_scalar_prefetch=2, grid=(N_EXPERTS // eps,), in_specs=in_specs, out_specs=out_specs,
            scratch_shapes=[pltpu.VMEM((n_rows * D_TILES, LANES), F32),
                            pltpu.VMEM((n_rows * D_TILES, LANES), F32),
                            pltpu.VMEM((n_rows // 2 * D_TILES, LANES), F32),
                            pltpu.SMEM((2,), jnp.int32)]),
        out_shape=out_shape,
        input_output_aliases=aliases,
        compiler_params=_params(1), name="moe",
    )(*args)
    return results[0], (tuple(results[1:]) if emit_bf16 else (w1, w3, w2))


def _final_kernel(x1_ref, routed_ref, p_ref, s1_32_ref, s3_32_ref, s2_32_ref, g2_ref, b2_ref, wg32_ref,
                  wp32_ref, o_ref, s1_ref, s3_ref, s2_ref, wg_ref, wp_ref, *, alpha):
    _cast_weights_once([(s1_32_ref, s1_ref), (s3_32_ref, s3_ref), (s2_32_ref, s2_ref),
                        (wg32_ref, wg_ref), (wp32_ref, wp_ref)])
    x1 = x1_ref[...]
    xb = x1.astype(BF16)
    act = jax.nn.silu(_dot(xb, s1_ref[...])) * _dot(xb, s3_ref[...])
    ff = _load_row_major_tiles(routed_ref.at[0], x1.shape[0]) + _dot(act.astype(BF16), s2_ref[...])
    x2 = _layer_norm(alpha * x1 + ff, g2_ref[...], b2_ref[...])
    gate = jax.nn.sigmoid(_dot(x2.astype(BF16), wg_ref[...]))
    o_ref[...] = x2 + gate * _dot(p_ref[...].astype(BF16), wp_ref[...])


def _final(x1, routed, p2d, s1, s3, s2, g2, b2, wg, wp, *, alpha):
    n_rows = x1.shape[0]
    tile = ROW_TILE
    row = lambda width: pl.BlockSpec((tile, width), lambda i: (i, 0))
    tiles_per_group = routed.shape[1] // D_TILES // tile
    routed_spec = pl.BlockSpec((1, tile * D_TILES, LANES),
                               lambda i: (i // tiles_per_group, i % tiles_per_group, 0))
    return pl.pallas_call(
        functools.partial(_final_kernel, alpha=alpha),
        grid=(n_rows // tile,),
        in_specs=[row(D_MODEL), routed_spec,
                  row(p2d.shape[1]),
                  _resident_spec(s1.shape), _resident_spec(s3.shape), _resident_spec(s2.shape),
                  _const_spec((1, D_MODEL)), _const_spec((1, D_MODEL)),
                  _resident_spec(wg.shape), _resident_spec(wp.shape)],
        out_specs=row(D_MODEL),
        out_shape=jax.ShapeDtypeStruct((n_rows, D_MODEL), F32),
        scratch_shapes=_bf16_scratch(s1, s3, s2, wg, wp), compiler_params=_params(1), name="final",
    )(x1, routed, p2d, s1, s3, s2, g2, b2, wg, wp)


def _trunk_layer(x, p, cache_k, cache_v, lw, *, alpha, first_layer_ln):
    batch, seq, d = x.shape
    n_rows = batch * seq
    x2d = x.reshape(n_rows, d)
    prompt = cache_k is None
    mlp_len = min(seq, MLP_CHUNK)
    keep = min(PAST_CHUNKS * CHUNK, seq) if prompt else seq
    ln_g, ln_b = first_layer_ln

    w_s = (lw["mlp_w_s"][:, :mlp_len, :mlp_len]
           * jnp.tril(jnp.ones((mlp_len, mlp_len), F32))).astype(BF16)
    b_s = jnp.repeat(lw["mlp_b_s"][:, :mlp_len].T, MLP_WIDTH // MLP_GROUPS, axis=1)
    q, k, v, kf, vf, vn, ga, gb, ob = _proj(
        x2d, ln_g, ln_b, lw["w_in"], lw["mlp_ln_g"][None], lw["mlp_ln_b"][None],
        w_s, b_s, seq=seq, mlp_len=mlp_len, keep_rows=keep)
    if prompt:
        oa = _attn_prompt(q, k, v, lw["attn_rel_bias"], batch=batch, seq=seq)
    else:
        oa = _attn_sample(q, k, v, cache_k, cache_v, lw["attn_rel_bias"], batch=batch, seq=seq)
    x1, x1_tiles, scores = _mix(
        x2d, oa, ob, ga, gb, ln_g, ln_b, lw["w_branch_a"], lw["w_branch_b"], lw["w_out"],
        lw["ln1_g"][None], lw["ln1_b"][None], _split_bf16(lw["router_w"].T), alpha=alpha)
    ids, weights = _route_call(scores, lw["router_bias"][:, None])
    routed, expert_bf16 = _moe(ids, weights, x1_tiles, lw["exp_w1"], lw["exp_w3"], lw["exp_w2"])
    y = _final(x1, routed, p.reshape(n_rows, -1), lw["shared_w1"], lw["shared_w3"], lw["shared_w2"],
               lw["ln2_g"][None], lw["ln2_b"][None], lw["ple_w_gate"], lw["ple_w_proj"], alpha=alpha)
    new_k = kf.reshape(batch, keep, HEADS, HEAD_DIM)
    new_v = vf.reshape(batch, keep, HEADS, HEAD_DIM)
    new_mlp_v = vn.reshape(batch, mlp_len, MLP_WIDTH)
    return y.reshape(batch, seq, d), new_k, new_v, new_mlp_v, expert_bf16


def kernel(x_prompt, x_sample, cache_attn_k, cache_attn_v, p_prompt, p_sample, ln_in_g, ln_in_b, w_in, attn_rel_bias, mlp_ln_g, mlp_ln_b, mlp_w_s, mlp_b_s, w_branch_a, w_branch_b, w_out, ln1_g, ln1_b, router_w, router_bias, exp_w1, exp_w3, exp_w2, shared_w1, shared_w3, shared_w2, ln2_g, ln2_b, ple_w_gate, ple_w_proj):
    depth = w_in.shape[0]
    assert depth == 1, "the fused proj/mix kernels apply LayerNorm_in themselves: single layer only"
    alpha = (2 * depth) ** 0.25
    stacked = dict(w_in=w_in, attn_rel_bias=attn_rel_bias, mlp_ln_g=mlp_ln_g, mlp_ln_b=mlp_ln_b,
                   mlp_w_s=mlp_w_s, mlp_b_s=mlp_b_s, w_branch_a=w_branch_a, w_branch_b=w_branch_b,
                   w_out=w_out, ln1_g=ln1_g, ln1_b=ln1_b, router_w=router_w, router_bias=router_bias,
                   exp_w1=exp_w1, exp_w3=exp_w3, exp_w2=exp_w2, shared_w1=shared_w1,
                   shared_w3=shared_w3, shared_w2=shared_w2, ln2_g=ln2_g, ln2_b=ln2_b,
                   ple_w_gate=ple_w_gate, ple_w_proj=ple_w_proj)
    lw = {name: value[0] for name, value in stacked.items()}
    ln_in =(ln_in_g[None], ln_in_b[None])
    yp, kp, vp, mp, expert_bf16 = _trunk_layer(x_prompt, p_prompt[0], None, None, lw, alpha=alpha,
                                               first_layer_ln=ln_in)
    lw = dict(lw, **dict(zip(("exp_w1", "exp_w3", "exp_w2"), expert_bf16)))
    ys, ks, vs, ms, _ = _trunk_layer(x_sample, p_sample[0], cache_attn_k[0], cache_attn_v[0], lw,
                                     alpha=alpha, first_layer_ln=ln_in)
    return (yp, ys, kp[None], vp[None], ks[None], vs[None], mp[None], ms[None])
```

```python
import functools

import jax
import jax.numpy as jnp
import numpy as np
from jax import lax
from jax.experimental import pallas as pl
from jax.experimental.pallas import tpu as pltpu

F32 = jnp.float32
BF16 = jnp.bfloat16

D_MODEL = 1024
CHUNK = 64
HEADS = 8
HEAD_DIM = 64
A_WIDTH = HEADS * HEAD_DIM
PAST_CHUNKS = 8
MAX_REL = 256
MLP_CHUNK = 128
MLP_GROUPS = 8
MLP_WIDTH = 512
N_EXPERTS = 64
TOP_K = 8
N_GROUPS = 8
TOPK_GROUPS = 4
GROUP_SIZE = N_EXPERTS // N_GROUPS
EXPERT_DIM = 256
ROUTED_SCALE = 2.5
LN_EPS = 1e-5
MASKED = -1e30

LANES = 128
SUBLANES = 8
VMEM_LIMIT_BYTES = 56 * 1024 * 1024

ROW_TILE = 512
ROUTE_TILE = 2048
Q_CHUNKS = 4
MOE_GROUP_TOKENS = 4096
MOE_BLOCK_ROWS = 256
MOE_SMALL_BLOCK_ROWS = 128
MOE_EXPERTS_PER_STEP = {MOE_BLOCK_ROWS: 4, MOE_SMALL_BLOCK_ROWS: 8}
CAST_COLS = 512
SCATTER_BATCH = 8
D_TILES = D_MODEL // LANES


def _layer_norm(x, g, b):
    xc = x - jnp.mean(x, axis=-1, keepdims=True)
    var = jnp.mean(xc * xc, axis=-1, keepdims=True)
    return xc * lax.rsqrt(var + LN_EPS) * g + b


def _dot(a, b):
    return jnp.dot(a, b, preferred_element_type=F32)


def _dot_nt(a, b, precision=None):
    return lax.dot_general(a, b, (((1,), (1,)), ((), ())),
                           preferred_element_type=F32, precision=precision)


def _const_spec(shape):
    zeros = (0,) * len(shape)
    return pl.BlockSpec(shape, lambda *_: zeros)


def _resident_spec(shape):
    zeros = (0,) * len(shape)
    return pl.BlockSpec(shape, lambda *_: zeros, pipeline_mode=pl.Buffered(1))


def _cast_weights_once(pairs):
    @pl.when(pl.program_id(0) == 0)
    def _():
        for src, dst in pairs:
            cols = src.shape[1]
            for c in range(0, cols, CAST_COLS):
                dst[:, c:min(c + CAST_COLS, cols)] = src[:, c:min(c + CAST_COLS, cols)].astype(BF16)


def _bf16_scratch(*weights):
    return [pltpu.VMEM(w.shape, BF16) for w in weights]


def _params(n_axes):
    return pltpu.CompilerParams(dimension_semantics=("arbitrary",) * n_axes,
                                vmem_limit_bytes=VMEM_LIMIT_BYTES)


def _proj_kernel(x_ref, lng_ref, lnb_ref, w32_ref, mg_ref, mb_ref, ws_ref, bs_ref,
                 q_ref, k_ref, v_ref, kf_ref, vf_ref, vn_ref, ga_ref, gb_ref, ob_ref, w_ref,
                 *, mlp_len, vn_rows):
    _cast_weights_once([(w32_ref, w_ref)])
    rows = x_ref.shape[0]
    xn = _layer_norm(x_ref[...], lng_ref[...], lnb_ref[...]).astype(BF16)

    def section(lo, width):
        return _dot(xn, w_ref[:, lo:lo + width])

    q = section(0, A_WIDTH)
    q_ref[...] = (q * (HEAD_DIM ** -0.5)).astype(BF16)
    k = section(A_WIDTH, A_WIDTH)
    k_ref[...] = k.astype(BF16)
    kf_ref[...] = k
    v = section(2 * A_WIDTH, A_WIDTH)
    v_ref[...] = v.astype(BF16)
    vf_ref[...] = v
    base = 3 * A_WIDTH
    un = jax.nn.gelu(section(base, MLP_WIDTH))
    vn = _layer_norm(jax.nn.gelu(section(base + MLP_WIDTH, MLP_WIDTH)), mg_ref[...], mb_ref[...])
    vn_ref[...] = vn[rows - vn_rows:, :]
    base += 2 * MLP_WIDTH
    ga_ref[...] = jax.nn.sigmoid(section(base, D_MODEL)).astype(BF16)
    gb_ref[...] = jax.nn.sigmoid(section(base + D_MODEL, D_MODEL)).astype(BF16)

    vnb = vn.astype(BF16)
    low_half = lax.broadcasted_iota(jnp.int32, (mlp_len, LANES), 1) < (MLP_WIDTH // MLP_GROUPS)
    for c in range(rows // mlp_len):
        r0 = c * mlp_len
        for p in range(MLP_GROUPS // 2):
            c0 = p * LANES
            slab = vnb[r0:r0 + mlp_len, c0:c0 + LANES]
            mixed = jnp.where(low_half, _dot(ws_ref[2 * p], slab), _dot(ws_ref[2 * p + 1], slab))
            gated = un[r0:r0 + mlp_len, c0:c0 + LANES] * (mixed + bs_ref[:, c0:c0 + LANES])
            ob_ref[r0:r0 + mlp_len, c0:c0 + LANES] = gated.astype(BF16)


def _proj(x2d, ln_g, ln_b, w_in, mlp_g, mlp_b, w_s, b_s, *, seq, mlp_len, keep_rows):
    n_rows = x2d.shape[0]
    tile = ROW_TILE
    assert n_rows % tile == 0 and tile % mlp_len == 0
    n_seq = n_rows // seq
    if seq >= tile:
        assert seq % tile == 0 and keep_rows == tile
        per_seq = seq // tile
        kv_rows, vn_rows = n_seq * tile, mlp_len
        kv_map = lambda i: (i // per_seq, 0)
    else:
        assert keep_rows == seq and mlp_len == seq
        kv_rows, vn_rows = n_rows, tile
        kv_map = lambda i: (i, 0)
    row = lambda width: pl.BlockSpec((tile, width), lambda i: (i, 0))
    n_in = w_in.shape[1]
    out_shape = (
        jax.ShapeDtypeStruct((n_rows, A_WIDTH), BF16),
        jax.ShapeDtypeStruct((n_rows, A_WIDTH), BF16),
        jax.ShapeDtypeStruct((n_rows, A_WIDTH), BF16),
        jax.ShapeDtypeStruct((kv_rows, A_WIDTH), F32),
        jax.ShapeDtypeStruct((kv_rows, A_WIDTH), F32),
        jax.ShapeDtypeStruct((kv_rows // tile * vn_rows, MLP_WIDTH), F32),
        jax.ShapeDtypeStruct((n_rows, D_MODEL), BF16),
        jax.ShapeDtypeStruct((n_rows, D_MODEL), BF16),
        jax.ShapeDtypeStruct((n_rows, MLP_WIDTH), BF16),
    )
    out_specs = (
        row(A_WIDTH), row(A_WIDTH), row(A_WIDTH),
        pl.BlockSpec((tile, A_WIDTH), kv_map), pl.BlockSpec((tile, A_WIDTH), kv_map),
        pl.BlockSpec((vn_rows, MLP_WIDTH), kv_map),
        row(D_MODEL), row(D_MODEL), row(MLP_WIDTH),
    )
    in_specs = [
        row(D_MODEL), _const_spec((1, D_MODEL)), _const_spec((1, D_MODEL)),
        _resident_spec((D_MODEL, n_in)), _const_spec((1, MLP_WIDTH)), _const_spec((1, MLP_WIDTH)),
        _const_spec((MLP_GROUPS, mlp_len, mlp_len)), _const_spec((mlp_len, MLP_WIDTH)),
    ]
    return pl.pallas_call(
        functools.partial(_proj_kernel, mlp_len=mlp_len, vn_rows=vn_rows),
        grid=(n_rows // tile,), in_specs=in_specs, out_specs=out_specs, out_shape=out_shape,
        scratch_shapes=_bf16_scratch(w_in), compiler_params=_params(1), name="proj",
    )(x2d, ln_g, ln_b, w_in, mlp_g, mlp_b, w_s, b_s)


def _attend_heads(q, keys, values, pair_bias, o_ref):
    n_q = q.shape[0]
    low_half = lax.broadcasted_iota(jnp.int32, (n_q, LANES), 1) < HEAD_DIM
    zero = jnp.zeros((), BF16)
    for p in range(HEADS // 2):
        c0 = p * LANES
        q2, k2, v2 = q[:, c0:c0 + LANES], keys[:, c0:c0 + LANES], values[:, c0:c0 + LANES]
        qs = jnp.concatenate([jnp.where(low_half, q2, zero), jnp.where(low_half, zero, q2)], axis=0)
        s = _dot_nt(qs, k2) + pair_bias(p).reshape(2 * n_q, keys.shape[0])
        e = jnp.exp(s - jnp.max(s, axis=-1, keepdims=True))
        denom = jnp.sum(e, axis=-1, keepdims=True)
        o = _dot(e.astype(BF16), v2) / denom
        o_ref[:, c0:c0 + LANES] = jnp.where(low_half, o[:n_q], o[n_q:]).astype(o_ref.dtype)


def _fill_toeplitz_bias(bias_ref, diag_ref, first_key_chunk=None):
    _, n_q, n_k = bias_ref.shape
    if first_key_chunk is not None:
        q_chunk = lax.broadcasted_iota(jnp.int32, (n_q, n_k), 0) // CHUNK
        k_chunk = lax.broadcasted_iota(jnp.int32, (n_q, n_k), 1) // CHUNK + first_key_chunk
        in_band = (k_chunk <= q_chunk) & (k_chunk >= q_chunk - PAST_CHUNKS)
    for h in range(bias_ref.shape[0]):
        rows = jnp.broadcast_to(diag_ref[h:h + 1, :], (n_q, diag_ref.shape[1]))
        rows = pltpu.roll(rows, 0, 1, stride=1, stride_axis=0)[:, :n_k]
        bias_ref[h] = rows if first_key_chunk is None else jnp.where(in_band, rows, MASKED)


def _attn_prompt_kernel(q_ref, k_ref, v_ref, diag_ref, o_ref, bias_ref, *, window, lead_steps):
    n_q = q_ref.shape[0]

    @pl.when((pl.program_id(0) == 0) & (pl.program_id(1) == 0))
    def _():
        _fill_toeplitz_bias(bias_ref, diag_ref, first_key_chunk=-lead_steps * (n_q // CHUNK))

    step = jnp.minimum(pl.program_id(1), lead_steps)
    start = pl.multiple_of((pl.program_id(1) - step) * n_q, n_q)
    shift = pl.multiple_of((lead_steps - step) * n_q, n_q)
    keys = k_ref[pl.ds(start, window), :]
    values = v_ref[pl.ds(start, window), :]
    _attend_heads(q_ref[...], keys, values,
                  lambda p: bias_ref[2 * p:2 * p + 2, :, pl.ds(shift, window)], o_ref)


def _bias_diagonals(rel_table, n_neg, n_pos, key_offset):
    period = pl.next_power_of_2(n_neg + n_pos)
    m = np.arange(period)
    diag = np.where(m < n_pos, m, m - period)
    return rel_table[:, np.clip(diag - key_offset, -MAX_REL, MAX_REL) + MAX_REL].astype(F32), period


def _attn_prompt(q, k, v, rel_table, *, batch, seq):
    n_q = Q_CHUNKS * CHUNK
    window = (Q_CHUNKS + PAST_CHUNKS) * CHUNK
    assert PAST_CHUNKS % Q_CHUNKS == 0 and seq >= window
    lead = PAST_CHUNKS // Q_CHUNKS
    steps = seq // n_q
    n_bias = lead * n_q + window
    diag, _ = _bias_diagonals(rel_table, n_q, n_bias, PAST_CHUNKS * CHUNK)
    qspec = pl.BlockSpec((n_q, A_WIDTH), lambda b, i: (b * steps + i, 0))
    kvspec = pl.BlockSpec((seq, A_WIDTH), lambda b, i: (b, 0))
    return pl.pallas_call(
        functools.partial(_attn_prompt_kernel, window=window, lead_steps=lead),
        grid=(batch, steps),
        in_specs=[qspec, kvspec, kvspec, _const_spec(diag.shape)],
        out_specs=qspec,
        out_shape=jax.ShapeDtypeStruct((batch * seq, A_WIDTH), BF16),
        scratch_shapes=[pltpu.VMEM((HEADS, n_q, n_bias), F32)],
        compiler_params=_params(2), name="attn_prompt",
    )(q, k, v, diag)


def _attn_sample_kernel(q_ref, k_ref, v_ref, ck_ref, cv_ref, diag_ref, o_ref, kk_ref, vv_ref, bias_ref):
    @pl.when(pl.program_id(0) == 0)
    def _():
        _fill_toeplitz_bias(bias_ref, diag_ref)

    n_cache = ck_ref.shape[1]
    n_new = k_ref.shape[1]
    kk_ref[0:n_cache, :] = ck_ref[0].astype(BF16)
    kk_ref[n_cache:n_cache + n_new, :] = k_ref[0]
    vv_ref[0:n_cache, :] = cv_ref[0].astype(BF16)
    vv_ref[n_cache:n_cache + n_new, :] = v_ref[0]
    _attend_heads(q_ref[0], kk_ref[...], vv_ref[...], lambda p: bias_ref[2 * p:2 * p + 2], o_ref.at[0])


def _attn_sample(q, k, v, cache_k, cache_v, rel_table, *, batch, seq):
    n_cache = cache_k.shape[1]
    diag, _ = _bias_diagonals(rel_table, seq, n_cache + seq, n_cache)
    new = pl.BlockSpec((1, seq, A_WIDTH), lambda b: (b, 0, 0))
    old = pl.BlockSpec((1, n_cache, A_WIDTH), lambda b: (b, 0, 0))
    out = pl.pallas_call(
        _attn_sample_kernel,
        grid=(batch,),
        in_specs=[new, new, new, old, old, _const_spec(diag.shape)],
        out_specs=new,
        out_shape=jax.ShapeDtypeStruct((batch, seq, A_WIDTH), BF16),
        scratch_shapes=[pltpu.VMEM((n_cache + seq, A_WIDTH), BF16),
                        pltpu.VMEM((n_cache + seq, A_WIDTH), BF16),
                        pltpu.VMEM((HEADS, seq, n_cache + seq), F32)],
        compiler_params=_params(1), name="attn_sample",
    )(q.reshape(batch, seq, A_WIDTH), k.reshape(batch, seq, A_WIDTH), v.reshape(batch, seq, A_WIDTH),
      cache_k.reshape(batch, n_cache, A_WIDTH), cache_v.reshape(batch, n_cache, A_WIDTH), diag)
    return out.reshape(batch * seq, A_WIDTH)


def _first_index(hit, iota, axis, limit):
    return jnp.min(jnp.where(hit, iota, limit), axis=axis, keepdims=True)


def _route(scores, sel):
    n_tok = scores.shape[1]
    neg = -jnp.inf
    grouped = sel.reshape(N_GROUPS, GROUP_SIZE, n_tok)
    member = lax.broadcasted_iota(jnp.int32, grouped.shape, 1)
    best = jnp.max(grouped, axis=1, keepdims=True)
    first = _first_index(grouped == best, member, 1, GROUP_SIZE)
    second = jnp.max(jnp.where(member == first, neg, grouped), axis=1, keepdims=True)
    group_score = best + second

    group_id = lax.broadcasted_iota(jnp.int32, group_score.shape, 0)
    keep = jnp.zeros(group_score.shape, F32)
    for _ in range(TOPK_GROUPS):
        top = jnp.max(group_score, axis=0, keepdims=True)
        hit = group_id == _first_index(group_score == top, group_id, 0, N_GROUPS)
        keep = jnp.where(hit, 1.0, keep)
        group_score = jnp.where(hit, neg, group_score)
    keep = jnp.broadcast_to(keep, grouped.shape).reshape(N_EXPERTS, n_tok)

    cand = jnp.where(keep > 0.0, sel, neg)
    expert_id = lax.broadcasted_iota(jnp.int32, cand.shape, 0)
    ids, weights = [], []
    for _ in range(TOP_K):
        top = jnp.max(cand, axis=0, keepdims=True)
        first = _first_index(cand == top, expert_id, 0, N_EXPERTS)
        hit = expert_id == first
        ids.append(first)
        weights.append(jnp.sum(jnp.where(hit, scores, 0.0), axis=0, keepdims=True))
        cand = jnp.where(hit, neg, cand)
    ids = jnp.concatenate(ids, axis=0)
    weights = jnp.concatenate(weights, axis=0)
    weights = weights / jnp.sum(weights, axis=0, keepdims=True) * ROUTED_SCALE
    return ids, weights


def _store_row_major_tiles(flat_ref, x):
    rows = x.shape[0]
    for j in range(D_TILES):
        flat_ref[pl.ds(j, rows, stride=D_TILES), :] = x[:, j * LANES:(j + 1) * LANES]


def _load_row_major_tiles(flat_ref, rows):
    return jnp.concatenate(
        [flat_ref[pl.ds(j, rows, stride=D_TILES), :] for j in range(D_TILES)], axis=1)


def _split_bf16(w):
    hi = w.astype(BF16)
    return jnp.concatenate([hi, (w - hi.astype(F32)).astype(BF16)], axis=0)


def _mix_kernel(x_ref, oa_ref, ob_ref, ga_ref, gb_ref, lng_ref, lnb_ref, wa32_ref, wb32_ref, wo32_ref,
                g1_ref, b1_ref, rw_ref, x1_ref, x1t_ref, scores_ref, wa_ref, wb_ref, wo_ref, *, alpha):
    _cast_weights_once([(wa32_ref, wa_ref), (wb32_ref, wb_ref), (wo32_ref, wo_ref)])
    xn = _layer_norm(x_ref[...], lng_ref[...], lnb_ref[...])
    mix = (ga_ref[...].astype(F32) * _dot(oa_ref[...], wa_ref[...])
           + gb_ref[...].astype(F32) * _dot(ob_ref[...], wb_ref[...]))
    x1 = _layer_norm(alpha * xn + _dot(mix.astype(BF16), wo_ref[...]), g1_ref[...], b1_ref[...])
    x1_ref[...] = x1
    _store_row_major_tiles(x1t_ref, x1)
    x1_hi = x1.astype(BF16)
    x1_lo = (x1 - x1_hi.astype(F32)).astype(BF16)
    by_hi = _dot_nt(rw_ref[...], x1_hi)
    logits = by_hi[:N_EXPERTS] + by_hi[N_EXPERTS:] + _dot_nt(rw_ref[:N_EXPERTS, :], x1_lo)
    scores_ref[...] = jax.nn.sigmoid(logits)


def _mix(x2d, oa, ob, ga, gb, ln_g, ln_b, wa, wb, wo, g1, b1, rw_t, *, alpha):
    n_rows = x2d.shape[0]
    tile = ROW_TILE
    row = lambda width: pl.BlockSpec((tile, width), lambda i: (i, 0))
    return pl.pallas_call(
        functools.partial(_mix_kernel, alpha=alpha),
        grid=(n_rows // tile,),
        in_specs=[row(D_MODEL), row(A_WIDTH), row(MLP_WIDTH), row(D_MODEL), row(D_MODEL),
                  _const_spec((1, D_MODEL)), _const_spec((1, D_MODEL)),
                  _resident_spec(wa.shape), _resident_spec(wb.shape), _resident_spec(wo.shape),
                  _const_spec((1, D_MODEL)), _const_spec((1, D_MODEL)),
                  _const_spec(rw_t.shape)],
        out_specs=(row(D_MODEL), pl.BlockSpec((tile * D_TILES, LANES), lambda i: (i, 0)),
                   pl.BlockSpec((N_EXPERTS, tile), lambda i: (0, i))),
        out_shape=(jax.ShapeDtypeStruct((n_rows, D_MODEL), F32),
                   jax.ShapeDtypeStruct((n_rows * D_TILES, LANES), F32),
                   jax.ShapeDtypeStruct((N_EXPERTS, n_rows), F32)),
        scratch_shapes=_bf16_scratch(wa, wb, wo), compiler_params=_params(1), name="mix",
    )(x2d, oa, ob, ga, gb, ln_g, ln_b, wa, wb, wo, g1, b1, rw_t)


def _route_kernel(scores_ref, rb_ref, ids_ref, wts_ref):
    scores = scores_ref[...]
    ids, weights = _route(scores, scores + rb_ref[...])
    ids_ref[...] = ids
    wts_ref[...] = weights


def _route_call(scores, rb):
    n_tok = scores.shape[1]
    tile = min(ROUTE_TILE, n_tok)
    col = lambda rows: pl.BlockSpec((rows, tile), lambda i: (0, i))
    return pl.pallas_call(
        _route_kernel,
        grid=(n_tok // tile,),
        in_specs=[col(N_EXPERTS), _const_spec(rb.shape)],
        out_specs=(col(TOP_K), col(TOP_K)),
        out_shape=(jax.ShapeDtypeStruct((TOP_K, n_tok), jnp.int32),
                   jax.ShapeDtypeStruct((TOP_K, n_tok), F32)),
        compiler_params=_params(1), name="route",
    )(scores, rb)


def _moe_kernel(rows_ref, start_ref, x_ref, wts_ref, w1_ref, w3_ref, w2_ref, *rest, n_rows, emit_bf16,
                experts_per_step):
    gat_ref, y_full_ref, y_half_ref, pending_ref = rest[-4:]
    if emit_bf16:
        out_ref, w1b_ref, w3b_ref, w2b_ref = rest[-8:-4]
        w1b_ref[...] = w1_ref[...].astype(BF16)
        w3b_ref[...] = w3_ref[...].astype(BF16)
        w2b_ref[...] = w2_ref[...].astype(BF16)
        w1_ref, w3_ref, w2_ref = w1b_ref, w3b_ref, w2b_ref
    else:
        out_ref = rest[-5]
    acc = out_ref.at[0]
    step = pl.program_id(0)
    half = n_rows // 2
    y_refs = {n_rows: y_full_ref, half: y_half_ref}

    @pl.when(step == 0)
    def _():
        out_ref[...] = jnp.zeros(out_ref.shape, out_ref.dtype)
        y_full_ref[...] = jnp.zeros(y_full_ref.shape, y_full_ref.dtype)
        y_half_ref[...] = jnp.zeros(y_half_ref.shape, y_half_ref.dtype)
        pending_ref[0] = 0
        pending_ref[1] = 0

    def tile_at(ref, row8):
        return ref.at[pl.ds(pl.multiple_of(row8, SUBLANES), SUBLANES), :]

    def scatter_add(base, n):
        y_ref = y_refs[n]
        for m0 in range(0, n, SCATTER_BATCH):
            updates = []
            for m in range(m0, m0 + SCATTER_BATCH):
                dst = tile_at(acc, rows_ref[base + m])
                updates.append((dst, dst[...] + y_ref[m * SUBLANES:(m + 1) * SUBLANES, :]))
            for dst, val in updates:
                dst[...] = val

    def run_block(j, first, n, nominal, begin, pending):
        redo = nominal - begin
        base = first + begin + n_rows
        for m in range(n):
            gat_ref[m * SUBLANES:(m + 1) * SUBLANES, :] = tile_at(x_ref, rows_ref[base + m])[...]
        xb = _load_row_major_tiles(gat_ref, n).astype(BF16)
        act = jax.nn.silu(_dot(xb, w1_ref[j])) * _dot(xb, w3_ref[j])
        y = _dot(act.astype(BF16), w2_ref[j])

        w_rows = pl.cdiv(n, LANES) + 1
        row_id = lax.broadcasted_iota(jnp.int32, (n, w_rows * LANES), 0)
        lane_id = lax.broadcasted_iota(jnp.int32, (n, w_rows * LANES), 1)
        w_row = base // LANES
        span = jnp.concatenate([wts_ref[w_row + r] for r in range(w_rows)], axis=1)
        picked = jnp.where((lane_id == row_id + base % LANES) & (row_id >= redo),
                           jnp.broadcast_to(span, lane_id.shape), 0.0)
        ys = y * jnp.sum(picked, axis=1, keepdims=True)
        scatter_add(pending, n)
        _store_row_major_tiles(y_refs[n], ys)
        return base

    def one_expert(j):
        e = step * experts_per_step + j
        first = start_ref[e]
        count = start_ref[e + 1] - first
        full_blocks = count // n_rows
        rest_rows = count - full_blocks * n_rows
        main_blocks = full_blocks + (rest_rows > half).astype(jnp.int32)

        def main_block(b, pending):
            nominal = b * n_rows
            return run_block(j, first, n_rows, nominal, jnp.minimum(nominal, count - n_rows), pending)

        pending_ref[0] = lax.fori_loop(0, main_blocks, main_block, pending_ref[0])

        @pl.when((rest_rows > 0) & (rest_rows <= half))
        def _():
            pending_ref[1] = run_block(j, first, half, full_blocks * n_rows, count - half, pending_ref[1])

    for j in range(experts_per_step):
        one_expert(j)

    @pl.when(step == pl.num_programs(0) - 1)
    def _():
        scatter_add(pending_ref[0], n_rows)
        scatter_add(pending_ref[1], half)


def _moe(ids, weights, x_tiles, w1, w3, w2):
    n_tok = ids.shape[1]
    g_tok = min(MOE_GROUP_TOKENS, n_tok)
    n_groups = n_tok // g_tok
    assert n_groups * g_tok == n_tok
    per_group = g_tok * TOP_K
    tok = lax.broadcasted_iota(jnp.int32, ids.shape, 1)
    key = ((tok // g_tok) * N_EXPERTS + ids) * g_tok + tok % g_tok
    key, wts = lax.sort((key.reshape(-1), weights.reshape(-1)), num_keys=1, is_stable=False)
    rows8 = ((key % g_tok) * D_TILES).reshape(n_groups, per_group)
    wts = wts.reshape(n_groups, per_group)
    experts = jnp.arange(N_EXPERTS, dtype=jnp.int32)[None, None, :, None]
    counts = jnp.sum(ids.reshape(TOP_K, n_groups, 1, g_tok) == experts, axis=(0, 3), dtype=jnp.int32)
    start = jnp.concatenate([jnp.zeros((n_groups, 1), jnp.int32), jnp.cumsum(counts, axis=1)], axis=1)
    many_rows = per_group // N_EXPERTS >= 2 * MOE_BLOCK_ROWS
    n_rows = MOE_BLOCK_ROWS if many_rows else MOE_SMALL_BLOCK_ROWS
    pad = ((0, 0), (n_rows, n_rows + LANES))
    rows8 = jnp.pad(rows8, pad)
    wts = jnp.pad(wts, pad).reshape(n_groups, -1, 1, LANES)
    out = None
    for g in range(n_groups):
        out, (w1, w3, w2) = _moe_group_call(g, n_groups, g_tok, n_rows, rows8[g], start[g], x_tiles, wts[g],
                                            w1, w3, w2, out)
    return out, (w1, w3, w2)


def _moe_group_call(group, n_groups, g_tok, n_rows, rows8, start, x_tiles, wts, w1, w3, w2, prev):
    g_rows = g_tok * D_TILES
    assert x_tiles.shape[0] == n_groups * g_rows
    emit_bf16 = w1.dtype != BF16
    assert not (emit_bf16 and prev is not None)
    out_block = (1, g_rows, LANES)
    eps = MOE_EXPERTS_PER_STEP[n_rows] // 2 if emit_bf16 else MOE_EXPERTS_PER_STEP[n_rows]
    per_expert = lambda w: pl.BlockSpec((eps,) + w.shape[1:], lambda s, *_: (s, 0, 0))
    in_specs = [
        pl.BlockSpec((g_rows, LANES), lambda e, *_: (group, 0), pipeline_mode=pl.Buffered(1)),
        pl.BlockSpec(wts.shape, lambda e, *_: (0, 0, 0)),
        per_expert(w1), per_expert(w3), per_expert(w2),
    ]
    args = [rows8, start, x_tiles, wts, w1, w3, w2]
    aliases = {}
    if prev is not None:
        in_specs.append(pl.BlockSpec(memory_space=pl.ANY))
        aliases = {len(args): 0}
        args.append(prev)
    out_specs = [pl.BlockSpec(out_block, lambda e, *_: (group, 0, 0), pipeline_mode=pl.Buffered(1))]
    out_shape = [jax.ShapeDtypeStruct((n_groups,) + out_block[1:], F32)]
    if emit_bf16:
        out_specs += [per_expert(w1), per_expert(w3), per_expert(w2)]
        out_shape += [jax.ShapeDtypeStruct(w.shape, BF16) for w in (w1, w3, w2)]
    results = pl.pallas_call(
        functools.partial(_moe_kernel, n_rows=n_rows, emit_bf16=emit_bf16, experts_per_step=eps),
        grid_spec=pltpu.PrefetchScalarGridSpec(
            num_scalar_prefetch=2, grid=(N_EXPERTS // eps,), in_specs=in_specs, out_specs=out_specs,
            scratch_shapes=[pltpu.VMEM((n_rows * D_TILES, LANES), F32),
                            pltpu.VMEM((n_rows * D_TILES, LANES), F32),
                            pltpu.VMEM((n_rows // 2 * D_TILES, LANES), F32),
                            pltpu.SMEM((2,), jnp.int32)]),
        out_shape=out_shape,
        input_output_aliases=aliases,
        compiler_params=_params(1), name="moe",
    )(*args)
    return results[0], (tuple(results[1:]) if emit_bf16 else (w1, w3, w2))


def _final_kernel(x1_ref, routed_ref, p_ref, s1_32_ref, s3_32_ref, s2_32_ref, g2_ref, b2_ref, wg32_ref,
                  wp32_ref, o_ref, s1_ref, s3_ref, s2_ref, wg_ref, wp_ref, *, alpha):
    _cast_weights_once([(s1_32_ref, s1_ref), (s3_32_ref, s3_ref), (s2_32_ref, s2_ref),
                        (wg32_ref, wg_ref), (wp32_ref, wp_ref)])
    x1 = x1_ref[...]
    xb = x1.astype(BF16)
    act = jax.nn.silu(_dot(xb, s1_ref[...])) * _dot(xb, s3_ref[...])
    ff = _load_row_major_tiles(routed_ref.at[0], x1.shape[0]) + _dot(act.astype(BF16), s2_ref[...])
    x2 = _layer_norm(alpha * x1 + ff, g2_ref[...], b2_ref[...])
    gate = jax.nn.sigmoid(_dot(x2.astype(BF16), wg_ref[...]))
    o_ref[...] = x2 + gate * _dot(p_ref[...].astype(BF16), wp_ref[...])


def _final(x1, routed, p2d, s1, s3, s2, g2, b2, wg, wp, *, alpha):
    n_rows = x1.shape[0]
    tile = ROW_TILE
    row = lambda width: pl.BlockSpec((tile, width), lambda i: (i, 0))
    tiles_per_group = routed.shape[1] // D_TILES // tile
    routed_spec = pl.BlockSpec((1, tile * D_TILES, LANES),
                               lambda i: (i // tiles_per_group, i % tiles_per_group, 0))
    return pl.pallas_call(
        functools.partial(_final_kernel, alpha=alpha),
        grid=(n_rows // tile,),
        in_specs=[row(D_MODEL), routed_spec,
                  row(p2d.shape[1]),
                  _resident_spec(s1.shape), _resident_spec(s3.shape), _resident_spec(s2.shape),
                  _const_spec((1, D_MODEL)), _const_spec((1, D_MODEL)),
                  _resident_spec(wg.shape), _resident_spec(wp.shape)],
        out_specs=row(D_MODEL),
        out_shape=jax.ShapeDtypeStruct((n_rows, D_MODEL), F32),
        scratch_shapes=_bf16_scratch(s1, s3, s2, wg, wp), compiler_params=_params(1), name="final",
    )(x1, routed, p2d, s1, s3, s2, g2, b2, wg, wp)


def _trunk_layer(x, p, cache_k, cache_v, lw, *, alpha, first_layer_ln):
    batch, seq, d = x.shape
    n_rows = batch * seq
    x2d = x.reshape(n_rows, d)
    prompt = cache_k is None
    mlp_len = min(seq, MLP_CHUNK)
    keep = min(PAST_CHUNKS * CHUNK, seq) if prompt else seq
    ln_g, ln_b = first_layer_ln

    w_s = (lw["mlp_w_s"][:, :mlp_len, :mlp_len]
           * jnp.tril(jnp.ones((mlp_len, mlp_len), F32))).astype(BF16)
    b_s = jnp.repeat(lw["mlp_b_s"][:, :mlp_len].T, MLP_WIDTH // MLP_GROUPS, axis=1)
    q, k, v, kf, vf, vn, ga, gb, ob = _proj(
        x2d, ln_g, ln_b, lw["w_in"], lw["mlp_ln_g"][None], lw["mlp_ln_b"][None],
        w_s, b_s, seq=seq, mlp_len=mlp_len, keep_rows=keep)
    if prompt:
        oa = _attn_prompt(q, k, v, lw["attn_rel_bias"], batch=batch, seq=seq)
    else:
        oa = _attn_sample(q, k, v, cache_k, cache_v, lw["attn_rel_bias"], batch=batch, seq=seq)
    x1, x1_tiles, scores = _mix(
        x2d, oa, ob, ga, gb, ln_g, ln_b, lw["w_branch_a"], lw["w_branch_b"], lw["w_out"],
        lw["ln1_g"][None], lw["ln1_b"][None], _split_bf16(lw["router_w"].T), alpha=alpha)
    ids, weights = _route_call(scores, lw["router_bias"][:, None])
    routed, expert_bf16 = _moe(ids, weights, x1_tiles, lw["exp_w1"], lw["exp_w3"], lw["exp_w2"])
    y = _final(x1, routed, p.reshape(n_rows, -1), lw["shared_w1"], lw["shared_w3"], lw["shared_w2"],
               lw["ln2_g"][None], lw["ln2_b"][None], lw["ple_w_gate"], lw["ple_w_proj"], alpha=alpha)
    new_k = kf.reshape(batch, keep, HEADS, HEAD_DIM)
    new_v = vf.reshape(batch, keep, HEADS, HEAD_DIM)
    new_mlp_v = vn.reshape(batch, mlp_len, MLP_WIDTH)
    return y.reshape(batch, seq, d), new_k, new_v, new_mlp_v, expert_bf16


def kernel(x_prompt, x_sample, cache_attn_k, cache_attn_v, p_prompt, p_sample, ln_in_g, ln_in_b, w_in, attn_rel_bias, mlp_ln_g, mlp_ln_b, mlp_w_s, mlp_b_s, w_branch_a, w_branch_b, w_out, ln1_g, ln1_b, router_w, router_bias, exp_w1, exp_w3, exp_w2, shared_w1, shared_w3, shared_w2, ln2_g, ln2_b, ple_w_gate, ple_w_proj):
    depth = w_in.shape[0]
    assert depth == 1, "the fused proj/mix kernels apply LayerNorm_in themselves: single layer only"
    alpha = (2 * depth) ** 0.25
    stacked = dict(w_in=w_in, attn_rel_bias=attn_rel_bias, mlp_ln_g=mlp_ln_g, mlp_ln_b=mlp_ln_b,
                   mlp_w_s=mlp_w_s, mlp_b_s=mlp_b_s, w_branch_a=w_branch_a, w_branch_b=w_branch_b,
                   w_out=w_out, ln1_g=ln1_g, ln1_b=ln1_b, router_w=router_w, router_bias=router_bias,
                   exp_w1=exp_w1, exp_w3=exp_w3, exp_w2=exp_w2, shared_w1=shared_w1,
                   shared_w3=shared_w3, shared_w2=shared_w2, ln2_g=ln2_g, ln2_b=ln2_b,
                   ple_w_gate=ple_w_gate, ple_w_proj=ple_w_proj)
    lw = {name: value[0] for name, value in stacked.items()}
    ln_in =(ln_in_g[None], ln_in_b[None])
    yp, kp, vp, mp, expert_bf16 = _trunk_layer(x_prompt, p_prompt[0], None, None, lw, alpha=alpha,
                                               first_layer_ln=ln_in)
    lw = dict(lw, **dict(zip(("exp_w1", "exp_w3", "exp_w2"), expert_bf16)))
    ys, ks, vs, ms, _ = _trunk_layer(x_sample, p_sample[0], cache_attn_k[0], cache_attn_v[0], lw,
                                     alpha=alpha, first_layer_ln=ln_in)
    return (yp, ys, kp[None], vp[None], ks[None], vs[None], mp[None], ms[None])
```
